```python
import math
import jax
import jax.numpy as jnp
from jax import lax
import numpy as np

D_MODEL = 1024
BATCH = 16
SEQ = 2048
DEPTH = 4
DEC_BATCH = 128
DEC_SEQ = 8
PAST_LEN = 8192
PAGE_SIZE = 128

N_HEADS = 16
HEAD_DIM = D_MODEL // N_HEADS
ROPE_DIM = HEAD_DIM // 4
ROPE_THETA = 500000.0
EPS = 1e-6
Q_BLOCK = 128
SWA_KV_HEADS = 4
SWA_WINDOW = 128
MLA_Q_LORA = 384
MLA_KV_LORA = 256
MLA_NOPE = 64
MLA_ROPE = 32
MLA_V = 64
MLA_THETA = 10000.0
NSA_KV_HEADS = 2
NSA_CMP_BLOCK = 32
NSA_SEL_BLOCK = 64
NSA_TOPK = 16
NSA_WINDOW = 128
NSA_Q_BLOCK = 64
D_FF = 2816
CONV_W = 3
N_MIXERS = 3
N_SWA_LAYERS = (DEPTH + 2) // 3
N_MLA_LAYERS = (DEPTH + 1) // 3
N_NSA_LAYERS = DEPTH // 3
A_IN = N_HEADS * HEAD_DIM + 2 * SWA_KV_HEADS * HEAD_DIM
B_IN = MLA_Q_LORA + MLA_KV_LORA + MLA_ROPE
C_IN = N_HEADS * HEAD_DIM + 6 * NSA_KV_HEADS * HEAD_DIM + 3 * N_HEADS

kernel_name = 'hybrid_swa_mla_nsa_convffn_step'


def rms_norm(x, g):
    xf = x.astype(jnp.float32)
    y = xf * lax.rsqrt(jnp.mean(xf * xf, axis=-1, keepdims=True) + EPS)
    return (y * g.astype(jnp.float32)).astype(x.dtype)


def rope(x, pos, rot_dim, theta):
    half = rot_dim // 2
    inv = jnp.power(jnp.float32(theta), -jnp.arange(half, dtype=jnp.float32) / half)
    ang = pos.astype(jnp.float32)[:, None] * inv[None, :]
    cos = jnp.cos(ang)[:, None, :]
    sin = jnp.sin(ang)[:, None, :]
    xr = x[..., :rot_dim].astype(jnp.float32)
    x1, x2 = xr[..., :half], xr[..., half:]
    rot = jnp.concatenate([x1 * cos - x2 * sin, x2 * cos + x1 * sin], axis=-1).astype(x.dtype)
    return jnp.concatenate([rot, x[..., rot_dim:]], axis=-1)


def masked_softmax(s, mask, sink=None):
    s = jnp.where(mask, s, -jnp.inf)
    m = jnp.max(s, axis=-1, keepdims=True)
    if sink is not None:
        m = jnp.maximum(m, sink)
    m = jnp.where(jnp.isfinite(m), m, 0.0)
    p = jnp.exp(s - m)
    den = jnp.sum(p, axis=-1, keepdims=True)
    if sink is not None:
        den = den + jnp.exp(sink - m)
    return p / jnp.maximum(den, jnp.finfo(jnp.float32).tiny)


def gqa_attend(q, k, v, mask, sink):
    scale = q.shape[-1] ** -0.5
    s = jnp.einsum('bnqhgd,bnkhd->bnhgqk', q, k).astype(jnp.float32) * scale
    p = masked_softmax(s, mask, sink)
    return jnp.einsum('bnhgqk,bnkhd->bnqhgd', p.astype(v.dtype), v)


def banded_window_attention(q, k, v, window, sink):
    B, S, H, D = q.shape
    KV = k.shape[2]
    G = H // KV
    nb = S // window
    qb = q.reshape(B, nb, window, KV, G, D)
    pad = jnp.zeros((B, window, KV, D), k.dtype)
    kb = jnp.concatenate([pad, k], axis=1).reshape(B, nb + 1, window, KV, D)
    vb = jnp.concatenate([pad, v], axis=1).reshape(B, nb + 1, window, KV, D)
    kk = jnp.concatenate([kb[:, :-1], kb[:, 1:]], axis=2)
    vv = jnp.concatenate([vb[:, :-1], vb[:, 1:]], axis=2)
    qi = jnp.arange(window)[:, None]
    kj = jnp.arange(2 * window)[None, :]
    diff = qi + window - kj
    key_pos = (jnp.arange(nb)[:, None, None] - 1) * window + kj[None]
    mask = (diff[None] >= 0) & (diff[None] <= window) & (key_pos >= 0)
    o = gqa_attend(qb, kk, vv, mask[None, :, None, None], sink)
    return o.reshape(B, S, H, D)


def window_attention_step(q, k, v, k_buf, v_buf, window, sink):
    B, S, H, D = q.shape
    KV = k.shape[2]
    G = H // KV
    kk = jnp.concatenate([k_buf, k], axis=1)
    vv = jnp.concatenate([v_buf, v], axis=1)
    diff = (window + jnp.arange(S))[:, None] - jnp.arange(kk.shape[1])[None, :]
    mask = (diff >= 0) & (diff <= window)
    o = gqa_attend(q.reshape(B, 1, S, KV, G, D), kk[:, None], vv[:, None],
                   mask[None, None, None, None], sink)
    return o.reshape(B, S, H, D), kk[:, S:], vv[:, S:]


def gather_pages(pool, page_table):
    rows = pool[page_table]
    return rows.reshape((page_table.shape[0], -1) + pool.shape[2:])


def swa_mixer(h, pos, w_in, q_norm, k_norm, sinks, w_out, k_buf=None, v_buf=None):
    B, S, _ = h.shape
    G = N_HEADS // SWA_KV_HEADS
    q, k, v = jnp.split(h @ w_in, [N_HEADS * HEAD_DIM, (N_HEADS + SWA_KV_HEADS) * HEAD_DIM], axis=-1)
    q = rope(rms_norm(q.reshape(B, S, N_HEADS, HEAD_DIM), q_norm), pos, ROPE_DIM, ROPE_THETA)
    k = rope(rms_norm(k.reshape(B, S, SWA_KV_HEADS, HEAD_DIM), k_norm), pos, ROPE_DIM, ROPE_THETA)
    v = v.reshape(B, S, SWA_KV_HEADS, HEAD_DIM)
    sink = sinks.astype(jnp.float32).reshape(1, 1, SWA_KV_HEADS, G, 1, 1)
    if k_buf is None:
        o = banded_window_attention(q, k, v, SWA_WINDOW, sink)
        new_k, new_v = k[:, -SWA_WINDOW:], v[:, -SWA_WINDOW:]
    else:
        o, new_k, new_v = window_attention_step(q, k, v, k_buf, v_buf, SWA_WINDOW, sink)
    return o.reshape(B, S, -1) @ w_out, new_k, new_v


def mla_attend(q_lat, q_rope, c, k_rope, q_pos, k_pos):
    scale = (MLA_NOPE + MLA_ROPE) ** -0.5
    s = (jnp.einsum('bqhc,bkc->bhqk', q_lat, c) +
         jnp.einsum('bqhr,bkr->bhqk', q_rope, k_rope)).astype(jnp.float32) * scale
    p = masked_softmax(s, k_pos[None, :] <= q_pos[:, None])
    return jnp.einsum('bhqk,bkc->bqhc', p.astype(c.dtype), c)


def mla_prompt_attention(q_lat, q_rope, c, k_rope, pos):
    B, S, H, C = q_lat.shape
    qb = math.gcd(S, Q_BLOCK)
    nb = S // qb

    def one_block(args):
        ql, qr, qp = args
        return mla_attend(ql, qr, c, k_rope, qp, pos)

    o = lax.map(one_block, (q_lat.reshape(B, nb, qb, H, C).swapaxes(0, 1),
                            q_rope.reshape(B, nb, qb, H, MLA_ROPE).swapaxes(0, 1),
                            pos.reshape(nb, qb)))
    return o.swapaxes(0, 1).reshape(B, S, H, C)


def mla_mixer(h, pos, w_in, qa_norm, w_qb, q_norm_nope, q_norm_rope, kv_norm, krope_norm,
              w_uk, w_uv, w_out, lat_pool=None, krope_pool=None, page_table=None):
    B, S, _ = h.shape
    qa, ckv, kr = jnp.split(h @ w_in, [MLA_Q_LORA, MLA_Q_LORA + MLA_KV_LORA], axis=-1)
    q = (rms_norm(qa, qa_norm) @ w_qb).reshape(B, S, N_HEADS, MLA_NOPE + MLA_ROPE)
    q_nope = rms_norm(q[..., :MLA_NOPE], q_norm_nope)
    q_rope = rope(rms_norm(q[..., MLA_NOPE:], q_norm_rope), pos, MLA_ROPE, MLA_THETA)
    c = rms_norm(ckv, kv_norm)
    kr = rope(rms_norm(kr, krope_norm)[:, :, None], pos, MLA_ROPE, MLA_THETA)[:, :, 0]
    q_lat = jnp.einsum('bshn,chn->bshc', q_nope, w_uk)
    if lat_pool is None:
        o_lat = mla_prompt_attention(q_lat, q_rope, c, kr, pos)
    else:
        c_all = jnp.concatenate([gather_pages(lat_pool, page_table), c], axis=1)
        kr_all = jnp.concatenate([gather_pages(krope_pool, page_table), kr], axis=1)
        o_lat = mla_attend(q_lat, q_rope, c_all, kr_all, pos, jnp.arange(c_all.shape[1], dtype=jnp.int32))
    o = jnp.einsum('bshc,chv->bshv', o_lat, w_uv).reshape(B, S, -1)
    return o @ w_out, c, kr


def nsa_compressed_selected(q, q_rot, pos, kc_rows, vc_rows, ks_rows, vs_rows, wk, wv, kc_norm):
    B, S, H, D = q.shape
    KV = kc_rows.shape[2]
    G = H // KV
    T = kc_rows.shape[1]
    T_pad = -(-T // NSA_SEL_BLOCK) * NSA_SEL_BLOCK
    pad_rows = lambda a: jnp.pad(a, ((0, 0), (0, T_pad - T), (0, 0), (0, 0)))
    nc = T_pad // NSA_CMP_BLOCK
    ns = T_pad // NSA_SEL_BLOCK
    scale = D ** -0.5
    kcb = pad_rows(kc_rows).reshape(B, nc, NSA_CMP_BLOCK, KV, D)
    vcb = pad_rows(vc_rows).reshape(B, nc, NSA_CMP_BLOCK, KV, D)
    k_cmp = rms_norm(jnp.einsum('bnjhd,jh->bnhd', kcb, wk), kc_norm)
    v_cmp = jnp.einsum('bnjhd,jh->bnhd', vcb, wv)
    qg = q.reshape(B, S, KV, G, D)
    s = jnp.einsum('bshgd,bnhd->bhgsn', qg, k_cmp).astype(jnp.float32) * scale
    cmp_end = (jnp.arange(nc) + 1) * NSA_CMP_BLOCK - 1
    p_cmp = masked_softmax(s, cmp_end[None, :] <= pos[:, None])
    o_cmp = jnp.einsum('bhgsn,bnhd->bshgd', p_cmp.astype(v_cmp.dtype), v_cmp).reshape(B, S, H, D)
    imp = p_cmp.sum(axis=2).reshape(B, KV, S, ns, NSA_SEL_BLOCK // NSA_CMP_BLOCK).sum(-1)
    blk = jnp.arange(ns)
    is_cur = blk[None, :] == (pos // NSA_SEL_BLOCK)[:, None]
    is_valid = blk[None, :] * NSA_SEL_BLOCK <= pos[:, None]
    imp = jnp.where(is_cur, jnp.inf, jnp.where(is_valid, imp, -jnp.inf))
    n_sel = min(NSA_TOPK, ns)
    _, idx = lax.top_k(imp, n_sel)
    ksb = pad_rows(ks_rows).reshape(B, ns, NSA_SEL_BLOCK, KV, D).transpose(0, 3, 1, 2, 4)
    vsb = pad_rows(vs_rows).reshape(B, ns, NSA_SEL_BLOCK, KV, D).transpose(0, 3, 1, 2, 4)
    qb = math.gcd(S, NSA_Q_BLOCK)
    nq = S // qb
    bi = jnp.arange(B)[:, None, None, None]
    hi = jnp.arange(KV)[None, :, None, None]

    def one_chunk(args):
        qc, ic, pc = args
        kg = ksb[bi, hi, ic]
        vg = vsb[bi, hi, ic]
        kpos = ic[..., None] * NSA_SEL_BLOCK + jnp.arange(NSA_SEL_BLOCK)
        sc = jnp.einsum('bqhgd,bhqksd->bhgqks', qc, kg).astype(jnp.float32) * scale
        mask = (kpos <= pc[None, None, :, None, None]).reshape(B, KV, 1, qb, -1)
        p = masked_softmax(sc.reshape(B, KV, G, qb, -1), mask)
        return jnp.einsum('bhgqk,bhqkd->bqhgd', p.astype(vg.dtype), vg.reshape(B, KV, qb, -1, D))

    o_sel = lax.map(one_chunk, (q_rot.reshape(B, nq, qb, KV, G, D).swapaxes(0, 1),
                                idx.reshape(B, KV, nq, qb, n_sel).transpose(2, 0, 1, 3, 4),
                                pos.reshape(nq, qb)))
    o_sel = o_sel.swapaxes(0, 1).reshape(B, S, H, D)
    return o_cmp, o_sel


def nsa_mixer(h, pos, w_in, q_norm, k_norm, cmp_wk, cmp_wv, w_out,
              pool_kc=None, pool_vc=None, pool_ks=None, pool_vs=None,
              kw_buf=None, vw_buf=None, page_table=None):
    B, S, _ = h.shape
    hq, kvd = N_HEADS * HEAD_DIM, NSA_KV_HEADS * HEAD_DIM
    cuts = [hq + i * kvd for i in range(7)]
    q, kc, vc, ks, vs, kw, vw, gl = jnp.split(h @ w_in, cuts, axis=-1)
    kv_shape = (B, S, NSA_KV_HEADS, HEAD_DIM)
    q = rms_norm(q.reshape(B, S, N_HEADS, HEAD_DIM), q_norm)
    q_rot = rope(q, pos, ROPE_DIM, ROPE_THETA)
    kc = kc.reshape(kv_shape)
    vc = vc.reshape(kv_shape)
    ks = rope(rms_norm(ks.reshape(kv_shape), k_norm[1]), pos, ROPE_DIM, ROPE_THETA)
    vs = vs.reshape(kv_shape)
    kw = rope(rms_norm(kw.reshape(kv_shape), k_norm[2]), pos, ROPE_DIM, ROPE_THETA)
    vw = vw.reshape(kv_shape)
    gates = jax.nn.sigmoid(gl.reshape(B, S, N_HEADS, 3).astype(jnp.float32)).astype(h.dtype)
    if pool_kc is None:
        kc_all, vc_all, ks_all, vs_all = kc, vc, ks, vs
        o_win = banded_window_attention(q_rot, kw, vw, NSA_WINDOW, None)
        new_kw, new_vw = kw[:, -NSA_WINDOW:], vw[:, -NSA_WINDOW:]
    else:
        kc_all = jnp.concatenate([gather_pages(pool_kc, page_table), kc], axis=1)
        vc_all = jnp.concatenate([gather_pages(pool_vc, page_table), vc], axis=1)
        ks_all = jnp.concatenate([gather_pages(pool_ks, page_table), ks], axis=1)
        vs_all = jnp.concatenate([gather_pages(pool_vs, page_table), vs], axis=1)
        o_win, new_kw, new_vw = window_attention_step(q_rot, kw, vw, kw_buf, vw_buf, NSA_WINDOW, None)
    o_cmp, o_sel = nsa_compressed_selected(q, q_rot, pos, kc_all, vc_all, ks_all, vs_all,
                                           cmp_wk, cmp_wv, k_norm[0])
    o = gates[..., 0:1] * o_cmp + gates[..., 1:2] * o_sel + gates[..., 2:3] * o_win
    return o.reshape(B, S, -1) @ w_out, kc, vc, ks, vs, new_kw, new_vw


def conv_ffn(h, w_up, conv_w, w_down, buf):
    S = h.shape[1]
    u = h @ w_up
    up = jnp.concatenate([buf, u], axis=1)
    mixed = conv_w[0] * up[:, 0:S]
    for j in range(1, CONV_W):
        mixed = mixed + conv_w[j] * up[:, j:j + S]
    g, val = jnp.split(mixed, 2, axis=-1)
    return (jax.nn.silu(g) * val) @ w_down, up[:, S:]


def modulate(c, w, b):
    mod = (jax.nn.silu(c) @ w + b)[:, None, :]
    return jnp.split(mod, 6, axis=-1)


def setup_inputs(seed: int = 0) -> dict:
    key = jax.random.key(seed)
    ks = iter(jax.random.split(key, 64))
    f32 = jnp.float32

    def nrm(shape, scale=1.0):
        return jax.random.normal(next(ks), shape, f32) * scale

    def gain(shape):
        return 1.0 + nrm(shape, 0.02)

    n_pages = PAST_LEN // PAGE_SIZE
    n_used = DEC_BATCH * n_pages
    n_pool = n_used + n_used // 4
    d, hd, h = D_MODEL, HEAD_DIM, N_HEADS
    kvn = (N_NSA_LAYERS, n_pool, PAGE_SIZE, NSA_KV_HEADS, hd)
    return {
        'x_prompt': nrm((BATCH, SEQ, d)),
        'x_sample': nrm((DEC_BATCH, DEC_SEQ, d)),
        'cache_swa_k': nrm((N_SWA_LAYERS, DEC_BATCH, SWA_WINDOW, SWA_KV_HEADS, hd)),
        'cache_swa_v': nrm((N_SWA_LAYERS, DEC_BATCH, SWA_WINDOW, SWA_KV_HEADS, hd)),
        'cache_mla_latent': nrm((N_MLA_LAYERS, n_pool, PAGE_SIZE, MLA_KV_LORA)),
        'cache_mla_krope': nrm((N_MLA_LAYERS, n_pool, PAGE_SIZE, MLA_ROPE)),
        'cache_nsa_kcmp': nrm(kvn),
        'cache_nsa_vcmp': nrm(kvn),
        'cache_nsa_ksel': nrm(kvn),
        'cache_nsa_vsel': nrm(kvn),
        'cache_nsa_kwin': nrm((N_NSA_LAYERS, DEC_BATCH, NSA_WINDOW, NSA_KV_HEADS, hd)),
        'cache_nsa_vwin': nrm((N_NSA_LAYERS, DEC_BATCH, NSA_WINDOW, NSA_KV_HEADS, hd)),
        'state_conv_ffn': nrm((DEPTH, DEC_BATCH, CONV_W - 1, 2 * D_FF)),
        'page_table': jax.random.permutation(next(ks), n_pool)[:n_used].reshape(DEC_BATCH, n_pages).astype(jnp.int32),
        'c_prompt': nrm((BATCH, d)),
        'c_sample': nrm((DEC_BATCH, d)),
        'ada_w': nrm((DEPTH, d, 6 * d), 0.5 * d ** -0.5),
        'ada_b': nrm((DEPTH, 6 * d), 0.02),
        'norm_mix': gain((DEPTH, d)),
        'norm_ffn': gain((DEPTH, d)),
        'ffn_w_up': nrm((DEPTH, d, 2 * D_FF), d ** -0.5),
        'ffn_conv': nrm((DEPTH, CONV_W, 2 * D_FF), 0.3) + jnp.eye(CONV_W, dtype=f32)[-1][:, None],
        'ffn_w_down': nrm((DEPTH, D_FF, d), D_FF ** -0.5),
        'a_w_in': nrm((N_SWA_LAYERS, d, A_IN), d ** -0.5),
        'a_q_norm': gain((N_SWA_LAYERS, hd)),
        'a_k_norm': gain((N_SWA_LAYERS, hd)),
        'a_sinks': nrm((N_SWA_LAYERS, h), 0.5),
        'a_w_out': nrm((N_SWA_LAYERS, h * hd, d), (h * hd) ** -0.5),
        'b_w_in': nrm((N_MLA_LAYERS, d, B_IN), d ** -0.5),
        'b_qa_norm': gain((N_MLA_LAYERS, MLA_Q_LORA)),
        'b_w_qb': nrm((N_MLA_LAYERS, MLA_Q_LORA, h * (MLA_NOPE + MLA_ROPE)), MLA_Q_LORA ** -0.5),
        'b_q_norm_nope': gain((N_MLA_LAYERS, MLA_NOPE)),
        'b_q_norm_rope': gain((N_MLA_LAYERS, MLA_ROPE)),
        'b_kv_norm': gain((N_MLA_LAYERS, MLA_KV_LORA)),
        'b_krope_norm': gain((N_MLA_LAYERS, MLA_ROPE)),
        'b_w_uk': nrm((N_MLA_LAYERS, MLA_KV_LORA, h, MLA_NOPE), MLA_KV_LORA ** -0.5),
        'b_w_uv': nrm((N_MLA_LAYERS, MLA_KV_LORA, h, MLA_V), MLA_KV_LORA ** -0.5),
        'b_w_out': nrm((N_MLA_LAYERS, h * MLA_V, d), (h * MLA_V) ** -0.5),
        'c_w_in': nrm((N_NSA_LAYERS, d, C_IN), d ** -0.5),
        'c_q_norm': gain((N_NSA_LAYERS, hd)),
        'c_k_norm': gain((N_NSA_LAYERS, 3, hd)),
        'c_cmp_wk': (1.0 + nrm((N_NSA_LAYERS, NSA_CMP_BLOCK, NSA_KV_HEADS), 0.1)) / NSA_CMP_BLOCK,
        'c_cmp_wv': (1.0 + nrm((N_NSA_LAYERS, NSA_CMP_BLOCK, NSA_KV_HEADS), 0.1)) / NSA_CMP_BLOCK,
        'c_w_out': nrm((N_NSA_LAYERS, h * hd, d), (h * hd) ** -0.5),
    }


def reference(x_prompt, x_sample, cache_swa_k, cache_swa_v, cache_mla_latent, cache_mla_krope,
              cache_nsa_kcmp, cache_nsa_vcmp, cache_nsa_ksel, cache_nsa_vsel, cache_nsa_kwin,
              cache_nsa_vwin, state_conv_ffn, page_table, c_prompt, c_sample,
              ada_w, ada_b, norm_mix, norm_ffn, ffn_w_up, ffn_conv, ffn_w_down,
              a_w_in, a_q_norm, a_k_norm, a_sinks, a_w_out,
              b_w_in, b_qa_norm, b_w_qb, b_q_norm_nope, b_q_norm_rope, b_kv_norm, b_krope_norm,
              b_w_uk, b_w_uv, b_w_out,
              c_w_in, c_q_norm, c_k_norm, c_cmp_wk, c_cmp_wv, c_w_out):
    xp, xs = x_prompt, x_sample
    pos_p = jnp.arange(x_prompt.shape[1], dtype=jnp.int32)
    pos_s = PAST_LEN + jnp.arange(x_sample.shape[1], dtype=jnp.int32)
    swa_k_p, swa_v_p, swa_k_s, swa_v_s = [], [], [], []
    mla_c_p, mla_r_p, mla_c_s, mla_r_s = [], [], [], []
    nsa_p = [[] for _ in range(6)]
    nsa_s = [[] for _ in range(6)]
    conv_p, conv_s = [], []
    ia = ib = ic = 0
    for layer in range(DEPTH):
        sh1p, sc1p, g1p, sh2p, sc2p, g2p = modulate(c_prompt, ada_w[layer], ada_b[layer])
        sh1s, sc1s, g1s, sh2s, sc2s, g2s = modulate(c_sample, ada_w[layer], ada_b[layer])
        hp = rms_norm(xp, norm_mix[layer]) * (1 + sc1p) + sh1p
        hs = rms_norm(xs, norm_mix[layer]) * (1 + sc1s) + sh1s
        kind = layer % N_MIXERS
        if kind == 0:
            wa = (a_w_in[ia], a_q_norm[ia], a_k_norm[ia], a_sinks[ia], a_w_out[ia])
            op, kp, vp = swa_mixer(hp, pos_p, *wa)
            os_, ks_, vs_ = swa_mixer(hs, pos_s, *wa, cache_swa_k[ia], cache_swa_v[ia])
            swa_k_p.append(kp); swa_v_p.append(vp); swa_k_s.append(ks_); swa_v_s.append(vs_)
            ia += 1
        elif kind == 1:
            wb = (b_w_in[ib], b_qa_norm[ib], b_w_qb[ib], b_q_norm_nope[ib], b_q_norm_rope[ib],
                  b_kv_norm[ib], b_krope_norm[ib], b_w_uk[ib], b_w_uv[ib], b_w_out[ib])
            op, cp, rp = mla_mixer(hp, pos_p, *wb)
            os_, cs_, rs_ = mla_mixer(hs, pos_s, *wb, cache_mla_latent[ib], cache_mla_krope[ib], page_table)
            mla_c_p.append(cp); mla_r_p.append(rp); mla_c_s.append(cs_); mla_r_s.append(rs_)
            ib += 1
        else:
            wc = (c_w_in[ic], c_q_norm[ic], c_k_norm[ic], c_cmp_wk[ic], c_cmp_wv[ic], c_w_out[ic])
            op, *st_p = nsa_mixer(hp, pos_p, *wc)
            os_, *st_s = nsa_mixer(hs, pos_s, *wc, cache_nsa_kcmp[ic], cache_nsa_vcmp[ic],
                                   cache_nsa_ksel[ic], cache_nsa_vsel[ic], cache_nsa_kwin[ic],
                                   cache_nsa_vwin[ic], page_table)
            for j in range(6):
                nsa_p[j].append(st_p[j])
                nsa_s[j].append(st_s[j])
            ic += 1
        xp = xp + g1p * op
        xs = xs + g1s * os_
        hp = rms_norm(xp, norm_ffn[layer]) * (1 + sc2p) + sh2p
        hs = rms_norm(xs, norm_ffn[layer]) * (1 + sc2s) + sh2s
        zero_buf = jnp.zeros((xp.shape[0], CONV_W - 1, 2 * D_FF), xp.dtype)
        fp, bp = conv_ffn(hp, ffn_w_up[layer], ffn_conv[layer], ffn_w_down[layer], zero_buf)
        fs, bs = conv_ffn(hs, ffn_w_up[layer], ffn_conv[layer], ffn_w_down[layer], state_conv_ffn[layer])
        conv_p.append(bp); conv_s.append(bs)
        xp = xp + g2p * fp
        xs = xs + g2s * fs
    new_swa_k_p, new_swa_v_p = jnp.stack(swa_k_p), jnp.stack(swa_v_p)
    new_swa_k_s, new_swa_v_s = jnp.stack(swa_k_s), jnp.stack(swa_v_s)
    new_mla_c_p, new_mla_r_p = jnp.stack(mla_c_p), jnp.stack(mla_r_p)
    new_mla_c_s, new_mla_r_s = jnp.stack(mla_c_s), jnp.stack(mla_r_s)
    nkc_p, nvc_p, nks_p, nvs_p, nkw_p, nvw_p = [jnp.stack(a) for a in nsa_p]
    nkc_s, nvc_s, nks_s, nvs_s, nkw_s, nvw_s = [jnp.stack(a) for a in nsa_s]
    new_conv_p, new_conv_s = jnp.stack(conv_p), jnp.stack(conv_s)
    return (xp, xs,
            new_swa_k_p, new_swa_v_p, new_mla_c_p, new_mla_r_p,
            nkc_p, nvc_p, nks_p, nvs_p, nkw_p, nvw_p, new_conv_p,
            new_swa_k_s, new_swa_v_s, new_mla_c_s, new_mla_r_s,
            nkc_s, nvc_s, nks_s, nvs_s, nkw_s, nvw_s, new_conv_s)
```

```python
import functools
import math

import jax
import jax.numpy as jnp
from jax import lax
from jax.experimental import pallas as pl
from jax.experimental.pallas import tpu as pltpu

F32 = jnp.float32
BF16 = jnp.bfloat16

N_HEADS = 16
HEAD_DIM = 64
ROPE_DIM = 16
ROPE_THETA = 500000.0
EPS = 1e-6
PAGE_SIZE = 128
SWA_KV_HEADS = 4
WINDOW = 128
MLA_Q_LORA = 384
MLA_KV_LORA = 256
MLA_NOPE = 64
MLA_ROPE = 32
MLA_THETA = 10000.0
NSA_KV_HEADS = 2
NSA_CMP_BLOCK = 32
NSA_SEL_BLOCK = 64
NSA_TOPK = 16
N_MIXERS = 3
NEG = -1e30
TINY = float(jnp.finfo(jnp.float32).tiny)
VMEM_LIMIT = 56 * 1024 * 1024
PAGE_SLOTS = 8


def _cparams(sem):
    return pltpu.CompilerParams(dimension_semantics=sem, vmem_limit_bytes=VMEM_LIMIT)


def _dot(a, b):
    return jnp.dot(a, b, preferred_element_type=F32)


def _dot_nt(a, b):
    return lax.dot_general(a, b, (((1,), (1,)), ((), ())), preferred_element_type=F32)


def _rms(x, g):
    return x * lax.rsqrt(jnp.mean(x * x, axis=-1, keepdims=True) + EPS) * g


def _rope(x, c, s, rot):
    half = rot // 2
    parts = [x[:, half:rot], x[:, :half]]
    if x.shape[1] > rot:
        parts.append(x[:, rot:])
    return x * c + jnp.concatenate(parts, axis=1) * s


def _silu(x):
    return x * jax.nn.sigmoid(x)


def _norm_mod(x, g, sc, sh):
    return _rms(x, g) * (1.0 + sc) + sh


def _rope_tables(pos, rot, theta, width):
    half = rot // 2
    inv = jnp.power(jnp.float32(theta), -jnp.arange(half, dtype=F32) / half)
    ang = pos.astype(F32)[:, None] * inv[None, :]
    cos, sin = jnp.cos(ang), jnp.sin(ang)
    n = pos.shape[0]
    c = jnp.concatenate([cos, cos, jnp.ones((n, width - rot), F32)], axis=1)
    s = jnp.concatenate([-sin, sin, jnp.zeros((n, width - rot), F32)], axis=1)
    return c, s


class _Group:
    def __init__(self, per_group, mod, tr):
        self.pm = per_group
        self.mod = mod
        self.tr = tr

    def mod_spec(self, layer, j, nargs=2):
        d = self.mod.shape[-1]
        if self.pm:
            shape, f = (1, 1, 1, 1, d), (lambda g: (layer, j, g, 0, 0))
        else:
            shape, f = (1, 1, self.mod.shape[2], d), (lambda g: (layer, j, 0, 0))
        if nargs == 2:
            return pl.BlockSpec(shape, lambda g, r: f(g))
        return pl.BlockSpec(shape, lambda g, r, k: f(g))


def _mod_val(ref, pm):
    return ref[0, 0, 0] if pm else ref[0, 0]


def _mod_kernel(c_ref, w_ref, b_ref, o_ref):
    a = _silu(c_ref[...])
    o_ref[0, 0] = _dot(a.astype(BF16), w_ref[0].astype(BF16)) + b_ref[0]


def _modulate(c_all, ada_w, ada_b):
    nl, d, d6 = ada_w.shape
    n = c_all.shape[0]
    nj = d6 // d
    return pl.pallas_call(
        _mod_kernel, grid=(nl, nj),
        in_specs=[pl.BlockSpec((n, d), lambda l, j: (0, 0)),
                  pl.BlockSpec((1, d, d), lambda l, j: (l, 0, j)),
                  pl.BlockSpec((1, 1, d), lambda l, j: (l, 0, j))],
        out_specs=pl.BlockSpec((1, 1, n, d), lambda l, j: (l, j, 0, 0)),
        out_shape=jax.ShapeDtypeStruct((nl, nj, n, d), F32),
        compiler_params=_cparams(("arbitrary", "arbitrary")),
    )(c_all, ada_w, ada_b.reshape(nl, 1, d6))


def _proj_kernel(x_ref, g_ref, sc_ref, sh_ref, w_ref, o_ref, *, pm):
    h = _norm_mod(x_ref[0], g_ref[0], _mod_val(sc_ref, pm), _mod_val(sh_ref, pm))
    o_ref[0] = _dot(h.astype(BF16), w_ref[...])


def _proj(x, grp, gains, layer, w_bf):
    ng, nr, d = x.shape
    n = w_bf.shape[1]
    tr = grp.tr
    return pl.pallas_call(
        functools.partial(_proj_kernel, pm=grp.pm), grid=(ng, nr // tr),
        in_specs=[pl.BlockSpec((1, tr, d), lambda g, r: (g, r, 0)),
                  pl.BlockSpec((1, 1, d), lambda g, r: (layer, 0, 0)),
                  grp.mod_spec(layer, 1), grp.mod_spec(layer, 0),
                  pl.BlockSpec((d, n), lambda g, r: (0, 0))],
        out_specs=pl.BlockSpec((1, tr, n), lambda g, r: (g, r, 0)),
        out_shape=jax.ShapeDtypeStruct((ng, nr, n), F32),
        compiler_params=_cparams(("arbitrary", "arbitrary")),
    )(x, gains, grp.mod, grp.mod, w_bf)


def _mla_proj_kernel(x_ref, g_ref, sc_ref, sh_ref, w_ref, qan_ref, wqb_ref, kvn_ref, krn_ref,
                     c_ref, s_ref, q_ref, lat_ref, kr_ref, *, pm):
    h = _norm_mod(x_ref[0], g_ref[0], _mod_val(sc_ref, pm), _mod_val(sh_ref, pm))
    y = _dot(h.astype(BF16), w_ref[...])
    a, b = MLA_Q_LORA, MLA_Q_LORA + MLA_KV_LORA
    qa = _rms(y[:, :a], qan_ref[...])
    q_ref[0] = _dot(qa.astype(BF16), wqb_ref[...])
    lat_ref[0] = _rms(y[:, a:b], kvn_ref[...])
    kr = _rms(y[:, b:], krn_ref[...])
    kr_ref[0] = _rope(kr, c_ref[0], s_ref[0], MLA_ROPE)


def _mla_proj(x, grp, gains, layer, w_bf, qa_norm, wqb_bf, kv_norm, krope_norm, cs, sn):
    ng, nr, d = x.shape
    tr = grp.tr
    nq = wqb_bf.shape[1]
    if grp.pm:
        tab = pl.BlockSpec((1, tr, MLA_ROPE), lambda g, r: (0, r, 0))
    else:
        tab = pl.BlockSpec((1, 1, MLA_ROPE), lambda g, r: (g, 0, 0))
    full = lambda a: pl.BlockSpec(a.shape, lambda g, r: (0,) * a.ndim)
    return pl.pallas_call(
        functools.partial(_mla_proj_kernel, pm=grp.pm), grid=(ng, nr // tr),
        in_specs=[pl.BlockSpec((1, tr, d), lambda g, r: (g, r, 0)),
                  pl.BlockSpec((1, 1, d), lambda g, r: (layer, 0, 0)),
                  grp.mod_spec(layer, 1), grp.mod_spec(layer, 0),
                  full(w_bf), full(qa_norm), full(wqb_bf), full(kv_norm), full(krope_norm), tab, tab],
        out_specs=[pl.BlockSpec((1, tr, nq), lambda g, r: (g, r, 0)),
                   pl.BlockSpec((1, tr, MLA_KV_LORA), lambda g, r: (g, r, 0)),
                   pl.BlockSpec((1, tr, MLA_ROPE), lambda g, r: (g, r, 0))],
        out_shape=[jax.ShapeDtypeStruct((ng, nr, nq), F32),
                   jax.ShapeDtypeStruct((ng, nr, MLA_KV_LORA), F32),
                   jax.ShapeDtypeStruct((ng, nr, MLA_ROPE), F32)],
        compiler_params=_cparams(("arbitrary", "arbitrary")),
    )(x, gains, grp.mod, grp.mod, w_bf, qa_norm, wqb_bf, kv_norm, krope_norm, cs, sn)


def _outproj_kernel(o_ref, x_ref, gt_ref, w_ref, y_ref, *, pm):
    y_ref[0] = x_ref[0] + _mod_val(gt_ref, pm) * _dot(o_ref[0].astype(BF16), w_ref[...])


def _outproj(o, x, grp, layer, w_bf):
    ng, nr, d = x.shape
    k = o.shape[-1]
    tr = grp.tr
    return pl.pallas_call(
        functools.partial(_outproj_kernel, pm=grp.pm), grid=(ng, nr // tr),
        in_specs=[pl.BlockSpec((1, tr, k), lambda g, r: (g, r, 0)),
                  pl.BlockSpec((1, tr, d), lambda g, r: (g, r, 0)),
                  grp.mod_spec(layer, 2),
                  pl.BlockSpec((k, d), lambda g, r: (0, 0))],
        out_specs=pl.BlockSpec((1, tr, d), lambda g, r: (g, r, 0)),
        out_shape=jax.ShapeDtypeStruct((ng, nr, d), F32),
        compiler_params=_cparams(("arbitrary", "arbitrary")),
    )(o, x, grp.mod, w_bf)


HALO = 16


def _ffn_tail(mg, mv, wd_ref, x_ref, gt, y_ref):
    f = pl.program_id(2)
    d = _dot((_silu(mg) * mv).astype(BF16), wd_ref[...])

    @pl.when(f == 0)
    def _():
        y_ref[0] = d

    @pl.when(f > 0)
    def _():
        y_ref[0] += d

    @pl.when(f == pl.num_programs(2) - 1)
    def _():
        y_ref[0] = x_ref[0] + gt * y_ref[0]


def _ffn_prompt_kernel(x_ref, xh_ref, g_ref, sc_ref, sh_ref, gt_ref, wg_ref, wv_ref, cg_ref, cv_ref, wd_ref,
                       y_ref, bg_ref, bv_ref, h_s, *, tr):
    r = pl.program_id(1)

    @pl.when(pl.program_id(2) == 0)
    def _():
        g, sc, sh = g_ref[0], sc_ref[0, 0, 0], sh_ref[0, 0, 0]
        h_s[HALO:, :] = _norm_mod(x_ref[0], g, sc, sh).astype(BF16)
        hh = _norm_mod(xh_ref[0], g, sc, sh)
        h_s[:HALO, :] = jnp.where(r > 0, hh, 0.0).astype(BF16)

    h = h_s[...]

    def conv(u, cw):
        a = pltpu.roll(u, 2, 0)[HALO:]
        b = pltpu.roll(u, 1, 0)[HALO:]
        return cw[0:1] * a + cw[1:2] * b + cw[2:3] * u[HALO:]

    ug = _dot(h, wg_ref[...])
    uv = _dot(h, wv_ref[...])
    bg_ref[0, 0] = ug[HALO + tr - 2:]
    bv_ref[0, 0] = uv[HALO + tr - 2:]
    _ffn_tail(conv(ug, cg_ref[...]), conv(uv, cv_ref[...]), wd_ref, x_ref, gt_ref[0, 0, 0], y_ref)


def _ffn_prompt(x, grp, gains, layer, wup_bf, conv_w, wd_bf, tr, tf):
    ng, nr, d = x.shape
    ff = wd_bf.shape[0]
    nf = ff // tf
    ms = lambda j: grp.mod_spec(layer, j, nargs=3)
    y, bg, bv = pl.pallas_call(
        functools.partial(_ffn_prompt_kernel, tr=tr), grid=(ng, nr // tr, nf),
        in_specs=[pl.BlockSpec((1, tr, d), lambda g, r, f: (g, r, 0)),
                  pl.BlockSpec((1, HALO, d), lambda g, r, f: (g, jnp.maximum(r * (tr // HALO) - 1, 0), 0)),
                  pl.BlockSpec((1, 1, d), lambda g, r, f: (layer, 0, 0)),
                  ms(4), ms(3), ms(5),
                  pl.BlockSpec((d, tf), lambda g, r, f: (0, f)),
                  pl.BlockSpec((d, tf), lambda g, r, f: (0, nf + f)),
                  pl.BlockSpec((3, tf), lambda g, r, f: (0, f)),
                  pl.BlockSpec((3, tf), lambda g, r, f: (0, nf + f)),
                  pl.BlockSpec((tf, d), lambda g, r, f: (f, 0))],
        out_specs=[pl.BlockSpec((1, tr, d), lambda g, r, f: (g, r, 0)),
                   pl.BlockSpec((1, 1, 2, tf), lambda g, r, f: (g, r, 0, f)),
                   pl.BlockSpec((1, 1, 2, tf), lambda g, r, f: (g, r, 0, f))],
        out_shape=[jax.ShapeDtypeStruct((ng, nr, d), F32),
                   jax.ShapeDtypeStruct((ng, nr // tr, 2, ff), F32),
                   jax.ShapeDtypeStruct((ng, nr // tr, 2, ff), F32)],
        scratch_shapes=[pltpu.VMEM((tr + HALO, d), BF16)],
        compiler_params=_cparams(("arbitrary", "arbitrary", "arbitrary")),
    )(x, x, gains, grp.mod, grp.mod, grp.mod, wup_bf, wup_bf, conv_w, conv_w, wd_bf)
    return y, jnp.concatenate([bg[:, -1], bv[:, -1]], axis=-1)


def _ffn_sample_kernel(x_ref, g_ref, sc_ref, sh_ref, gt_ref, sg_ref, sv_ref, wg_ref, wv_ref, cg_ref, cv_ref,
                       wd_ref, y_ref, bg_ref, bv_ref, h_s, *, nt, p):
    tile = lambda m: jnp.concatenate([m] * nt, axis=0)

    @pl.when(pl.program_id(2) == 0)
    def _():
        h_s[...] = _norm_mod(x_ref[0], g_ref[0], tile(sc_ref[0, 0]), tile(sh_ref[0, 0])).astype(BF16)

    h = h_s[...]
    n = nt * p

    def conv(st, u, cw):
        e = jnp.concatenate([st, u], axis=0)
        return cw[0:1] * e[0:n] + cw[1:2] * e[p:p + n] + cw[2:3] * e[2 * p:]

    ug = _dot(h, wg_ref[...])
    uv = _dot(h, wv_ref[...])
    bg_ref[...] = jnp.concatenate([sg_ref[...], ug], axis=0)[n:]
    bv_ref[...] = jnp.concatenate([sv_ref[...], uv], axis=0)[n:]
    _ffn_tail(conv(sg_ref[...], ug, cg_ref[...]), conv(sv_ref[...], uv, cv_ref[...]), wd_ref, x_ref,
              tile(gt_ref[0, 0]), y_ref)


def _ffn_sample(x, grp, gains, layer, wup_bf, conv_w, wd_bf, state, tf):
    nt, p, d = x.shape
    ff = wd_bf.shape[0]
    nf = ff // tf
    n = nt * p
    st = state.transpose(1, 0, 2).reshape(2 * p, 2 * ff)
    ms = lambda j: grp.mod_spec(layer, j, nargs=3)
    y, bg, bv = pl.pallas_call(
        functools.partial(_ffn_sample_kernel, nt=nt, p=p), grid=(1, 1, nf),
        in_specs=[pl.BlockSpec((1, n, d), lambda g, r, f: (0, 0, 0)),
                  pl.BlockSpec((1, 1, d), lambda g, r, f: (layer, 0, 0)),
                  ms(4), ms(3), ms(5),
                  pl.BlockSpec((2 * p, tf), lambda g, r, f: (0, f)),
                  pl.BlockSpec((2 * p, tf), lambda g, r, f: (0, nf + f)),
                  pl.BlockSpec((d, tf), lambda g, r, f: (0, f)),
                  pl.BlockSpec((d, tf), lambda g, r, f: (0, nf + f)),
                  pl.BlockSpec((3, tf), lambda g, r, f: (0, f)),
                  pl.BlockSpec((3, tf), lambda g, r, f: (0, nf + f)),
                  pl.BlockSpec((tf, d), lambda g, r, f: (f, 0))],
        out_specs=[pl.BlockSpec((1, n, d), lambda g, r, f: (0, 0, 0)),
                   pl.BlockSpec((2 * p, tf), lambda g, r, f: (0, f)),
                   pl.BlockSpec((2 * p, tf), lambda g, r, f: (0, f))],
        out_shape=[jax.ShapeDtypeStruct((1, n, d), F32),
                   jax.ShapeDtypeStruct((2 * p, ff), F32),
                   jax.ShapeDtypeStruct((2 * p, ff), F32)],
        scratch_shapes=[pltpu.VMEM((n, d), BF16)],
        compiler_params=_cparams(("arbitrary", "arbitrary", "arbitrary")),
    )(x.reshape(1, n, d), gains, grp.mod, grp.mod, grp.mod, st, st, wup_bf, wup_bf, conv_w, conv_w, wd_bf)
    new_state = jnp.concatenate([bg, bv], axis=-1).reshape(2, p, 2 * ff).transpose(1, 0, 2)
    return y.reshape(nt, p, d), new_state


def _attend_group(qs, kk, vv, mask, sink_col):
    ng, nq = len(qs), qs[0].shape[0]
    q = jnp.concatenate(qs, axis=0).astype(BF16)
    s = jnp.where(jnp.concatenate([mask] * ng, axis=0), _dot_nt(q, kk), NEG)
    m = jnp.max(s, axis=-1, keepdims=True)
    if sink_col is not None:
        m = jnp.maximum(m, sink_col)
    p = jnp.exp(s - m)
    den = jnp.sum(p, axis=-1, keepdims=True)
    if sink_col is not None:
        den = den + jnp.exp(sink_col - m)
    o = _dot(p.astype(BF16), vv) / den
    return [o[g * nq:(g + 1) * nq] for g in range(ng)]


def _sink_col(sink_ref, heads, nq):
    return jnp.concatenate([jnp.broadcast_to(sink_ref[:, h:h + 1], (nq, 1)) for h in heads], axis=0)


def _band_mask(nq, w, prev_off):
    qi = lax.broadcasted_iota(jnp.int32, (nq, w + nq), 0)
    kj = lax.broadcasted_iota(jnp.int32, (nq, w + nq), 1)
    return ((kj < w) & (kj >= qi + prev_off)) | ((kj >= w) & ((kj - w) <= qi))


def _head(x, h):
    return x[:, h * HEAD_DIM:(h + 1) * HEAD_DIM]


def _band_attn_kernel(q_ref, kp_ref, kc_ref, vp_ref, vc_ref, qn_ref, kn_ref, cq_ref, sq_ref, cp_ref, sp_ref,
                      sink_ref, o_ref, ko_ref, vo_ref, *, n_kv):
    i = pl.program_id(1)
    w = q_ref.shape[1]
    grp = N_HEADS // n_kv
    scale = HEAD_DIM ** -0.5
    mask = _band_mask(w, w, jnp.where(i > 0, 0, w))
    cq, sq, cp, sp = cq_ref[...], sq_ref[...], cp_ref[...], sp_ref[...]
    q, kp, kc, vp, vc = q_ref[0], kp_ref[0], kc_ref[0], vp_ref[0], vc_ref[0]
    outs, knew = [], []
    for kv in range(n_kv):
        kcn = _rope(_rms(_head(kc, kv), kn_ref[...]), cq, sq, ROPE_DIM)
        kpn = _rope(_rms(_head(kp, kv), kn_ref[...]), cp, sp, ROPE_DIM)
        knew.append(kcn)
        kk = jnp.concatenate([kpn, kcn], axis=0).astype(BF16)
        vv = jnp.concatenate([_head(vp, kv), _head(vc, kv)], axis=0).astype(BF16)
        heads = range(kv * grp, (kv + 1) * grp)
        qs = [_rope(_rms(_head(q, h), qn_ref[...]), cq, sq, ROPE_DIM) * scale for h in heads]
        outs += _attend_group(qs, kk, vv, mask, _sink_col(sink_ref, heads, w))
    o_ref[0] = jnp.concatenate(outs, axis=1).astype(o_ref.dtype)
    ko_ref[0] = jnp.concatenate(knew, axis=1)
    vo_ref[0] = vc


def _band_attn(y, q_norm, k_norm, sinks, cs, sn):
    b, s, _ = y.shape
    w = WINDOW
    kvd = SWA_KV_HEADS * HEAD_DIM
    hd = N_HEADS * HEAD_DIM
    kb, vb = hd // kvd, hd // kvd + 1
    prev = lambda i: jnp.maximum(i - 1, 0)
    full = lambda a: pl.BlockSpec(a.shape, lambda bb, i: (0,) * a.ndim)
    tab_c = pl.BlockSpec((w, HEAD_DIM), lambda bb, i: (i, 0))
    tab_p = pl.BlockSpec((w, HEAD_DIM), lambda bb, i: (prev(i), 0))
    return pl.pallas_call(
        functools.partial(_band_attn_kernel, n_kv=SWA_KV_HEADS), grid=(b, s // w),
        in_specs=[pl.BlockSpec((1, w, hd), lambda bb, i: (bb, i, 0)),
                  pl.BlockSpec((1, w, kvd), lambda bb, i: (bb, prev(i), kb)),
                  pl.BlockSpec((1, w, kvd), lambda bb, i: (bb, i, kb)),
                  pl.BlockSpec((1, w, kvd), lambda bb, i: (bb, prev(i), vb)),
                  pl.BlockSpec((1, w, kvd), lambda bb, i: (bb, i, vb)),
                  full(q_norm), full(k_norm), tab_c, tab_c, tab_p, tab_p, full(sinks)],
        out_specs=[pl.BlockSpec((1, w, hd), lambda bb, i: (bb, i, 0)),
                   pl.BlockSpec((1, w, kvd), lambda bb, i: (bb, 0, 0)),
                   pl.BlockSpec((1, w, kvd), lambda bb, i: (bb, 0, 0))],
        out_shape=[jax.ShapeDtypeStruct((b, s, hd), BF16),
                   jax.ShapeDtypeStruct((b, w, kvd), F32),
                   jax.ShapeDtypeStruct((b, w, kvd), F32)],
        compiler_params=_cparams(("arbitrary", "arbitrary")),
    )(y, y, y, y, y, q_norm, k_norm, cs, sn, cs, sn, sinks)


def _rows(ref, n):
    return jnp.concatenate([ref[t, 0] for t in range(n)], axis=0)


def _window_step(q, k_new, v_new, kb, vb, qn, kn, cs, sn, sink_ref, n_kv):
    ns, w = q.shape[0], kb.shape[0]
    grp = N_HEADS // n_kv
    scale = HEAD_DIM ** -0.5
    mask = _band_mask(ns, w, 0)
    outs, knew = [], []
    for kv in range(n_kv):
        kn_h = _rope(_rms(_head(k_new, kv), kn), cs, sn, ROPE_DIM)
        knew.append(kn_h)
        kk = jnp.concatenate([_head(kb, kv), kn_h], axis=0).astype(BF16)
        vv = jnp.concatenate([_head(vb, kv), _head(v_new, kv)], axis=0).astype(BF16)
        heads = range(kv * grp, (kv + 1) * grp)
        qs = [_rope(_rms(_head(q, h), qn), cs, sn, ROPE_DIM) * scale for h in heads]
        sink = None if sink_ref is None else _sink_col(sink_ref, heads, ns)
        outs += _attend_group(qs, kk, vv, mask, sink)
    knew = jnp.concatenate(knew, axis=1)
    k_out = jnp.concatenate([kb[ns:], knew], axis=0)
    v_out = jnp.concatenate([vb[ns:], v_new], axis=0)
    return outs, knew, k_out, v_out


def _step_attn_kernel(q_ref, k_ref, v_ref, kb_ref, vb_ref, qn_ref, kn_ref, cs_ref, sn_ref, sink_ref,
                      o_ref, ko_ref, vo_ref, *, n_kv):
    ns = q_ref.shape[0]
    outs, _, k_out, v_out = _window_step(_rows(q_ref, ns), _rows(k_ref, ns), _rows(v_ref, ns),
                                         kb_ref[0, 0], vb_ref[0, 0], qn_ref[...], kn_ref[...],
                                         cs_ref[...], sn_ref[...], sink_ref, n_kv)
    o = jnp.concatenate(outs, axis=1)
    for t in range(ns):
        o_ref[t, 0] = o[t:t + 1]
    ko_ref[0] = k_out
    vo_ref[0] = v_out


def _step_attn(y, k_cache, v_cache, li, q_norm, k_norm, sinks, cs, sn):
    ns, db, n = y.shape
    w = k_cache.shape[2]
    kvd = SWA_KV_HEADS * HEAD_DIM
    hd = N_HEADS * HEAD_DIM
    kb, vb = hd // kvd, hd // kvd + 1
    y4 = y.reshape(ns, db, 1, n)
    full = lambda a: pl.BlockSpec(a.shape, lambda b: (0,) * a.ndim)
    o, ko, vo = pl.pallas_call(
        functools.partial(_step_attn_kernel, n_kv=SWA_KV_HEADS), grid=(db,),
        in_specs=[pl.BlockSpec((ns, 1, 1, hd), lambda b: (0, b, 0, 0)),
                  pl.BlockSpec((ns, 1, 1, kvd), lambda b: (0, b, 0, kb)),
                  pl.BlockSpec((ns, 1, 1, kvd), lambda b: (0, b, 0, vb)),
                  pl.BlockSpec((1, 1, w, kvd), lambda b: (li, b, 0, 0)),
                  pl.BlockSpec((1, 1, w, kvd), lambda b: (li, b, 0, 0)),
                  full(q_norm), full(k_norm), full(cs), full(sn), full(sinks)],
        out_specs=[pl.BlockSpec((ns, 1, 1, hd), lambda b: (0, b, 0, 0)),
                   pl.BlockSpec((1, w, kvd), lambda b: (b, 0, 0)),
                   pl.BlockSpec((1, w, kvd), lambda b: (b, 0, 0))],
        out_shape=[jax.ShapeDtypeStruct((ns, db, 1, hd), F32),
                   jax.ShapeDtypeStruct((db, w, kvd), F32),
                   jax.ShapeDtypeStruct((db, w, kvd), F32)],
        compiler_params=_cparams(("arbitrary",)),
    )(y4, y4, y4, k_cache, v_cache, q_norm, k_norm, cs, sn, sinks)
    return o.reshape(ns, db, hd), ko, vo


def _online_update(s, vals, m_ref, l_ref, acc_ref):
    m_old = m_ref[...]
    m_new = jnp.maximum(m_old, jnp.max(s, axis=-1, keepdims=True))
    alpha = jnp.exp(m_old - m_new)
    p = jnp.exp(s - m_new)
    l_ref[...] = alpha * l_ref[...] + jnp.sum(p, axis=-1, keepdims=True)
    acc_ref[...] = alpha * acc_ref[...] + _dot(p.astype(BF16), vals)
    m_ref[...] = m_new


def _mla_queries(q, nn, nr, cs, sn, wuk_ref, ql_s, qr_s):
    nq = q.shape[0]
    scale = (MLA_NOPE + MLA_ROPE) ** -0.5
    off = N_HEADS * MLA_NOPE
    for h in range(N_HEADS):
        qn = _rms(q[:, h * MLA_NOPE:(h + 1) * MLA_NOPE], nn)
        ql_s[h * nq:(h + 1) * nq, :] = (_dot(qn.astype(BF16), wuk_ref[h]) * scale).astype(BF16)
        qr = _rms(q[:, off + h * MLA_ROPE:off + (h + 1) * MLA_ROPE], nr)
        qr_s[h * nq:(h + 1) * nq, :] = (_rope(qr, cs, sn, MLA_ROPE) * scale).astype(BF16)


def _mla_output(acc_s, l_s, wuv_ref, nq):
    o_lat = (acc_s[...] / l_s[...]).astype(BF16)
    return jnp.concatenate([_dot(o_lat[h * nq:(h + 1) * nq], wuv_ref[h]) for h in range(N_HEADS)], axis=1)


def _mla_attn_kernel(q_ref, c_ref, kr_ref, cs_ref, sn_ref, nn_ref, nr_ref, wuk_ref, wuv_ref, o_ref,
                     ql_s, qr_s, m_s, l_s, acc_s, *, tq, tk):
    i = pl.program_id(1)
    _mla_queries(q_ref[0], nn_ref[...], nr_ref[...], cs_ref[...], sn_ref[...], wuk_ref, ql_s, qr_s)
    m_s[...] = jnp.full(m_s.shape, NEG, F32)
    l_s[...] = jnp.zeros(l_s.shape, F32)
    acc_s[...] = jnp.zeros(acc_s.shape, F32)
    t_pos = i * tq + lax.broadcasted_iota(jnp.int32, (tq, tk), 0)
    k_off = lax.broadcasted_iota(jnp.int32, (tq, tk), 1)

    def body(j, carry):
        start = pl.multiple_of(j * tk, tk)
        cb = c_ref[0, pl.ds(start, tk), :].astype(BF16)
        kb = kr_ref[0, pl.ds(start, tk), :].astype(BF16)
        s = _dot_nt(ql_s[...], cb) + _dot_nt(qr_s[...], kb)
        mask = jnp.concatenate([(start + k_off) <= t_pos] * N_HEADS, axis=0)
        _online_update(jnp.where(mask, s, NEG), cb, m_s, l_s, acc_s)
        return carry

    lax.fori_loop(0, (i * tq + tq - 1) // tk + 1, body, 0)
    o_ref[0] = _mla_output(acc_s, l_s, wuv_ref, tq).astype(o_ref.dtype)


def _mla_attn(q, c, kr, cs, sn, nn, nr, wuk, wuv, tq, tk):
    b, s, nqc = q.shape
    hd = N_HEADS * HEAD_DIM
    full = lambda a: pl.BlockSpec(a.shape, lambda bb, i: (0,) * a.ndim)
    tab = pl.BlockSpec((tq, MLA_ROPE), lambda bb, i: (i, 0))
    return pl.pallas_call(
        functools.partial(_mla_attn_kernel, tq=tq, tk=tk), grid=(b, s // tq),
        in_specs=[pl.BlockSpec((1, tq, nqc), lambda bb, i: (bb, i, 0)),
                  pl.BlockSpec((1, s, MLA_KV_LORA), lambda bb, i: (bb, 0, 0)),
                  pl.BlockSpec((1, s, MLA_ROPE), lambda bb, i: (bb, 0, 0)),
                  tab, tab, full(nn), full(nr), full(wuk), full(wuv)],
        out_specs=pl.BlockSpec((1, tq, hd), lambda bb, i: (bb, i, 0)),
        out_shape=jax.ShapeDtypeStruct((b, s, hd), BF16),
        scratch_shapes=[pltpu.VMEM((N_HEADS * tq, MLA_KV_LORA), BF16),
                        pltpu.VMEM((N_HEADS * tq, MLA_ROPE), BF16),
                        pltpu.VMEM((N_HEADS * tq, 1), F32),
                        pltpu.VMEM((N_HEADS * tq, 1), F32),
                        pltpu.VMEM((N_HEADS * tq, MLA_KV_LORA), F32)],
        compiler_params=_cparams(("arbitrary", "arbitrary")),
    )(q, c, kr, cs, sn, nn, nr, wuk, wuv)


def _mla_decode_kernel(pt_ref, *refs, ns, slots):
    del pt_ref
    lat = refs[:slots]
    krp = refs[slots:2 * slots]
    (q_ref, cn_ref, kn_ref, cs_ref, sn_ref, nn_ref, nr_ref, wuk_ref, wuv_ref, o_ref,
     ql_s, qr_s, m_s, l_s, acc_s) = refs[2 * slots:]
    g = pl.program_id(1)

    @pl.when(g == 0)
    def _():
        _mla_queries(_rows(q_ref, ns), nn_ref[...], nr_ref[...], cs_ref[...], sn_ref[...], wuk_ref, ql_s, qr_s)
        m_s[...] = jnp.full(m_s.shape, NEG, F32)
        l_s[...] = jnp.zeros(l_s.shape, F32)
        acc_s[...] = jnp.zeros(acc_s.shape, F32)

    cb = jnp.concatenate([r[0, 0] for r in lat], axis=0).astype(BF16)
    kb = jnp.concatenate([r[0, 0] for r in krp], axis=0).astype(BF16)
    _online_update(_dot_nt(ql_s[...], cb) + _dot_nt(qr_s[...], kb), cb, m_s, l_s, acc_s)

    @pl.when(g == pl.num_programs(1) - 1)
    def _():
        pad = PAGE_SIZE - ns
        cb = jnp.concatenate([_rows(cn_ref, ns), jnp.zeros((pad, MLA_KV_LORA), F32)], axis=0).astype(BF16)
        kb = jnp.concatenate([_rows(kn_ref, ns), jnp.zeros((pad, MLA_ROPE), F32)], axis=0).astype(BF16)
        s = _dot_nt(ql_s[...], cb) + _dot_nt(qr_s[...], kb)
        row = lax.broadcasted_iota(jnp.int32, s.shape, 0) % ns
        col = lax.broadcasted_iota(jnp.int32, s.shape, 1)
        _online_update(jnp.where(col <= row, s, NEG), cb, m_s, l_s, acc_s)
        o = _mla_output(acc_s, l_s, wuv_ref, ns)
        for t in range(ns):
            o_ref[t, 0] = o[t:t + 1]


def _mla_decode(q, c, kr, lat_pool, kr_pool, li, page_table, cs, sn, nn, nr, wuk, wuv):
    ns, db, nqc = q.shape
    hd = N_HEADS * HEAD_DIM
    slots = PAGE_SLOTS
    npg = page_table.shape[1] // slots
    full = lambda a: pl.BlockSpec(a.shape, lambda b, g, pt: (0,) * a.ndim)

    def paged(width, k):
        return pl.BlockSpec((1, 1, PAGE_SIZE, width), lambda b, g, pt: (li, pt[b, g * slots + k], 0, 0))

    step = lambda width: pl.BlockSpec((ns, 1, 1, width), lambda b, g, pt: (0, b, 0, 0))
    grid_spec = pltpu.PrefetchScalarGridSpec(
        num_scalar_prefetch=1, grid=(db, npg),
        in_specs=([paged(MLA_KV_LORA, k) for k in range(slots)] + [paged(MLA_ROPE, k) for k in range(slots)]
                  + [step(nqc), step(MLA_KV_LORA), step(MLA_ROPE),
                     full(cs), full(sn), full(nn), full(nr), full(wuk), full(wuv)]),
        out_specs=step(hd),
        scratch_shapes=[pltpu.VMEM((N_HEADS * ns, MLA_KV_LORA), BF16),
                        pltpu.VMEM((N_HEADS * ns, MLA_ROPE), BF16),
                        pltpu.VMEM((N_HEADS * ns, 1), F32),
                        pltpu.VMEM((N_HEADS * ns, 1), F32),
                        pltpu.VMEM((N_HEADS * ns, MLA_KV_LORA), F32)])
    o = pl.pallas_call(
        functools.partial(_mla_decode_kernel, ns=ns, slots=slots), grid_spec=grid_spec,
        out_shape=jax.ShapeDtypeStruct((ns, db, 1, hd), F32),
        compiler_params=_cparams(("arbitrary", "arbitrary")),
    )(page_table, *([lat_pool] * slots), *([kr_pool] * slots),
      q.reshape(ns, db, 1, nqc), c.reshape(ns, db, 1, MLA_KV_LORA), kr.reshape(ns, db, 1, MLA_ROPE),
      cs, sn, nn, nr, wuk, wuv)
    return o.reshape(ns, db, hd)


NSA_KVD = NSA_KV_HEADS * HEAD_DIM
NSA_GRP = N_HEADS // NSA_KV_HEADS
NSA_PAIR = NSA_SEL_BLOCK // NSA_CMP_BLOCK


def _nsa_prep_kernel(kc_ref, vc_ref, ks_ref, kw_ref, wk_ref, wv_ref, kcn_ref, ksn_ref, kwn_ref, cs_ref, sn_ref,
                     kso_ref, kwo_ref, kcmp_ref, vcmp_ref):
    cs, sn = cs_ref[...], sn_ref[...]
    kc, vc, ks, kw = kc_ref[0], vc_ref[0], ks_ref[0], kw_ref[0]
    kso, kwo, kcmp, vcmp = [], [], [], []
    for kv in range(NSA_KV_HEADS):
        kso.append(_rope(_rms(_head(ks, kv), ksn_ref[...]), cs, sn, ROPE_DIM))
        kwo.append(_rope(_rms(_head(kw, kv), kwn_ref[...]), cs, sn, ROPE_DIM))
        kcmp.append(_rms(_dot(wk_ref[kv], _head(kc, kv).astype(BF16)), kcn_ref[...]))
        vcmp.append(_dot(wv_ref[kv], _head(vc, kv).astype(BF16)))
    kso_ref[0] = jnp.concatenate(kso, axis=1)
    kwo_ref[0] = jnp.concatenate(kwo, axis=1)
    kcmp_ref[0] = jnp.concatenate(kcmp, axis=1)
    vcmp_ref[0] = jnp.concatenate(vcmp, axis=1)


def _cmp_matrix(w, nc):
    eye = jnp.eye(nc, dtype=F32)
    return (eye[None, :, :, None] * w.T[:, None, None, :]).reshape(w.shape[1], nc, nc * NSA_CMP_BLOCK).astype(BF16)


def _nsa_prep(y, cmp_wk, cmp_wv, k_norm, cs, sn):
    b, s, _ = y.shape
    nc = s // NSA_CMP_BLOCK
    base = N_HEADS * HEAD_DIM // NSA_KVD
    col = lambda j: pl.BlockSpec((1, s, NSA_KVD), lambda bb: (bb, 0, base + j))
    full = lambda a: pl.BlockSpec(a.shape, lambda bb: (0,) * a.ndim)
    wk, wv = _cmp_matrix(cmp_wk, nc), _cmp_matrix(cmp_wv, nc)
    kn = [k_norm[j:j + 1] for j in range(3)]
    seq = pl.BlockSpec((1, s, NSA_KVD), lambda bb: (bb, 0, 0))
    blk = pl.BlockSpec((1, nc, NSA_KVD), lambda bb: (bb, 0, 0))
    return pl.pallas_call(
        _nsa_prep_kernel, grid=(b,),
        in_specs=[col(0), col(1), col(2), col(4), full(wk), full(wv), full(kn[0]), full(kn[1]), full(kn[2]),
                  full(cs), full(sn)],
        out_specs=[seq, seq, blk, blk],
        out_shape=[jax.ShapeDtypeStruct((b, s, NSA_KVD), F32), jax.ShapeDtypeStruct((b, s, NSA_KVD), F32),
                   jax.ShapeDtypeStruct((b, nc, NSA_KVD), F32), jax.ShapeDtypeStruct((b, nc, NSA_KVD), F32)],
        compiler_params=_cparams(("arbitrary",)),
    )(y, y, y, y, wk, wv, kn[0], kn[1], kn[2], cs, sn)


def _cmp_attend(qs, kcmp, vcmp, mask):
    ng, nq = len(qs), qs[0].shape[0]
    q = jnp.concatenate(qs, axis=0).astype(BF16)
    mk = jnp.concatenate([mask] * ng, axis=0)
    s = jnp.where(mk, _dot_nt(q, kcmp.astype(BF16)), NEG)
    m = jnp.max(s, axis=-1, keepdims=True)
    p = jnp.where(mk, jnp.exp(s - m), 0.0)
    p = p / jnp.maximum(jnp.sum(p, axis=-1, keepdims=True), TINY)
    o = _dot(p.astype(BF16), vcmp.astype(BF16))
    imp = p[0:nq]
    for g in range(1, ng):
        imp = imp + p[g * nq:(g + 1) * nq]
    return [o[g * nq:(g + 1) * nq] for g in range(ng)], imp


def _pair_sum(imp):
    n = imp.shape[1]
    lane = lax.broadcasted_iota(jnp.int32, imp.shape, 1)
    return imp + jnp.where(lane % 2 == 0, pltpu.roll(imp, n - 1, 1), pltpu.roll(imp, 1, 1))


def _select(impx, nblk, seg, n_sel):
    lane = lax.broadcasted_iota(jnp.int32, impx.shape, 1)
    blk = (lane % seg) // NSA_PAIR
    nseg = impx.shape[1] // seg
    cnt = jnp.zeros(impx.shape, jnp.int32)
    for j in range(nblk):
        col = impx[:, j * NSA_PAIR:j * NSA_PAIR + 1]
        for sg in range(1, nseg):
            c = sg * seg + j * NSA_PAIR
            col = jnp.where(lane < sg * seg, col, impx[:, c:c + 1])
        beats = (col > impx) | ((col == impx) & (j < blk))
        cnt = cnt + beats.astype(jnp.int32)
    return (cnt < n_sel).astype(F32)


def _expand_mask(sel, start, tk):
    nc = sel.shape[1]
    n = lax.broadcasted_iota(jnp.int32, (nc, tk), 0)
    k = lax.broadcasted_iota(jnp.int32, (nc, tk), 1)
    e = jnp.where((start + k) // NSA_CMP_BLOCK == n, 1.0, 0.0).astype(BF16)
    return _dot(sel.astype(BF16), e) > 0.5


def _nsa_attn_kernel(q_ref, gl_ref, ks_ref, vs_ref, kwp_ref, kwc_ref, vwp_ref, vwc_ref, kcmp_ref, vcmp_ref,
                     qn_ref, cs_ref, sn_ref, o_ref, m_s, l_s, acc_s, *, tk, n_sel):
    i = pl.program_id(1)
    tq = q_ref.shape[1]
    nc = kcmp_ref.shape[1]
    scale = HEAD_DIM ** -0.5
    cs, sn = cs_ref[...], sn_ref[...]
    q = q_ref[0]
    qn = [_rms(_head(q, h), qn_ref[...]) * scale for h in range(N_HEADS)]
    qr = [_rope(x, cs, sn, ROPE_DIM) for x in qn]
    grp = lambda xs, kv: xs[kv * NSA_GRP:(kv + 1) * NSA_GRP]

    t_c = i * tq + lax.broadcasted_iota(jnp.int32, (tq, nc), 0)
    n_c = lax.broadcasted_iota(jnp.int32, (tq, nc), 1)
    cmask = (n_c + 1) * NSA_CMP_BLOCK - 1 <= t_c
    o_cmp, imps = [], []
    for kv in range(NSA_KV_HEADS):
        oc, imp = _cmp_attend(grp(qn, kv), _head(kcmp_ref[0], kv), _head(vcmp_ref[0], kv), cmask)
        o_cmp += oc
        imps.append(imp)
    imp = _pair_sum(jnp.concatenate(imps, axis=1))
    t_i = i * tq + lax.broadcasted_iota(jnp.int32, imp.shape, 0)
    blk = (lax.broadcasted_iota(jnp.int32, imp.shape, 1) % nc) // NSA_PAIR
    impx = jnp.where(blk == t_i // NSA_SEL_BLOCK, jnp.inf, jnp.where(blk * NSA_SEL_BLOCK <= t_i, imp, -jnp.inf))
    sel = _select(impx, nc // NSA_PAIR, nc, n_sel)

    m_s[...] = jnp.full(m_s.shape, NEG, F32)
    l_s[...] = jnp.zeros(l_s.shape, F32)
    acc_s[...] = jnp.zeros(acc_s.shape, F32)
    qsel = [jnp.concatenate(grp(qr, kv), axis=0).astype(BF16) for kv in range(NSA_KV_HEADS)]
    selk = [sel[:, kv * nc:(kv + 1) * nc] for kv in range(NSA_KV_HEADS)]
    t_k = i * tq + lax.broadcasted_iota(jnp.int32, (tq, tk), 0)
    k_off = lax.broadcasted_iota(jnp.int32, (tq, tk), 1)

    def body(j, carry):
        start = pl.multiple_of(j * tk, tk)
        kb = ks_ref[0, pl.ds(start, tk), :].astype(BF16)
        vb = vs_ref[0, pl.ds(start, tk), :].astype(BF16)
        causal = (start + k_off) <= t_k
        for kv in range(NSA_KV_HEADS):
            mk = _expand_mask(selk[kv], start, tk) & causal
            s = jnp.where(jnp.concatenate([mk] * NSA_GRP, axis=0), _dot_nt(qsel[kv], _head(kb, kv)), NEG)
            _online_update(s, _head(vb, kv), m_s.at[kv], l_s.at[kv], acc_s.at[kv])
        return carry

    lax.fori_loop(0, (i * tq + tq - 1) // tk + 1, body, 0)

    wmask = _band_mask(tq, tq, jnp.where(i > 0, 0, tq))
    gates = jax.nn.sigmoid(gl_ref[0])
    outs = []
    for kv in range(NSA_KV_HEADS):
        kk = jnp.concatenate([_head(kwp_ref[0], kv), _head(kwc_ref[0], kv)], axis=0).astype(BF16)
        vv = jnp.concatenate([_head(vwp_ref[0], kv), _head(vwc_ref[0], kv)], axis=0).astype(BF16)
        o_win = _attend_group(grp(qr, kv), kk, vv, wmask, None)
        o_sel = acc_s[kv] / l_s[kv]
        for g in range(NSA_GRP):
            h = kv * NSA_GRP + g
            outs.append(gates[:, 3 * h:3 * h + 1] * o_cmp[h]
                        + gates[:, 3 * h + 1:3 * h + 2] * o_sel[g * tq:(g + 1) * tq]
                        + gates[:, 3 * h + 2:3 * h + 3] * o_win[g])
    o_ref[0] = jnp.concatenate(outs, axis=1).astype(o_ref.dtype)


def _nsa_attn(y, ksn, kwn, kcmp, vcmp, q_norm, cs, sn, tk):
    b, s, _ = y.shape
    tq = WINDOW
    hd = N_HEADS * HEAD_DIM
    nc = kcmp.shape[1]
    base = hd // NSA_KVD
    n_sel = min(NSA_TOPK, s // NSA_SEL_BLOCK)
    prev = lambda i: jnp.maximum(i - 1, 0)
    full = lambda a: pl.BlockSpec(a.shape, lambda bb, i: (0,) * a.ndim)
    tab = pl.BlockSpec((tq, HEAD_DIM), lambda bb, i: (i, 0))
    return pl.pallas_call(
        functools.partial(_nsa_attn_kernel, tk=tk, n_sel=n_sel), grid=(b, s // tq),
        in_specs=[pl.BlockSpec((1, tq, hd), lambda bb, i: (bb, i, 0)),
                  pl.BlockSpec((1, tq, NSA_KVD), lambda bb, i: (bb, i, base + 6)),
                  pl.BlockSpec((1, s, NSA_KVD), lambda bb, i: (bb, 0, 0)),
                  pl.BlockSpec((1, s, NSA_KVD), lambda bb, i: (bb, 0, base + 3)),
                  pl.BlockSpec((1, tq, NSA_KVD), lambda bb, i: (bb, prev(i), 0)),
                  pl.BlockSpec((1, tq, NSA_KVD), lambda bb, i: (bb, i, 0)),
                  pl.BlockSpec((1, tq, NSA_KVD), lambda bb, i: (bb, prev(i), base + 5)),
                  pl.BlockSpec((1, tq, NSA_KVD), lambda bb, i: (bb, i, base + 5)),
                  pl.BlockSpec((1, nc, NSA_KVD), lambda bb, i: (bb, 0, 0)),
                  pl.BlockSpec((1, nc, NSA_KVD), lambda bb, i: (bb, 0, 0)),
                  full(q_norm), tab, tab],
        out_specs=pl.BlockSpec((1, tq, hd), lambda bb, i: (bb, i, 0)),
        out_shape=jax.ShapeDtypeStruct((b, s, hd), BF16),
        scratch_shapes=[pltpu.VMEM((NSA_KV_HEADS, NSA_GRP * tq, 1), F32),
                        pltpu.VMEM((NSA_KV_HEADS, NSA_GRP * tq, 1), F32),
                        pltpu.VMEM((NSA_KV_HEADS, NSA_GRP * tq, HEAD_DIM), F32)],
        compiler_params=_cparams(("arbitrary", "arbitrary")),
    )(y, y, ksn, y, kwn, kwn, y, y, kcmp, vcmp, q_norm, cs, sn)


def _nsa_decode_kernel(pt_ref, *refs, ns, slots, npg, past, n_sel):
    del pt_ref
    kcp, vcp = refs[:slots], refs[slots:2 * slots]
    ksp, vsp = refs[2 * slots:3 * slots], refs[3 * slots:4 * slots]
    (q_ref, gl_ref, ks_ref, vs_ref, kw_ref, vw_ref, kwb_ref, vwb_ref, wk_ref, wv_ref,
     qn_ref, kcn_ref, ksn_ref, kwn_ref, cs_ref, sn_ref,
     o_ref, kso_ref, kwo_ref, vwo_ref,
     kcmp_s, vcmp_s, sel_s, ocmp_s, q_s, m_s, l_s, acc_s) = refs[4 * slots:]
    step = pl.program_id(1)
    scale = HEAD_DIM ** -0.5
    nc = kcmp_s.shape[0]
    rows_per_step = slots * PAGE_SIZE
    cper = rows_per_step // NSA_CMP_BLOCK
    grp = lambda xs, kv: xs[kv * NSA_GRP:(kv + 1) * NSA_GRP]

    @pl.when(step < npg)
    def _():
        kc = jnp.concatenate([r[0, 0] for r in kcp], axis=0) * wk_ref[...]
        vc = jnp.concatenate([r[0, 0] for r in vcp], axis=0) * wv_ref[...]
        blocks = lambda x: jnp.concatenate(
            [jnp.sum(x[n * NSA_CMP_BLOCK:(n + 1) * NSA_CMP_BLOCK], axis=0, keepdims=True) for n in range(cper)],
            axis=0)
        row = pl.multiple_of(step * cper, cper)
        kcmp_s[pl.ds(row, cper), :] = blocks(kc)
        vcmp_s[pl.ds(row, cper), :] = blocks(vc)

    @pl.when(step == npg - 1)
    def _():
        q = _rows(q_ref, ns)
        qn = [_rms(_head(q, h), qn_ref[...]) * scale for h in range(N_HEADS)]
        qr = [_rope(x, cs_ref[...], sn_ref[...], ROPE_DIM) for x in qn]
        t_c = past + lax.broadcasted_iota(jnp.int32, (ns, nc), 0)
        n_c = lax.broadcasted_iota(jnp.int32, (ns, nc), 1)
        cmask = (n_c + 1) * NSA_CMP_BLOCK - 1 <= t_c
        imps = []
        for kv in range(NSA_KV_HEADS):
            kcmp = _rms(_head(kcmp_s[...], kv), kcn_ref[...])
            oc, imp = _cmp_attend(grp(qn, kv), kcmp, _head(vcmp_s[...], kv), cmask)
            ocmp_s[kv] = jnp.concatenate(oc, axis=0)
            imps.append(_pair_sum(imp))
            q_s[kv] = jnp.concatenate(grp(qr, kv), axis=0).astype(BF16)
        for kv in range(NSA_KV_HEADS):
            sel_s[kv] = _select(imps[kv], nc // NSA_PAIR, nc, n_sel - 1)
        m_s[...] = jnp.full(m_s.shape, NEG, F32)
        l_s[...] = jnp.zeros(l_s.shape, F32)
        acc_s[...] = jnp.zeros(acc_s.shape, F32)

    @pl.when(step >= npg)
    def _():
        kb = jnp.concatenate([r[0, 0] for r in ksp], axis=0).astype(BF16)
        vb = jnp.concatenate([r[0, 0] for r in vsp], axis=0).astype(BF16)
        start = (step - npg) * rows_per_step
        for kv in range(NSA_KV_HEADS):
            mk = _expand_mask(sel_s[kv], start, rows_per_step)
            s = jnp.where(jnp.concatenate([mk] * NSA_GRP, axis=0), _dot_nt(q_s[kv], _head(kb, kv)), NEG)
            _online_update(s, _head(vb, kv), m_s.at[kv], l_s.at[kv], acc_s.at[kv])

    @pl.when(step == 2 * npg - 1)
    def _():
        pad = PAGE_SIZE - ns
        vs_new = _rows(vs_ref, ns)
        ks_raw = _rows(ks_ref, ns)
        ksn = jnp.concatenate([_rope(_rms(_head(ks_raw, kv), ksn_ref[...]), cs_ref[...], sn_ref[...], ROPE_DIM)
                               for kv in range(NSA_KV_HEADS)], axis=1)
        kso_ref[0] = ksn
        kb = jnp.concatenate([ksn, jnp.zeros((pad, NSA_KVD), F32)], axis=0).astype(BF16)
        vb = jnp.concatenate([vs_new, jnp.zeros((pad, NSA_KVD), F32)], axis=0).astype(BF16)
        row = lax.broadcasted_iota(jnp.int32, (NSA_GRP * ns, PAGE_SIZE), 0) % ns
        col = lax.broadcasted_iota(jnp.int32, (NSA_GRP * ns, PAGE_SIZE), 1)
        for kv in range(NSA_KV_HEADS):
            s = jnp.where(col <= row, _dot_nt(q_s[kv], _head(kb, kv)), NEG)
            _online_update(s, _head(vb, kv), m_s.at[kv], l_s.at[kv], acc_s.at[kv])
        q = _rows(q_ref, ns)
        outs_w, kwn, kw_out, vw_out = _window_step(
            q, _rows(kw_ref, ns), _rows(vw_ref, ns), kwb_ref[0, 0], vwb_ref[0, 0], qn_ref[...], kwn_ref[...],
            cs_ref[...], sn_ref[...], None, NSA_KV_HEADS)
        del kwn
        kwo_ref[0] = kw_out
        vwo_ref[0] = vw_out
        gates = jax.nn.sigmoid(_rows(gl_ref, ns))
        outs = []
        for kv in range(NSA_KV_HEADS):
            o_sel = acc_s[kv] / l_s[kv]
            o_cmp = ocmp_s[kv]
            for g in range(NSA_GRP):
                h = kv * NSA_GRP + g
                outs.append(gates[:, 3 * h:3 * h + 1] * o_cmp[g * ns:(g + 1) * ns]
                            + gates[:, 3 * h + 1:3 * h + 2] * o_sel[g * ns:(g + 1) * ns]
                            + gates[:, 3 * h + 2:3 * h + 3] * outs_w[h])
        o = jnp.concatenate(outs, axis=1)
        for t in range(ns):
            o_ref[t, 0] = o[t:t + 1]


def _nsa_decode(y, pools, kw_cache, vw_cache, li, page_table, cmp_wk, cmp_wv, q_norm, k_norm, cs, sn):
    ns, db, n = y.shape
    hd = N_HEADS * HEAD_DIM
    slots = PAGE_SLOTS
    n_pages = page_table.shape[1]
    npg = n_pages // slots
    past = n_pages * PAGE_SIZE
    nc = past // NSA_CMP_BLOCK
    base = hd // NSA_KVD
    w = kw_cache.shape[2]
    n_blocks = -(-(past + ns) // NSA_SEL_BLOCK)
    n_sel = min(NSA_TOPK, n_blocks)
    y4 = y.reshape(ns, db, 1, n)
    full = lambda a: pl.BlockSpec(a.shape, lambda b, g, pt: (0,) * a.ndim)

    def paged(phase, k):
        if phase == 0:
            grp_of = lambda g: jnp.minimum(g, npg - 1)
        else:
            grp_of = lambda g: jnp.maximum(g - npg, 0)
        return pl.BlockSpec((1, 1, PAGE_SIZE, NSA_KVD), lambda b, g, pt: (li, pt[b, grp_of(g) * slots + k], 0, 0))

    step = lambda width, cb: pl.BlockSpec((ns, 1, 1, width), lambda b, g, pt: (0, b, 0, cb))
    rows = slots * PAGE_SIZE
    expand = lambda wgt: jnp.tile(jnp.repeat(wgt, HEAD_DIM, axis=1), (rows // NSA_CMP_BLOCK, 1))
    kn = [k_norm[j:j + 1] for j in range(3)]
    seq_out = pl.BlockSpec((1, ns, NSA_KVD), lambda b, g, pt: (b, 0, 0))
    buf_out = pl.BlockSpec((1, w, NSA_KVD), lambda b, g, pt: (b, 0, 0))
    buf_in = pl.BlockSpec((1, 1, w, NSA_KVD), lambda b, g, pt: (li, b, 0, 0))
    grid_spec = pltpu.PrefetchScalarGridSpec(
        num_scalar_prefetch=1, grid=(db, 2 * npg),
        in_specs=([paged(0, k) for k in range(slots)] + [paged(0, k) for k in range(slots)]
                  + [paged(1, k) for k in range(slots)] + [paged(1, k) for k in range(slots)]
                  + [step(hd, 0), step(NSA_KVD, base + 6), step(NSA_KVD, base + 2), step(NSA_KVD, base + 3),
                     step(NSA_KVD, base + 4), step(NSA_KVD, base + 5), buf_in, buf_in,
                     pl.BlockSpec((rows, NSA_KVD), lambda b, g, pt: (0, 0)),
                     pl.BlockSpec((rows, NSA_KVD), lambda b, g, pt: (0, 0)),
                     full(q_norm), full(kn[0]), full(kn[1]), full(kn[2]), full(cs), full(sn)]),
        out_specs=[step(hd, 0), seq_out, buf_out, buf_out],
        scratch_shapes=[pltpu.VMEM((nc, NSA_KVD), F32), pltpu.VMEM((nc, NSA_KVD), F32),
                        pltpu.VMEM((NSA_KV_HEADS, ns, nc), F32),
                        pltpu.VMEM((NSA_KV_HEADS, NSA_GRP * ns, HEAD_DIM), F32),
                        pltpu.VMEM((NSA_KV_HEADS, NSA_GRP * ns, HEAD_DIM), BF16),
                        pltpu.VMEM((NSA_KV_HEADS, NSA_GRP * ns, 1), F32),
                        pltpu.VMEM((NSA_KV_HEADS, NSA_GRP * ns, 1), F32),
                        pltpu.VMEM((NSA_KV_HEADS, NSA_GRP * ns, HEAD_DIM), F32)])
    kc_pool, vc_pool, ks_pool, vs_pool = pools
    o, kso, kwo, vwo = pl.pallas_call(
        functools.partial(_nsa_decode_kernel, ns=ns, slots=slots, npg=npg, past=past, n_sel=n_sel),
        grid_spec=grid_spec,
        out_shape=[jax.ShapeDtypeStruct((ns, db, 1, hd), F32),
                   jax.ShapeDtypeStruct((db, ns, NSA_KVD), F32),
                   jax.ShapeDtypeStruct((db, w, NSA_KVD), F32),
                   jax.ShapeDtypeStruct((db, w, NSA_KVD), F32)],
        compiler_params=_cparams(("arbitrary", "arbitrary")),
    )(page_table, *([kc_pool] * slots), *([vc_pool] * slots), *([ks_pool] * slots), *([vs_pool] * slots),
      y4, y4, y4, y4, y4, y4, kw_cache, vw_cache, expand(cmp_wk), expand(cmp_wv),
      q_norm, kn[0], kn[1], kn[2], cs, sn)
    return o.reshape(ns, db, hd), kso, kwo, vwo


def kernel(x_prompt, x_sample, cache_swa_k, cache_swa_v, cache_mla_latent, cache_mla_krope, cache_nsa_kcmp, cache_nsa_vcmp, cache_nsa_ksel, cache_nsa_vsel, cache_nsa_kwin, cache_nsa_vwin, state_conv_ffn, page_table, c_prompt, c_sample, ada_w, ada_b, norm_mix, norm_ffn, ffn_w_up, ffn_conv, ffn_w_down, a_w_in, a_q_norm, a_k_norm, a_sinks, a_w_out, b_w_in, b_qa_norm, b_w_qb, b_q_norm_nope, b_q_norm_rope, b_kv_norm, b_krope_norm, b_w_uk, b_w_uv, b_w_out, c_w_in, c_q_norm, c_k_norm, c_cmp_wk, c_cmp_wv, c_w_out):
    nb, seq, d = x_prompt.shape
    db, ds, _ = x_sample.shape
    depth = ada_w.shape[0]
    ff = ffn_w_down.shape[1]
    past = page_table.shape[1] * PAGE_SIZE
    hd = N_HEADS * HEAD_DIM

    mod = _modulate(jnp.concatenate([c_prompt, c_sample], axis=0), ada_w, ada_b)
    tr_p = math.gcd(seq, 512)
    gp = _Group(True, mod[:, :, :nb].reshape(depth, 6, nb, 1, d), tr_p)
    gs = _Group(False, mod[:, :, nb:], db)
    norm_mix3 = norm_mix.reshape(depth, 1, d)
    norm_ffn3 = norm_ffn.reshape(depth, 1, d)

    pos_p = jnp.arange(seq, dtype=jnp.int32)
    pos_s = past + jnp.arange(ds, dtype=jnp.int32)
    cw_p, sw_p = _rope_tables(pos_p, ROPE_DIM, ROPE_THETA, HEAD_DIM)
    cw_s, sw_s = _rope_tables(pos_s, ROPE_DIM, ROPE_THETA, HEAD_DIM)
    cm_p, sm_p = _rope_tables(pos_p, MLA_ROPE, MLA_THETA, MLA_ROPE)
    cm_s, sm_s = _rope_tables(pos_s, MLA_ROPE, MLA_THETA, MLA_ROPE)

    xp = x_prompt
    xs = x_sample.transpose(1, 0, 2)
    row2 = lambda v: v.reshape(1, -1)
    tm = lambda a: a.transpose(1, 0, 2)
    out = {k: [] for k in ("swa_k_p", "swa_v_p", "swa_k_s", "swa_v_s", "mla_c_p", "mla_r_p", "mla_c_s", "mla_r_s",
                           "conv_p", "conv_s")}
    nsa_p = [[] for _ in range(6)]
    nsa_s = [[] for _ in range(6)]
    ia = ib = ic = 0
    for layer in range(depth):
        kind = layer % N_MIXERS
        if kind == 0:
            w_in = a_w_in[ia].astype(BF16)
            qn, kn, sinks = row2(a_q_norm[ia]), row2(a_k_norm[ia]), row2(a_sinks[ia])
            yp = _proj(xp, gp, norm_mix3, layer, w_in)
            op, kp, vp = _band_attn(yp, qn, kn, sinks, cw_p, sw_p)
            ys = _proj(xs, gs, norm_mix3, layer, w_in)
            kvd = SWA_KV_HEADS * HEAD_DIM
            os_, ks_, vs_ = _step_attn(ys, cache_swa_k.reshape(cache_swa_k.shape[:3] + (kvd,)),
                                       cache_swa_v.reshape(cache_swa_v.shape[:3] + (kvd,)), ia,
                                       qn, kn, sinks, cw_s, sw_s)
            kv4 = lambda a: a.reshape(a.shape[0], a.shape[1], SWA_KV_HEADS, HEAD_DIM)
            out["swa_k_p"].append(kv4(kp)); out["swa_v_p"].append(kv4(vp))
            out["swa_k_s"].append(kv4(ks_)); out["swa_v_s"].append(kv4(vs_))
            w_out = a_w_out[ia].astype(BF16)
            ia += 1
        elif kind == 1:
            w_in = b_w_in[ib].astype(BF16)
            wqb = b_w_qb[ib].reshape(MLA_Q_LORA, N_HEADS, MLA_NOPE + MLA_ROPE)
            wqb = jnp.concatenate([wqb[:, :, :MLA_NOPE].reshape(MLA_Q_LORA, -1),
                                   wqb[:, :, MLA_NOPE:].reshape(MLA_Q_LORA, -1)], axis=1).astype(BF16)
            wuk = b_w_uk[ib].transpose(1, 2, 0).astype(BF16)
            wuv = b_w_uv[ib].transpose(1, 0, 2).astype(BF16)
            norms = (row2(b_qa_norm[ib]), wqb, row2(b_kv_norm[ib]), row2(b_krope_norm[ib]))
            nn, nr = row2(b_q_norm_nope[ib]), row2(b_q_norm_rope[ib])
            qp, cp, rp = _mla_proj(xp, gp, norm_mix3, layer, w_in, *norms, cm_p[None], sm_p[None])
            op = _mla_attn(qp, cp, rp, cm_p, sm_p, nn, nr, wuk, wuv, math.gcd(seq, 128), math.gcd(seq, 256))
            qs, cs_, rs_ = _mla_proj(xs, gs, norm_mix3, layer, w_in, *norms, cm_s[:, None], sm_s[:, None])
            os_ = _mla_decode(qs, cs_, rs_, cache_mla_latent, cache_mla_krope, ib, page_table,
                              cm_s, sm_s, nn, nr, wuk, wuv)
            out["mla_c_p"].append(cp); out["mla_r_p"].append(rp)
            out["mla_c_s"].append(tm(cs_)); out["mla_r_s"].append(tm(rs_))
            w_out = b_w_out[ib].astype(BF16)
            ib += 1
        else:
            n_in = c_w_in.shape[2]
            n_pad = -(-n_in // 128) * 128
            w_in = jnp.pad(c_w_in[ic], ((0, 0), (0, n_pad - n_in))).astype(BF16)
            qn, kn3 = row2(c_q_norm[ic]), c_k_norm[ic]
            kv4 = lambda a: a.reshape(a.shape[0], a.shape[1], NSA_KV_HEADS, HEAD_DIM)
            col = lambda a, j: a[:, :, hd + j * NSA_KVD:hd + (j + 1) * NSA_KVD]
            yp = _proj(xp, gp, norm_mix3, layer, w_in)
            ksn, kwn, kcmp, vcmp = _nsa_prep(yp, c_cmp_wk[ic], c_cmp_wv[ic], kn3, cw_p, sw_p)
            op = _nsa_attn(yp, ksn, kwn, kcmp, vcmp, qn, cw_p, sw_p, math.gcd(seq, 256))
            for j, a in enumerate((col(yp, 0), col(yp, 1), ksn, col(yp, 3), kwn[:, -WINDOW:],
                                   col(yp, 5)[:, -WINDOW:])):
                nsa_p[j].append(kv4(a))
            ys = _proj(xs, gs, norm_mix3, layer, w_in)
            pools = tuple(p.reshape(p.shape[:3] + (NSA_KVD,))
                          for p in (cache_nsa_kcmp, cache_nsa_vcmp, cache_nsa_ksel, cache_nsa_vsel))
            os_, kso, kwo, vwo = _nsa_decode(
                ys, pools, cache_nsa_kwin.reshape(cache_nsa_kwin.shape[:3] + (NSA_KVD,)),
                cache_nsa_vwin.reshape(cache_nsa_vwin.shape[:3] + (NSA_KVD,)), ic, page_table,
                c_cmp_wk[ic], c_cmp_wv[ic], qn, kn3, cw_s, sw_s)
            for j, a in enumerate((tm(col(ys, 0)), tm(col(ys, 1)), kso, tm(col(ys, 3)), kwo, vwo)):
                nsa_s[j].append(kv4(a))
            w_out = c_w_out[ic].astype(BF16)
            ic += 1
        xp = _outproj(op, xp, gp, layer, w_out)
        xs = _outproj(os_, xs, gs, layer, w_out)
        wup, wd = ffn_w_up[layer].astype(BF16), ffn_w_down[layer].astype(BF16)
        tf = math.gcd(ff, 256)
        xp, bp = _ffn_prompt(xp, gp, norm_ffn3, layer, wup, ffn_conv[layer], wd, math.gcd(seq, 1024), tf)
        xs, bs = _ffn_sample(xs, gs, norm_ffn3, layer, wup, ffn_conv[layer], wd, state_conv_ffn[layer], tf)
        out["conv_p"].append(bp); out["conv_s"].append(bs)

    st = lambda xs_: jnp.stack(xs_)
    return (xp, xs.transpose(1, 0, 2),
            st(out["swa_k_p"]), st(out["swa_v_p"]), st(out["mla_c_p"]), st(out["mla_r_p"]),
            *[st(a) for a in nsa_p], st(out["conv_p"]),
            st(out["swa_k_s"]), st(out["swa_v_s"]), st(out["mla_c_s"]), st(out["mla_r_s"]),
            *[st(a) for a in nsa_s], st(out["conv_s"]))
```

```python
import functools
import math

import jax
import jax.numpy as jnp
from jax import lax
from jax.experimental import pallas as pl
from jax.experimental.pallas import tpu as pltpu

F32 = jnp.float32
BF16 = jnp.bfloat16

N_HEADS = 16
HEAD_DIM = 64
ROPE_DIM = 16
ROPE_THETA = 500000.0
EPS = 1e-6
PAGE_SIZE = 128
SWA_KV_HEADS = 4
WINDOW = 128
MLA_Q_LORA = 384
MLA_KV_LORA = 256
MLA_NOPE = 64
MLA_ROPE = 32
MLA_THETA = 10000.0
NSA_KV_HEADS = 2
NSA_CMP_BLOCK = 32
NSA_SEL_BLOCK = 64
NSA_TOPK = 16
N_MIXERS = 3
NEG = -1e30
TINY = float(jnp.finfo(jnp.float32).tiny)
VMEM_LIMIT = 56 * 1024 * 1024
PAGE_SLOTS = 8


def _cparams(sem):
    return pltpu.CompilerParams(dimension_semantics=sem, vmem_limit_bytes=VMEM_LIMIT)


def _dot(a, b):
    return jnp.dot(a, b, preferred_element_type=F32)


def _dot_nt(a, b):
    return lax.dot_general(a, b, (((1,), (1,)), ((), ())), preferred_element_type=F32)


def _rms(x, g):
    return x * lax.rsqrt(jnp.mean(x * x, axis=-1, keepdims=True) + EPS) * g


def _rope(x, c, s, rot):
    half = rot // 2
    parts = [x[:, half:rot], x[:, :half]]
    if x.shape[1] > rot:
        parts.append(x[:, rot:])
    return x * c + jnp.concatenate(parts, axis=1) * s


def _silu(x):
    return x * jax.nn.sigmoid(x)


def _norm_mod(x, g, sc, sh):
    return _rms(x, g) * (1.0 + sc) + sh


def _rope_tables(pos, rot, theta, width):
    half = rot // 2
    inv = jnp.power(jnp.float32(theta), -jnp.arange(half, dtype=F32) / half)
    ang = pos.astype(F32)[:, None] * inv[None, :]
    cos, sin = jnp.cos(ang), jnp.sin(ang)
    n = pos.shape[0]
    c = jnp.concatenate([cos, cos, jnp.ones((n, width - rot), F32)], axis=1)
    s = jnp.concatenate([-sin, sin, jnp.zeros((n, width - rot), F32)], axis=1)
    return c, s


class _Group:
    def __init__(self, per_group, mod, tr):
        self.pm = per_group
        self.mods = [mod[layer] for layer in range(mod.shape[0])]
        self.tr = tr

    def mod(self, layer):
        return self.mods[layer]

    def mod_spec(self, j, nargs=2):
        d = self.mods[0].shape[-1]
        if self.pm:
            shape, f = (1, 1, 1, d), (lambda g: (j, g, 0, 0))
        else:
            shape, f = (1, self.mods[0].shape[1], d), (lambda g: (j, 0, 0))
        if nargs == 2:
            return pl.BlockSpec(shape, lambda g, r: f(g))
        return pl.BlockSpec(shape, lambda g, r, k: f(g))


def _mod_val(ref, pm):
    return ref[0, 0] if pm else ref[0]


def _mod_kernel(c_ref, w_ref, b_ref, o_ref):
    a = _silu(c_ref[...])
    o_ref[0, 0] = _dot(a.astype(BF16), w_ref[0].astype(BF16)) + b_ref[0]


def _modulate(c_all, ada_w, ada_b):
    nl, d, d6 = ada_w.shape
    n = c_all.shape[0]
    nj = d6 // d
    return pl.pallas_call(
        _mod_kernel, grid=(nl, nj),
        in_specs=[pl.BlockSpec((n, d), lambda l, j: (0, 0)),
                  pl.BlockSpec((1, d, d), lambda l, j: (l, 0, j)),
                  pl.BlockSpec((1, 1, d), lambda l, j: (l, 0, j))],
        out_specs=pl.BlockSpec((1, 1, n, d), lambda l, j: (l, j, 0, 0)),
        out_shape=jax.ShapeDtypeStruct((nl, nj, n, d), F32),
        compiler_params=_cparams(("arbitrary", "arbitrary")),
    )(c_all, ada_w, ada_b.reshape(nl, 1, d6))


def _proj_kernel(x_ref, g_ref, sc_ref, sh_ref, w_ref, o_ref, *, pm):
    h = _norm_mod(x_ref[0], g_ref[0], _mod_val(sc_ref, pm), _mod_val(sh_ref, pm))
    o_ref[0] = _dot(h.astype(BF16), w_ref[...])


def _proj(x, grp, gains, layer, w_bf):
    ng, nr, d = x.shape
    n = w_bf.shape[1]
    tr = grp.tr
    return pl.pallas_call(
        functools.partial(_proj_kernel, pm=grp.pm), grid=(ng, nr // tr),
        in_specs=[pl.BlockSpec((1, tr, d), lambda g, r: (g, r, 0)),
                  pl.BlockSpec((1, 1, d), lambda g, r: (0, 0, 0)),
                  grp.mod_spec(1), grp.mod_spec(0),
                  pl.BlockSpec((d, n), lambda g, r: (0, 0))],
        out_specs=pl.BlockSpec((1, tr, n), lambda g, r: (g, r, 0)),
        out_shape=jax.ShapeDtypeStruct((ng, nr, n), F32),
        compiler_params=_cparams(("arbitrary", "arbitrary")),
    )(x, gains[layer:layer + 1], grp.mod(layer), grp.mod(layer), w_bf)


def _mla_proj_kernel(x_ref, g_ref, sc_ref, sh_ref, w_ref, qan_ref, wqb_ref, kvn_ref, krn_ref,
                     c_ref, s_ref, q_ref, lat_ref, kr_ref, *, pm):
    h = _norm_mod(x_ref[0], g_ref[0], _mod_val(sc_ref, pm), _mod_val(sh_ref, pm))
    y = _dot(h.astype(BF16), w_ref[...])
    a, b = MLA_Q_LORA, MLA_Q_LORA + MLA_KV_LORA
    qa = _rms(y[:, :a], qan_ref[...])
    q_ref[0] = _dot(qa.astype(BF16), wqb_ref[...])
    lat_ref[0] = _rms(y[:, a:b], kvn_ref[...])
    kr = _rms(y[:, b:], krn_ref[...])
    kr_ref[0] = _rope(kr, c_ref[0], s_ref[0], MLA_ROPE)


def _mla_proj(x, grp, gains, layer, w_bf, qa_norm, wqb_bf, kv_norm, krope_norm, cs, sn):
    ng, nr, d = x.shape
    tr = grp.tr
    nq = wqb_bf.shape[1]
    if grp.pm:
        tab = pl.BlockSpec((1, tr, MLA_ROPE), lambda g, r: (0, r, 0))
    else:
        tab = pl.BlockSpec((1, 1, MLA_ROPE), lambda g, r: (g, 0, 0))
    full = lambda a: pl.BlockSpec(a.shape, lambda g, r: (0,) * a.ndim)
    return pl.pallas_call(
        functools.partial(_mla_proj_kernel, pm=grp.pm), grid=(ng, nr // tr),
        in_specs=[pl.BlockSpec((1, tr, d), lambda g, r: (g, r, 0)),
                  pl.BlockSpec((1, 1, d), lambda g, r: (0, 0, 0)),
                  grp.mod_spec(1), grp.mod_spec(0),
                  full(w_bf), full(qa_norm), full(wqb_bf), full(kv_norm), full(krope_norm), tab, tab],
        out_specs=[pl.BlockSpec((1, tr, nq), lambda g, r: (g, r, 0)),
                   pl.BlockSpec((1, tr, MLA_KV_LORA), lambda g, r: (g, r, 0)),
                   pl.BlockSpec((1, tr, MLA_ROPE), lambda g, r: (g, r, 0))],
        out_shape=[jax.ShapeDtypeStruct((ng, nr, nq), F32),
                   jax.ShapeDtypeStruct((ng, nr, MLA_KV_LORA), F32),
                   jax.ShapeDtypeStruct((ng, nr, MLA_ROPE), F32)],
        compiler_params=_cparams(("arbitrary", "arbitrary")),
    )(x, gains[layer:layer + 1], grp.mod(layer), grp.mod(layer), w_bf, qa_norm, wqb_bf, kv_norm, krope_norm, cs, sn)


def _outproj_kernel(o_ref, x_ref, gt_ref, w_ref, y_ref, *, pm):
    y_ref[0] = x_ref[0] + _mod_val(gt_ref, pm) * _dot(o_ref[0].astype(BF16), w_ref[...])


def _outproj(o, x, grp, layer, w_bf):
    ng, nr, d = x.shape
    k = o.shape[-1]
    tr = grp.tr
    return pl.pallas_call(
        functools.partial(_outproj_kernel, pm=grp.pm), grid=(ng, nr // tr),
        in_specs=[pl.BlockSpec((1, tr, k), lambda g, r: (g, r, 0)),
                  pl.BlockSpec((1, tr, d), lambda g, r: (g, r, 0)),
                  grp.mod_spec(2),
                  pl.BlockSpec((k, d), lambda g, r: (0, 0))],
        out_specs=pl.BlockSpec((1, tr, d), lambda g, r: (g, r, 0)),
        out_shape=jax.ShapeDtypeStruct((ng, nr, d), F32),
        compiler_params=_cparams(("arbitrary", "arbitrary")),
    )(o, x, grp.mod(layer), w_bf)


HALO = 16


def _ffn_tail(mg, mv, wd_ref, x_ref, gt, y_ref):
    f = pl.program_id(2)
    d = _dot((_silu(mg) * mv).astype(BF16), wd_ref[...])

    @pl.when(f == 0)
    def _():
        y_ref[0] = d

    @pl.when(f > 0)
    def _():
        y_ref[0] += d

    @pl.when(f == pl.num_programs(2) - 1)
    def _():
        y_ref[0] = x_ref[0] + gt * y_ref[0]


def _ffn_prompt_kernel(x_ref, xh_ref, g_ref, sc_ref, sh_ref, gt_ref, wup_ref, cw_ref, wd_ref,
                       y_ref, b_ref, h_s, act_s, u_s, *, tr, tf):
    r = pl.program_id(1)
    ff = wd_ref.shape[0]
    g, sc, sh = g_ref[0], sc_ref[0, 0], sh_ref[0, 0]
    h_s[HALO:, :] = _norm_mod(x_ref[0], g, sc, sh).astype(BF16)
    hh = _norm_mod(xh_ref[0], g, sc, sh)
    h_s[:HALO, :] = jnp.where(r > 0, hh, 0.0).astype(BF16)
    nf = ff // tf
    cols = lambda half, f: pl.ds(pl.multiple_of(half * ff + f * tf, tf), tf)

    def up(f, slot):
        for half in range(2):
            u_s[slot, half] = _dot(h_s[...], wup_ref[:, cols(half, f)])

    def mixed(f, slot, half):
        u = u_s[slot, half]
        cw = cw_ref[:, cols(half, f)]
        b_ref[0, 0, :, cols(half, f)] = u[HALO + tr - 2:]
        a = pltpu.roll(u, 2, 0)[HALO:]
        b = pltpu.roll(u, 1, 0)[HALO:]
        return cw[0:1] * a + cw[1:2] * b + cw[2:3] * u[HALO:]

    def activate(f, slot):
        act_s[:, cols(0, f)] = (_silu(mixed(f, slot, 0)) * mixed(f, slot, 1)).astype(BF16)

    def body(it, carry):
        f = 2 * it
        up(f + 1, 1)
        activate(f, 0)
        up(f + 2, 0)
        activate(f + 1, 1)
        return carry

    assert nf % 2 == 1
    up(0, 0)
    lax.fori_loop(0, nf // 2, body, 0)
    activate(nf - 1, 0)
    y_ref[0] = x_ref[0] + gt_ref[0, 0] * _dot(act_s[...], wd_ref[...])


def _ffn_prompt(x, grp, gains, layer, wup_bf, conv_w, wd_bf, tr, tf):
    ng, nr, d = x.shape
    ff = wd_bf.shape[0]
    ms = lambda j: grp.mod_spec(j)
    once = lambda a: pl.BlockSpec(a.shape, lambda g, r: (0,) * a.ndim, pipeline_mode=pl.Buffered(1))
    y, buf = pl.pallas_call(
        functools.partial(_ffn_prompt_kernel, tr=tr, tf=tf), grid=(ng, nr // tr),
        in_specs=[pl.BlockSpec((1, tr, d), lambda g, r: (g, r, 0)),
                  pl.BlockSpec((1, HALO, d), lambda g, r: (g, jnp.maximum(r * (tr // HALO) - 1, 0), 0)),
                  pl.BlockSpec((1, 1, d), lambda g, r: (0, 0, 0)),
                  ms(4), ms(3), ms(5), once(wup_bf), once(conv_w), once(wd_bf)],
        out_specs=[pl.BlockSpec((1, tr, d), lambda g, r: (g, r, 0)),
                   pl.BlockSpec((1, 1, 2, 2 * ff), lambda g, r: (g, r, 0, 0))],
        out_shape=[jax.ShapeDtypeStruct((ng, nr, d), F32),
                   jax.ShapeDtypeStruct((ng, nr // tr, 2, 2 * ff), F32)],
        scratch_shapes=[pltpu.VMEM((tr + HALO, d), BF16), pltpu.VMEM((tr, ff), BF16),
                        pltpu.VMEM((2, 2, tr + HALO, tf), F32)],
        compiler_params=_cparams(("arbitrary", "arbitrary")),
    )(x, x, gains[layer:layer + 1], grp.mod(layer), grp.mod(layer), grp.mod(layer), wup_bf, conv_w, wd_bf)
    return y, buf[:, -1]


def _ffn_sample_kernel(x_ref, g_ref, sc_ref, sh_ref, gt_ref, sg_ref, sv_ref, wg_ref, wv_ref, cg_ref, cv_ref,
                       wd_ref, y_ref, bg_ref, bv_ref, h_s, *, nt, p):
    tile = lambda m: jnp.concatenate([m] * nt, axis=0)

    @pl.when(pl.program_id(2) == 0)
    def _():
        h_s[...] = _norm_mod(x_ref[0], g_ref[0], tile(sc_ref[0]), tile(sh_ref[0])).astype(BF16)

    h = h_s[...]
    n = nt * p

    def conv(st, u, cw):
        e = jnp.concatenate([st, u], axis=0)
        return cw[0:1] * e[0:n] + cw[1:2] * e[p:p + n] + cw[2:3] * e[2 * p:]

    ug = _dot(h, wg_ref[...])
    uv = _dot(h, wv_ref[...])
    bg_ref[...] = jnp.concatenate([sg_ref[...], ug], axis=0)[n:]
    bv_ref[...] = jnp.concatenate([sv_ref[...], uv], axis=0)[n:]
    _ffn_tail(conv(sg_ref[...], ug, cg_ref[...]), conv(sv_ref[...], uv, cv_ref[...]), wd_ref, x_ref,
              tile(gt_ref[0]), y_ref)


def _ffn_sample(x, grp, gains, layer, wup_bf, conv_w, wd_bf, state, tf):
    nt, p, d = x.shape
    ff = wd_bf.shape[0]
    nf = ff // tf
    n = nt * p
    st = state.transpose(1, 0, 2).reshape(2 * p, 2 * ff)
    ms = lambda j: grp.mod_spec(j, nargs=3)
    y, bg, bv = pl.pallas_call(
        functools.partial(_ffn_sample_kernel, nt=nt, p=p), grid=(1, 1, nf),
        in_specs=[pl.BlockSpec((1, n, d), lambda g, r, f: (0, 0, 0)),
                  pl.BlockSpec((1, 1, d), lambda g, r, f: (0, 0, 0)),
                  ms(4), ms(3), ms(5),
                  pl.BlockSpec((2 * p, tf), lambda g, r, f: (0, f)),
                  pl.BlockSpec((2 * p, tf), lambda g, r, f: (0, nf + f)),
                  pl.BlockSpec((d, tf), lambda g, r, f: (0, f)),
                  pl.BlockSpec((d, tf), lambda g, r, f: (0, nf + f)),
                  pl.BlockSpec((3, tf), lambda g, r, f: (0, f)),
                  pl.BlockSpec((3, tf), lambda g, r, f: (0, nf + f)),
                  pl.BlockSpec((tf, d), lambda g, r, f: (f, 0))],
        out_specs=[pl.BlockSpec((1, n, d), lambda g, r, f: (0, 0, 0)),
                   pl.BlockSpec((2 * p, tf), lambda g, r, f: (0, f)),
                   pl.BlockSpec((2 * p, tf), lambda g, r, f: (0, f))],
        out_shape=[jax.ShapeDtypeStruct((1, n, d), F32),
                   jax.ShapeDtypeStruct((2 * p, ff), F32),
                   jax.ShapeDtypeStruct((2 * p, ff), F32)],
        scratch_shapes=[pltpu.VMEM((n, d), BF16)],
        compiler_params=_cparams(("arbitrary", "arbitrary", "arbitrary")),
    )(x.reshape(1, n, d), gains[layer:layer + 1], grp.mod(layer), grp.mod(layer), grp.mod(layer), st, st, wup_bf, wup_bf, conv_w, conv_w, wd_bf)
    new_state = jnp.concatenate([bg, bv], axis=-1).reshape(2, p, 2 * ff).transpose(1, 0, 2)
    return y.reshape(nt, p, d), new_state


def _attend_group(qs, kk, vv, mask, sink_col):
    ng, nq = len(qs), qs[0].shape[0]
    q = jnp.concatenate(qs, axis=0).astype(BF16)
    s = jnp.where(jnp.concatenate([mask] * ng, axis=0), _dot_nt(q, kk), NEG)
    m = jnp.max(s, axis=-1, keepdims=True)
    if sink_col is not None:
        m = jnp.maximum(m, sink_col)
    p = jnp.exp(s - m)
    den = jnp.sum(p, axis=-1, keepdims=True)
    if sink_col is not None:
        den = den + jnp.exp(sink_col - m)
    o = _dot(p.astype(BF16), vv) / den
    return [o[g * nq:(g + 1) * nq] for g in range(ng)]


def _sink_col(sink_ref, heads, nq):
    return jnp.concatenate([jnp.broadcast_to(sink_ref[:, h:h + 1], (nq, 1)) for h in heads], axis=0)


def _band_mask(nq, w, prev_off):
    qi = lax.broadcasted_iota(jnp.int32, (nq, w + nq), 0)
    kj = lax.broadcasted_iota(jnp.int32, (nq, w + nq), 1)
    return ((kj < w) & (kj >= qi + prev_off)) | ((kj >= w) & ((kj - w) <= qi))


def _head(x, h):
    return x[:, h * HEAD_DIM:(h + 1) * HEAD_DIM]


def _band_attn_kernel(q_ref, kp_ref, kc_ref, vp_ref, vc_ref, qg_ref, kg_ref, cq_ref, sq_ref, cp_ref, sp_ref,
                      sink_ref, o_ref, ko_ref, vo_ref, q_s, m_s, acc_s, *, n_kv, chunk):
    i = pl.program_id(1)
    w = q_ref.shape[1]
    rows_kv = (N_HEADS // n_kv) * w
    seg = _seg_ones(2 * LANES)
    cq, sq = cq_ref[...], sq_ref[...]
    _stack_heads(_heads_rope(_heads_rms(q_ref[0], qg_ref[...], seg), cq, sq) * HEAD_DIM ** -0.5, q_s)
    kc = _heads_rope(_heads_rms(kc_ref[0], kg_ref[...], seg), cq, sq)
    kp = _heads_rope(_heads_rms(kp_ref[0], kg_ref[...], seg), cp_ref[...], sp_ref[...])
    kk = jnp.concatenate([kp, kc], axis=0).astype(BF16)
    vv = jnp.concatenate([vp_ref[0], vc_ref[0]], axis=0).astype(BF16)
    kvs = [([_head(kk, kv)], _with_ones(_head(vv, kv))) for kv in range(n_kv)]
    for h in range(N_HEADS):
        m_s[h * w:(h + 1) * w, :] = jnp.broadcast_to(sink_ref[:, h:h + 1], (w, LANES))
    acc_s[...] = jnp.concatenate([jnp.zeros((N_HEADS * w, HEAD_DIM), F32), jnp.ones((N_HEADS * w, HEAD_DIM), F32)],
                                 axis=1)
    mask = _band_mask(w, w, jnp.where(i > 0, 0, w))
    mk = jnp.concatenate([mask] * (chunk // w), axis=0)
    _flash_step([q_s], lambda c: kvs[c * chunk // rows_kv], lambda c: mk, m_s, acc_s, chunk)
    o = acc_s[:, :HEAD_DIM] / acc_s[:, HEAD_DIM:]
    o_ref[0] = _unstack_heads(o, w).astype(o_ref.dtype)
    ko_ref[0] = kc
    vo_ref[0] = vc_ref[0]


def _band_attn(y, q_norm, k_norm, sinks, cs, sn):
    b, s, _ = y.shape
    w = WINDOW
    kvd = SWA_KV_HEADS * HEAD_DIM
    hd = N_HEADS * HEAD_DIM
    kb, vb = hd // kvd, hd // kvd + 1
    prev = lambda i: jnp.maximum(i - 1, 0)
    full = lambda a: pl.BlockSpec(a.shape, lambda bb, i: (0,) * a.ndim)
    tab_c = pl.BlockSpec((w, LANES), lambda bb, i: (i, 0))
    tab_p = pl.BlockSpec((w, LANES), lambda bb, i: (prev(i), 0))
    q_norm, k_norm = jnp.tile(q_norm, (1, N_HEADS)), jnp.tile(k_norm, (1, SWA_KV_HEADS))
    return pl.pallas_call(
        functools.partial(_band_attn_kernel, n_kv=SWA_KV_HEADS, chunk=2 * w), grid=(b, s // w),
        scratch_shapes=[pltpu.VMEM((N_HEADS * w, HEAD_DIM), BF16),
                        pltpu.VMEM((N_HEADS * w, LANES), F32),
                        pltpu.VMEM((N_HEADS * w, 2 * HEAD_DIM), F32)],
        in_specs=[pl.BlockSpec((1, w, hd), lambda bb, i: (bb, i, 0)),
                  pl.BlockSpec((1, w, kvd), lambda bb, i: (bb, prev(i), kb)),
                  pl.BlockSpec((1, w, kvd), lambda bb, i: (bb, i, kb)),
                  pl.BlockSpec((1, w, kvd), lambda bb, i: (bb, prev(i), vb)),
                  pl.BlockSpec((1, w, kvd), lambda bb, i: (bb, i, vb)),
                  full(q_norm), full(k_norm), tab_c, tab_c, tab_p, tab_p, full(sinks)],
        out_specs=[pl.BlockSpec((1, w, hd), lambda bb, i: (bb, i, 0)),
                   pl.BlockSpec((1, w, kvd), lambda bb, i: (bb, 0, 0)),
                   pl.BlockSpec((1, w, kvd), lambda bb, i: (bb, 0, 0))],
        out_shape=[jax.ShapeDtypeStruct((b, s, hd), BF16),
                   jax.ShapeDtypeStruct((b, w, kvd), F32),
                   jax.ShapeDtypeStruct((b, w, kvd), F32)],
        compiler_params=_cparams(("arbitrary", "arbitrary")),
    )(y, y, y, y, y, q_norm, k_norm, cs, sn, cs, sn, sinks)


def _rows(ref, n):
    return jnp.concatenate([ref[t, 0] for t in range(n)], axis=0)


def _window_step(q, k_new, v_new, kb, vb, qn, kn, cs, sn, sink_ref, n_kv):
    ns, w = q.shape[0], kb.shape[0]
    grp = N_HEADS // n_kv
    scale = HEAD_DIM ** -0.5
    mask = _band_mask(ns, w, 0)
    outs, knew = [], []
    for kv in range(n_kv):
        kn_h = _rope(_rms(_head(k_new, kv), kn), cs, sn, ROPE_DIM)
        knew.append(kn_h)
        kk = jnp.concatenate([_head(kb, kv), kn_h], axis=0).astype(BF16)
        vv = jnp.concatenate([_head(vb, kv), _head(v_new, kv)], axis=0).astype(BF16)
        heads = range(kv * grp, (kv + 1) * grp)
        qs = [_rope(_rms(_head(q, h), qn), cs, sn, ROPE_DIM) * scale for h in heads]
        sink = None if sink_ref is None else _sink_col(sink_ref, heads, ns)
        outs += _attend_group(qs, kk, vv, mask, sink)
    knew = jnp.concatenate(knew, axis=1)
    k_out = jnp.concatenate([kb[ns:], knew], axis=0)
    v_out = jnp.concatenate([vb[ns:], v_new], axis=0)
    return outs, knew, k_out, v_out


def _step_attn_kernel(q_ref, k_ref, v_ref, kb_ref, vb_ref, qn_ref, kn_ref, cs_ref, sn_ref, sink_ref,
                      o_ref, ko_ref, vo_ref, *, n_kv):
    ns = q_ref.shape[0]
    outs, _, k_out, v_out = _window_step(_rows(q_ref, ns), _rows(k_ref, ns), _rows(v_ref, ns),
                                         kb_ref[0, 0], vb_ref[0, 0], qn_ref[...], kn_ref[...],
                                         cs_ref[...], sn_ref[...], sink_ref, n_kv)
    o = jnp.concatenate(outs, axis=1)
    for t in range(ns):
        o_ref[t, 0] = o[t:t + 1]
    ko_ref[0] = k_out
    vo_ref[0] = v_out


def _step_attn(y, k_cache, v_cache, li, q_norm, k_norm, sinks, cs, sn):
    ns, db, n = y.shape
    w = k_cache.shape[2]
    kvd = SWA_KV_HEADS * HEAD_DIM
    hd = N_HEADS * HEAD_DIM
    kb, vb = hd // kvd, hd // kvd + 1
    y4 = y.reshape(ns, db, 1, n)
    full = lambda a: pl.BlockSpec(a.shape, lambda b: (0,) * a.ndim)
    o, ko, vo = pl.pallas_call(
        functools.partial(_step_attn_kernel, n_kv=SWA_KV_HEADS), grid=(db,),
        in_specs=[pl.BlockSpec((ns, 1, 1, hd), lambda b: (0, b, 0, 0)),
                  pl.BlockSpec((ns, 1, 1, kvd), lambda b: (0, b, 0, kb)),
                  pl.BlockSpec((ns, 1, 1, kvd), lambda b: (0, b, 0, vb)),
                  pl.BlockSpec((1, 1, w, kvd), lambda b: (li, b, 0, 0)),
                  pl.BlockSpec((1, 1, w, kvd), lambda b: (li, b, 0, 0)),
                  full(q_norm), full(k_norm), full(cs), full(sn), full(sinks)],
        out_specs=[pl.BlockSpec((ns, 1, 1, hd), lambda b: (0, b, 0, 0)),
                   pl.BlockSpec((1, w, kvd), lambda b: (b, 0, 0)),
                   pl.BlockSpec((1, w, kvd), lambda b: (b, 0, 0))],
        out_shape=[jax.ShapeDtypeStruct((ns, db, 1, hd), F32),
                   jax.ShapeDtypeStruct((db, w, kvd), F32),
                   jax.ShapeDtypeStruct((db, w, kvd), F32)],
        compiler_params=_cparams(("arbitrary",)),
    )(y4, y4, y4, k_cache, v_cache, q_norm, k_norm, cs, sn, sinks)
    return o.reshape(ns, db, hd), ko, vo


def _online_update(s, vals, m_ref, l_ref, acc_ref):
    m_old = m_ref[...]
    m_new = jnp.maximum(m_old, jnp.max(s, axis=-1, keepdims=True))
    alpha = jnp.exp(m_old - m_new)
    p = jnp.exp(s - m_new)
    l_ref[...] = alpha * l_ref[...] + jnp.sum(p, axis=-1, keepdims=True)
    acc_ref[...] = alpha * acc_ref[...] + _dot(p.astype(BF16), vals)
    m_ref[...] = m_new


LANES = 128


def _lanes(x, n):
    if n <= LANES:
        return x[:, :n]
    return jnp.concatenate([x] * (n // LANES), axis=1)


def _flash_step(q_refs, kv_fn, mask_fn, m_ref, acc_ref, chunk):
    n = m_ref.shape[0] // chunk

    def scores(c):
        rows = pl.ds(c * chunk, chunk)
        k_parts = kv_fn(c)[0]
        s = _dot_nt(q_refs[0][rows, :], k_parts[0])
        for qr, kp in zip(q_refs[1:], k_parts[1:]):
            s = s + _dot_nt(qr[rows, :], kp)
        mk = None if mask_fn is None else mask_fn(c)
        return s if mk is None else jnp.where(mk, s, NEG)

    s = scores(0)
    for c in range(n):
        s_next = scores(c + 1) if c + 1 < n else None
        rows = pl.ds(c * chunk, chunk)
        vals = kv_fn(c)[1]
        m_old = m_ref[rows, :]
        m_new = jnp.maximum(m_old, jnp.max(s, axis=-1, keepdims=True))
        alpha = jnp.exp(m_old - m_new)
        p = jnp.exp(s - _lanes(m_new, s.shape[1])).astype(BF16)
        acc_ref[rows, :] = _lanes(alpha, vals.shape[1]) * acc_ref[rows, :] + _dot(p, vals)
        m_ref[rows, :] = m_new
        s = s_next


def _seg_ones(n):
    r = lax.broadcasted_iota(jnp.int32, (n, n), 0) // HEAD_DIM
    c = lax.broadcasted_iota(jnp.int32, (n, n), 1) // HEAD_DIM
    return jnp.where(r == c, 1.0, 0.0).astype(BF16)


def _heads_rms(x, g, seg):
    sq = x * x
    hi = sq.astype(BF16)
    lo = (sq - hi.astype(F32)).astype(BF16)
    n = seg.shape[0]
    parts = []
    for j in range(0, x.shape[1], n):
        k = min(n, x.shape[1] - j)
        b = seg[:k, :k]
        parts.append(_dot(hi[:, j:j + k], b) + _dot(lo[:, j:j + k], b))
    ss = parts[0] if len(parts) == 1 else jnp.concatenate(parts, axis=1)
    return x * lax.rsqrt(ss * (1.0 / HEAD_DIM) + EPS) * g


def _heads_rope(x, c2, s2):
    w = x.shape[1]
    half = ROPE_DIM // 2
    lane = lax.broadcasted_iota(jnp.int32, x.shape, 1) % HEAD_DIM
    sw = jnp.where(lane < half, pltpu.roll(x, w - half, 1), pltpu.roll(x, half, 1))
    return x * _lanes(c2, w) + sw * _lanes(s2, w)


def _stack_heads(x, q_s):
    nq = x.shape[0]
    for h in range(x.shape[1] // HEAD_DIM):
        q_s[h * nq:(h + 1) * nq, :] = _head(x, h).astype(q_s.dtype)


def _unstack_heads(o, nq):
    return jnp.concatenate([o[h * nq:(h + 1) * nq] for h in range(o.shape[0] // nq)], axis=1)


def _flash_init(m_ref, acc_ref):
    m_ref[...] = jnp.full(m_ref.shape, NEG, F32)
    acc_ref[...] = jnp.zeros(acc_ref.shape, F32)


def _with_ones(v):
    pad = LANES - v.shape[1] % LANES
    return jnp.concatenate([v, jnp.ones((v.shape[0], pad), v.dtype)], axis=1)


def _mla_queries(q, nn, nr, cs, sn, wuk_ref, ql_s, qr_s):
    nq = q.shape[0]
    scale = (MLA_NOPE + MLA_ROPE) ** -0.5
    off = N_HEADS * MLA_NOPE
    for h in range(N_HEADS):
        qn = _rms(q[:, h * MLA_NOPE:(h + 1) * MLA_NOPE], nn)
        ql_s[h * nq:(h + 1) * nq, :] = (_dot(qn.astype(BF16), wuk_ref[h]) * scale).astype(BF16)
        qr = _rms(q[:, off + h * MLA_ROPE:off + (h + 1) * MLA_ROPE], nr)
        qr_s[h * nq:(h + 1) * nq, :] = (_rope(qr, cs, sn, MLA_ROPE) * scale).astype(BF16)


def _mla_output(acc, den, wuv_ref, nq):
    o_lat = (acc / den).astype(BF16)
    return jnp.concatenate([_dot(o_lat[h * nq:(h + 1) * nq], wuv_ref[h]) for h in range(N_HEADS)], axis=1)


def _mla_attn_kernel(q_ref, c_ref, kr_ref, cs_ref, sn_ref, nn_ref, nr_ref, wuk_ref, wuv_ref, o_ref,
                     ql_s, qr_s, m_s, acc_s, cb_s, kb_s, *, tq, tk, chunk):
    i = pl.program_id(1)

    @pl.when(i == 0)
    def _():
        cb_s[...] = _with_ones(c_ref[0].astype(BF16))
        kb_s[...] = kr_ref[0].astype(BF16)

    _mla_queries(q_ref[0], nn_ref[...], nr_ref[...], cs_ref[...], sn_ref[...], wuk_ref, ql_s, qr_s)
    _flash_init(m_s, acc_s)
    t_pos = i * tq + lax.broadcasted_iota(jnp.int32, (tq, tk), 0)
    k_off = lax.broadcasted_iota(jnp.int32, (tq, tk), 1)

    def step(j, masked):
        start = pl.multiple_of(j * tk, tk)
        cbx = cb_s[pl.ds(start, tk), :]
        kb = kb_s[pl.ds(start, tk), :]
        mask_fn = None
        if masked:
            mk = jnp.concatenate([(start + k_off) <= t_pos] * (chunk // tq), axis=0)
            mask_fn = lambda c: mk
        kv = ([cbx[:, :MLA_KV_LORA], kb], cbx)
        _flash_step([ql_s, qr_s], lambda c: kv, mask_fn, m_s, acc_s, chunk)

    def full_body(j, carry):
        step(j, False)
        return carry

    def diag_body(j, carry):
        step(j, True)
        return carry

    n_full = (i * tq) // tk
    lax.fori_loop(0, n_full, full_body, 0)
    lax.fori_loop(n_full, (i * tq + tq - 1) // tk + 1, diag_body, 0)
    o_ref[0] = _mla_output(acc_s[:, :MLA_KV_LORA], _lanes(acc_s[:, MLA_KV_LORA:], MLA_KV_LORA), wuv_ref,
                           tq).astype(o_ref.dtype)


def _mla_attn(q, c, kr, cs, sn, nn, nr, wuk, wuv, tq, tk, chunk):
    b, s, nqc = q.shape
    hd = N_HEADS * HEAD_DIM
    full = lambda a: pl.BlockSpec(a.shape, lambda bb, i: (0,) * a.ndim)
    tab = pl.BlockSpec((tq, MLA_ROPE), lambda bb, i: (i, 0))
    return pl.pallas_call(
        functools.partial(_mla_attn_kernel, tq=tq, tk=tk, chunk=chunk), grid=(b, s // tq),
        in_specs=[pl.BlockSpec((1, tq, nqc), lambda bb, i: (bb, i, 0)),
                  pl.BlockSpec((1, s, MLA_KV_LORA), lambda bb, i: (bb, 0, 0)),
                  pl.BlockSpec((1, s, MLA_ROPE), lambda bb, i: (bb, 0, 0)),
                  tab, tab, full(nn), full(nr), full(wuk), full(wuv)],
        out_specs=pl.BlockSpec((1, tq, hd), lambda bb, i: (bb, i, 0)),
        out_shape=jax.ShapeDtypeStruct((b, s, hd), BF16),
        scratch_shapes=[pltpu.VMEM((N_HEADS * tq, MLA_KV_LORA), BF16),
                        pltpu.VMEM((N_HEADS * tq, MLA_ROPE), BF16),
                        pltpu.VMEM((N_HEADS * tq, LANES), F32),
                        pltpu.VMEM((N_HEADS * tq, MLA_KV_LORA + LANES), F32),
                        pltpu.VMEM((s, MLA_KV_LORA + LANES), BF16),
                        pltpu.VMEM((s, MLA_ROPE), BF16)],
        compiler_params=_cparams(("arbitrary", "arbitrary")),
    )(q, c, kr, cs, sn, nn, nr, wuk, wuv)


def _mla_decode_kernel(pt_ref, *refs, ns, slots):
    del pt_ref
    lat = refs[:slots]
    krp = refs[slots:2 * slots]
    (q_ref, cn_ref, kn_ref, cs_ref, sn_ref, nn_ref, nr_ref, wuk_ref, wuv_ref, o_ref,
     ql_s, qr_s, m_s, l_s, acc_s) = refs[2 * slots:]
    g = pl.program_id(1)

    @pl.when(g == 0)
    def _():
        _mla_queries(_rows(q_ref, ns), nn_ref[...], nr_ref[...], cs_ref[...], sn_ref[...], wuk_ref, ql_s, qr_s)
        m_s[...] = jnp.full(m_s.shape, NEG, F32)
        l_s[...] = jnp.zeros(l_s.shape, F32)
        acc_s[...] = jnp.zeros(acc_s.shape, F32)

    cb = jnp.concatenate([r[0, 0] for r in lat], axis=0).astype(BF16)
    kb = jnp.concatenate([r[0, 0] for r in krp], axis=0).astype(BF16)
    _online_update(_dot_nt(ql_s[...], cb) + _dot_nt(qr_s[...], kb), cb, m_s, l_s, acc_s)

    @pl.when(g == pl.num_programs(1) - 1)
    def _():
        pad = PAGE_SIZE - ns
        cb = jnp.concatenate([_rows(cn_ref, ns), jnp.zeros((pad, MLA_KV_LORA), F32)], axis=0).astype(BF16)
        kb = jnp.concatenate([_rows(kn_ref, ns), jnp.zeros((pad, MLA_ROPE), F32)], axis=0).astype(BF16)
        s = _dot_nt(ql_s[...], cb) + _dot_nt(qr_s[...], kb)
        row = lax.broadcasted_iota(jnp.int32, s.shape, 0) % ns
        col = lax.broadcasted_iota(jnp.int32, s.shape, 1)
        _online_update(jnp.where(col <= row, s, NEG), cb, m_s, l_s, acc_s)
        o = _mla_output(acc_s[...], l_s[...], wuv_ref, ns)
        for t in range(ns):
            o_ref[t, 0] = o[t:t + 1]


def _mla_decode(q, c, kr, lat_pool, kr_pool, li, page_table, cs, sn, nn, nr, wuk, wuv):
    ns, db, nqc = q.shape
    hd = N_HEADS * HEAD_DIM
    slots = PAGE_SLOTS
    npg = page_table.shape[1] // slots
    full = lambda a: pl.BlockSpec(a.shape, lambda b, g, pt: (0,) * a.ndim)

    def paged(width, k):
        return pl.BlockSpec((1, 1, PAGE_SIZE, width), lambda b, g, pt: (li, pt[b, g * slots + k], 0, 0))

    step = lambda width: pl.BlockSpec((ns, 1, 1, width), lambda b, g, pt: (0, b, 0, 0))
    grid_spec = pltpu.PrefetchScalarGridSpec(
        num_scalar_prefetch=1, grid=(db, npg),
        in_specs=([paged(MLA_KV_LORA, k) for k in range(slots)] + [paged(MLA_ROPE, k) for k in range(slots)]
                  + [step(nqc), step(MLA_KV_LORA), step(MLA_ROPE),
                     full(cs), full(sn), full(nn), full(nr), full(wuk), full(wuv)]),
        out_specs=step(hd),
        scratch_shapes=[pltpu.VMEM((N_HEADS * ns, MLA_KV_LORA), BF16),
                        pltpu.VMEM((N_HEADS * ns, MLA_ROPE), BF16),
                        pltpu.VMEM((N_HEADS * ns, 1), F32),
                        pltpu.VMEM((N_HEADS * ns, 1), F32),
                        pltpu.VMEM((N_HEADS * ns, MLA_KV_LORA), F32)])
    o = pl.pallas_call(
        functools.partial(_mla_decode_kernel, ns=ns, slots=slots), grid_spec=grid_spec,
        out_shape=jax.ShapeDtypeStruct((ns, db, 1, hd), F32),
        compiler_params=_cparams(("arbitrary", "arbitrary")),
    )(page_table, *([lat_pool] * slots), *([kr_pool] * slots),
      q.reshape(ns, db, 1, nqc), c.reshape(ns, db, 1, MLA_KV_LORA), kr.reshape(ns, db, 1, MLA_ROPE),
      cs, sn, nn, nr, wuk, wuv)
    return o.reshape(ns, db, hd)


NSA_KVD = NSA_KV_HEADS * HEAD_DIM
NSA_GRP = N_HEADS // NSA_KV_HEADS
NSA_PAIR = NSA_SEL_BLOCK // NSA_CMP_BLOCK


def _nsa_prep_kernel(kc_ref, vc_ref, ks_ref, kw_ref, wk_ref, wv_ref, kcn_ref, ksn_ref, kwn_ref, cs_ref, sn_ref,
                     kso_ref, kwo_ref, kcmp_ref, vcmp_ref):
    cs, sn = cs_ref[...], sn_ref[...]
    kc, vc, ks, kw = kc_ref[0], vc_ref[0], ks_ref[0], kw_ref[0]
    kso, kwo, kcmp, vcmp = [], [], [], []
    for kv in range(NSA_KV_HEADS):
        kso.append(_rope(_rms(_head(ks, kv), ksn_ref[...]), cs, sn, ROPE_DIM))
        kwo.append(_rope(_rms(_head(kw, kv), kwn_ref[...]), cs, sn, ROPE_DIM))
        kcmp.append(_rms(_dot(wk_ref[kv], _head(kc, kv).astype(BF16)), kcn_ref[...]))
        vcmp.append(_dot(wv_ref[kv], _head(vc, kv).astype(BF16)))
    kso_ref[0] = jnp.concatenate(kso, axis=1)
    kwo_ref[0] = jnp.concatenate(kwo, axis=1)
    kcmp_ref[0] = jnp.concatenate(kcmp, axis=1)
    vcmp_ref[0] = jnp.concatenate(vcmp, axis=1)


def _cmp_matrix(w, nc):
    eye = jnp.eye(nc, dtype=F32)
    return (eye[None, :, :, None] * w.T[:, None, None, :]).reshape(w.shape[1], nc, nc * NSA_CMP_BLOCK).astype(BF16)


def _nsa_prep(y, cmp_wk, cmp_wv, k_norm, cs, sn):
    b, s, _ = y.shape
    nc = s // NSA_CMP_BLOCK
    base = N_HEADS * HEAD_DIM // NSA_KVD
    col = lambda j: pl.BlockSpec((1, s, NSA_KVD), lambda bb: (bb, 0, base + j))
    full = lambda a: pl.BlockSpec(a.shape, lambda bb: (0,) * a.ndim)
    wk, wv = _cmp_matrix(cmp_wk, nc), _cmp_matrix(cmp_wv, nc)
    kn = [k_norm[j:j + 1] for j in range(3)]
    seq = pl.BlockSpec((1, s, NSA_KVD), lambda bb: (bb, 0, 0))
    blk = pl.BlockSpec((1, nc, NSA_KVD), lambda bb: (bb, 0, 0))
    return pl.pallas_call(
        _nsa_prep_kernel, grid=(b,),
        in_specs=[col(0), col(1), col(2), col(4), full(wk), full(wv), full(kn[0]), full(kn[1]), full(kn[2]),
                  full(cs), full(sn)],
        out_specs=[seq, seq, blk, blk],
        out_shape=[jax.ShapeDtypeStruct((b, s, NSA_KVD), F32), jax.ShapeDtypeStruct((b, s, NSA_KVD), F32),
                   jax.ShapeDtypeStruct((b, nc, NSA_KVD), F32), jax.ShapeDtypeStruct((b, nc, NSA_KVD), F32)],
        compiler_params=_cparams(("arbitrary",)),
    )(y, y, y, y, wk, wv, kn[0], kn[1], kn[2], cs, sn)


def _cmp_attend(q, kcmp, vcmp, mask, ng):
    nq = q.shape[0] // ng
    mk = jnp.concatenate([mask] * ng, axis=0)
    s = jnp.where(mk, _dot_nt(q, kcmp.astype(BF16)), NEG)
    m = jnp.max(s, axis=-1, keepdims=True)
    p = jnp.where(mk, jnp.exp(s - m), 0.0)
    p = p / jnp.maximum(jnp.sum(p, axis=-1, keepdims=True), TINY)
    o = _dot(p.astype(BF16), vcmp.astype(BF16))
    imp = p[0:nq]
    for g in range(1, ng):
        imp = imp + p[g * nq:(g + 1) * nq]
    return o, imp


def _pair_sum(imp):
    n = imp.shape[1]
    lane = lax.broadcasted_iota(jnp.int32, imp.shape, 1)
    return imp + jnp.where(lane % 2 == 0, pltpu.roll(imp, n - 1, 1), pltpu.roll(imp, 1, 1))


def _select(impx, nblk, seg, n_sel):
    lane = lax.broadcasted_iota(jnp.int32, impx.shape, 1)
    blk = (lane % seg) // NSA_PAIR
    nseg = impx.shape[1] // seg
    cnt = jnp.zeros(impx.shape, jnp.int32)
    for j in range(nblk):
        col = impx[:, j * NSA_PAIR:j * NSA_PAIR + 1]
        for sg in range(1, nseg):
            c = sg * seg + j * NSA_PAIR
            col = jnp.where(lane < sg * seg, col, impx[:, c:c + 1])
        beats = (col > impx) | ((col == impx) & (j < blk))
        cnt = cnt + beats.astype(jnp.int32)
    return (cnt < n_sel).astype(F32)


def _expand_mask(sel, start, tk):
    nc = sel.shape[1]
    n = lax.broadcasted_iota(jnp.int32, (nc, tk), 0)
    k = lax.broadcasted_iota(jnp.int32, (nc, tk), 1)
    e = jnp.where((start + k) // NSA_CMP_BLOCK == n, 1.0, 0.0).astype(BF16)
    return _dot(sel.astype(BF16), e) > 0.5


def _nsa_attn_kernel(q_ref, gl_ref, ks_ref, vs_ref, kwp_ref, kwc_ref, vwp_ref, vwc_ref, kcmp_ref, vcmp_ref,
                     qg_ref, c2_ref, s2_ref, o_ref, qn_s, qr_s, ks_s, vs_s, m_s, acc_s, mw_s, accw_s,
                     *, tk, n_sel, chunk):
    i = pl.program_id(1)
    tq = q_ref.shape[1]
    nc = kcmp_ref.shape[1]
    rows_kv = NSA_GRP * tq
    rep = chunk // tq
    kv_of = lambda c: c * chunk // rows_kv
    kv_range = range(NSA_KV_HEADS)

    @pl.when(i == 0)
    def _():
        ks_s[...] = ks_ref[0].astype(BF16)
        vs = vs_ref[0].astype(BF16)
        vs_s[...] = jnp.concatenate([_with_ones(_head(vs, kv)) for kv in kv_range], axis=1)

    qn = _heads_rms(q_ref[0], qg_ref[...], _seg_ones(2 * LANES)) * HEAD_DIM ** -0.5
    _stack_heads(qn, qn_s)
    _stack_heads(_heads_rope(qn, c2_ref[...], s2_ref[...]), qr_s)

    t_c = i * tq + lax.broadcasted_iota(jnp.int32, (tq, nc), 0)
    n_c = lax.broadcasted_iota(jnp.int32, (tq, nc), 1)
    cmask = (n_c + 1) * NSA_CMP_BLOCK - 1 <= t_c
    o_cmp, imps = [], []
    for kv in kv_range:
        oc, imp = _cmp_attend(qn_s[kv * rows_kv:(kv + 1) * rows_kv, :], _head(kcmp_ref[0], kv),
                              _head(vcmp_ref[0], kv), cmask, NSA_GRP)
        o_cmp.append(oc)
        imps.append(imp)
    imp = _pair_sum(jnp.concatenate(imps, axis=1))
    t_i = i * tq + lax.broadcasted_iota(jnp.int32, imp.shape, 0)
    blk = (lax.broadcasted_iota(jnp.int32, imp.shape, 1) % nc) // NSA_PAIR
    impx = jnp.where(blk == t_i // NSA_SEL_BLOCK, jnp.inf, jnp.where(blk * NSA_SEL_BLOCK <= t_i, imp, -jnp.inf))
    sel = _select(impx, nc // NSA_PAIR, nc, n_sel)

    _flash_init(m_s, acc_s)
    selk = [sel[:, kv * nc:(kv + 1) * nc] for kv in kv_range]
    t_k = i * tq + lax.broadcasted_iota(jnp.int32, (tq, tk), 0)
    k_off = lax.broadcasted_iota(jnp.int32, (tq, tk), 1)

    def body(j, carry):
        start = pl.multiple_of(j * tk, tk)
        kb = ks_s[pl.ds(start, tk), :]
        vbx = vs_s[pl.ds(start, tk), :]
        causal = (start + k_off) <= t_k
        mks = [jnp.concatenate([_expand_mask(selk[kv], start, tk) & causal] * rep, axis=0) for kv in kv_range]
        kvs = [([_head(kb, kv)], vbx[:, kv * LANES:(kv + 1) * LANES]) for kv in kv_range]
        _flash_step([qr_s], lambda c: kvs[kv_of(c)], lambda c: mks[kv_of(c)], m_s, acc_s, chunk)
        return carry

    lax.fori_loop(0, (i * tq + tq - 1) // tk + 1, body, 0)

    _flash_init(mw_s, accw_s)
    kk = jnp.concatenate([kwp_ref[0], kwc_ref[0]], axis=0).astype(BF16)
    vv = jnp.concatenate([vwp_ref[0], vwc_ref[0]], axis=0).astype(BF16)
    kvw = [([_head(kk, kv)], _with_ones(_head(vv, kv))) for kv in kv_range]
    wmk = jnp.concatenate([_band_mask(tq, tq, jnp.where(i > 0, 0, tq))] * rep, axis=0)
    _flash_step([qr_s], lambda c: kvw[kv_of(c)], lambda c: wmk, mw_s, accw_s, chunk)

    o_sel = acc_s[:, :HEAD_DIM] / acc_s[:, HEAD_DIM:]
    o_win = accw_s[:, :HEAD_DIM] / accw_s[:, HEAD_DIM:]
    gates = jax.nn.sigmoid(gl_ref[0])
    outs = []
    for h in range(N_HEADS):
        kv, g = divmod(h, NSA_GRP)
        outs.append(gates[:, 3 * h:3 * h + 1] * o_cmp[kv][g * tq:(g + 1) * tq]
                    + gates[:, 3 * h + 1:3 * h + 2] * o_sel[h * tq:(h + 1) * tq]
                    + gates[:, 3 * h + 2:3 * h + 3] * o_win[h * tq:(h + 1) * tq])
    o_ref[0] = jnp.concatenate(outs, axis=1).astype(o_ref.dtype)


def _nsa_attn(y, ksn, kwn, kcmp, vcmp, q_norm, cs, sn, tk):
    b, s, _ = y.shape
    tq = WINDOW
    hd = N_HEADS * HEAD_DIM
    nc = kcmp.shape[1]
    base = hd // NSA_KVD
    n_sel = min(NSA_TOPK, s // NSA_SEL_BLOCK)
    prev = lambda i: jnp.maximum(i - 1, 0)
    full = lambda a: pl.BlockSpec(a.shape, lambda bb, i: (0,) * a.ndim)
    tab = pl.BlockSpec((tq, LANES), lambda bb, i: (i, 0))
    q_norm = jnp.tile(q_norm, (1, N_HEADS))
    return pl.pallas_call(
        functools.partial(_nsa_attn_kernel, tk=tk, n_sel=n_sel, chunk=2 * tq), grid=(b, s // tq),
        in_specs=[pl.BlockSpec((1, tq, hd), lambda bb, i: (bb, i, 0)),
                  pl.BlockSpec((1, tq, NSA_KVD), lambda bb, i: (bb, i, base + 6)),
                  pl.BlockSpec((1, s, NSA_KVD), lambda bb, i: (bb, 0, 0)),
                  pl.BlockSpec((1, s, NSA_KVD), lambda bb, i: (bb, 0, base + 3)),
                  pl.BlockSpec((1, tq, NSA_KVD), lambda bb, i: (bb, prev(i), 0)),
                  pl.BlockSpec((1, tq, NSA_KVD), lambda bb, i: (bb, i, 0)),
                  pl.BlockSpec((1, tq, NSA_KVD), lambda bb, i: (bb, prev(i), base + 5)),
                  pl.BlockSpec((1, tq, NSA_KVD), lambda bb, i: (bb, i, base + 5)),
                  pl.BlockSpec((1, nc, NSA_KVD), lambda bb, i: (bb, 0, 0)),
                  pl.BlockSpec((1, nc, NSA_KVD), lambda bb, i: (bb, 0, 0)),
                  full(q_norm), tab, tab],
        out_specs=pl.BlockSpec((1, tq, hd), lambda bb, i: (bb, i, 0)),
        out_shape=jax.ShapeDtypeStruct((b, s, hd), BF16),
        scratch_shapes=[pltpu.VMEM((N_HEADS * tq, HEAD_DIM), BF16),
                        pltpu.VMEM((N_HEADS * tq, HEAD_DIM), BF16),
                        pltpu.VMEM((s, NSA_KVD), BF16),
                        pltpu.VMEM((s, NSA_KV_HEADS * LANES), BF16),
                        pltpu.VMEM((N_HEADS * tq, LANES), F32),
                        pltpu.VMEM((N_HEADS * tq, 2 * HEAD_DIM), F32),
                        pltpu.VMEM((N_HEADS * tq, LANES), F32),
                        pltpu.VMEM((N_HEADS * tq, 2 * HEAD_DIM), F32)],
        compiler_params=_cparams(("arbitrary", "arbitrary")),
    )(y, y, ksn, y, kwn, kwn, y, y, kcmp, vcmp, q_norm, cs, sn)


def _nsa_decode_kernel(pt_ref, *refs, ns, slots, npg, past, n_sel):
    del pt_ref
    kcp, vcp = refs[:slots], refs[slots:2 * slots]
    ksp, vsp = refs[2 * slots:3 * slots], refs[3 * slots:4 * slots]
    (q_ref, gl_ref, ks_ref, vs_ref, kw_ref, vw_ref, kwb_ref, vwb_ref, wk_ref, wv_ref,
     qn_ref, kcn_ref, ksn_ref, kwn_ref, cs_ref, sn_ref,
     o_ref, kso_ref, kwo_ref, vwo_ref,
     kcmp_s, vcmp_s, sel_s, ocmp_s, q_s, m_s, l_s, acc_s) = refs[4 * slots:]
    step = pl.program_id(1)
    scale = HEAD_DIM ** -0.5
    nc = kcmp_s.shape[0]
    rows_per_step = slots * PAGE_SIZE
    cper = rows_per_step // NSA_CMP_BLOCK
    grp = lambda xs, kv: xs[kv * NSA_GRP:(kv + 1) * NSA_GRP]

    @pl.when(step < npg)
    def _():
        kc = jnp.concatenate([r[0, 0] for r in kcp], axis=0) * wk_ref[...]
        vc = jnp.concatenate([r[0, 0] for r in vcp], axis=0) * wv_ref[...]
        blocks = lambda x: jnp.concatenate(
            [jnp.sum(x[n * NSA_CMP_BLOCK:(n + 1) * NSA_CMP_BLOCK], axis=0, keepdims=True) for n in range(cper)],
            axis=0)
        row = pl.multiple_of(step * cper, cper)
        kcmp_s[pl.ds(row, cper), :] = blocks(kc)
        vcmp_s[pl.ds(row, cper), :] = blocks(vc)

    @pl.when(step == npg - 1)
    def _():
        q = _rows(q_ref, ns)
        qn = [_rms(_head(q, h), qn_ref[...]) * scale for h in range(N_HEADS)]
        qr = [_rope(x, cs_ref[...], sn_ref[...], ROPE_DIM) for x in qn]
        t_c = past + lax.broadcasted_iota(jnp.int32, (ns, nc), 0)
        n_c = lax.broadcasted_iota(jnp.int32, (ns, nc), 1)
        cmask = (n_c + 1) * NSA_CMP_BLOCK - 1 <= t_c
        imps = []
        for kv in range(NSA_KV_HEADS):
            kcmp = _rms(_head(kcmp_s[...], kv), kcn_ref[...])
            oc, imp = _cmp_attend(jnp.concatenate(grp(qn, kv), axis=0).astype(BF16), kcmp, _head(vcmp_s[...], kv),
                                  cmask, NSA_GRP)
            ocmp_s[kv] = oc
            imps.append(_pair_sum(imp))
            q_s[kv] = jnp.concatenate(grp(qr, kv), axis=0).astype(BF16)
        for kv in range(NSA_KV_HEADS):
            sel_s[kv] = _select(imps[kv], nc // NSA_PAIR, nc, n_sel - 1)
        m_s[...] = jnp.full(m_s.shape, NEG, F32)
        l_s[...] = jnp.zeros(l_s.shape, F32)
        acc_s[...] = jnp.zeros(acc_s.shape, F32)

    @pl.when(step >= npg)
    def _():
        kb = jnp.concatenate([r[0, 0] for r in ksp], axis=0).astype(BF16)
        vb = jnp.concatenate([r[0, 0] for r in vsp], axis=0).astype(BF16)
        start = (step - npg) * rows_per_step
        for kv in range(NSA_KV_HEADS):
            mk = _expand_mask(sel_s[kv], start, rows_per_step)
            s = jnp.where(jnp.concatenate([mk] * NSA_GRP, axis=0), _dot_nt(q_s[kv], _head(kb, kv)), NEG)
            _online_update(s, _head(vb, kv), m_s.at[kv], l_s.at[kv], acc_s.at[kv])

    @pl.when(step == 2 * npg - 1)
    def _():
        pad = PAGE_SIZE - ns
        vs_new = _rows(vs_ref, ns)
        ks_raw = _rows(ks_ref, ns)
        ksn = jnp.concatenate([_rope(_rms(_head(ks_raw, kv), ksn_ref[...]), cs_ref[...], sn_ref[...], ROPE_DIM)
                               for kv in range(NSA_KV_HEADS)], axis=1)
        kso_ref[0] = ksn
        kb = jnp.concatenate([ksn, jnp.zeros((pad, NSA_KVD), F32)], axis=0).astype(BF16)
        vb = jnp.concatenate([vs_new, jnp.zeros((pad, NSA_KVD), F32)], axis=0).astype(BF16)
        row = lax.broadcasted_iota(jnp.int32, (NSA_GRP * ns, PAGE_SIZE), 0) % ns
        col = lax.broadcasted_iota(jnp.int32, (NSA_GRP * ns, PAGE_SIZE), 1)
        for kv in range(NSA_KV_HEADS):
            s = jnp.where(col <= row, _dot_nt(q_s[kv], _head(kb, kv)), NEG)
            _online_update(s, _head(vb, kv), m_s.at[kv], l_s.at[kv], acc_s.at[kv])
        q = _rows(q_ref, ns)
        outs_w, kwn, kw_out, vw_out = _window_step(
            q, _rows(kw_ref, ns), _rows(vw_ref, ns), kwb_ref[0, 0], vwb_ref[0, 0], qn_ref[...], kwn_ref[...],
            cs_ref[...], sn_ref[...], None, NSA_KV_HEADS)
        del kwn
        kwo_ref[0] = kw_out
        vwo_ref[0] = vw_out
        gates = jax.nn.sigmoid(_rows(gl_ref, ns))
        outs = []
        for kv in range(NSA_KV_HEADS):
            o_sel = acc_s[kv] / l_s[kv]
            o_cmp = ocmp_s[kv]
            for g in range(NSA_GRP):
                h = kv * NSA_GRP + g
                outs.append(gates[:, 3 * h:3 * h + 1] * o_cmp[g * ns:(g + 1) * ns]
                            + gates[:, 3 * h + 1:3 * h + 2] * o_sel[g * ns:(g + 1) * ns]
                            + gates[:, 3 * h + 2:3 * h + 3] * outs_w[h])
        o = jnp.concatenate(outs, axis=1)
        for t in range(ns):
            o_ref[t, 0] = o[t:t + 1]


def _nsa_decode(y, pools, kw_cache, vw_cache, li, page_table, cmp_wk, cmp_wv, q_norm, k_norm, cs, sn):
    ns, db, n = y.shape
    hd = N_HEADS * HEAD_DIM
    slots = PAGE_SLOTS
    n_pages = page_table.shape[1]
    npg = n_pages // slots
    past = n_pages * PAGE_SIZE
    nc = past // NSA_CMP_BLOCK
    base = hd // NSA_KVD
    w = kw_cache.shape[2]
    n_blocks = -(-(past + ns) // NSA_SEL_BLOCK)
    n_sel = min(NSA_TOPK, n_blocks)
    y4 = y.reshape(ns, db, 1, n)
    full = lambda a: pl.BlockSpec(a.shape, lambda b, g, pt: (0,) * a.ndim)

    def paged(phase, k):
        if phase == 0:
            grp_of = lambda g: jnp.minimum(g, npg - 1)
        else:
            grp_of = lambda g: jnp.maximum(g - npg, 0)
        return pl.BlockSpec((1, 1, PAGE_SIZE, NSA_KVD), lambda b, g, pt: (li, pt[b, grp_of(g) * slots + k], 0, 0))

    step = lambda width, cb: pl.BlockSpec((ns, 1, 1, width), lambda b, g, pt: (0, b, 0, cb))
    rows = slots * PAGE_SIZE
    expand = lambda wgt: jnp.tile(jnp.repeat(wgt, HEAD_DIM, axis=1), (rows // NSA_CMP_BLOCK, 1))
    kn = [k_norm[j:j + 1] for j in range(3)]
    seq_out = pl.BlockSpec((1, ns, NSA_KVD), lambda b, g, pt: (b, 0, 0))
    buf_out = pl.BlockSpec((1, w, NSA_KVD), lambda b, g, pt: (b, 0, 0))
    buf_in = pl.BlockSpec((1, 1, w, NSA_KVD), lambda b, g, pt: (li, b, 0, 0))
    grid_spec = pltpu.PrefetchScalarGridSpec(
        num_scalar_prefetch=1, grid=(db, 2 * npg),
        in_specs=([paged(0, k) for k in range(slots)] + [paged(0, k) for k in range(slots)]
                  + [paged(1, k) for k in range(slots)] + [paged(1, k) for k in range(slots)]
                  + [step(hd, 0), step(NSA_KVD, base + 6), step(NSA_KVD, base + 2), step(NSA_KVD, base + 3),
                     step(NSA_KVD, base + 4), step(NSA_KVD, base + 5), buf_in, buf_in,
                     pl.BlockSpec((rows, NSA_KVD), lambda b, g, pt: (0, 0)),
                     pl.BlockSpec((rows, NSA_KVD), lambda b, g, pt: (0, 0)),
                     full(q_norm), full(kn[0]), full(kn[1]), full(kn[2]), full(cs), full(sn)]),
        out_specs=[step(hd, 0), seq_out, buf_out, buf_out],
        scratch_shapes=[pltpu.VMEM((nc, NSA_KVD), F32), pltpu.VMEM((nc, NSA_KVD), F32),
                        pltpu.VMEM((NSA_KV_HEADS, ns, nc), F32),
                        pltpu.VMEM((NSA_KV_HEADS, NSA_GRP * ns, HEAD_DIM), F32),
                        pltpu.VMEM((NSA_KV_HEADS, NSA_GRP * ns, HEAD_DIM), BF16),
                        pltpu.VMEM((NSA_KV_HEADS, NSA_GRP * ns, 1), F32),
                        pltpu.VMEM((NSA_KV_HEADS, NSA_GRP * ns, 1), F32),
                        pltpu.VMEM((NSA_KV_HEADS, NSA_GRP * ns, HEAD_DIM), F32)])
    kc_pool, vc_pool, ks_pool, vs_pool = pools
    o, kso, kwo, vwo = pl.pallas_call(
        functools.partial(_nsa_decode_kernel, ns=ns, slots=slots, npg=npg, past=past, n_sel=n_sel),
        grid_spec=grid_spec,
        out_shape=[jax.ShapeDtypeStruct((ns, db, 1, hd), F32),
                   jax.ShapeDtypeStruct((db, ns, NSA_KVD), F32),
                   jax.ShapeDtypeStruct((db, w, NSA_KVD), F32),
                   jax.ShapeDtypeStruct((db, w, NSA_KVD), F32)],
        compiler_params=_cparams(("arbitrary", "arbitrary")),
    )(page_table, *([kc_pool] * slots), *([vc_pool] * slots), *([ks_pool] * slots), *([vs_pool] * slots),
      y4, y4, y4, y4, y4, y4, kw_cache, vw_cache, expand(cmp_wk), expand(cmp_wv),
      q_norm, kn[0], kn[1], kn[2], cs, sn)
    return o.reshape(ns, db, hd), kso, kwo, vwo


def kernel(x_prompt, x_sample, cache_swa_k, cache_swa_v, cache_mla_latent, cache_mla_krope, cache_nsa_kcmp, cache_nsa_vcmp, cache_nsa_ksel, cache_nsa_vsel, cache_nsa_kwin, cache_nsa_vwin, state_conv_ffn, page_table, c_prompt, c_sample, ada_w, ada_b, norm_mix, norm_ffn, ffn_w_up, ffn_conv, ffn_w_down, a_w_in, a_q_norm, a_k_norm, a_sinks, a_w_out, b_w_in, b_qa_norm, b_w_qb, b_q_norm_nope, b_q_norm_rope, b_kv_norm, b_krope_norm, b_w_uk, b_w_uv, b_w_out, c_w_in, c_q_norm, c_k_norm, c_cmp_wk, c_cmp_wv, c_w_out):
    nb, seq, d = x_prompt.shape
    db, ds, _ = x_sample.shape
    depth = ada_w.shape[0]
    ff = ffn_w_down.shape[1]
    past = page_table.shape[1] * PAGE_SIZE
    hd = N_HEADS * HEAD_DIM

    mod = _modulate(jnp.concatenate([c_prompt, c_sample], axis=0), ada_w, ada_b)
    tr_p = math.gcd(seq, 512)
    gp = _Group(True, mod[:, :, :nb].reshape(depth, 6, nb, 1, d), tr_p)
    gs = _Group(False, mod[:, :, nb:], db)
    norm_mix3 = norm_mix.reshape(depth, 1, d)
    norm_ffn3 = norm_ffn.reshape(depth, 1, d)

    pos_p = jnp.arange(seq, dtype=jnp.int32)
    pos_s = past + jnp.arange(ds, dtype=jnp.int32)
    cw_p, sw_p = _rope_tables(pos_p, ROPE_DIM, ROPE_THETA, HEAD_DIM)
    cw2_p, sw2_p = jnp.tile(cw_p, (1, LANES // HEAD_DIM)), jnp.tile(sw_p, (1, LANES // HEAD_DIM))
    cw_s, sw_s = _rope_tables(pos_s, ROPE_DIM, ROPE_THETA, HEAD_DIM)
    cm_p, sm_p = _rope_tables(pos_p, MLA_ROPE, MLA_THETA, MLA_ROPE)
    cm_s, sm_s = _rope_tables(pos_s, MLA_ROPE, MLA_THETA, MLA_ROPE)

    xp = x_prompt
    xs = x_sample.transpose(1, 0, 2)
    row2 = lambda v: v.reshape(1, -1)
    tm = lambda a: a.transpose(1, 0, 2)
    out = {k: [] for k in ("swa_k_p", "swa_v_p", "swa_k_s", "swa_v_s", "mla_c_p", "mla_r_p", "mla_c_s", "mla_r_s",
                           "conv_p", "conv_s")}
    nsa_p = [[] for _ in range(6)]
    nsa_s = [[] for _ in range(6)]
    ia = ib = ic = 0
    for layer in range(depth):
        kind = layer % N_MIXERS
        if kind == 0:
            w_in = a_w_in[ia].astype(BF16)
            qn, kn, sinks = row2(a_q_norm[ia]), row2(a_k_norm[ia]), row2(a_sinks[ia])
            yp = _proj(xp, gp, norm_mix3, layer, w_in)
            op, kp, vp = _band_attn(yp, qn, kn, sinks, cw2_p, sw2_p)
            ys = _proj(xs, gs, norm_mix3, layer, w_in)
            kvd = SWA_KV_HEADS * HEAD_DIM
            os_, ks_, vs_ = _step_attn(ys, cache_swa_k.reshape(cache_swa_k.shape[:3] + (kvd,)),
                                       cache_swa_v.reshape(cache_swa_v.shape[:3] + (kvd,)), ia,
                                       qn, kn, sinks, cw_s, sw_s)
            kv4 = lambda a: a.reshape(a.shape[0], a.shape[1], SWA_KV_HEADS, HEAD_DIM)
            out["swa_k_p"].append(kv4(kp)); out["swa_v_p"].append(kv4(vp))
            out["swa_k_s"].append(kv4(ks_)); out["swa_v_s"].append(kv4(vs_))
            w_out = a_w_out[ia].astype(BF16)
            ia += 1
        elif kind == 1:
            w_in = b_w_in[ib].astype(BF16)
            wqb = b_w_qb[ib].reshape(MLA_Q_LORA, N_HEADS, MLA_NOPE + MLA_ROPE)
            wqb = jnp.concatenate([wqb[:, :, :MLA_NOPE].reshape(MLA_Q_LORA, -1),
                                   wqb[:, :, MLA_NOPE:].reshape(MLA_Q_LORA, -1)], axis=1).astype(BF16)
            wuk = b_w_uk[ib].transpose(1, 2, 0).astype(BF16)
            wuv = b_w_uv[ib].transpose(1, 0, 2).astype(BF16)
            norms = (row2(b_qa_norm[ib]), wqb, row2(b_kv_norm[ib]), row2(b_krope_norm[ib]))
            nn, nr = row2(b_q_norm_nope[ib]), row2(b_q_norm_rope[ib])
            qp, cp, rp = _mla_proj(xp, gp, norm_mix3, layer, w_in, *norms, cm_p[None], sm_p[None])
            op = _mla_attn(qp, cp, rp, cm_p, sm_p, nn, nr, wuk, wuv, math.gcd(seq, 128), math.gcd(seq, 256), 256)
            qs, cs_, rs_ = _mla_proj(xs, gs, norm_mix3, layer, w_in, *norms, cm_s[:, None], sm_s[:, None])
            os_ = _mla_decode(qs, cs_, rs_, cache_mla_latent, cache_mla_krope, ib, page_table,
                              cm_s, sm_s, nn, nr, wuk, wuv)
            out["mla_c_p"].append(cp); out["mla_r_p"].append(rp)
            out["mla_c_s"].append(tm(cs_)); out["mla_r_s"].append(tm(rs_))
            w_out = b_w_out[ib].astype(BF16)
            ib += 1
        else:
            n_in = c_w_in.shape[2]
            n_pad = -(-n_in // 128) * 128
            w_in = jnp.pad(c_w_in[ic], ((0, 0), (0, n_pad - n_in))).astype(BF16)
            qn, kn3 = row2(c_q_norm[ic]), c_k_norm[ic]
            kv4 = lambda a: a.reshape(a.shape[0], a.shape[1], NSA_KV_HEADS, HEAD_DIM)
            col = lambda a, j: a[:, :, hd + j * NSA_KVD:hd + (j + 1) * NSA_KVD]
            yp = _proj(xp, gp, norm_mix3, layer, w_in)
            ksn, kwn, kcmp, vcmp = _nsa_prep(yp, c_cmp_wk[ic], c_cmp_wv[ic], kn3, cw_p, sw_p)
            op = _nsa_attn(yp, ksn, kwn, kcmp, vcmp, qn, cw2_p, sw2_p, math.gcd(seq, 256))
            for j, a in enumerate((col(yp, 0), col(yp, 1), ksn, col(yp, 3), kwn[:, -WINDOW:],
                                   col(yp, 5)[:, -WINDOW:])):
                nsa_p[j].append(kv4(a))
            ys = _proj(xs, gs, norm_mix3, layer, w_in)
            pools = tuple(p.reshape(p.shape[:3] + (NSA_KVD,))
                          for p in (cache_nsa_kcmp, cache_nsa_vcmp, cache_nsa_ksel, cache_nsa_vsel))
            os_, kso, kwo, vwo = _nsa_decode(
                ys, pools, cache_nsa_kwin.reshape(cache_nsa_kwin.shape[:3] + (NSA_KVD,)),
                cache_nsa_vwin.reshape(cache_nsa_vwin.shape[:3] + (NSA_KVD,)), ic, page_table,
                c_cmp_wk[ic], c_cmp_wv[ic], qn, kn3, cw_s, sw_s)
            for j, a in enumerate((tm(col(ys, 0)), tm(col(ys, 1)), kso, tm(col(ys, 3)), kwo, vwo)):
                nsa_s[j].append(kv4(a))
            w_out = c_w_out[ic].astype(BF16)
            ic += 1
        xp = _outproj(op, xp, gp, layer, w_out)
        xs = _outproj(os_, xs, gs, layer, w_out)
        wup, wd = ffn_w_up[layer].astype(BF16), ffn_w_down[layer].astype(BF16)
        tf = math.gcd(ff, 256)
        xp, bp = _ffn_prompt(xp, gp, norm_ffn3, layer, wup, ffn_conv[layer], wd, math.gcd(seq, 1024), tf)
        xs, bs = _ffn_sample(xs, gs, norm_ffn3, layer, wup, ffn_conv[layer], wd, state_conv_ffn[layer], tf)
        out["conv_p"].append(bp); out["conv_s"].append(bs)

    st = lambda xs_: jnp.stack(xs_)
    return (xp, xs.transpose(1, 0, 2),
            st(out["swa_k_p"]), st(out["swa_v_p"]), st(out["mla_c_p"]), st(out["mla_r_p"]),
            *[st(a) for a in nsa_p], st(out["conv_p"]),
            st(out["swa_k_s"]), st(out["swa_v_s"]), st(out["mla_c_s"]), st(out["mla_r_s"]),
            *[st(a) for a in nsa_s], st(out["conv_s"]))
```

```python
import functools
import math

import jax
import jax.numpy as jnp
from jax import lax
from jax.experimental import pallas as pl
from jax.experimental.pallas import tpu as pltpu

F32 = jnp.float32
BF16 = jnp.bfloat16

N_HEADS = 16
HEAD_DIM = 64
ROPE_DIM = 16
ROPE_THETA = 500000.0
EPS = 1e-6
PAGE_SIZE = 128
SWA_KV_HEADS = 4
WINDOW = 128
MLA_Q_LORA = 384
MLA_KV_LORA = 256
MLA_NOPE = 64
MLA_ROPE = 32
MLA_THETA = 10000.0
NSA_KV_HEADS = 2
NSA_CMP_BLOCK = 32
NSA_SEL_BLOCK = 64
NSA_TOPK = 16
N_MIXERS = 3
NEG = -1e30
TINY = float(jnp.finfo(jnp.float32).tiny)
VMEM_LIMIT = 56 * 1024 * 1024
PAGE_SLOTS = 8


def _cparams(sem):
    return pltpu.CompilerParams(dimension_semantics=sem, vmem_limit_bytes=VMEM_LIMIT)


def _dot(a, b):
    return jnp.dot(a, b, preferred_element_type=F32)


def _dot_nt(a, b):
    return lax.dot_general(a, b, (((1,), (1,)), ((), ())), preferred_element_type=F32)


def _rms(x, g):
    return x * lax.rsqrt(jnp.mean(x * x, axis=-1, keepdims=True) + EPS) * g


def _rope(x, c, s, rot):
    half = rot // 2
    parts = [x[:, half:rot], x[:, :half]]
    if x.shape[1] > rot:
        parts.append(x[:, rot:])
    return x * c + jnp.concatenate(parts, axis=1) * s


def _silu(x):
    return x * jax.nn.sigmoid(x)


def _norm_mod(x, g, sc, sh):
    return _rms(x, g) * (1.0 + sc) + sh


def _rope_tables(pos, rot, theta, width):
    half = rot // 2
    inv = jnp.power(jnp.float32(theta), -jnp.arange(half, dtype=F32) / half)
    ang = pos.astype(F32)[:, None] * inv[None, :]
    cos, sin = jnp.cos(ang), jnp.sin(ang)
    n = pos.shape[0]
    c = jnp.concatenate([cos, cos, jnp.ones((n, width - rot), F32)], axis=1)
    s = jnp.concatenate([-sin, sin, jnp.zeros((n, width - rot), F32)], axis=1)
    return c, s


class _Group:
    def __init__(self, per_group, mod, tr):
        self.pm = per_group
        self.mods = [mod[layer] for layer in range(mod.shape[0])]
        self.tr = tr

    def mod(self, layer):
        return self.mods[layer]

    def mod_spec(self, j, nargs=2):
        d = self.mods[0].shape[-1]
        if self.pm:
            shape, f = (1, 1, 1, d), (lambda g: (j, g, 0, 0))
        else:
            shape, f = (1, self.mods[0].shape[1], d), (lambda g: (j, 0, 0))
        if nargs == 2:
            return pl.BlockSpec(shape, lambda g, r: f(g))
        return pl.BlockSpec(shape, lambda g, r, k: f(g))


def _mod_val(ref, pm):
    return ref[0, 0] if pm else ref[0]


def _mod_kernel(c_ref, w_ref, b_ref, o_ref):
    a = _silu(c_ref[...])
    o_ref[0, 0] = _dot(a.astype(BF16), w_ref[0].astype(BF16)) + b_ref[0]


def _modulate(c_all, ada_w, ada_b):
    nl, d, d6 = ada_w.shape
    n = c_all.shape[0]
    nj = d6 // d
    return pl.pallas_call(
        _mod_kernel, grid=(nl, nj),
        in_specs=[pl.BlockSpec((n, d), lambda l, j: (0, 0)),
                  pl.BlockSpec((1, d, d), lambda l, j: (l, 0, j)),
                  pl.BlockSpec((1, 1, d), lambda l, j: (l, 0, j))],
        out_specs=pl.BlockSpec((1, 1, n, d), lambda l, j: (l, j, 0, 0)),
        out_shape=jax.ShapeDtypeStruct((nl, nj, n, d), F32),
        compiler_params=_cparams(("arbitrary", "arbitrary")),
    )(c_all, ada_w, ada_b.reshape(nl, 1, d6))


def _proj_kernel(x_ref, g_ref, sc_ref, sh_ref, w_ref, o_ref, *, pm):
    h = _norm_mod(x_ref[0], g_ref[0], _mod_val(sc_ref, pm), _mod_val(sh_ref, pm))
    o_ref[0] = _dot(h.astype(BF16), w_ref[...])


def _proj(x, grp, gains, layer, w_bf):
    ng, nr, d = x.shape
    n = w_bf.shape[1]
    tr = grp.tr
    return pl.pallas_call(
        functools.partial(_proj_kernel, pm=grp.pm), grid=(ng, nr // tr),
        in_specs=[pl.BlockSpec((1, tr, d), lambda g, r: (g, r, 0)),
                  pl.BlockSpec((1, 1, d), lambda g, r: (0, 0, 0)),
                  grp.mod_spec(1), grp.mod_spec(0),
                  pl.BlockSpec((d, n), lambda g, r: (0, 0))],
        out_specs=pl.BlockSpec((1, tr, n), lambda g, r: (g, r, 0)),
        out_shape=jax.ShapeDtypeStruct((ng, nr, n), F32),
        compiler_params=_cparams(("arbitrary", "arbitrary")),
    )(x, gains[layer:layer + 1], grp.mod(layer), grp.mod(layer), w_bf)


def _mla_proj_kernel(x_ref, g_ref, sc_ref, sh_ref, w_ref, qan_ref, wqb_ref, kvn_ref, krn_ref,
                     c_ref, s_ref, q_ref, lat_ref, kr_ref, *, pm):
    h = _norm_mod(x_ref[0], g_ref[0], _mod_val(sc_ref, pm), _mod_val(sh_ref, pm))
    y = _dot(h.astype(BF16), w_ref[...])
    a, b = MLA_Q_LORA, MLA_Q_LORA + MLA_KV_LORA
    qa = _rms(y[:, :a], qan_ref[...])
    q_ref[0] = _dot(qa.astype(BF16), wqb_ref[...])
    lat_ref[0] = _rms(y[:, a:b], kvn_ref[...])
    kr = _rms(y[:, b:], krn_ref[...])
    kr_ref[0] = _rope(kr, c_ref[0], s_ref[0], MLA_ROPE)


def _mla_proj(x, grp, gains, layer, w_bf, qa_norm, wqb_bf, kv_norm, krope_norm, cs, sn):
    ng, nr, d = x.shape
    tr = grp.tr
    nq = wqb_bf.shape[1]
    if grp.pm:
        tab = pl.BlockSpec((1, tr, MLA_ROPE), lambda g, r: (0, r, 0))
    else:
        tab = pl.BlockSpec((1, 1, MLA_ROPE), lambda g, r: (g, 0, 0))
    full = lambda a: pl.BlockSpec(a.shape, lambda g, r: (0,) * a.ndim)
    return pl.pallas_call(
        functools.partial(_mla_proj_kernel, pm=grp.pm), grid=(ng, nr // tr),
        in_specs=[pl.BlockSpec((1, tr, d), lambda g, r: (g, r, 0)),
                  pl.BlockSpec((1, 1, d), lambda g, r: (0, 0, 0)),
                  grp.mod_spec(1), grp.mod_spec(0),
                  full(w_bf), full(qa_norm), full(wqb_bf), full(kv_norm), full(krope_norm), tab, tab],
        out_specs=[pl.BlockSpec((1, tr, nq), lambda g, r: (g, r, 0)),
                   pl.BlockSpec((1, tr, MLA_KV_LORA), lambda g, r: (g, r, 0)),
                   pl.BlockSpec((1, tr, MLA_ROPE), lambda g, r: (g, r, 0))],
        out_shape=[jax.ShapeDtypeStruct((ng, nr, nq), F32),
                   jax.ShapeDtypeStruct((ng, nr, MLA_KV_LORA), F32),
                   jax.ShapeDtypeStruct((ng, nr, MLA_ROPE), F32)],
        compiler_params=_cparams(("arbitrary", "arbitrary")),
    )(x, gains[layer:layer + 1], grp.mod(layer), grp.mod(layer), w_bf, qa_norm, wqb_bf, kv_norm, krope_norm, cs, sn)


def _outproj_kernel(o_ref, x_ref, gt_ref, w_ref, y_ref, *, pm):
    y_ref[0] = x_ref[0] + _mod_val(gt_ref, pm) * _dot(o_ref[0].astype(BF16), w_ref[...])


def _outproj(o, x, grp, layer, w_bf):
    ng, nr, d = x.shape
    k = o.shape[-1]
    tr = grp.tr
    return pl.pallas_call(
        functools.partial(_outproj_kernel, pm=grp.pm), grid=(ng, nr // tr),
        in_specs=[pl.BlockSpec((1, tr, k), lambda g, r: (g, r, 0)),
                  pl.BlockSpec((1, tr, d), lambda g, r: (g, r, 0)),
                  grp.mod_spec(2),
                  pl.BlockSpec((k, d), lambda g, r: (0, 0))],
        out_specs=pl.BlockSpec((1, tr, d), lambda g, r: (g, r, 0)),
        out_shape=jax.ShapeDtypeStruct((ng, nr, d), F32),
        compiler_params=_cparams(("arbitrary", "arbitrary")),
    )(o, x, grp.mod(layer), w_bf)


HALO = 16


def _ffn_tail(mg, mv, wd_ref, x_ref, gt, y_ref):
    f = pl.program_id(2)
    d = _dot((_silu(mg) * mv).astype(BF16), wd_ref[...])

    @pl.when(f == 0)
    def _():
        y_ref[0] = d

    @pl.when(f > 0)
    def _():
        y_ref[0] += d

    @pl.when(f == pl.num_programs(2) - 1)
    def _():
        y_ref[0] = x_ref[0] + gt * y_ref[0]


def _ffn_prompt_kernel(x_ref, xh_ref, g_ref, sc_ref, sh_ref, gt_ref, wup_ref, cw_ref, wd_ref,
                       y_ref, b_ref, h_s, act_s, u_s, *, tr, tf):
    r = pl.program_id(1)
    ff = wd_ref.shape[0]
    g, sc, sh = g_ref[0], sc_ref[0, 0], sh_ref[0, 0]
    h_s[HALO:, :] = _norm_mod(x_ref[0], g, sc, sh).astype(BF16)
    hh = _norm_mod(xh_ref[0], g, sc, sh)
    h_s[:HALO, :] = jnp.where(r > 0, hh, 0.0).astype(BF16)
    nf = ff // tf
    cols = lambda half, f: pl.ds(pl.multiple_of(half * ff + f * tf, tf), tf)

    def up(f, slot):
        for half in range(2):
            u_s[slot, half] = _dot(h_s[...], wup_ref[:, cols(half, f)])

    def mixed(f, slot, half):
        u = u_s[slot, half]
        cw = cw_ref[:, cols(half, f)]
        b_ref[0, 0, :, cols(half, f)] = u[HALO + tr - 2:]
        a = pltpu.roll(u, 2, 0)[HALO:]
        b = pltpu.roll(u, 1, 0)[HALO:]
        return cw[0:1] * a + cw[1:2] * b + cw[2:3] * u[HALO:]

    def activate(f, slot):
        act_s[:, cols(0, f)] = (_silu(mixed(f, slot, 0)) * mixed(f, slot, 1)).astype(BF16)

    def body(it, carry):
        f = 2 * it
        up(f + 1, 1)
        activate(f, 0)
        up(f + 2, 0)
        activate(f + 1, 1)
        return carry

    assert nf % 2 == 1
    up(0, 0)
    lax.fori_loop(0, nf // 2, body, 0)
    activate(nf - 1, 0)
    y_ref[0] = x_ref[0] + gt_ref[0, 0] * _dot(act_s[...], wd_ref[...])


def _ffn_prompt(x, grp, gains, layer, wup_bf, conv_w, wd_bf, tr, tf):
    ng, nr, d = x.shape
    ff = wd_bf.shape[0]
    ms = lambda j: grp.mod_spec(j)
    once = lambda a: pl.BlockSpec(a.shape, lambda g, r: (0,) * a.ndim, pipeline_mode=pl.Buffered(1))
    y, buf = pl.pallas_call(
        functools.partial(_ffn_prompt_kernel, tr=tr, tf=tf), grid=(ng, nr // tr),
        in_specs=[pl.BlockSpec((1, tr, d), lambda g, r: (g, r, 0)),
                  pl.BlockSpec((1, HALO, d), lambda g, r: (g, jnp.maximum(r * (tr // HALO) - 1, 0), 0)),
                  pl.BlockSpec((1, 1, d), lambda g, r: (0, 0, 0)),
                  ms(4), ms(3), ms(5), once(wup_bf), once(conv_w), once(wd_bf)],
        out_specs=[pl.BlockSpec((1, tr, d), lambda g, r: (g, r, 0)),
                   pl.BlockSpec((1, 1, 2, 2 * ff), lambda g, r: (g, r, 0, 0))],
        out_shape=[jax.ShapeDtypeStruct((ng, nr, d), F32),
                   jax.ShapeDtypeStruct((ng, nr // tr, 2, 2 * ff), F32)],
        scratch_shapes=[pltpu.VMEM((tr + HALO, d), BF16), pltpu.VMEM((tr, ff), BF16),
                        pltpu.VMEM((2, 2, tr + HALO, tf), F32)],
        compiler_params=_cparams(("arbitrary", "arbitrary")),
    )(x, x, gains[layer:layer + 1], grp.mod(layer), grp.mod(layer), grp.mod(layer), wup_bf, conv_w, wd_bf)
    return y, buf[:, -1]


def _ffn_sample_kernel(x_ref, g_ref, sc_ref, sh_ref, gt_ref, sg_ref, sv_ref, wg_ref, wv_ref, cg_ref, cv_ref,
                       wd_ref, y_ref, bg_ref, bv_ref, h_s, *, nt, p):
    tile = lambda m: jnp.concatenate([m] * nt, axis=0)

    @pl.when(pl.program_id(2) == 0)
    def _():
        h_s[...] = _norm_mod(x_ref[0], g_ref[0], tile(sc_ref[0]), tile(sh_ref[0])).astype(BF16)

    h = h_s[...]
    n = nt * p

    def conv(st, u, cw):
        e = jnp.concatenate([st, u], axis=0)
        return cw[0:1] * e[0:n] + cw[1:2] * e[p:p + n] + cw[2:3] * e[2 * p:]

    ug = _dot(h, wg_ref[...])
    uv = _dot(h, wv_ref[...])
    bg_ref[...] = jnp.concatenate([sg_ref[...], ug], axis=0)[n:]
    bv_ref[...] = jnp.concatenate([sv_ref[...], uv], axis=0)[n:]
    _ffn_tail(conv(sg_ref[...], ug, cg_ref[...]), conv(sv_ref[...], uv, cv_ref[...]), wd_ref, x_ref,
              tile(gt_ref[0]), y_ref)


def _ffn_sample(x, grp, gains, layer, wup_bf, conv_w, wd_bf, state, tf):
    nt, p, d = x.shape
    ff = wd_bf.shape[0]
    nf = ff // tf
    n = nt * p
    st = state.transpose(1, 0, 2).reshape(2 * p, 2 * ff)
    ms = lambda j: grp.mod_spec(j, nargs=3)
    y, bg, bv = pl.pallas_call(
        functools.partial(_ffn_sample_kernel, nt=nt, p=p), grid=(1, 1, nf),
        in_specs=[pl.BlockSpec((1, n, d), lambda g, r, f: (0, 0, 0)),
                  pl.BlockSpec((1, 1, d), lambda g, r, f: (0, 0, 0)),
                  ms(4), ms(3), ms(5),
                  pl.BlockSpec((2 * p, tf), lambda g, r, f: (0, f)),
                  pl.BlockSpec((2 * p, tf), lambda g, r, f: (0, nf + f)),
                  pl.BlockSpec((d, tf), lambda g, r, f: (0, f)),
                  pl.BlockSpec((d, tf), lambda g, r, f: (0, nf + f)),
                  pl.BlockSpec((3, tf), lambda g, r, f: (0, f)),
                  pl.BlockSpec((3, tf), lambda g, r, f: (0, nf + f)),
                  pl.BlockSpec((tf, d), lambda g, r, f: (f, 0))],
        out_specs=[pl.BlockSpec((1, n, d), lambda g, r, f: (0, 0, 0)),
                   pl.BlockSpec((2 * p, tf), lambda g, r, f: (0, f)),
                   pl.BlockSpec((2 * p, tf), lambda g, r, f: (0, f))],
        out_shape=[jax.ShapeDtypeStruct((1, n, d), F32),
                   jax.ShapeDtypeStruct((2 * p, ff), F32),
                   jax.ShapeDtypeStruct((2 * p, ff), F32)],
        scratch_shapes=[pltpu.VMEM((n, d), BF16)],
        compiler_params=_cparams(("arbitrary", "arbitrary", "arbitrary")),
    )(x.reshape(1, n, d), gains[layer:layer + 1], grp.mod(layer), grp.mod(layer), grp.mod(layer), st, st, wup_bf, wup_bf, conv_w, conv_w, wd_bf)
    new_state = jnp.concatenate([bg, bv], axis=-1).reshape(2, p, 2 * ff).transpose(1, 0, 2)
    return y.reshape(nt, p, d), new_state


def _sink_col(sink_ref, heads, nq):
    return jnp.concatenate([jnp.broadcast_to(sink_ref[:, h:h + 1], (nq, 1)) for h in heads], axis=0)


def _band_mask(nq, w, prev_off):
    qi = lax.broadcasted_iota(jnp.int32, (nq, w + nq), 0)
    kj = lax.broadcasted_iota(jnp.int32, (nq, w + nq), 1)
    return ((kj < w) & (kj >= qi + prev_off)) | ((kj >= w) & ((kj - w) <= qi))


def _head(x, h):
    return x[:, h * HEAD_DIM:(h + 1) * HEAD_DIM]


def _band_attn_kernel(q_ref, kp_ref, kc_ref, vp_ref, vc_ref, qg_ref, kg_ref, cq_ref, sq_ref, cp_ref, sp_ref,
                      sink_ref, o_ref, ko_ref, vo_ref, q_s, m_s, acc_s, *, n_kv, chunk):
    i = pl.program_id(1)
    w = q_ref.shape[1]
    rows_kv = (N_HEADS // n_kv) * w
    seg = _seg_ones(2 * LANES)
    cq, sq = cq_ref[...], sq_ref[...]
    _stack_heads(_heads_rope(_heads_rms(q_ref[0], qg_ref[...], seg), cq, sq) * HEAD_DIM ** -0.5, q_s)
    kc = _heads_rope(_heads_rms(kc_ref[0], kg_ref[...], seg), cq, sq)
    kp = _heads_rope(_heads_rms(kp_ref[0], kg_ref[...], seg), cp_ref[...], sp_ref[...])
    kk = jnp.concatenate([kp, kc], axis=0).astype(BF16)
    vv = jnp.concatenate([vp_ref[0], vc_ref[0]], axis=0).astype(BF16)
    kvs = [([_head(kk, kv)], _with_ones(_head(vv, kv))) for kv in range(n_kv)]
    for h in range(N_HEADS):
        m_s[h * w:(h + 1) * w, :] = jnp.broadcast_to(sink_ref[:, h:h + 1], (w, LANES))
    acc_s[...] = jnp.concatenate([jnp.zeros((N_HEADS * w, HEAD_DIM), F32), jnp.ones((N_HEADS * w, HEAD_DIM), F32)],
                                 axis=1)
    mask = _band_mask(w, w, jnp.where(i > 0, 0, w))
    mk = jnp.concatenate([mask] * (chunk // w), axis=0)
    _flash_step([q_s], lambda c: kvs[c * chunk // rows_kv], lambda c: mk, m_s, acc_s, chunk)
    o = acc_s[:, :HEAD_DIM] / acc_s[:, HEAD_DIM:]
    o_ref[0] = _unstack_heads(o, w).astype(o_ref.dtype)
    ko_ref[0] = kc
    vo_ref[0] = vc_ref[0]


def _band_attn(y, q_norm, k_norm, sinks, cs, sn):
    b, s, _ = y.shape
    w = WINDOW
    kvd = SWA_KV_HEADS * HEAD_DIM
    hd = N_HEADS * HEAD_DIM
    kb, vb = hd // kvd, hd // kvd + 1
    prev = lambda i: jnp.maximum(i - 1, 0)
    full = lambda a: pl.BlockSpec(a.shape, lambda bb, i: (0,) * a.ndim)
    tab_c = pl.BlockSpec((w, LANES), lambda bb, i: (i, 0))
    tab_p = pl.BlockSpec((w, LANES), lambda bb, i: (prev(i), 0))
    q_norm, k_norm = jnp.tile(q_norm, (1, N_HEADS)), jnp.tile(k_norm, (1, SWA_KV_HEADS))
    return pl.pallas_call(
        functools.partial(_band_attn_kernel, n_kv=SWA_KV_HEADS, chunk=2 * w), grid=(b, s // w),
        scratch_shapes=[pltpu.VMEM((N_HEADS * w, HEAD_DIM), BF16),
                        pltpu.VMEM((N_HEADS * w, LANES), F32),
                        pltpu.VMEM((N_HEADS * w, 2 * HEAD_DIM), F32)],
        in_specs=[pl.BlockSpec((1, w, hd), lambda bb, i: (bb, i, 0)),
                  pl.BlockSpec((1, w, kvd), lambda bb, i: (bb, prev(i), kb)),
                  pl.BlockSpec((1, w, kvd), lambda bb, i: (bb, i, kb)),
                  pl.BlockSpec((1, w, kvd), lambda bb, i: (bb, prev(i), vb)),
                  pl.BlockSpec((1, w, kvd), lambda bb, i: (bb, i, vb)),
                  full(q_norm), full(k_norm), tab_c, tab_c, tab_p, tab_p, full(sinks)],
        out_specs=[pl.BlockSpec((1, w, hd), lambda bb, i: (bb, i, 0)),
                   pl.BlockSpec((1, w, kvd), lambda bb, i: (bb, 0, 0)),
                   pl.BlockSpec((1, w, kvd), lambda bb, i: (bb, 0, 0))],
        out_shape=[jax.ShapeDtypeStruct((b, s, hd), BF16),
                   jax.ShapeDtypeStruct((b, w, kvd), F32),
                   jax.ShapeDtypeStruct((b, w, kvd), F32)],
        compiler_params=_cparams(("arbitrary", "arbitrary")),
    )(y, y, y, y, y, q_norm, k_norm, cs, sn, cs, sn, sinks)


def _rows(ref, n):
    return jnp.concatenate([ref[t, 0] for t in range(n)], axis=0)


def _window_step(q, k_new, v_new, kbt, vbt, qn, kn, cs, sn, sink_ref, n_kv):
    ns, w = q.shape[0], kbt[0].shape[1]
    grp = N_HEADS // n_kv
    scale = HEAD_DIM ** -0.5
    t_q = lax.broadcasted_iota(jnp.int32, (grp * ns, w), 0) % ns
    col = lax.broadcasted_iota(jnp.int32, (grp * ns, w), 1)
    see_buf = col >= t_q
    see_new = col <= t_q
    pad = lambda a: jnp.concatenate([a, jnp.zeros((w - ns, a.shape[1]), a.dtype)], axis=0)
    outs, knew = [], []
    for kv in range(n_kv):
        kn_h = _rope(_rms(_head(k_new, kv), kn), cs, sn, ROPE_DIM)
        knew.append(kn_h)
        heads = range(kv * grp, (kv + 1) * grp)
        qg = jnp.concatenate([_rope(_rms(_head(q, h), qn), cs, sn, ROPE_DIM) * scale for h in heads],
                             axis=0).astype(BF16)
        s1 = jnp.where(see_buf, _dot(qg, kbt[kv].astype(BF16)), NEG)
        s2 = jnp.where(see_new, _dot_nt(qg, pad(kn_h).astype(BF16)), NEG)
        m = jnp.maximum(jnp.max(s1, axis=-1, keepdims=True), jnp.max(s2, axis=-1, keepdims=True))
        if sink_ref is not None:
            sink = _sink_col(sink_ref, heads, ns)
            m = jnp.maximum(m, sink)
        p1, p2 = jnp.exp(s1 - m), jnp.exp(s2 - m)
        den = jnp.sum(p1, axis=-1, keepdims=True) + jnp.sum(p2, axis=-1, keepdims=True)
        if sink_ref is not None:
            den = den + jnp.exp(sink - m)
        o = (_dot_nt(p1.astype(BF16), vbt[kv].astype(BF16))
             + _dot(p2.astype(BF16), pad(_head(v_new, kv)).astype(BF16))) / den
        outs += [o[g * ns:(g + 1) * ns] for g in range(grp)]

    def shifted(bt, new):
        new_t = jnp.concatenate([jnp.zeros((w - ns, new.shape[1]), F32), new], axis=0).T
        lane = lax.broadcasted_iota(jnp.int32, (HEAD_DIM, w), 1)
        return [jnp.where(lane >= w - ns, new_t[kv * HEAD_DIM:(kv + 1) * HEAD_DIM], pltpu.roll(bt[kv], w - ns, 1))
                for kv in range(n_kv)]

    return outs, shifted(kbt, jnp.concatenate(knew, axis=1)), shifted(vbt, v_new)


def _step_attn_kernel(q_ref, k_ref, v_ref, kb_ref, vb_ref, qn_ref, kn_ref, cs_ref, sn_ref, sink_ref,
                      o_ref, ko_ref, vo_ref, *, n_kv):
    ns = q_ref.shape[0]
    outs, k_out, v_out = _window_step(_rows(q_ref, ns), _rows(k_ref, ns), _rows(v_ref, ns),
                                      [kb_ref[0, 0, kv] for kv in range(n_kv)],
                                      [vb_ref[0, 0, kv] for kv in range(n_kv)], qn_ref[...], kn_ref[...],
                                      cs_ref[...], sn_ref[...], sink_ref, n_kv)
    o = jnp.concatenate(outs, axis=1)
    for t in range(ns):
        o_ref[t, 0] = o[t:t + 1]
    for kv in range(n_kv):
        ko_ref[0, kv] = k_out[kv]
        vo_ref[0, kv] = v_out[kv]


def _row_minor(cache):
    return cache.transpose(0, 1, 3, 4, 2)


def _step_attn(y, k_cache, v_cache, li, q_norm, k_norm, sinks, cs, sn):
    ns, db, n = y.shape
    w, n_kv = k_cache.shape[2], k_cache.shape[3]
    kvd = n_kv * HEAD_DIM
    hd = N_HEADS * HEAD_DIM
    kb, vb = hd // kvd, hd // kvd + 1
    y4 = y.reshape(ns, db, 1, n)
    full = lambda a: pl.BlockSpec(a.shape, lambda b: (0,) * a.ndim)
    buf_in = pl.BlockSpec((1, 1, n_kv, HEAD_DIM, w), lambda b: (li, b, 0, 0, 0))
    buf_out = pl.BlockSpec((1, n_kv, HEAD_DIM, w), lambda b: (b, 0, 0, 0))
    o, ko, vo = pl.pallas_call(
        functools.partial(_step_attn_kernel, n_kv=n_kv), grid=(db,),
        in_specs=[pl.BlockSpec((ns, 1, 1, hd), lambda b: (0, b, 0, 0)),
                  pl.BlockSpec((ns, 1, 1, kvd), lambda b: (0, b, 0, kb)),
                  pl.BlockSpec((ns, 1, 1, kvd), lambda b: (0, b, 0, vb)),
                  buf_in, buf_in,
                  full(q_norm), full(k_norm), full(cs), full(sn), full(sinks)],
        out_specs=[pl.BlockSpec((ns, 1, 1, hd), lambda b: (0, b, 0, 0)), buf_out, buf_out],
        out_shape=[jax.ShapeDtypeStruct((ns, db, 1, hd), F32),
                   jax.ShapeDtypeStruct((db, n_kv, HEAD_DIM, w), F32),
                   jax.ShapeDtypeStruct((db, n_kv, HEAD_DIM, w), F32)],
        compiler_params=_cparams(("arbitrary",)),
    )(y4, y4, y4, _row_minor(k_cache), _row_minor(v_cache), q_norm, k_norm, cs, sn, sinks)
    back = lambda a: a.transpose(0, 3, 1, 2)
    return o.reshape(ns, db, hd), back(ko), back(vo)


LANES = 128


def _lanes(x, n):
    if n <= LANES:
        return x[:, :n]
    return jnp.concatenate([x] * (n // LANES), axis=1)


def _flash_step(q_refs, kv_fn, mask_fn, m_ref, acc_ref, chunk):
    n = m_ref.shape[0] // chunk

    def scores(c):
        rows = pl.ds(c * chunk, chunk)
        k_parts = kv_fn(c)[0]
        s = _dot_nt(q_refs[0][rows, :], k_parts[0])
        for qr, kp in zip(q_refs[1:], k_parts[1:]):
            s = s + _dot_nt(qr[rows, :], kp)
        mk = None if mask_fn is None else mask_fn(c)
        return s if mk is None else jnp.where(mk, s, NEG)

    s = scores(0)
    for c in range(n):
        s_next = scores(c + 1) if c + 1 < n else None
        rows = pl.ds(c * chunk, chunk)
        vals = kv_fn(c)[1]
        m_old = m_ref[rows, :]
        m_new = jnp.maximum(m_old, jnp.max(s, axis=-1, keepdims=True))
        alpha = jnp.exp(m_old - m_new)
        p = jnp.exp(s - _lanes(m_new, s.shape[1])).astype(BF16)
        acc_ref[rows, :] = _lanes(alpha, vals.shape[1]) * acc_ref[rows, :] + _dot(p, vals)
        m_ref[rows, :] = m_new
        s = s_next


def _seg_ones(n):
    r = lax.broadcasted_iota(jnp.int32, (n, n), 0) // HEAD_DIM
    c = lax.broadcasted_iota(jnp.int32, (n, n), 1) // HEAD_DIM
    return jnp.where(r == c, 1.0, 0.0).astype(BF16)


def _heads_rms(x, g, seg):
    sq = x * x
    hi = sq.astype(BF16)
    lo = (sq - hi.astype(F32)).astype(BF16)
    n = seg.shape[0]
    parts = []
    for j in range(0, x.shape[1], n):
        k = min(n, x.shape[1] - j)
        b = seg[:k, :k]
        parts.append(_dot(hi[:, j:j + k], b) + _dot(lo[:, j:j + k], b))
    ss = parts[0] if len(parts) == 1 else jnp.concatenate(parts, axis=1)
    return x * lax.rsqrt(ss * (1.0 / HEAD_DIM) + EPS) * g


def _heads_rope(x, c2, s2):
    w = x.shape[1]
    half = ROPE_DIM // 2
    lane = lax.broadcasted_iota(jnp.int32, x.shape, 1) % HEAD_DIM
    sw = jnp.where(lane < half, pltpu.roll(x, w - half, 1), pltpu.roll(x, half, 1))
    return x * _lanes(c2, w) + sw * _lanes(s2, w)


def _stack_heads(x, q_s):
    nq = x.shape[0]
    for h in range(x.shape[1] // HEAD_DIM):
        q_s[h * nq:(h + 1) * nq, :] = _head(x, h).astype(q_s.dtype)


def _unstack_heads(o, nq):
    return jnp.concatenate([o[h * nq:(h + 1) * nq] for h in range(o.shape[0] // nq)], axis=1)


def _flash_init(m_ref, acc_ref):
    m_ref[...] = jnp.full(m_ref.shape, NEG, F32)
    acc_ref[...] = jnp.zeros(acc_ref.shape, F32)


def _with_ones(v):
    pad = LANES - v.shape[1] % LANES
    return jnp.concatenate([v, jnp.ones((v.shape[0], pad), v.dtype)], axis=1)


def _mla_queries(q, nn, nr, cs, sn, wuk_ref, ql_s, qr_s):
    nq = q.shape[0]
    scale = (MLA_NOPE + MLA_ROPE) ** -0.5
    off = N_HEADS * MLA_NOPE
    for h in range(N_HEADS):
        qn = _rms(q[:, h * MLA_NOPE:(h + 1) * MLA_NOPE], nn)
        ql_s[h * nq:(h + 1) * nq, :] = (_dot(qn.astype(BF16), wuk_ref[h]) * scale).astype(BF16)
        qr = _rms(q[:, off + h * MLA_ROPE:off + (h + 1) * MLA_ROPE], nr)
        qr_s[h * nq:(h + 1) * nq, :] = (_rope(qr, cs, sn, MLA_ROPE) * scale).astype(BF16)


def _mla_output(acc, den, wuv_ref, nq):
    o_lat = (acc / den).astype(BF16)
    return jnp.concatenate([_dot(o_lat[h * nq:(h + 1) * nq], wuv_ref[h]) for h in range(N_HEADS)], axis=1)


def _mla_attn_kernel(q_ref, c_ref, kr_ref, cs_ref, sn_ref, nn_ref, nr_ref, wuk_ref, wuv_ref, o_ref,
                     ql_s, qr_s, m_s, acc_s, cb_s, kb_s, *, tq, tk, chunk):
    i = pl.program_id(1)

    @pl.when(i == 0)
    def _():
        cb_s[...] = _with_ones(c_ref[0].astype(BF16))
        kb_s[...] = kr_ref[0].astype(BF16)

    _mla_queries(q_ref[0], nn_ref[...], nr_ref[...], cs_ref[...], sn_ref[...], wuk_ref, ql_s, qr_s)
    _flash_init(m_s, acc_s)
    t_pos = i * tq + lax.broadcasted_iota(jnp.int32, (tq, tk), 0)
    k_off = lax.broadcasted_iota(jnp.int32, (tq, tk), 1)

    def step(j, masked):
        start = pl.multiple_of(j * tk, tk)
        cbx = cb_s[pl.ds(start, tk), :]
        kb = kb_s[pl.ds(start, tk), :]
        mask_fn = None
        if masked:
            mk = jnp.concatenate([(start + k_off) <= t_pos] * (chunk // tq), axis=0)
            mask_fn = lambda c: mk
        kv = ([cbx[:, :MLA_KV_LORA], kb], cbx)
        _flash_step([ql_s, qr_s], lambda c: kv, mask_fn, m_s, acc_s, chunk)

    def full_body(j, carry):
        step(j, False)
        return carry

    def diag_body(j, carry):
        step(j, True)
        return carry

    n_full = (i * tq) // tk
    lax.fori_loop(0, n_full, full_body, 0)
    lax.fori_loop(n_full, (i * tq + tq - 1) // tk + 1, diag_body, 0)
    o_ref[0] = _mla_output(acc_s[:, :MLA_KV_LORA], _lanes(acc_s[:, MLA_KV_LORA:], MLA_KV_LORA), wuv_ref,
                           tq).astype(o_ref.dtype)


def _mla_attn(q, c, kr, cs, sn, nn, nr, wuk, wuv, tq, tk, chunk):
    b, s, nqc = q.shape
    hd = N_HEADS * HEAD_DIM
    full = lambda a: pl.BlockSpec(a.shape, lambda bb, i: (0,) * a.ndim)
    tab = pl.BlockSpec((tq, MLA_ROPE), lambda bb, i: (i, 0))
    return pl.pallas_call(
        functools.partial(_mla_attn_kernel, tq=tq, tk=tk, chunk=chunk), grid=(b, s // tq),
        in_specs=[pl.BlockSpec((1, tq, nqc), lambda bb, i: (bb, i, 0)),
                  pl.BlockSpec((1, s, MLA_KV_LORA), lambda bb, i: (bb, 0, 0)),
                  pl.BlockSpec((1, s, MLA_ROPE), lambda bb, i: (bb, 0, 0)),
                  tab, tab, full(nn), full(nr), full(wuk), full(wuv)],
        out_specs=pl.BlockSpec((1, tq, hd), lambda bb, i: (bb, i, 0)),
        out_shape=jax.ShapeDtypeStruct((b, s, hd), BF16),
        scratch_shapes=[pltpu.VMEM((N_HEADS * tq, MLA_KV_LORA), BF16),
                        pltpu.VMEM((N_HEADS * tq, MLA_ROPE), BF16),
                        pltpu.VMEM((N_HEADS * tq, LANES), F32),
                        pltpu.VMEM((N_HEADS * tq, MLA_KV_LORA + LANES), F32),
                        pltpu.VMEM((s, MLA_KV_LORA + LANES), BF16),
                        pltpu.VMEM((s, MLA_ROPE), BF16)],
        compiler_params=_cparams(("arbitrary", "arbitrary")),
    )(q, c, kr, cs, sn, nn, nr, wuk, wuv)


def _online_step(s, pv_fn, m_ref, acc_ref):
    m_old = m_ref[...]
    m_new = jnp.maximum(m_old, jnp.max(s, axis=-1, keepdims=True))
    alpha = jnp.exp(m_old - m_new)
    p = jnp.exp(s - _lanes(m_new, s.shape[1])).astype(BF16)
    acc_ref[...] = _lanes(alpha, acc_ref.shape[1]) * acc_ref[...] + pv_fn(p)
    m_ref[...] = m_new


def _mla_decode_kernel(pt_ref, *refs, ns, slots):
    del pt_ref
    lat = refs[:slots]
    krp = refs[slots:2 * slots]
    (q_ref, cn_ref, kn_ref, cs_ref, sn_ref, nn_ref, nr_ref, wuk_ref, wuv_ref, o_ref,
     ql_s, qr_s, m_s, acc_s) = refs[2 * slots:]
    g = pl.program_id(1)

    @pl.when(g == 0)
    def _():
        _mla_queries(_rows(q_ref, ns), nn_ref[...], nr_ref[...], cs_ref[...], sn_ref[...], wuk_ref, ql_s, qr_s)
        _flash_init(m_s, acc_s)

    cbx = _with_ones(jnp.concatenate([r[0, 0] for r in lat], axis=0).astype(BF16))
    krt = jnp.concatenate([r[0, 0] for r in krp], axis=1).astype(BF16)
    _online_step(_dot_nt(ql_s[...], cbx[:, :MLA_KV_LORA]) + _dot(qr_s[...], krt), lambda p: _dot(p, cbx), m_s, acc_s)

    @pl.when(g == pl.num_programs(1) - 1)
    def _():
        pad = PAGE_SIZE - ns
        cb = jnp.concatenate([_rows(cn_ref, ns), jnp.zeros((pad, MLA_KV_LORA), F32)], axis=0).astype(BF16)
        kb = jnp.concatenate([_rows(kn_ref, ns), jnp.zeros((pad, MLA_ROPE), F32)], axis=0).astype(BF16)
        s = _dot_nt(ql_s[...], cb) + _dot_nt(qr_s[...], kb)
        row = lax.broadcasted_iota(jnp.int32, s.shape, 0) % ns
        col = lax.broadcasted_iota(jnp.int32, s.shape, 1)
        cbx_new = _with_ones(cb)
        _online_step(jnp.where(col <= row, s, NEG), lambda p: _dot(p, cbx_new), m_s, acc_s)
        o = _mla_output(acc_s[:, :MLA_KV_LORA], _lanes(acc_s[:, MLA_KV_LORA:], MLA_KV_LORA), wuv_ref, ns)
        for t in range(ns):
            o_ref[t, 0] = o[t:t + 1]


def _mla_decode(q, c, kr, lat_pool, kr_pool, li, page_table, cs, sn, nn, nr, wuk, wuv):
    ns, db, nqc = q.shape
    hd = N_HEADS * HEAD_DIM
    slots = PAGE_SLOTS
    npg = page_table.shape[1] // slots
    full = lambda a: pl.BlockSpec(a.shape, lambda b, g, pt: (0,) * a.ndim)
    page = lambda k: (lambda b, g, pt: (li, pt[b, g * slots + k], 0, 0))
    lat_spec = lambda k: pl.BlockSpec((1, 1, PAGE_SIZE, MLA_KV_LORA), page(k))
    kr_spec = lambda k: pl.BlockSpec((1, 1, MLA_ROPE, PAGE_SIZE), page(k))
    step = lambda width: pl.BlockSpec((ns, 1, 1, width), lambda b, g, pt: (0, b, 0, 0))
    grid_spec = pltpu.PrefetchScalarGridSpec(
        num_scalar_prefetch=1, grid=(db, npg),
        in_specs=([lat_spec(k) for k in range(slots)] + [kr_spec(k) for k in range(slots)]
                  + [step(nqc), step(MLA_KV_LORA), step(MLA_ROPE),
                     full(cs), full(sn), full(nn), full(nr), full(wuk), full(wuv)]),
        out_specs=step(hd),
        scratch_shapes=[pltpu.VMEM((N_HEADS * ns, MLA_KV_LORA), BF16),
                        pltpu.VMEM((N_HEADS * ns, MLA_ROPE), BF16),
                        pltpu.VMEM((N_HEADS * ns, LANES), F32),
                        pltpu.VMEM((N_HEADS * ns, MLA_KV_LORA + LANES), F32)])
    kr_pool_t = kr_pool.transpose(0, 1, 3, 2)
    o = pl.pallas_call(
        functools.partial(_mla_decode_kernel, ns=ns, slots=slots), grid_spec=grid_spec,
        out_shape=jax.ShapeDtypeStruct((ns, db, 1, hd), F32),
        compiler_params=_cparams(("arbitrary", "arbitrary")),
    )(page_table, *([lat_pool] * slots), *([kr_pool_t] * slots),
      q.reshape(ns, db, 1, nqc), c.reshape(ns, db, 1, MLA_KV_LORA), kr.reshape(ns, db, 1, MLA_ROPE),
      cs, sn, nn, nr, wuk, wuv)
    return o.reshape(ns, db, hd)


NSA_KVD = NSA_KV_HEADS * HEAD_DIM
NSA_GRP = N_HEADS // NSA_KV_HEADS
NSA_PAIR = NSA_SEL_BLOCK // NSA_CMP_BLOCK


def _nsa_prep_kernel(kc_ref, vc_ref, ks_ref, kw_ref, wk_ref, wv_ref, kcn_ref, ksn_ref, kwn_ref, cs_ref, sn_ref,
                     kso_ref, kwo_ref, kcmp_ref, vcmp_ref):
    cs, sn = cs_ref[...], sn_ref[...]
    kc, vc, ks, kw = kc_ref[0], vc_ref[0], ks_ref[0], kw_ref[0]
    kso, kwo, kcmp, vcmp = [], [], [], []
    for kv in range(NSA_KV_HEADS):
        kso.append(_rope(_rms(_head(ks, kv), ksn_ref[...]), cs, sn, ROPE_DIM))
        kwo.append(_rope(_rms(_head(kw, kv), kwn_ref[...]), cs, sn, ROPE_DIM))
        kcmp.append(_rms(_dot(wk_ref[kv], _head(kc, kv).astype(BF16)), kcn_ref[...]))
        vcmp.append(_dot(wv_ref[kv], _head(vc, kv).astype(BF16)))
    kso_ref[0] = jnp.concatenate(kso, axis=1)
    kwo_ref[0] = jnp.concatenate(kwo, axis=1)
    kcmp_ref[0] = jnp.concatenate(kcmp, axis=1)
    vcmp_ref[0] = jnp.concatenate(vcmp, axis=1)


def _cmp_matrix(w, nc):
    eye = jnp.eye(nc, dtype=F32)
    return (eye[None, :, :, None] * w.T[:, None, None, :]).reshape(w.shape[1], nc, nc * NSA_CMP_BLOCK).astype(BF16)


def _nsa_prep(y, cmp_wk, cmp_wv, k_norm, cs, sn):
    b, s, _ = y.shape
    nc = s // NSA_CMP_BLOCK
    base = N_HEADS * HEAD_DIM // NSA_KVD
    col = lambda j: pl.BlockSpec((1, s, NSA_KVD), lambda bb: (bb, 0, base + j))
    full = lambda a: pl.BlockSpec(a.shape, lambda bb: (0,) * a.ndim)
    wk, wv = _cmp_matrix(cmp_wk, nc), _cmp_matrix(cmp_wv, nc)
    kn = [k_norm[j:j + 1] for j in range(3)]
    seq = pl.BlockSpec((1, s, NSA_KVD), lambda bb: (bb, 0, 0))
    blk = pl.BlockSpec((1, nc, NSA_KVD), lambda bb: (bb, 0, 0))
    return pl.pallas_call(
        _nsa_prep_kernel, grid=(b,),
        in_specs=[col(0), col(1), col(2), col(4), full(wk), full(wv), full(kn[0]), full(kn[1]), full(kn[2]),
                  full(cs), full(sn)],
        out_specs=[seq, seq, blk, blk],
        out_shape=[jax.ShapeDtypeStruct((b, s, NSA_KVD), F32), jax.ShapeDtypeStruct((b, s, NSA_KVD), F32),
                   jax.ShapeDtypeStruct((b, nc, NSA_KVD), F32), jax.ShapeDtypeStruct((b, nc, NSA_KVD), F32)],
        compiler_params=_cparams(("arbitrary",)),
    )(y, y, y, y, wk, wv, kn[0], kn[1], kn[2], cs, sn)


def _cmp_attend(q, kcmp, vcmp, mask, ng, row_minor=False):
    nq = q.shape[0] // ng
    mk = jnp.concatenate([mask] * ng, axis=0)
    qk, pv = (_dot, _dot_nt) if row_minor else (_dot_nt, _dot)
    s = jnp.where(mk, qk(q, kcmp.astype(BF16)), NEG)
    m = jnp.max(s, axis=-1, keepdims=True)
    p = jnp.where(mk, jnp.exp(s - m), 0.0)
    p = p / jnp.maximum(jnp.sum(p, axis=-1, keepdims=True), TINY)
    o = pv(p.astype(BF16), vcmp.astype(BF16))
    imp = p[0:nq]
    for g in range(1, ng):
        imp = imp + p[g * nq:(g + 1) * nq]
    return o, imp


def _pair_sum(imp):
    n = imp.shape[1]
    lane = lax.broadcasted_iota(jnp.int32, imp.shape, 1)
    return imp + jnp.where(lane % 2 == 0, pltpu.roll(imp, n - 1, 1), pltpu.roll(imp, 1, 1))


def _select(impx, nblk, seg, n_sel):
    lane = lax.broadcasted_iota(jnp.int32, impx.shape, 1)
    blk = (lane % seg) // NSA_PAIR
    nseg = impx.shape[1] // seg
    cnt = jnp.zeros(impx.shape, jnp.int32)
    for j in range(nblk):
        col = impx[:, j * NSA_PAIR:j * NSA_PAIR + 1]
        for sg in range(1, nseg):
            c = sg * seg + j * NSA_PAIR
            col = jnp.where(lane < sg * seg, col, impx[:, c:c + 1])
        beats = (col > impx) | ((col == impx) & (j < blk))
        cnt = cnt + beats.astype(jnp.int32)
    return (cnt < n_sel).astype(F32)


def _expand_mask(sel, start, tk):
    nc = sel.shape[1]
    n = lax.broadcasted_iota(jnp.int32, (nc, tk), 0)
    k = lax.broadcasted_iota(jnp.int32, (nc, tk), 1)
    e = jnp.where((start + k) // NSA_CMP_BLOCK == n, 1.0, 0.0).astype(BF16)
    return _dot(sel.astype(BF16), e) > 0.5


def _nsa_attn_kernel(q_ref, gl_ref, ks_ref, vs_ref, kwp_ref, kwc_ref, vwp_ref, vwc_ref, kcmp_ref, vcmp_ref,
                     qg_ref, c2_ref, s2_ref, o_ref, qn_s, qr_s, ks_s, vs_s, m_s, acc_s, mw_s, accw_s,
                     *, tk, n_sel, chunk):
    i = pl.program_id(1)
    tq = q_ref.shape[1]
    nc = kcmp_ref.shape[1]
    rows_kv = NSA_GRP * tq
    rep = chunk // tq
    kv_of = lambda c: c * chunk // rows_kv
    kv_range = range(NSA_KV_HEADS)

    @pl.when(i == 0)
    def _():
        ks_s[...] = ks_ref[0].astype(BF16)
        vs = vs_ref[0].astype(BF16)
        vs_s[...] = jnp.concatenate([_with_ones(_head(vs, kv)) for kv in kv_range], axis=1)

    qn = _heads_rms(q_ref[0], qg_ref[...], _seg_ones(2 * LANES)) * HEAD_DIM ** -0.5
    _stack_heads(qn, qn_s)
    _stack_heads(_heads_rope(qn, c2_ref[...], s2_ref[...]), qr_s)

    t_c = i * tq + lax.broadcasted_iota(jnp.int32, (tq, nc), 0)
    n_c = lax.broadcasted_iota(jnp.int32, (tq, nc), 1)
    cmask = (n_c + 1) * NSA_CMP_BLOCK - 1 <= t_c
    o_cmp, imps = [], []
    for kv in kv_range:
        oc, imp = _cmp_attend(qn_s[kv * rows_kv:(kv + 1) * rows_kv, :], _head(kcmp_ref[0], kv),
                              _head(vcmp_ref[0], kv), cmask, NSA_GRP)
        o_cmp.append(oc)
        imps.append(imp)
    imp = _pair_sum(jnp.concatenate(imps, axis=1))
    t_i = i * tq + lax.broadcasted_iota(jnp.int32, imp.shape, 0)
    blk = (lax.broadcasted_iota(jnp.int32, imp.shape, 1) % nc) // NSA_PAIR
    impx = jnp.where(blk == t_i // NSA_SEL_BLOCK, jnp.inf, jnp.where(blk * NSA_SEL_BLOCK <= t_i, imp, -jnp.inf))
    sel = _select(impx, nc // NSA_PAIR, nc, n_sel)

    _flash_init(m_s, acc_s)
    selk = [sel[:, kv * nc:(kv + 1) * nc] for kv in kv_range]
    t_k = i * tq + lax.broadcasted_iota(jnp.int32, (tq, tk), 0)
    k_off = lax.broadcasted_iota(jnp.int32, (tq, tk), 1)

    def body(j, carry):
        start = pl.multiple_of(j * tk, tk)
        kb = ks_s[pl.ds(start, tk), :]
        vbx = vs_s[pl.ds(start, tk), :]
        causal = (start + k_off) <= t_k
        mks = [jnp.concatenate([_expand_mask(selk[kv], start, tk) & causal] * rep, axis=0) for kv in kv_range]
        kvs = [([_head(kb, kv)], vbx[:, kv * LANES:(kv + 1) * LANES]) for kv in kv_range]
        _flash_step([qr_s], lambda c: kvs[kv_of(c)], lambda c: mks[kv_of(c)], m_s, acc_s, chunk)
        return carry

    lax.fori_loop(0, (i * tq + tq - 1) // tk + 1, body, 0)

    _flash_init(mw_s, accw_s)
    kk = jnp.concatenate([kwp_ref[0], kwc_ref[0]], axis=0).astype(BF16)
    vv = jnp.concatenate([vwp_ref[0], vwc_ref[0]], axis=0).astype(BF16)
    kvw = [([_head(kk, kv)], _with_ones(_head(vv, kv))) for kv in kv_range]
    wmk = jnp.concatenate([_band_mask(tq, tq, jnp.where(i > 0, 0, tq))] * rep, axis=0)
    _flash_step([qr_s], lambda c: kvw[kv_of(c)], lambda c: wmk, mw_s, accw_s, chunk)

    o_sel = acc_s[:, :HEAD_DIM] / acc_s[:, HEAD_DIM:]
    o_win = accw_s[:, :HEAD_DIM] / accw_s[:, HEAD_DIM:]
    gates = jax.nn.sigmoid(gl_ref[0])
    outs = []
    for h in range(N_HEADS):
        kv, g = divmod(h, NSA_GRP)
        outs.append(gates[:, 3 * h:3 * h + 1] * o_cmp[kv][g * tq:(g + 1) * tq]
                    + gates[:, 3 * h + 1:3 * h + 2] * o_sel[h * tq:(h + 1) * tq]
                    + gates[:, 3 * h + 2:3 * h + 3] * o_win[h * tq:(h + 1) * tq])
    o_ref[0] = jnp.concatenate(outs, axis=1).astype(o_ref.dtype)


def _nsa_attn(y, ksn, kwn, kcmp, vcmp, q_norm, cs, sn, tk):
    b, s, _ = y.shape
    tq = WINDOW
    hd = N_HEADS * HEAD_DIM
    nc = kcmp.shape[1]
    base = hd // NSA_KVD
    n_sel = min(NSA_TOPK, s // NSA_SEL_BLOCK)
    prev = lambda i: jnp.maximum(i - 1, 0)
    full = lambda a: pl.BlockSpec(a.shape, lambda bb, i: (0,) * a.ndim)
    tab = pl.BlockSpec((tq, LANES), lambda bb, i: (i, 0))
    q_norm = jnp.tile(q_norm, (1, N_HEADS))
    return pl.pallas_call(
        functools.partial(_nsa_attn_kernel, tk=tk, n_sel=n_sel, chunk=2 * tq), grid=(b, s // tq),
        in_specs=[pl.BlockSpec((1, tq, hd), lambda bb, i: (bb, i, 0)),
                  pl.BlockSpec((1, tq, NSA_KVD), lambda bb, i: (bb, i, base + 6)),
                  pl.BlockSpec((1, s, NSA_KVD), lambda bb, i: (bb, 0, 0)),
                  pl.BlockSpec((1, s, NSA_KVD), lambda bb, i: (bb, 0, base + 3)),
                  pl.BlockSpec((1, tq, NSA_KVD), lambda bb, i: (bb, prev(i), 0)),
                  pl.BlockSpec((1, tq, NSA_KVD), lambda bb, i: (bb, i, 0)),
                  pl.BlockSpec((1, tq, NSA_KVD), lambda bb, i: (bb, prev(i), base + 5)),
                  pl.BlockSpec((1, tq, NSA_KVD), lambda bb, i: (bb, i, base + 5)),
                  pl.BlockSpec((1, nc, NSA_KVD), lambda bb, i: (bb, 0, 0)),
                  pl.BlockSpec((1, nc, NSA_KVD), lambda bb, i: (bb, 0, 0)),
                  full(q_norm), tab, tab],
        out_specs=pl.BlockSpec((1, tq, hd), lambda bb, i: (bb, i, 0)),
        out_shape=jax.ShapeDtypeStruct((b, s, hd), BF16),
        scratch_shapes=[pltpu.VMEM((N_HEADS * tq, HEAD_DIM), BF16),
                        pltpu.VMEM((N_HEADS * tq, HEAD_DIM), BF16),
                        pltpu.VMEM((s, NSA_KVD), BF16),
                        pltpu.VMEM((s, NSA_KV_HEADS * LANES), BF16),
                        pltpu.VMEM((N_HEADS * tq, LANES), F32),
                        pltpu.VMEM((N_HEADS * tq, 2 * HEAD_DIM), F32),
                        pltpu.VMEM((N_HEADS * tq, LANES), F32),
                        pltpu.VMEM((N_HEADS * tq, 2 * HEAD_DIM), F32)],
        compiler_params=_cparams(("arbitrary", "arbitrary")),
    )(y, y, ksn, y, kwn, kwn, y, y, kcmp, vcmp, q_norm, cs, sn)


def _nsa_decode_kernel(pt_ref, *refs, ns, slots, npg, past, n_sel):
    del pt_ref
    kcp, vcp = refs[:slots], refs[slots:2 * slots]
    ksp, vsp = refs[2 * slots:3 * slots], refs[3 * slots:4 * slots]
    (q_ref, gl_ref, ks_ref, vs_ref, kw_ref, vw_ref, kwb_ref, vwb_ref, wrow_ref, eloc_ref,
     qn_ref, kcn_ref, ksn_ref, kwn_ref, cs_ref, sn_ref,
     o_ref, kso_ref, kwo_ref, vwo_ref,
     cmp_s, sel_s, ocmp_s, q_s, m_s, acc_s) = refs[4 * slots:]
    step = pl.program_id(1)
    scale = HEAD_DIM ** -0.5
    nc = cmp_s.shape[1]
    rows_per_step = slots * PAGE_SIZE
    cper = rows_per_step // NSA_CMP_BLOCK
    kv_range = range(NSA_KV_HEADS)
    grp = lambda xs, kv: xs[kv * NSA_GRP:(kv + 1) * NSA_GRP]
    pages = lambda prefs, kv: jnp.concatenate([r[0, 0, kv] for r in prefs], axis=1)

    @pl.when(step < npg)
    def _():
        lhs = jnp.concatenate([pages(prefs, kv) * wrow_ref[2 * j + kv:2 * j + kv + 1, :]
                               for j, prefs in enumerate((kcp, vcp)) for kv in kv_range], axis=0)
        res = _dot(lhs.astype(BF16), eloc_ref[...])
        for k in range(npg):
            @pl.when(step == k)
            def _():
                cmp_s[:, k * cper:(k + 1) * cper] = res

    @pl.when(step == npg - 1)
    def _():
        q = _rows(q_ref, ns)
        qn = [_rms(_head(q, h), qn_ref[...]) * scale for h in range(N_HEADS)]
        qr = [_rope(x, cs_ref[...], sn_ref[...], ROPE_DIM) for x in qn]
        t_c = past + lax.broadcasted_iota(jnp.int32, (ns, nc), 0)
        n_c = lax.broadcasted_iota(jnp.int32, (ns, nc), 1)
        cmask = (n_c + 1) * NSA_CMP_BLOCK - 1 <= t_c
        imps = []
        for kv in kv_range:
            kct = cmp_s[kv * HEAD_DIM:(kv + 1) * HEAD_DIM, :]
            kct = kct * lax.rsqrt(jnp.mean(kct * kct, axis=0, keepdims=True) + EPS) * kcn_ref[...]
            vct = cmp_s[(NSA_KV_HEADS + kv) * HEAD_DIM:(NSA_KV_HEADS + kv + 1) * HEAD_DIM, :]
            oc, imp = _cmp_attend(jnp.concatenate(grp(qn, kv), axis=0).astype(BF16), kct, vct, cmask, NSA_GRP,
                                  row_minor=True)
            ocmp_s[kv] = oc
            imps.append(_pair_sum(imp))
            q_s[kv] = jnp.concatenate(grp(qr, kv), axis=0).astype(BF16)
        for kv in kv_range:
            sel_s[kv] = _select(imps[kv], nc // NSA_PAIR, nc, n_sel - 1)
        _flash_init(m_s, acc_s)

    @pl.when(step >= npg)
    def _():
        start = (step - npg) * rows_per_step
        for kv in kv_range:
            kst = pages(ksp, kv).astype(BF16)
            vst = pages(vsp, kv)
            vtx = jnp.concatenate([vst, jnp.ones(vst.shape, F32)], axis=0).astype(BF16)
            mk = jnp.concatenate([_expand_mask(sel_s[kv], start, rows_per_step)] * NSA_GRP, axis=0)
            _online_step(jnp.where(mk, _dot(q_s[kv], kst), NEG), lambda p, v=vtx: _dot_nt(p, v),
                         m_s.at[kv], acc_s.at[kv])

    @pl.when(step == 2 * npg - 1)
    def _():
        pad = lambda a: jnp.concatenate([a, jnp.zeros((PAGE_SIZE - ns, a.shape[1]), a.dtype)], axis=0)
        vs_new = _rows(vs_ref, ns)
        ks_raw = _rows(ks_ref, ns)
        ksn = jnp.concatenate([_rope(_rms(_head(ks_raw, kv), ksn_ref[...]), cs_ref[...], sn_ref[...], ROPE_DIM)
                               for kv in kv_range], axis=1)
        kso_ref[0] = ksn
        kb = pad(ksn).astype(BF16)
        row = lax.broadcasted_iota(jnp.int32, (NSA_GRP * ns, PAGE_SIZE), 0) % ns
        col = lax.broadcasted_iota(jnp.int32, (NSA_GRP * ns, PAGE_SIZE), 1)
        for kv in kv_range:
            vbx = _with_ones(pad(_head(vs_new, kv)).astype(BF16))
            _online_step(jnp.where(col <= row, _dot_nt(q_s[kv], _head(kb, kv)), NEG), lambda p, v=vbx: _dot(p, v),
                         m_s.at[kv], acc_s.at[kv])
        outs_w, kw_out, vw_out = _window_step(
            _rows(q_ref, ns), _rows(kw_ref, ns), _rows(vw_ref, ns), [kwb_ref[0, 0, kv] for kv in kv_range],
            [vwb_ref[0, 0, kv] for kv in kv_range], qn_ref[...], kwn_ref[...], cs_ref[...], sn_ref[...], None,
            NSA_KV_HEADS)
        for kv in kv_range:
            kwo_ref[0, kv] = kw_out[kv]
            vwo_ref[0, kv] = vw_out[kv]
        gates = jax.nn.sigmoid(_rows(gl_ref, ns))
        outs = []
        for kv in kv_range:
            o_sel = acc_s[kv, :, :HEAD_DIM] / acc_s[kv, :, HEAD_DIM:]
            o_cmp = ocmp_s[kv]
            for g in range(NSA_GRP):
                h = kv * NSA_GRP + g
                outs.append(gates[:, 3 * h:3 * h + 1] * o_cmp[g * ns:(g + 1) * ns]
                            + gates[:, 3 * h + 1:3 * h + 2] * o_sel[g * ns:(g + 1) * ns]
                            + gates[:, 3 * h + 2:3 * h + 3] * outs_w[h])
        o = jnp.concatenate(outs, axis=1)
        for t in range(ns):
            o_ref[t, 0] = o[t:t + 1]


def _nsa_decode(y, pools, kw_cache, vw_cache, li, page_table, cmp_wk, cmp_wv, q_norm, k_norm, cs, sn):
    ns, db, n = y.shape
    hd = N_HEADS * HEAD_DIM
    slots = PAGE_SLOTS
    n_pages = page_table.shape[1]
    npg = n_pages // slots
    past = n_pages * PAGE_SIZE
    nc = past // NSA_CMP_BLOCK
    base = hd // NSA_KVD
    w = kw_cache.shape[2]
    n_blocks = -(-(past + ns) // NSA_SEL_BLOCK)
    n_sel = min(NSA_TOPK, n_blocks)
    y4 = y.reshape(ns, db, 1, n)
    full = lambda a: pl.BlockSpec(a.shape, lambda b, g, pt: (0,) * a.ndim)

    def paged(phase, k):
        if phase == 0:
            grp_of = lambda g: jnp.minimum(g, npg - 1)
        else:
            grp_of = lambda g: jnp.maximum(g - npg, 0)
        return pl.BlockSpec((1, 1, NSA_KV_HEADS, HEAD_DIM, PAGE_SIZE),
                            lambda b, g, pt: (li, pt[b, grp_of(g) * slots + k], 0, 0, 0))

    step = lambda width, cb: pl.BlockSpec((ns, 1, 1, width), lambda b, g, pt: (0, b, 0, cb))
    rows = slots * PAGE_SIZE
    cper = rows // NSA_CMP_BLOCK
    wrow = jnp.tile(jnp.concatenate([cmp_wk.T, cmp_wv.T], axis=0), (1, cper))
    eloc = (jnp.arange(rows)[:, None] // NSA_CMP_BLOCK == jnp.arange(cper)[None, :]).astype(BF16)
    kn = [k_norm[j:j + 1] for j in range(3)]
    kcn_col = k_norm[0].reshape(HEAD_DIM, 1)
    seq_out = pl.BlockSpec((1, ns, NSA_KVD), lambda b, g, pt: (b, 0, 0))
    buf_out = pl.BlockSpec((1, NSA_KV_HEADS, HEAD_DIM, w), lambda b, g, pt: (b, 0, 0, 0))
    buf_in = pl.BlockSpec((1, 1, NSA_KV_HEADS, HEAD_DIM, w), lambda b, g, pt: (li, b, 0, 0, 0))
    grid_spec = pltpu.PrefetchScalarGridSpec(
        num_scalar_prefetch=1, grid=(db, 2 * npg),
        in_specs=([paged(0, k) for k in range(slots)] + [paged(0, k) for k in range(slots)]
                  + [paged(1, k) for k in range(slots)] + [paged(1, k) for k in range(slots)]
                  + [step(hd, 0), step(NSA_KVD, base + 6), step(NSA_KVD, base + 2), step(NSA_KVD, base + 3),
                     step(NSA_KVD, base + 4), step(NSA_KVD, base + 5), buf_in, buf_in,
                     full(wrow), full(eloc),
                     full(q_norm), full(kcn_col), full(kn[1]), full(kn[2]), full(cs), full(sn)]),
        out_specs=[step(hd, 0), seq_out, buf_out, buf_out],
        scratch_shapes=[pltpu.VMEM((2 * NSA_KV_HEADS * HEAD_DIM, nc), F32),
                        pltpu.VMEM((NSA_KV_HEADS, ns, nc), F32),
                        pltpu.VMEM((NSA_KV_HEADS, NSA_GRP * ns, HEAD_DIM), F32),
                        pltpu.VMEM((NSA_KV_HEADS, NSA_GRP * ns, HEAD_DIM), BF16),
                        pltpu.VMEM((NSA_KV_HEADS, NSA_GRP * ns, LANES), F32),
                        pltpu.VMEM((NSA_KV_HEADS, NSA_GRP * ns, 2 * HEAD_DIM), F32)])
    kc_pool, vc_pool, ks_pool, vs_pool = [_row_minor(p) for p in pools]
    o, kso, kwo, vwo = pl.pallas_call(
        functools.partial(_nsa_decode_kernel, ns=ns, slots=slots, npg=npg, past=past, n_sel=n_sel),
        grid_spec=grid_spec,
        out_shape=[jax.ShapeDtypeStruct((ns, db, 1, hd), F32),
                   jax.ShapeDtypeStruct((db, ns, NSA_KVD), F32),
                   jax.ShapeDtypeStruct((db, NSA_KV_HEADS, HEAD_DIM, w), F32),
                   jax.ShapeDtypeStruct((db, NSA_KV_HEADS, HEAD_DIM, w), F32)],
        compiler_params=_cparams(("arbitrary", "arbitrary")),
    )(page_table, *([kc_pool] * slots), *([vc_pool] * slots), *([ks_pool] * slots), *([vs_pool] * slots),
      y4, y4, y4, y4, y4, y4, _row_minor(kw_cache), _row_minor(vw_cache), wrow, eloc,
      q_norm, kcn_col, kn[1], kn[2], cs, sn)
    back = lambda a: a.transpose(0, 3, 1, 2)
    return o.reshape(ns, db, hd), kso, back(kwo), back(vwo)


def kernel(x_prompt, x_sample, cache_swa_k, cache_swa_v, cache_mla_latent, cache_mla_krope, cache_nsa_kcmp, cache_nsa_vcmp, cache_nsa_ksel, cache_nsa_vsel, cache_nsa_kwin, cache_nsa_vwin, state_conv_ffn, page_table, c_prompt, c_sample, ada_w, ada_b, norm_mix, norm_ffn, ffn_w_up, ffn_conv, ffn_w_down, a_w_in, a_q_norm, a_k_norm, a_sinks, a_w_out, b_w_in, b_qa_norm, b_w_qb, b_q_norm_nope, b_q_norm_rope, b_kv_norm, b_krope_norm, b_w_uk, b_w_uv, b_w_out, c_w_in, c_q_norm, c_k_norm, c_cmp_wk, c_cmp_wv, c_w_out):
    nb, seq, d = x_prompt.shape
    db, ds, _ = x_sample.shape
    depth = ada_w.shape[0]
    ff = ffn_w_down.shape[1]
    past = page_table.shape[1] * PAGE_SIZE
    hd = N_HEADS * HEAD_DIM

    mod = _modulate(jnp.concatenate([c_prompt, c_sample], axis=0), ada_w, ada_b)
    tr_p = math.gcd(seq, 512)
    gp = _Group(True, mod[:, :, :nb].reshape(depth, 6, nb, 1, d), tr_p)
    gs = _Group(False, mod[:, :, nb:], db)
    norm_mix3 = norm_mix.reshape(depth, 1, d)
    norm_ffn3 = norm_ffn.reshape(depth, 1, d)

    pos_p = jnp.arange(seq, dtype=jnp.int32)
    pos_s = past + jnp.arange(ds, dtype=jnp.int32)
    cw_p, sw_p = _rope_tables(pos_p, ROPE_DIM, ROPE_THETA, HEAD_DIM)
    cw2_p, sw2_p = jnp.tile(cw_p, (1, LANES // HEAD_DIM)), jnp.tile(sw_p, (1, LANES // HEAD_DIM))
    cw_s, sw_s = _rope_tables(pos_s, ROPE_DIM, ROPE_THETA, HEAD_DIM)
    cm_p, sm_p = _rope_tables(pos_p, MLA_ROPE, MLA_THETA, MLA_ROPE)
    cm_s, sm_s = _rope_tables(pos_s, MLA_ROPE, MLA_THETA, MLA_ROPE)

    xp = x_prompt
    xs = x_sample.transpose(1, 0, 2)
    row2 = lambda v: v.reshape(1, -1)
    tm = lambda a: a.transpose(1, 0, 2)
    out = {k: [] for k in ("swa_k_p", "swa_v_p", "swa_k_s", "swa_v_s", "mla_c_p", "mla_r_p", "mla_c_s", "mla_r_s",
                           "conv_p", "conv_s")}
    nsa_p = [[] for _ in range(6)]
    nsa_s = [[] for _ in range(6)]
    ia = ib = ic = 0
    for layer in range(depth):
        kind = layer % N_MIXERS
        if kind == 0:
            w_in = a_w_in[ia].astype(BF16)
            qn, kn, sinks = row2(a_q_norm[ia]), row2(a_k_norm[ia]), row2(a_sinks[ia])
            yp = _proj(xp, gp, norm_mix3, layer, w_in)
            op, kp, vp = _band_attn(yp, qn, kn, sinks, cw2_p, sw2_p)
            ys = _proj(xs, gs, norm_mix3, layer, w_in)
            os_, ks_, vs_ = _step_attn(ys, cache_swa_k, cache_swa_v, ia, qn, kn, sinks, cw_s, sw_s)
            kv4 = lambda a: a.reshape(a.shape[0], a.shape[1], SWA_KV_HEADS, HEAD_DIM)
            out["swa_k_p"].append(kv4(kp)); out["swa_v_p"].append(kv4(vp))
            out["swa_k_s"].append(ks_); out["swa_v_s"].append(vs_)
            w_out = a_w_out[ia].astype(BF16)
            ia += 1
        elif kind == 1:
            w_in = b_w_in[ib].astype(BF16)
            wqb = b_w_qb[ib].reshape(MLA_Q_LORA, N_HEADS, MLA_NOPE + MLA_ROPE)
            wqb = jnp.concatenate([wqb[:, :, :MLA_NOPE].reshape(MLA_Q_LORA, -1),
                                   wqb[:, :, MLA_NOPE:].reshape(MLA_Q_LORA, -1)], axis=1).astype(BF16)
            wuk = b_w_uk[ib].transpose(1, 2, 0).astype(BF16)
            wuv = b_w_uv[ib].transpose(1, 0, 2).astype(BF16)
            norms = (row2(b_qa_norm[ib]), wqb, row2(b_kv_norm[ib]), row2(b_krope_norm[ib]))
            nn, nr = row2(b_q_norm_nope[ib]), row2(b_q_norm_rope[ib])
            qp, cp, rp = _mla_proj(xp, gp, norm_mix3, layer, w_in, *norms, cm_p[None], sm_p[None])
            op = _mla_attn(qp, cp, rp, cm_p, sm_p, nn, nr, wuk, wuv, math.gcd(seq, 128), math.gcd(seq, 256), 256)
            qs, cs_, rs_ = _mla_proj(xs, gs, norm_mix3, layer, w_in, *norms, cm_s[:, None], sm_s[:, None])
            os_ = _mla_decode(qs, cs_, rs_, cache_mla_latent, cache_mla_krope, ib, page_table,
                              cm_s, sm_s, nn, nr, wuk, wuv)
            out["mla_c_p"].append(cp); out["mla_r_p"].append(rp)
            out["mla_c_s"].append(tm(cs_)); out["mla_r_s"].append(tm(rs_))
            w_out = b_w_out[ib].astype(BF16)
            ib += 1
        else:
            n_in = c_w_in.shape[2]
            n_pad = -(-n_in // 128) * 128
            w_in = jnp.pad(c_w_in[ic], ((0, 0), (0, n_pad - n_in))).astype(BF16)
            qn, kn3 = row2(c_q_norm[ic]), c_k_norm[ic]
            kv4 = lambda a: a.reshape(a.shape[0], a.shape[1], NSA_KV_HEADS, HEAD_DIM)
            col = lambda a, j: a[:, :, hd + j * NSA_KVD:hd + (j + 1) * NSA_KVD]
            yp = _proj(xp, gp, norm_mix3, layer, w_in)
            ksn, kwn, kcmp, vcmp = _nsa_prep(yp, c_cmp_wk[ic], c_cmp_wv[ic], kn3, cw_p, sw_p)
            op = _nsa_attn(yp, ksn, kwn, kcmp, vcmp, qn, cw2_p, sw2_p, math.gcd(seq, 256))
            for j, a in enumerate((col(yp, 0), col(yp, 1), ksn, col(yp, 3), kwn[:, -WINDOW:],
                                   col(yp, 5)[:, -WINDOW:])):
                nsa_p[j].append(kv4(a))
            ys = _proj(xs, gs, norm_mix3, layer, w_in)
            pools = (cache_nsa_kcmp, cache_nsa_vcmp, cache_nsa_ksel, cache_nsa_vsel)
            os_, kso, kwo, vwo = _nsa_decode(ys, pools, cache_nsa_kwin, cache_nsa_vwin, ic, page_table,
                                             c_cmp_wk[ic], c_cmp_wv[ic], qn, kn3, cw_s, sw_s)
            for j, a in enumerate((tm(col(ys, 0)), tm(col(ys, 1)), kso, tm(col(ys, 3)))):
                nsa_s[j].append(kv4(a))
            nsa_s[4].append(kwo)
            nsa_s[5].append(vwo)
            w_out = c_w_out[ic].astype(BF16)
            ic += 1
        xp = _outproj(op, xp, gp, layer, w_out)
        xs = _outproj(os_, xs, gs, layer, w_out)
        wup, wd = ffn_w_up[layer].astype(BF16), ffn_w_down[layer].astype(BF16)
        tf = math.gcd(ff, 256)
        xp, bp = _ffn_prompt(xp, gp, norm_ffn3, layer, wup, ffn_conv[layer], wd, math.gcd(seq, 1024), tf)
        xs, bs = _ffn_sample(xs, gs, norm_ffn3, layer, wup, ffn_conv[layer], wd, state_conv_ffn[layer], tf)
        out["conv_p"].append(bp); out["conv_s"].append(bs)

    st = lambda xs_: jnp.stack(xs_)
    return (xp, xs.transpose(1, 0, 2),
            st(out["swa_k_p"]), st(out["swa_v_p"]), st(out["mla_c_p"]), st(out["mla_r_p"]),
            *[st(a) for a in nsa_p], st(out["conv_p"]),
            st(out["swa_k_s"]), st(out["swa_v_s"]), st(out["mla_c_s"]), st(out["mla_r_s"]),
            *[st(a) for a in nsa_s], st(out["conv_s"]))
```

```python
import functools
import math

import jax
import jax.numpy as jnp
from jax import lax
from jax.experimental import pallas as pl
from jax.experimental.pallas import tpu as pltpu

F32 = jnp.float32
BF16 = jnp.bfloat16

N_HEADS = 16
HEAD_DIM = 64
ROPE_DIM = 16
ROPE_THETA = 500000.0
EPS = 1e-6
PAGE_SIZE = 128
SWA_KV_HEADS = 4
WINDOW = 128
MLA_Q_LORA = 384
MLA_KV_LORA = 256
MLA_NOPE = 64
MLA_ROPE = 32
MLA_THETA = 10000.0
NSA_KV_HEADS = 2
NSA_CMP_BLOCK = 32
NSA_SEL_BLOCK = 64
NSA_TOPK = 16
N_MIXERS = 3
NEG = -1e30
TINY = float(jnp.finfo(jnp.float32).tiny)
VMEM_LIMIT = 56 * 1024 * 1024
PAGE_SLOTS = 8


def _cparams(sem):
    return pltpu.CompilerParams(dimension_semantics=sem, vmem_limit_bytes=VMEM_LIMIT)


def _dot(a, b):
    return jnp.dot(a, b, preferred_element_type=F32)


def _dot_nt(a, b):
    return lax.dot_general(a, b, (((1,), (1,)), ((), ())), preferred_element_type=F32)


def _rms(x, g):
    return x * lax.rsqrt(jnp.mean(x * x, axis=-1, keepdims=True) + EPS) * g


def _rope(x, c, s, rot):
    half = rot // 2
    parts = [x[:, half:rot], x[:, :half]]
    if x.shape[1] > rot:
        parts.append(x[:, rot:])
    return x * c + jnp.concatenate(parts, axis=1) * s


def _silu(x):
    return x * jax.nn.sigmoid(x)


def _norm_mod(x, g, sc, sh):
    return _rms(x, g) * (1.0 + sc) + sh


def _rope_tables(pos, rot, theta, width):
    half = rot // 2
    inv = jnp.power(jnp.float32(theta), -jnp.arange(half, dtype=F32) / half)
    ang = pos.astype(F32)[:, None] * inv[None, :]
    cos, sin = jnp.cos(ang), jnp.sin(ang)
    n = pos.shape[0]
    c = jnp.concatenate([cos, cos, jnp.ones((n, width - rot), F32)], axis=1)
    s = jnp.concatenate([-sin, sin, jnp.zeros((n, width - rot), F32)], axis=1)
    return c, s


class _Group:
    def __init__(self, per_group, mod, tr):
        self.pm = per_group
        self.mods = [mod[layer] for layer in range(mod.shape[0])]
        self.tr = tr

    def mod(self, layer):
        return self.mods[layer]

    def mod_spec(self, j, nargs=2):
        d = self.mods[0].shape[-1]
        if self.pm:
            shape, f = (1, 1, 1, d), (lambda g: (j, g, 0, 0))
        else:
            shape, f = (1, self.mods[0].shape[1], d), (lambda g: (j, 0, 0))
        if nargs == 2:
            return pl.BlockSpec(shape, lambda g, r: f(g))
        return pl.BlockSpec(shape, lambda g, r, k: f(g))


def _mod_val(ref, pm):
    return ref[0, 0] if pm else ref[0]


def _mod_kernel(c_ref, w_ref, b_ref, o_ref):
    a = _silu(c_ref[...])
    o_ref[0, 0] = _dot(a.astype(BF16), w_ref[0].astype(BF16)) + b_ref[0]


def _modulate(c_all, ada_w, ada_b):
    nl, d, d6 = ada_w.shape
    n = c_all.shape[0]
    nj = d6 // d
    return pl.pallas_call(
        _mod_kernel, grid=(nl, nj),
        in_specs=[pl.BlockSpec((n, d), lambda l, j: (0, 0)),
                  pl.BlockSpec((1, d, d), lambda l, j: (l, 0, j)),
                  pl.BlockSpec((1, 1, d), lambda l, j: (l, 0, j))],
        out_specs=pl.BlockSpec((1, 1, n, d), lambda l, j: (l, j, 0, 0)),
        out_shape=jax.ShapeDtypeStruct((nl, nj, n, d), F32),
        compiler_params=_cparams(("arbitrary", "arbitrary")),
    )(c_all, ada_w, ada_b.reshape(nl, 1, d6))


def _proj_kernel(x_ref, g_ref, sc_ref, sh_ref, w_ref, o_ref, *, pm):
    h = _norm_mod(x_ref[0], g_ref[0], _mod_val(sc_ref, pm), _mod_val(sh_ref, pm))
    o_ref[0] = _dot(h.astype(BF16), w_ref[...])


def _proj(x, grp, gains, layer, w_bf):
    ng, nr, d = x.shape
    n = w_bf.shape[1]
    tr = grp.tr
    return pl.pallas_call(
        functools.partial(_proj_kernel, pm=grp.pm), grid=(ng, nr // tr),
        in_specs=[pl.BlockSpec((1, tr, d), lambda g, r: (g, r, 0)),
                  pl.BlockSpec((1, 1, d), lambda g, r: (0, 0, 0)),
                  grp.mod_spec(1), grp.mod_spec(0),
                  pl.BlockSpec((d, n), lambda g, r: (0, 0))],
        out_specs=pl.BlockSpec((1, tr, n), lambda g, r: (g, r, 0)),
        out_shape=jax.ShapeDtypeStruct((ng, nr, n), F32),
        compiler_params=_cparams(("arbitrary", "arbitrary")),
    )(x, gains[layer:layer + 1], grp.mod(layer), grp.mod(layer), w_bf)


def _mla_proj_kernel(x_ref, g_ref, sc_ref, sh_ref, w_ref, qan_ref, wqb_ref, kvn_ref, krn_ref,
                     c_ref, s_ref, q_ref, lat_ref, kr_ref, *, pm):
    h = _norm_mod(x_ref[0], g_ref[0], _mod_val(sc_ref, pm), _mod_val(sh_ref, pm))
    y = _dot(h.astype(BF16), w_ref[...])
    a, b = MLA_Q_LORA, MLA_Q_LORA + MLA_KV_LORA
    qa = _rms(y[:, :a], qan_ref[...])
    q_ref[0] = _dot(qa.astype(BF16), wqb_ref[...])
    lat_ref[0] = _rms(y[:, a:b], kvn_ref[...])
    kr = _rms(y[:, b:], krn_ref[...])
    kr_ref[0] = _rope(kr, c_ref[0], s_ref[0], MLA_ROPE)


def _mla_proj(x, grp, gains, layer, w_bf, qa_norm, wqb_bf, kv_norm, krope_norm, cs, sn):
    ng, nr, d = x.shape
    tr = grp.tr
    nq = wqb_bf.shape[1]
    if grp.pm:
        tab = pl.BlockSpec((1, tr, MLA_ROPE), lambda g, r: (0, r, 0))
    else:
        tab = pl.BlockSpec((1, 1, MLA_ROPE), lambda g, r: (g, 0, 0))
    full = lambda a: pl.BlockSpec(a.shape, lambda g, r: (0,) * a.ndim)
    return pl.pallas_call(
        functools.partial(_mla_proj_kernel, pm=grp.pm), grid=(ng, nr // tr),
        in_specs=[pl.BlockSpec((1, tr, d), lambda g, r: (g, r, 0)),
                  pl.BlockSpec((1, 1, d), lambda g, r: (0, 0, 0)),
                  grp.mod_spec(1), grp.mod_spec(0),
                  full(w_bf), full(qa_norm), full(wqb_bf), full(kv_norm), full(krope_norm), tab, tab],
        out_specs=[pl.BlockSpec((1, tr, nq), lambda g, r: (g, r, 0)),
                   pl.BlockSpec((1, tr, MLA_KV_LORA), lambda g, r: (g, r, 0)),
                   pl.BlockSpec((1, tr, MLA_ROPE), lambda g, r: (g, r, 0))],
        out_shape=[jax.ShapeDtypeStruct((ng, nr, nq), F32),
                   jax.ShapeDtypeStruct((ng, nr, MLA_KV_LORA), F32),
                   jax.ShapeDtypeStruct((ng, nr, MLA_ROPE), F32)],
        compiler_params=_cparams(("arbitrary", "arbitrary")),
    )(x, gains[layer:layer + 1], grp.mod(layer), grp.mod(layer), w_bf, qa_norm, wqb_bf, kv_norm, krope_norm, cs, sn)


def _outproj_kernel(o_ref, x_ref, gt_ref, w_ref, y_ref, *, pm):
    y_ref[0] = x_ref[0] + _mod_val(gt_ref, pm) * _dot(o_ref[0].astype(BF16), w_ref[...])


def _outproj(o, x, grp, layer, w_bf):
    ng, nr, d = x.shape
    k = o.shape[-1]
    tr = grp.tr
    return pl.pallas_call(
        functools.partial(_outproj_kernel, pm=grp.pm), grid=(ng, nr // tr),
        in_specs=[pl.BlockSpec((1, tr, k), lambda g, r: (g, r, 0)),
                  pl.BlockSpec((1, tr, d), lambda g, r: (g, r, 0)),
                  grp.mod_spec(2),
                  pl.BlockSpec((k, d), lambda g, r: (0, 0))],
        out_specs=pl.BlockSpec((1, tr, d), lambda g, r: (g, r, 0)),
        out_shape=jax.ShapeDtypeStruct((ng, nr, d), F32),
        compiler_params=_cparams(("arbitrary", "arbitrary")),
    )(o, x, grp.mod(layer), w_bf)


HALO = 16


def _ffn_tail(mg, mv, wd_ref, x_ref, gt, y_ref):
    f = pl.program_id(2)
    d = _dot((_silu(mg) * mv).astype(BF16), wd_ref[...])

    @pl.when(f == 0)
    def _():
        y_ref[0] = d

    @pl.when(f > 0)
    def _():
        y_ref[0] += d

    @pl.when(f == pl.num_programs(2) - 1)
    def _():
        y_ref[0] = x_ref[0] + gt * y_ref[0]


def _ffn_prompt_kernel(x_ref, xh_ref, g_ref, sc_ref, sh_ref, gt_ref, wup_ref, cw_ref, wd_ref,
                       y_ref, b_ref, h_s, act_s, u_s, *, tr, tf):
    r = pl.program_id(1)
    ff = wd_ref.shape[0]
    g, sc, sh = g_ref[0], sc_ref[0, 0], sh_ref[0, 0]
    h_s[HALO:, :] = _norm_mod(x_ref[0], g, sc, sh).astype(BF16)
    hh = _norm_mod(xh_ref[0], g, sc, sh)
    h_s[:HALO, :] = jnp.where(r > 0, hh, 0.0).astype(BF16)
    nf = ff // tf
    cols = lambda half, f: pl.ds(pl.multiple_of(half * ff + f * tf, tf), tf)

    def up(f, slot):
        for half in range(2):
            u_s[slot, half] = _dot(h_s[...], wup_ref[:, cols(half, f)])

    def mixed(f, slot, half):
        u = u_s[slot, half]
        cw = cw_ref[:, cols(half, f)]
        b_ref[0, 0, :, cols(half, f)] = u[HALO + tr - 2:]
        a = pltpu.roll(u, 2, 0)[HALO:]
        b = pltpu.roll(u, 1, 0)[HALO:]
        return cw[0:1] * a + cw[1:2] * b + cw[2:3] * u[HALO:]

    def activate(f, slot):
        act_s[:, cols(0, f)] = (_silu(mixed(f, slot, 0)) * mixed(f, slot, 1)).astype(BF16)

    def body(it, carry):
        f = 2 * it
        up(f + 1, 1)
        activate(f, 0)
        up(f + 2, 0)
        activate(f + 1, 1)
        return carry

    assert nf % 2 == 1
    up(0, 0)
    lax.fori_loop(0, nf // 2, body, 0)
    activate(nf - 1, 0)
    y_ref[0] = x_ref[0] + gt_ref[0, 0] * _dot(act_s[...], wd_ref[...])


def _ffn_prompt(x, grp, gains, layer, wup_bf, conv_w, wd_bf, tr, tf):
    ng, nr, d = x.shape
    ff = wd_bf.shape[0]
    ms = lambda j: grp.mod_spec(j)
    once = lambda a: pl.BlockSpec(a.shape, lambda g, r: (0,) * a.ndim, pipeline_mode=pl.Buffered(1))
    y, buf = pl.pallas_call(
        functools.partial(_ffn_prompt_kernel, tr=tr, tf=tf), grid=(ng, nr // tr),
        in_specs=[pl.BlockSpec((1, tr, d), lambda g, r: (g, r, 0)),
                  pl.BlockSpec((1, HALO, d), lambda g, r: (g, jnp.maximum(r * (tr // HALO) - 1, 0), 0)),
                  pl.BlockSpec((1, 1, d), lambda g, r: (0, 0, 0)),
                  ms(4), ms(3), ms(5), once(wup_bf), once(conv_w), once(wd_bf)],
        out_specs=[pl.BlockSpec((1, tr, d), lambda g, r: (g, r, 0)),
                   pl.BlockSpec((1, 1, 2, 2 * ff), lambda g, r: (g, r, 0, 0))],
        out_shape=[jax.ShapeDtypeStruct((ng, nr, d), F32),
                   jax.ShapeDtypeStruct((ng, nr // tr, 2, 2 * ff), F32)],
        scratch_shapes=[pltpu.VMEM((tr + HALO, d), BF16), pltpu.VMEM((tr, ff), BF16),
                        pltpu.VMEM((2, 2, tr + HALO, tf), F32)],
        compiler_params=_cparams(("arbitrary", "arbitrary")),
    )(x, x, gains[layer:layer + 1], grp.mod(layer), grp.mod(layer), grp.mod(layer), wup_bf, conv_w, wd_bf)
    return y, buf[:, -1]


def _ffn_sample_kernel(x_ref, g_ref, sc_ref, sh_ref, gt_ref, sg_ref, sv_ref, wg_ref, wv_ref, cg_ref, cv_ref,
                       wd_ref, y_ref, bg_ref, bv_ref, h_s, *, nt, p):
    tile = lambda m: jnp.concatenate([m] * nt, axis=0)

    @pl.when(pl.program_id(2) == 0)
    def _():
        h_s[...] = _norm_mod(x_ref[0], g_ref[0], tile(sc_ref[0]), tile(sh_ref[0])).astype(BF16)

    h = h_s[...]
    n = nt * p

    def conv(st, u, cw):
        e = jnp.concatenate([st, u], axis=0)
        return cw[0:1] * e[0:n] + cw[1:2] * e[p:p + n] + cw[2:3] * e[2 * p:]

    ug = _dot(h, wg_ref[...])
    uv = _dot(h, wv_ref[...])
    bg_ref[...] = jnp.concatenate([sg_ref[...], ug], axis=0)[n:]
    bv_ref[...] = jnp.concatenate([sv_ref[...], uv], axis=0)[n:]
    _ffn_tail(conv(sg_ref[...], ug, cg_ref[...]), conv(sv_ref[...], uv, cv_ref[...]), wd_ref, x_ref,
              tile(gt_ref[0]), y_ref)


def _ffn_sample(x, grp, gains, layer, wup_bf, conv_w, wd_bf, state, tf):
    nt, p, d = x.shape
    ff = wd_bf.shape[0]
    nf = ff // tf
    n = nt * p
    st = state.transpose(1, 0, 2).reshape(2 * p, 2 * ff)
    ms = lambda j: grp.mod_spec(j, nargs=3)
    y, bg, bv = pl.pallas_call(
        functools.partial(_ffn_sample_kernel, nt=nt, p=p), grid=(1, 1, nf),
        in_specs=[pl.BlockSpec((1, n, d), lambda g, r, f: (0, 0, 0)),
                  pl.BlockSpec((1, 1, d), lambda g, r, f: (0, 0, 0)),
                  ms(4), ms(3), ms(5),
                  pl.BlockSpec((2 * p, tf), lambda g, r, f: (0, f)),
                  pl.BlockSpec((2 * p, tf), lambda g, r, f: (0, nf + f)),
                  pl.BlockSpec((d, tf), lambda g, r, f: (0, f)),
                  pl.BlockSpec((d, tf), lambda g, r, f: (0, nf + f)),
                  pl.BlockSpec((3, tf), lambda g, r, f: (0, f)),
                  pl.BlockSpec((3, tf), lambda g, r, f: (0, nf + f)),
                  pl.BlockSpec((tf, d), lambda g, r, f: (f, 0))],
        out_specs=[pl.BlockSpec((1, n, d), lambda g, r, f: (0, 0, 0)),
                   pl.BlockSpec((2 * p, tf), lambda g, r, f: (0, f)),
                   pl.BlockSpec((2 * p, tf), lambda g, r, f: (0, f))],
        out_shape=[jax.ShapeDtypeStruct((1, n, d), F32),
                   jax.ShapeDtypeStruct((2 * p, ff), F32),
                   jax.ShapeDtypeStruct((2 * p, ff), F32)],
        scratch_shapes=[pltpu.VMEM((n, d), BF16)],
        compiler_params=_cparams(("arbitrary", "arbitrary", "arbitrary")),
    )(x.reshape(1, n, d), gains[layer:layer + 1], grp.mod(layer), grp.mod(layer), grp.mod(layer), st, st, wup_bf, wup_bf, conv_w, conv_w, wd_bf)
    new_state = jnp.concatenate([bg, bv], axis=-1).reshape(2, p, 2 * ff).transpose(1, 0, 2)
    return y.reshape(nt, p, d), new_state


def _sink_col(sink_ref, heads, nq):
    return jnp.concatenate([jnp.broadcast_to(sink_ref[:, h:h + 1], (nq, 1)) for h in heads], axis=0)


def _band_mask(nq, w, prev_off):
    qi = lax.broadcasted_iota(jnp.int32, (nq, w + nq), 0)
    kj = lax.broadcasted_iota(jnp.int32, (nq, w + nq), 1)
    return ((kj < w) & (kj >= qi + prev_off)) | ((kj >= w) & ((kj - w) <= qi))


def _head(x, h):
    return x[:, h * HEAD_DIM:(h + 1) * HEAD_DIM]


def _band_attn_kernel(q_ref, kp_ref, kc_ref, vp_ref, vc_ref, qg_ref, kg_ref, cq_ref, sq_ref, cp_ref, sp_ref,
                      sink_ref, o_ref, ko_ref, vo_ref, q_s, m_s, acc_s, *, n_kv, chunk):
    i = pl.program_id(1)
    w = q_ref.shape[1]
    rows_kv = (N_HEADS // n_kv) * w
    seg = _seg_ones(2 * LANES)
    cq, sq = cq_ref[...], sq_ref[...]
    _stack_heads(_heads_rope(_heads_rms(q_ref[0], qg_ref[...], seg), cq, sq) * HEAD_DIM ** -0.5, q_s)
    kc = _heads_rope(_heads_rms(kc_ref[0], kg_ref[...], seg), cq, sq)
    kp = _heads_rope(_heads_rms(kp_ref[0], kg_ref[...], seg), cp_ref[...], sp_ref[...])
    kk = jnp.concatenate([kp, kc], axis=0).astype(BF16)
    vv = jnp.concatenate([vp_ref[0], vc_ref[0]], axis=0).astype(BF16)
    kvs = [([_head(kk, kv)], _with_ones(_head(vv, kv))) for kv in range(n_kv)]
    for h in range(N_HEADS):
        m_s[h * w:(h + 1) * w, :] = jnp.broadcast_to(sink_ref[:, h:h + 1], (w, LANES))
    acc_s[...] = jnp.concatenate([jnp.zeros((N_HEADS * w, HEAD_DIM), F32), jnp.ones((N_HEADS * w, HEAD_DIM), F32)],
                                 axis=1)
    mask = _band_mask(w, w, jnp.where(i > 0, 0, w))
    mk = jnp.concatenate([mask] * (chunk // w), axis=0)
    _flash_step([q_s], lambda c: kvs[c * chunk // rows_kv], lambda c: mk, m_s, acc_s, chunk)
    o = acc_s[:, :HEAD_DIM] / acc_s[:, HEAD_DIM:]
    o_ref[0] = _unstack_heads(o, w).astype(o_ref.dtype)
    ko_ref[0] = kc
    vo_ref[0] = vc_ref[0]


def _band_attn(y, q_norm, k_norm, sinks, cs, sn):
    b, s, _ = y.shape
    w = WINDOW
    kvd = SWA_KV_HEADS * HEAD_DIM
    hd = N_HEADS * HEAD_DIM
    kb, vb = hd // kvd, hd // kvd + 1
    prev = lambda i: jnp.maximum(i - 1, 0)
    full = lambda a: pl.BlockSpec(a.shape, lambda bb, i: (0,) * a.ndim)
    tab_c = pl.BlockSpec((w, LANES), lambda bb, i: (i, 0))
    tab_p = pl.BlockSpec((w, LANES), lambda bb, i: (prev(i), 0))
    q_norm, k_norm = jnp.tile(q_norm, (1, N_HEADS)), jnp.tile(k_norm, (1, SWA_KV_HEADS))
    return pl.pallas_call(
        functools.partial(_band_attn_kernel, n_kv=SWA_KV_HEADS, chunk=2 * w), grid=(b, s // w),
        scratch_shapes=[pltpu.VMEM((N_HEADS * w, HEAD_DIM), BF16),
                        pltpu.VMEM((N_HEADS * w, LANES), F32),
                        pltpu.VMEM((N_HEADS * w, 2 * HEAD_DIM), F32)],
        in_specs=[pl.BlockSpec((1, w, hd), lambda bb, i: (bb, i, 0)),
                  pl.BlockSpec((1, w, kvd), lambda bb, i: (bb, prev(i), kb)),
                  pl.BlockSpec((1, w, kvd), lambda bb, i: (bb, i, kb)),
                  pl.BlockSpec((1, w, kvd), lambda bb, i: (bb, prev(i), vb)),
                  pl.BlockSpec((1, w, kvd), lambda bb, i: (bb, i, vb)),
                  full(q_norm), full(k_norm), tab_c, tab_c, tab_p, tab_p, full(sinks)],
        out_specs=[pl.BlockSpec((1, w, hd), lambda bb, i: (bb, i, 0)),
                   pl.BlockSpec((1, w, kvd), lambda bb, i: (bb, 0, 0)),
                   pl.BlockSpec((1, w, kvd), lambda bb, i: (bb, 0, 0))],
        out_shape=[jax.ShapeDtypeStruct((b, s, hd), BF16),
                   jax.ShapeDtypeStruct((b, w, kvd), F32),
                   jax.ShapeDtypeStruct((b, w, kvd), F32)],
        compiler_params=_cparams(("arbitrary", "arbitrary")),
    )(y, y, y, y, y, q_norm, k_norm, cs, sn, cs, sn, sinks)


def _rows(ref, n):
    return jnp.concatenate([ref[t, 0] for t in range(n)], axis=0)


def _window_step(q, k_new, v_new, kbt, vbt, qn, kn, cs, sn, sink_ref, n_kv):
    ns, w = q.shape[0], kbt[0].shape[1]
    grp = N_HEADS // n_kv
    scale = HEAD_DIM ** -0.5
    t_q = lax.broadcasted_iota(jnp.int32, (grp * ns, w), 0) % ns
    col = lax.broadcasted_iota(jnp.int32, (grp * ns, w), 1)
    see_buf = col >= t_q
    see_new = col <= t_q
    pad = lambda a: jnp.concatenate([a, jnp.zeros((w - ns, a.shape[1]), a.dtype)], axis=0)
    outs, knew = [], []
    for kv in range(n_kv):
        kn_h = _rope(_rms(_head(k_new, kv), kn), cs, sn, ROPE_DIM)
        knew.append(kn_h)
        heads = range(kv * grp, (kv + 1) * grp)
        qg = jnp.concatenate([_rope(_rms(_head(q, h), qn), cs, sn, ROPE_DIM) * scale for h in heads],
                             axis=0).astype(BF16)
        s1 = jnp.where(see_buf, _dot(qg, kbt[kv].astype(BF16)), NEG)
        s2 = jnp.where(see_new, _dot_nt(qg, pad(kn_h).astype(BF16)), NEG)
        m = jnp.maximum(jnp.max(s1, axis=-1, keepdims=True), jnp.max(s2, axis=-1, keepdims=True))
        if sink_ref is not None:
            sink = _sink_col(sink_ref, heads, ns)
            m = jnp.maximum(m, sink)
        p1, p2 = jnp.exp(s1 - m), jnp.exp(s2 - m)
        den = jnp.sum(p1, axis=-1, keepdims=True) + jnp.sum(p2, axis=-1, keepdims=True)
        if sink_ref is not None:
            den = den + jnp.exp(sink - m)
        o = (_dot_nt(p1.astype(BF16), vbt[kv].astype(BF16))
             + _dot(p2.astype(BF16), pad(_head(v_new, kv)).astype(BF16))) / den
        outs += [o[g * ns:(g + 1) * ns] for g in range(grp)]

    def shifted(bt, new):
        new_t = jnp.concatenate([jnp.zeros((w - ns, new.shape[1]), F32), new], axis=0).T
        lane = lax.broadcasted_iota(jnp.int32, (HEAD_DIM, w), 1)
        return [jnp.where(lane >= w - ns, new_t[kv * HEAD_DIM:(kv + 1) * HEAD_DIM], pltpu.roll(bt[kv], w - ns, 1))
                for kv in range(n_kv)]

    return outs, shifted(kbt, jnp.concatenate(knew, axis=1)), shifted(vbt, v_new)


def _step_attn_kernel(q_ref, k_ref, v_ref, kb_ref, vb_ref, qn_ref, kn_ref, cs_ref, sn_ref, sink_ref,
                      o_ref, ko_ref, vo_ref, *, n_kv):
    ns = q_ref.shape[0]
    outs, k_out, v_out = _window_step(_rows(q_ref, ns), _rows(k_ref, ns), _rows(v_ref, ns),
                                      [kb_ref[0, 0, kv] for kv in range(n_kv)],
                                      [vb_ref[0, 0, kv] for kv in range(n_kv)], qn_ref[...], kn_ref[...],
                                      cs_ref[...], sn_ref[...], sink_ref, n_kv)
    o = jnp.concatenate(outs, axis=1)
    for t in range(ns):
        o_ref[t, 0] = o[t:t + 1]
    for kv in range(n_kv):
        ko_ref[0, kv] = k_out[kv]
        vo_ref[0, kv] = v_out[kv]


def _row_minor(cache):
    return cache.transpose(0, 1, 3, 4, 2)


def _step_attn(y, k_cache, v_cache, li, q_norm, k_norm, sinks, cs, sn):
    ns, db, n = y.shape
    w, n_kv = k_cache.shape[2], k_cache.shape[3]
    kvd = n_kv * HEAD_DIM
    hd = N_HEADS * HEAD_DIM
    kb, vb = hd // kvd, hd // kvd + 1
    y4 = y.reshape(ns, db, 1, n)
    full = lambda a: pl.BlockSpec(a.shape, lambda b: (0,) * a.ndim)
    buf_in = pl.BlockSpec((1, 1, n_kv, HEAD_DIM, w), lambda b: (li, b, 0, 0, 0))
    buf_out = pl.BlockSpec((1, n_kv, HEAD_DIM, w), lambda b: (b, 0, 0, 0))
    o, ko, vo = pl.pallas_call(
        functools.partial(_step_attn_kernel, n_kv=n_kv), grid=(db,),
        in_specs=[pl.BlockSpec((ns, 1, 1, hd), lambda b: (0, b, 0, 0)),
                  pl.BlockSpec((ns, 1, 1, kvd), lambda b: (0, b, 0, kb)),
                  pl.BlockSpec((ns, 1, 1, kvd), lambda b: (0, b, 0, vb)),
                  buf_in, buf_in,
                  full(q_norm), full(k_norm), full(cs), full(sn), full(sinks)],
        out_specs=[pl.BlockSpec((ns, 1, 1, hd), lambda b: (0, b, 0, 0)), buf_out, buf_out],
        out_shape=[jax.ShapeDtypeStruct((ns, db, 1, hd), F32),
                   jax.ShapeDtypeStruct((db, n_kv, HEAD_DIM, w), F32),
                   jax.ShapeDtypeStruct((db, n_kv, HEAD_DIM, w), F32)],
        compiler_params=_cparams(("arbitrary",)),
    )(y4, y4, y4, _row_minor(k_cache), _row_minor(v_cache), q_norm, k_norm, cs, sn, sinks)
    back = lambda a: a.transpose(0, 3, 1, 2)
    return o.reshape(ns, db, hd), back(ko), back(vo)


LANES = 128


def _lanes(x, n):
    if n <= LANES:
        return x[:, :n]
    return jnp.concatenate([x] * (n // LANES), axis=1)


def _flash_step(q_refs, kv_fn, mask_fn, m_ref, acc_ref, chunk):
    n = m_ref.shape[0] // chunk

    def scores(c):
        rows = pl.ds(c * chunk, chunk)
        k_parts = kv_fn(c)[0]
        s = _dot_nt(q_refs[0][rows, :], k_parts[0])
        for qr, kp in zip(q_refs[1:], k_parts[1:]):
            s = s + _dot_nt(qr[rows, :], kp)
        mk = None if mask_fn is None else mask_fn(c)
        return s if mk is None else jnp.where(mk, s, NEG)

    s = scores(0)
    for c in range(n):
        s_next = scores(c + 1) if c + 1 < n else None
        rows = pl.ds(c * chunk, chunk)
        vals = kv_fn(c)[1]
        m_old = m_ref[rows, :]
        m_new = jnp.maximum(m_old, jnp.max(s, axis=-1, keepdims=True))
        alpha = jnp.exp(m_old - m_new)
        p = jnp.exp(s - _lanes(m_new, s.shape[1])).astype(BF16)
        acc_ref[rows, :] = _lanes(alpha, vals.shape[1]) * acc_ref[rows, :] + _dot(p, vals)
        m_ref[rows, :] = m_new
        s = s_next


def _seg_ones(n, head=HEAD_DIM):
    r = lax.broadcasted_iota(jnp.int32, (n, n), 0) // head
    c = lax.broadcasted_iota(jnp.int32, (n, n), 1) // head
    return jnp.where(r == c, 1.0, 0.0).astype(BF16)


def _heads_rms(x, g, seg, head=HEAD_DIM):
    sq = x * x
    hi = sq.astype(BF16)
    lo = (sq - hi.astype(F32)).astype(BF16)
    n = seg.shape[0]
    parts = []
    for j in range(0, x.shape[1], n):
        k = min(n, x.shape[1] - j)
        b = seg[:k, :k]
        parts.append(_dot(hi[:, j:j + k], b) + _dot(lo[:, j:j + k], b))
    ss = parts[0] if len(parts) == 1 else jnp.concatenate(parts, axis=1)
    return x * lax.rsqrt(ss * (1.0 / head) + EPS) * g


def _heads_rope(x, c2, s2, head=HEAD_DIM, rot=ROPE_DIM):
    w = x.shape[1]
    half = rot // 2
    lane = lax.broadcasted_iota(jnp.int32, x.shape, 1) % head
    sw = jnp.where(lane < half, pltpu.roll(x, w - half, 1), pltpu.roll(x, half, 1))
    return x * _lanes(c2, w) + sw * _lanes(s2, w)


def _stack_heads(x, q_s):
    nq = x.shape[0]
    for h in range(x.shape[1] // HEAD_DIM):
        q_s[h * nq:(h + 1) * nq, :] = _head(x, h).astype(q_s.dtype)


def _unstack_heads(o, nq):
    return jnp.concatenate([o[h * nq:(h + 1) * nq] for h in range(o.shape[0] // nq)], axis=1)


def _flash_init(m_ref, acc_ref):
    m_ref[...] = jnp.full(m_ref.shape, NEG, F32)
    acc_ref[...] = jnp.zeros(acc_ref.shape, F32)


def _with_ones(v):
    pad = LANES - v.shape[1] % LANES
    return jnp.concatenate([v, jnp.ones((v.shape[0], pad), v.dtype)], axis=1)


def _mla_queries(q, nn, nr, cs, sn, wuk_ref, ql_s, qr_s):
    nq = q.shape[0]
    scale = (MLA_NOPE + MLA_ROPE) ** -0.5
    off = N_HEADS * MLA_NOPE
    qn = _heads_rms(q[:, :off], nn, _seg_ones(2 * LANES, MLA_NOPE), MLA_NOPE).astype(BF16)
    qr = _heads_rms(q[:, off:], nr, _seg_ones(2 * LANES, MLA_ROPE), MLA_ROPE)
    qr = (_heads_rope(qr, cs, sn, MLA_ROPE, MLA_ROPE) * scale).astype(BF16)
    for h in range(N_HEADS):
        rows = slice(h * nq, (h + 1) * nq)
        ql_s[rows, :] = (_dot(qn[:, h * MLA_NOPE:(h + 1) * MLA_NOPE], wuk_ref[h]) * scale).astype(BF16)
        qr_s[rows, :] = qr[:, h * MLA_ROPE:(h + 1) * MLA_ROPE]


def _mla_output(acc, den, wuv_ref, nq):
    o_lat = (acc / den).astype(BF16)
    return jnp.concatenate([_dot(o_lat[h * nq:(h + 1) * nq], wuv_ref[h]) for h in range(N_HEADS)], axis=1)


def _mla_attn_kernel(q_ref, c_ref, kr_ref, cs_ref, sn_ref, nn_ref, nr_ref, wuk_ref, wuv_ref, o_ref,
                     ql_s, qr_s, m_s, acc_s, cb_s, kb_s, *, tq, tk, chunk):
    i = pl.program_id(1)

    @pl.when(i == 0)
    def _():
        cb_s[...] = _with_ones(c_ref[0].astype(BF16))
        kb_s[...] = kr_ref[0].astype(BF16)

    _mla_queries(q_ref[0], nn_ref[...], nr_ref[...], cs_ref[...], sn_ref[...], wuk_ref, ql_s, qr_s)
    _flash_init(m_s, acc_s)
    t_pos = i * tq + lax.broadcasted_iota(jnp.int32, (tq, tk), 0)
    k_off = lax.broadcasted_iota(jnp.int32, (tq, tk), 1)

    def step(j, masked):
        start = pl.multiple_of(j * tk, tk)
        cbx = cb_s[pl.ds(start, tk), :]
        kb = kb_s[pl.ds(start, tk), :]
        mask_fn = None
        if masked:
            mk = jnp.concatenate([(start + k_off) <= t_pos] * (chunk // tq), axis=0)
            mask_fn = lambda c: mk
        kv = ([cbx[:, :MLA_KV_LORA], kb], cbx)
        _flash_step([ql_s, qr_s], lambda c: kv, mask_fn, m_s, acc_s, chunk)

    def full_body(j, carry):
        step(j, False)
        return carry

    def diag_body(j, carry):
        step(j, True)
        return carry

    n_full = (i * tq) // tk
    lax.fori_loop(0, n_full, full_body, 0)
    lax.fori_loop(n_full, (i * tq + tq - 1) // tk + 1, diag_body, 0)
    o_ref[0] = _mla_output(acc_s[:, :MLA_KV_LORA], _lanes(acc_s[:, MLA_KV_LORA:], MLA_KV_LORA), wuv_ref,
                           tq).astype(o_ref.dtype)


def _mla_attn(q, c, kr, cs, sn, nn, nr, wuk, wuv, tq, tk, chunk):
    b, s, nqc = q.shape
    hd = N_HEADS * HEAD_DIM
    full = lambda a: pl.BlockSpec(a.shape, lambda bb, i: (0,) * a.ndim)
    tab = pl.BlockSpec((tq, LANES), lambda bb, i: (i, 0))
    return pl.pallas_call(
        functools.partial(_mla_attn_kernel, tq=tq, tk=tk, chunk=chunk), grid=(b, s // tq),
        in_specs=[pl.BlockSpec((1, tq, nqc), lambda bb, i: (bb, i, 0)),
                  pl.BlockSpec((1, s, MLA_KV_LORA), lambda bb, i: (bb, 0, 0)),
                  pl.BlockSpec((1, s, MLA_ROPE), lambda bb, i: (bb, 0, 0)),
                  tab, tab, full(nn), full(nr), full(wuk), full(wuv)],
        out_specs=pl.BlockSpec((1, tq, hd), lambda bb, i: (bb, i, 0)),
        out_shape=jax.ShapeDtypeStruct((b, s, hd), BF16),
        scratch_shapes=[pltpu.VMEM((N_HEADS * tq, MLA_KV_LORA), BF16),
                        pltpu.VMEM((N_HEADS * tq, MLA_ROPE), BF16),
                        pltpu.VMEM((N_HEADS * tq, LANES), F32),
                        pltpu.VMEM((N_HEADS * tq, MLA_KV_LORA + LANES), F32),
                        pltpu.VMEM((s, MLA_KV_LORA + LANES), BF16),
                        pltpu.VMEM((s, MLA_ROPE), BF16)],
        compiler_params=_cparams(("arbitrary", "arbitrary")),
    )(q, c, kr, cs, sn, nn, nr, wuk, wuv)


def _online_step(s, pv_fn, m_ref, acc_ref):
    m_old = m_ref[...]
    m_new = jnp.maximum(m_old, jnp.max(s, axis=-1, keepdims=True))
    alpha = jnp.exp(m_old - m_new)
    p = jnp.exp(s - _lanes(m_new, s.shape[1])).astype(BF16)
    acc_ref[...] = _lanes(alpha, acc_ref.shape[1]) * acc_ref[...] + pv_fn(p)
    m_ref[...] = m_new


class _PageStream:
    def __init__(self, pt_ref, seq, layer, pools, bufs, sems, slots):
        self.pt_ref, self.seq, self.layer, self.slots = pt_ref, seq, layer, slots
        self.pools, self.bufs, self.sems = pools, bufs, sems

    def _copies(self, group, half):
        out = []
        for k in range(self.slots):
            page = self.pt_ref[self.seq, group * self.slots + k]
            for i, (pool, buf) in enumerate(zip(self.pools, self.bufs)):
                out.append(pltpu.make_async_copy(pool.at[self.layer, page], buf.at[half, k], self.sems.at[half, i]))
        return out

    def start(self, group, half):
        for c in self._copies(group, half):
            c.start()

    def wait(self, group, half):
        for c in self._copies(group, half):
            c.wait()

    def sweep(self, n_groups, compute):
        def body(g, carry):
            half = g % 2

            @pl.when(g + 1 < n_groups)
            def _():
                self.start(g + 1, 1 - half)

            self.wait(g, half)
            compute(g, half)
            return carry

        lax.fori_loop(0, n_groups, body, 0)


def _mla_decode_kernel(pt_ref, lat_hbm, kr_hbm, q_ref, cn_ref, kn_ref, cs_ref, sn_ref, nn_ref, nr_ref, wuk_ref,
                       wuv_ref, o_ref, ql_s, qr_s, m_s, acc_s, lat_buf, kr_buf, sems, *, ns, slots, npg, layer):
    stream = _PageStream(pt_ref, pl.program_id(0), layer, [lat_hbm, kr_hbm], [lat_buf, kr_buf], sems, slots)
    stream.start(0, 0)
    _mla_queries(_rows(q_ref, ns), nn_ref[...], nr_ref[...], cs_ref[...], sn_ref[...], wuk_ref, ql_s, qr_s)
    _flash_init(m_s, acc_s)

    def cached(g, half):
        cbx = _with_ones(lat_buf[half].reshape(slots * PAGE_SIZE, MLA_KV_LORA).astype(BF16))
        krt = jnp.concatenate([kr_buf[half, k] for k in range(slots)], axis=1).astype(BF16)
        _online_step(_dot_nt(ql_s[...], cbx[:, :MLA_KV_LORA]) + _dot(qr_s[...], krt), lambda p: _dot(p, cbx),
                     m_s, acc_s)

    stream.sweep(npg, cached)

    pad = PAGE_SIZE - ns
    cb = jnp.concatenate([_rows(cn_ref, ns), jnp.zeros((pad, MLA_KV_LORA), F32)], axis=0).astype(BF16)
    kb = jnp.concatenate([_rows(kn_ref, ns), jnp.zeros((pad, MLA_ROPE), F32)], axis=0).astype(BF16)
    s = _dot_nt(ql_s[...], cb) + _dot_nt(qr_s[...], kb)
    row = lax.broadcasted_iota(jnp.int32, s.shape, 0) % ns
    col = lax.broadcasted_iota(jnp.int32, s.shape, 1)
    cbx_new = _with_ones(cb)
    _online_step(jnp.where(col <= row, s, NEG), lambda p: _dot(p, cbx_new), m_s, acc_s)
    o = _mla_output(acc_s[:, :MLA_KV_LORA], _lanes(acc_s[:, MLA_KV_LORA:], MLA_KV_LORA), wuv_ref, ns)
    for t in range(ns):
        o_ref[t, 0] = o[t:t + 1]


def _mla_decode(q, c, kr, lat_pool, kr_pool, li, page_table, cs, sn, nn, nr, wuk, wuv):
    ns, db, nqc = q.shape
    hd = N_HEADS * HEAD_DIM
    slots = PAGE_SLOTS
    npg = page_table.shape[1] // slots
    full = lambda a: pl.BlockSpec(a.shape, lambda b, pt: (0,) * a.ndim)
    hbm = pl.BlockSpec(memory_space=pl.ANY)
    step = lambda width: pl.BlockSpec((ns, 1, 1, width), lambda b, pt: (0, b, 0, 0))
    grid_spec = pltpu.PrefetchScalarGridSpec(
        num_scalar_prefetch=1, grid=(db,),
        in_specs=[hbm, hbm, step(nqc), step(MLA_KV_LORA), step(MLA_ROPE),
                  full(cs), full(sn), full(nn), full(nr), full(wuk), full(wuv)],
        out_specs=step(hd),
        scratch_shapes=[pltpu.VMEM((N_HEADS * ns, MLA_KV_LORA), BF16),
                        pltpu.VMEM((N_HEADS * ns, MLA_ROPE), BF16),
                        pltpu.VMEM((N_HEADS * ns, LANES), F32),
                        pltpu.VMEM((N_HEADS * ns, MLA_KV_LORA + LANES), F32),
                        pltpu.VMEM((2, slots, PAGE_SIZE, MLA_KV_LORA), F32),
                        pltpu.VMEM((2, slots, MLA_ROPE, PAGE_SIZE), F32),
                        pltpu.SemaphoreType.DMA((2, 2))])
    kr_pool_t = kr_pool.transpose(0, 1, 3, 2)
    o = pl.pallas_call(
        functools.partial(_mla_decode_kernel, ns=ns, slots=slots, npg=npg, layer=li), grid_spec=grid_spec,
        out_shape=jax.ShapeDtypeStruct((ns, db, 1, hd), F32),
        compiler_params=_cparams(("arbitrary",)),
    )(page_table, lat_pool, kr_pool_t,
      q.reshape(ns, db, 1, nqc), c.reshape(ns, db, 1, MLA_KV_LORA), kr.reshape(ns, db, 1, MLA_ROPE),
      cs, sn, nn, nr, wuk, wuv)
    return o.reshape(ns, db, hd)


NSA_KVD = NSA_KV_HEADS * HEAD_DIM
NSA_GRP = N_HEADS // NSA_KV_HEADS
NSA_PAIR = NSA_SEL_BLOCK // NSA_CMP_BLOCK


def _nsa_prep_kernel(kc_ref, vc_ref, ks_ref, kw_ref, wk_ref, wv_ref, kcn_ref, ksn_ref, kwn_ref, cs_ref, sn_ref,
                     kso_ref, kwo_ref, kcmp_ref, vcmp_ref):
    cs, sn = cs_ref[...], sn_ref[...]
    kc, vc, ks, kw = kc_ref[0], vc_ref[0], ks_ref[0], kw_ref[0]
    kso, kwo, kcmp, vcmp = [], [], [], []
    for kv in range(NSA_KV_HEADS):
        kso.append(_rope(_rms(_head(ks, kv), ksn_ref[...]), cs, sn, ROPE_DIM))
        kwo.append(_rope(_rms(_head(kw, kv), kwn_ref[...]), cs, sn, ROPE_DIM))
        kcmp.append(_rms(_dot(wk_ref[kv], _head(kc, kv).astype(BF16)), kcn_ref[...]))
        vcmp.append(_dot(wv_ref[kv], _head(vc, kv).astype(BF16)))
    kso_ref[0] = jnp.concatenate(kso, axis=1)
    kwo_ref[0] = jnp.concatenate(kwo, axis=1)
    kcmp_ref[0] = jnp.concatenate(kcmp, axis=1)
    vcmp_ref[0] = jnp.concatenate(vcmp, axis=1)


def _cmp_matrix(w, nc):
    eye = jnp.eye(nc, dtype=F32)
    return (eye[None, :, :, None] * w.T[:, None, None, :]).reshape(w.shape[1], nc, nc * NSA_CMP_BLOCK).astype(BF16)


def _nsa_prep(y, cmp_wk, cmp_wv, k_norm, cs, sn):
    b, s, _ = y.shape
    nc = s // NSA_CMP_BLOCK
    base = N_HEADS * HEAD_DIM // NSA_KVD
    col = lambda j: pl.BlockSpec((1, s, NSA_KVD), lambda bb: (bb, 0, base + j))
    full = lambda a: pl.BlockSpec(a.shape, lambda bb: (0,) * a.ndim)
    wk, wv = _cmp_matrix(cmp_wk, nc), _cmp_matrix(cmp_wv, nc)
    kn = [k_norm[j:j + 1] for j in range(3)]
    seq = pl.BlockSpec((1, s, NSA_KVD), lambda bb: (bb, 0, 0))
    blk = pl.BlockSpec((1, nc, NSA_KVD), lambda bb: (bb, 0, 0))
    return pl.pallas_call(
        _nsa_prep_kernel, grid=(b,),
        in_specs=[col(0), col(1), col(2), col(4), full(wk), full(wv), full(kn[0]), full(kn[1]), full(kn[2]),
                  full(cs), full(sn)],
        out_specs=[seq, seq, blk, blk],
        out_shape=[jax.ShapeDtypeStruct((b, s, NSA_KVD), F32), jax.ShapeDtypeStruct((b, s, NSA_KVD), F32),
                   jax.ShapeDtypeStruct((b, nc, NSA_KVD), F32), jax.ShapeDtypeStruct((b, nc, NSA_KVD), F32)],
        compiler_params=_cparams(("arbitrary",)),
    )(y, y, y, y, wk, wv, kn[0], kn[1], kn[2], cs, sn)


def _cmp_attend(q, kcmp, vcmp, mask, ng, row_minor=False):
    nq = q.shape[0] // ng
    mk = jnp.concatenate([mask] * ng, axis=0)
    qk, pv = (_dot, _dot_nt) if row_minor else (_dot_nt, _dot)
    s = jnp.where(mk, qk(q, kcmp.astype(BF16)), NEG)
    m = jnp.max(s, axis=-1, keepdims=True)
    p = jnp.where(mk, jnp.exp(s - m), 0.0)
    p = p / jnp.maximum(jnp.sum(p, axis=-1, keepdims=True), TINY)
    o = pv(p.astype(BF16), vcmp.astype(BF16))
    imp = p[0:nq]
    for g in range(1, ng):
        imp = imp + p[g * nq:(g + 1) * nq]
    return o, imp


def _pair_sum(imp):
    n = imp.shape[1]
    lane = lax.broadcasted_iota(jnp.int32, imp.shape, 1)
    return imp + jnp.where(lane % 2 == 0, pltpu.roll(imp, n - 1, 1), pltpu.roll(imp, 1, 1))


def _select(impx, nblk, seg, n_sel):
    lane = lax.broadcasted_iota(jnp.int32, impx.shape, 1)
    blk = (lane % seg) // NSA_PAIR
    nseg = impx.shape[1] // seg
    cnt = jnp.zeros(impx.shape, jnp.int32)
    for j in range(nblk):
        col = impx[:, j * NSA_PAIR:j * NSA_PAIR + 1]
        for sg in range(1, nseg):
            c = sg * seg + j * NSA_PAIR
            col = jnp.where(lane < sg * seg, col, impx[:, c:c + 1])
        beats = (col > impx) | ((col == impx) & (j < blk))
        cnt = cnt + beats.astype(jnp.int32)
    return (cnt < n_sel).astype(F32)


def _expand_mask(sel, start, tk):
    nc = sel.shape[1]
    n = lax.broadcasted_iota(jnp.int32, (nc, tk), 0)
    k = lax.broadcasted_iota(jnp.int32, (nc, tk), 1)
    e = jnp.where((start + k) // NSA_CMP_BLOCK == n, 1.0, 0.0).astype(BF16)
    return _dot(sel.astype(BF16), e) > 0.5


def _nsa_attn_kernel(q_ref, gl_ref, ks_ref, vs_ref, kwp_ref, kwc_ref, vwp_ref, vwc_ref, kcmp_ref, vcmp_ref,
                     qg_ref, c2_ref, s2_ref, o_ref, qn_s, qr_s, ks_s, vs_s, m_s, acc_s, mw_s, accw_s,
                     *, tk, n_sel, chunk):
    i = pl.program_id(1)
    tq = q_ref.shape[1]
    nc = kcmp_ref.shape[1]
    rows_kv = NSA_GRP * tq
    rep = chunk // tq
    kv_of = lambda c: c * chunk // rows_kv
    kv_range = range(NSA_KV_HEADS)

    @pl.when(i == 0)
    def _():
        ks_s[...] = ks_ref[0].astype(BF16)
        vs = vs_ref[0].astype(BF16)
        vs_s[...] = jnp.concatenate([_with_ones(_head(vs, kv)) for kv in kv_range], axis=1)

    qn = _heads_rms(q_ref[0], qg_ref[...], _seg_ones(2 * LANES)) * HEAD_DIM ** -0.5
    _stack_heads(qn, qn_s)
    _stack_heads(_heads_rope(qn, c2_ref[...], s2_ref[...]), qr_s)

    t_c = i * tq + lax.broadcasted_iota(jnp.int32, (tq, nc), 0)
    n_c = lax.broadcasted_iota(jnp.int32, (tq, nc), 1)
    cmask = (n_c + 1) * NSA_CMP_BLOCK - 1 <= t_c
    o_cmp, imps = [], []
    for kv in kv_range:
        oc, imp = _cmp_attend(qn_s[kv * rows_kv:(kv + 1) * rows_kv, :], _head(kcmp_ref[0], kv),
                              _head(vcmp_ref[0], kv), cmask, NSA_GRP)
        o_cmp.append(oc)
        imps.append(imp)
    imp = _pair_sum(jnp.concatenate(imps, axis=1))
    t_i = i * tq + lax.broadcasted_iota(jnp.int32, imp.shape, 0)
    blk = (lax.broadcasted_iota(jnp.int32, imp.shape, 1) % nc) // NSA_PAIR
    impx = jnp.where(blk == t_i // NSA_SEL_BLOCK, jnp.inf, jnp.where(blk * NSA_SEL_BLOCK <= t_i, imp, -jnp.inf))
    sel = _select(impx, nc // NSA_PAIR, nc, n_sel)

    _flash_init(m_s, acc_s)
    selk = [sel[:, kv * nc:(kv + 1) * nc] for kv in kv_range]
    t_k = i * tq + lax.broadcasted_iota(jnp.int32, (tq, tk), 0)
    k_off = lax.broadcasted_iota(jnp.int32, (tq, tk), 1)

    def body(j, carry):
        start = pl.multiple_of(j * tk, tk)
        kb = ks_s[pl.ds(start, tk), :]
        vbx = vs_s[pl.ds(start, tk), :]
        causal = (start + k_off) <= t_k
        mks = [jnp.concatenate([_expand_mask(selk[kv], start, tk) & causal] * rep, axis=0) for kv in kv_range]
        kvs = [([_head(kb, kv)], vbx[:, kv * LANES:(kv + 1) * LANES]) for kv in kv_range]
        _flash_step([qr_s], lambda c: kvs[kv_of(c)], lambda c: mks[kv_of(c)], m_s, acc_s, chunk)
        return carry

    lax.fori_loop(0, (i * tq + tq - 1) // tk + 1, body, 0)

    _flash_init(mw_s, accw_s)
    kk = jnp.concatenate([kwp_ref[0], kwc_ref[0]], axis=0).astype(BF16)
    vv = jnp.concatenate([vwp_ref[0], vwc_ref[0]], axis=0).astype(BF16)
    kvw = [([_head(kk, kv)], _with_ones(_head(vv, kv))) for kv in kv_range]
    wmk = jnp.concatenate([_band_mask(tq, tq, jnp.where(i > 0, 0, tq))] * rep, axis=0)
    _flash_step([qr_s], lambda c: kvw[kv_of(c)], lambda c: wmk, mw_s, accw_s, chunk)

    o_sel = acc_s[:, :HEAD_DIM] / acc_s[:, HEAD_DIM:]
    o_win = accw_s[:, :HEAD_DIM] / accw_s[:, HEAD_DIM:]
    gates = jax.nn.sigmoid(gl_ref[0])
    outs = []
    for h in range(N_HEADS):
        kv, g = divmod(h, NSA_GRP)
        outs.append(gates[:, 3 * h:3 * h + 1] * o_cmp[kv][g * tq:(g + 1) * tq]
                    + gates[:, 3 * h + 1:3 * h + 2] * o_sel[h * tq:(h + 1) * tq]
                    + gates[:, 3 * h + 2:3 * h + 3] * o_win[h * tq:(h + 1) * tq])
    o_ref[0] = jnp.concatenate(outs, axis=1).astype(o_ref.dtype)


def _nsa_attn(y, ksn, kwn, kcmp, vcmp, q_norm, cs, sn, tk):
    b, s, _ = y.shape
    tq = WINDOW
    hd = N_HEADS * HEAD_DIM
    nc = kcmp.shape[1]
    base = hd // NSA_KVD
    n_sel = min(NSA_TOPK, s // NSA_SEL_BLOCK)
    prev = lambda i: jnp.maximum(i - 1, 0)
    full = lambda a: pl.BlockSpec(a.shape, lambda bb, i: (0,) * a.ndim)
    tab = pl.BlockSpec((tq, LANES), lambda bb, i: (i, 0))
    q_norm = jnp.tile(q_norm, (1, N_HEADS))
    return pl.pallas_call(
        functools.partial(_nsa_attn_kernel, tk=tk, n_sel=n_sel, chunk=2 * tq), grid=(b, s // tq),
        in_specs=[pl.BlockSpec((1, tq, hd), lambda bb, i: (bb, i, 0)),
                  pl.BlockSpec((1, tq, NSA_KVD), lambda bb, i: (bb, i, base + 6)),
                  pl.BlockSpec((1, s, NSA_KVD), lambda bb, i: (bb, 0, 0)),
                  pl.BlockSpec((1, s, NSA_KVD), lambda bb, i: (bb, 0, base + 3)),
                  pl.BlockSpec((1, tq, NSA_KVD), lambda bb, i: (bb, prev(i), 0)),
                  pl.BlockSpec((1, tq, NSA_KVD), lambda bb, i: (bb, i, 0)),
                  pl.BlockSpec((1, tq, NSA_KVD), lambda bb, i: (bb, prev(i), base + 5)),
                  pl.BlockSpec((1, tq, NSA_KVD), lambda bb, i: (bb, i, base + 5)),
                  pl.BlockSpec((1, nc, NSA_KVD), lambda bb, i: (bb, 0, 0)),
                  pl.BlockSpec((1, nc, NSA_KVD), lambda bb, i: (bb, 0, 0)),
                  full(q_norm), tab, tab],
        out_specs=pl.BlockSpec((1, tq, hd), lambda bb, i: (bb, i, 0)),
        out_shape=jax.ShapeDtypeStruct((b, s, hd), BF16),
        scratch_shapes=[pltpu.VMEM((N_HEADS * tq, HEAD_DIM), BF16),
                        pltpu.VMEM((N_HEADS * tq, HEAD_DIM), BF16),
                        pltpu.VMEM((s, NSA_KVD), BF16),
                        pltpu.VMEM((s, NSA_KV_HEADS * LANES), BF16),
                        pltpu.VMEM((N_HEADS * tq, LANES), F32),
                        pltpu.VMEM((N_HEADS * tq, 2 * HEAD_DIM), F32),
                        pltpu.VMEM((N_HEADS * tq, LANES), F32),
                        pltpu.VMEM((N_HEADS * tq, 2 * HEAD_DIM), F32)],
        compiler_params=_cparams(("arbitrary", "arbitrary")),
    )(y, y, ksn, y, kwn, kwn, y, y, kcmp, vcmp, q_norm, cs, sn)


def _nsa_decode_kernel(pt_ref, kc_hbm, vc_hbm, ks_hbm, vs_hbm,
                       q_ref, gl_ref, ks_ref, vs_ref, kw_ref, vw_ref, kwb_ref, vwb_ref, wrow_ref, eloc_ref,
                       qn_ref, kcn_ref, ksn_ref, kwn_ref, cs_ref, sn_ref,
                       o_ref, kso_ref, kwo_ref, vwo_ref,
                       cmp_s, sel_s, ocmp_s, q_s, m_s, acc_s, k_buf, v_buf, sems,
                       *, ns, slots, npg, past, n_sel, layer):
    seq = pl.program_id(0)
    scale = HEAD_DIM ** -0.5
    nc = cmp_s.shape[1]
    rows_per_step = slots * PAGE_SIZE
    cper = rows_per_step // NSA_CMP_BLOCK
    kv_range = range(NSA_KV_HEADS)
    grp = lambda xs, kv: xs[kv * NSA_GRP:(kv + 1) * NSA_GRP]
    cmp_stream = _PageStream(pt_ref, seq, layer, [kc_hbm, vc_hbm], [k_buf, v_buf], sems, slots)
    sel_stream = _PageStream(pt_ref, seq, layer, [ks_hbm, vs_hbm], [k_buf, v_buf], sems, slots)
    cmp_stream.start(0, 0)
    pages = lambda buf, half, kv: jnp.concatenate([buf[half, k, kv] for k in range(slots)], axis=1)

    def block_sums(g, half):
        lhs = jnp.concatenate([pages(buf, half, kv) * wrow_ref[2 * j + kv:2 * j + kv + 1, :]
                               for j, buf in enumerate((k_buf, v_buf)) for kv in kv_range], axis=0)
        res = _dot(lhs.astype(BF16), eloc_ref[...])
        for k in range(npg):
            @pl.when(g == k)
            def _():
                cmp_s[:, k * cper:(k + 1) * cper] = res

    cmp_stream.sweep(npg, block_sums)
    sel_stream.start(0, 0)

    q = _rows(q_ref, ns)
    qn = [_rms(_head(q, h), qn_ref[...]) * scale for h in range(N_HEADS)]
    qr = [_rope(x, cs_ref[...], sn_ref[...], ROPE_DIM) for x in qn]
    t_c = past + lax.broadcasted_iota(jnp.int32, (ns, nc), 0)
    n_c = lax.broadcasted_iota(jnp.int32, (ns, nc), 1)
    cmask = (n_c + 1) * NSA_CMP_BLOCK - 1 <= t_c
    imps = []
    for kv in kv_range:
        kct = cmp_s[kv * HEAD_DIM:(kv + 1) * HEAD_DIM, :]
        kct = kct * lax.rsqrt(jnp.mean(kct * kct, axis=0, keepdims=True) + EPS) * kcn_ref[...]
        vct = cmp_s[(NSA_KV_HEADS + kv) * HEAD_DIM:(NSA_KV_HEADS + kv + 1) * HEAD_DIM, :]
        oc, imp = _cmp_attend(jnp.concatenate(grp(qn, kv), axis=0).astype(BF16), kct, vct, cmask, NSA_GRP,
                              row_minor=True)
        ocmp_s[kv] = oc
        imps.append(_pair_sum(imp))
        q_s[kv] = jnp.concatenate(grp(qr, kv), axis=0).astype(BF16)
    for kv in kv_range:
        sel_s[kv] = _select(imps[kv], nc // NSA_PAIR, nc, n_sel - 1)
    _flash_init(m_s, acc_s)

    def selected(g, half):
        start = g * rows_per_step
        for kv in kv_range:
            kst = pages(k_buf, half, kv).astype(BF16)
            vst = pages(v_buf, half, kv)
            vtx = jnp.concatenate([vst, jnp.ones(vst.shape, F32)], axis=0).astype(BF16)
            mk = jnp.concatenate([_expand_mask(sel_s[kv], start, rows_per_step)] * NSA_GRP, axis=0)
            _online_step(jnp.where(mk, _dot(q_s[kv], kst), NEG), lambda p, v=vtx: _dot_nt(p, v),
                         m_s.at[kv], acc_s.at[kv])

    sel_stream.sweep(npg, selected)

    pad = lambda a: jnp.concatenate([a, jnp.zeros((PAGE_SIZE - ns, a.shape[1]), a.dtype)], axis=0)
    vs_new = _rows(vs_ref, ns)
    ks_raw = _rows(ks_ref, ns)
    ksn = jnp.concatenate([_rope(_rms(_head(ks_raw, kv), ksn_ref[...]), cs_ref[...], sn_ref[...], ROPE_DIM)
                           for kv in kv_range], axis=1)
    kso_ref[0] = ksn
    kb = pad(ksn).astype(BF16)
    row = lax.broadcasted_iota(jnp.int32, (NSA_GRP * ns, PAGE_SIZE), 0) % ns
    col = lax.broadcasted_iota(jnp.int32, (NSA_GRP * ns, PAGE_SIZE), 1)
    for kv in kv_range:
        vbx = _with_ones(pad(_head(vs_new, kv)).astype(BF16))
        _online_step(jnp.where(col <= row, _dot_nt(q_s[kv], _head(kb, kv)), NEG), lambda p, v=vbx: _dot(p, v),
                     m_s.at[kv], acc_s.at[kv])
    outs_w, kw_out, vw_out = _window_step(
        q, _rows(kw_ref, ns), _rows(vw_ref, ns), [kwb_ref[0, 0, kv] for kv in kv_range],
        [vwb_ref[0, 0, kv] for kv in kv_range], qn_ref[...], kwn_ref[...], cs_ref[...], sn_ref[...], None,
        NSA_KV_HEADS)
    for kv in kv_range:
        kwo_ref[0, kv] = kw_out[kv]
        vwo_ref[0, kv] = vw_out[kv]
    gates = jax.nn.sigmoid(_rows(gl_ref, ns))
    outs = []
    for kv in kv_range:
        o_sel = acc_s[kv, :, :HEAD_DIM] / acc_s[kv, :, HEAD_DIM:]
        o_cmp = ocmp_s[kv]
        for g in range(NSA_GRP):
            h = kv * NSA_GRP + g
            outs.append(gates[:, 3 * h:3 * h + 1] * o_cmp[g * ns:(g + 1) * ns]
                        + gates[:, 3 * h + 1:3 * h + 2] * o_sel[g * ns:(g + 1) * ns]
                        + gates[:, 3 * h + 2:3 * h + 3] * outs_w[h])
    o = jnp.concatenate(outs, axis=1)
    for t in range(ns):
        o_ref[t, 0] = o[t:t + 1]


def _nsa_decode(y, pools, kw_cache, vw_cache, li, page_table, cmp_wk, cmp_wv, q_norm, k_norm, cs, sn):
    ns, db, n = y.shape
    hd = N_HEADS * HEAD_DIM
    slots = PAGE_SLOTS
    n_pages = page_table.shape[1]
    npg = n_pages // slots
    past = n_pages * PAGE_SIZE
    nc = past // NSA_CMP_BLOCK
    base = hd // NSA_KVD
    w = kw_cache.shape[2]
    n_blocks = -(-(past + ns) // NSA_SEL_BLOCK)
    n_sel = min(NSA_TOPK, n_blocks)
    y4 = y.reshape(ns, db, 1, n)
    full = lambda a: pl.BlockSpec(a.shape, lambda b, pt: (0,) * a.ndim)
    hbm = pl.BlockSpec(memory_space=pl.ANY)
    step = lambda width, cb: pl.BlockSpec((ns, 1, 1, width), lambda b, pt: (0, b, 0, cb))
    rows = slots * PAGE_SIZE
    cper = rows // NSA_CMP_BLOCK
    wrow = jnp.tile(jnp.concatenate([cmp_wk.T, cmp_wv.T], axis=0), (1, cper))
    eloc = (jnp.arange(rows)[:, None] // NSA_CMP_BLOCK == jnp.arange(cper)[None, :]).astype(BF16)
    kn = [k_norm[j:j + 1] for j in range(3)]
    kcn_col = k_norm[0].reshape(HEAD_DIM, 1)
    seq_out = pl.BlockSpec((1, ns, NSA_KVD), lambda b, pt: (b, 0, 0))
    buf_out = pl.BlockSpec((1, NSA_KV_HEADS, HEAD_DIM, w), lambda b, pt: (b, 0, 0, 0))
    buf_in = pl.BlockSpec((1, 1, NSA_KV_HEADS, HEAD_DIM, w), lambda b, pt: (li, b, 0, 0, 0))
    page_buf = pltpu.VMEM((2, slots, NSA_KV_HEADS, HEAD_DIM, PAGE_SIZE), F32)
    grid_spec = pltpu.PrefetchScalarGridSpec(
        num_scalar_prefetch=1, grid=(db,),
        in_specs=[hbm, hbm, hbm, hbm,
                  step(hd, 0), step(NSA_KVD, base + 6), step(NSA_KVD, base + 2), step(NSA_KVD, base + 3),
                  step(NSA_KVD, base + 4), step(NSA_KVD, base + 5), buf_in, buf_in,
                  full(wrow), full(eloc),
                  full(q_norm), full(kcn_col), full(kn[1]), full(kn[2]), full(cs), full(sn)],
        out_specs=[step(hd, 0), seq_out, buf_out, buf_out],
        scratch_shapes=[pltpu.VMEM((2 * NSA_KV_HEADS * HEAD_DIM, nc), F32),
                        pltpu.VMEM((NSA_KV_HEADS, ns, nc), F32),
                        pltpu.VMEM((NSA_KV_HEADS, NSA_GRP * ns, HEAD_DIM), F32),
                        pltpu.VMEM((NSA_KV_HEADS, NSA_GRP * ns, HEAD_DIM), BF16),
                        pltpu.VMEM((NSA_KV_HEADS, NSA_GRP * ns, LANES), F32),
                        pltpu.VMEM((NSA_KV_HEADS, NSA_GRP * ns, 2 * HEAD_DIM), F32),
                        page_buf, page_buf, pltpu.SemaphoreType.DMA((2, 2))])
    kc_pool, vc_pool, ks_pool, vs_pool = [_row_minor(p) for p in pools]
    o, kso, kwo, vwo = pl.pallas_call(
        functools.partial(_nsa_decode_kernel, ns=ns, slots=slots, npg=npg, past=past, n_sel=n_sel, layer=li),
        grid_spec=grid_spec,
        out_shape=[jax.ShapeDtypeStruct((ns, db, 1, hd), F32),
                   jax.ShapeDtypeStruct((db, ns, NSA_KVD), F32),
                   jax.ShapeDtypeStruct((db, NSA_KV_HEADS, HEAD_DIM, w), F32),
                   jax.ShapeDtypeStruct((db, NSA_KV_HEADS, HEAD_DIM, w), F32)],
        compiler_params=_cparams(("arbitrary",)),
    )(page_table, kc_pool, vc_pool, ks_pool, vs_pool,
      y4, y4, y4, y4, y4, y4, _row_minor(kw_cache), _row_minor(vw_cache), wrow, eloc,
      q_norm, kcn_col, kn[1], kn[2], cs, sn)
    back = lambda a: a.transpose(0, 3, 1, 2)
    return o.reshape(ns, db, hd), kso, back(kwo), back(vwo)


def kernel(x_prompt, x_sample, cache_swa_k, cache_swa_v, cache_mla_latent, cache_mla_krope, cache_nsa_kcmp, cache_nsa_vcmp, cache_nsa_ksel, cache_nsa_vsel, cache_nsa_kwin, cache_nsa_vwin, state_conv_ffn, page_table, c_prompt, c_sample, ada_w, ada_b, norm_mix, norm_ffn, ffn_w_up, ffn_conv, ffn_w_down, a_w_in, a_q_norm, a_k_norm, a_sinks, a_w_out, b_w_in, b_qa_norm, b_w_qb, b_q_norm_nope, b_q_norm_rope, b_kv_norm, b_krope_norm, b_w_uk, b_w_uv, b_w_out, c_w_in, c_q_norm, c_k_norm, c_cmp_wk, c_cmp_wv, c_w_out):
    nb, seq, d = x_prompt.shape
    db, ds, _ = x_sample.shape
    depth = ada_w.shape[0]
    ff = ffn_w_down.shape[1]
    past = page_table.shape[1] * PAGE_SIZE
    hd = N_HEADS * HEAD_DIM

    mod = _modulate(jnp.concatenate([c_prompt, c_sample], axis=0), ada_w, ada_b)
    tr_p = math.gcd(seq, 512)
    gp = _Group(True, mod[:, :, :nb].reshape(depth, 6, nb, 1, d), tr_p)
    gs = _Group(False, mod[:, :, nb:], db)
    norm_mix3 = norm_mix.reshape(depth, 1, d)
    norm_ffn3 = norm_ffn.reshape(depth, 1, d)

    pos_p = jnp.arange(seq, dtype=jnp.int32)
    pos_s = past + jnp.arange(ds, dtype=jnp.int32)
    cw_p, sw_p = _rope_tables(pos_p, ROPE_DIM, ROPE_THETA, HEAD_DIM)
    cw2_p, sw2_p = jnp.tile(cw_p, (1, LANES // HEAD_DIM)), jnp.tile(sw_p, (1, LANES // HEAD_DIM))
    cw_s, sw_s = _rope_tables(pos_s, ROPE_DIM, ROPE_THETA, HEAD_DIM)
    cm_p, sm_p = _rope_tables(pos_p, MLA_ROPE, MLA_THETA, MLA_ROPE)
    cm_s, sm_s = _rope_tables(pos_s, MLA_ROPE, MLA_THETA, MLA_ROPE)

    xp = x_prompt
    xs = x_sample.transpose(1, 0, 2)
    row2 = lambda v: v.reshape(1, -1)
    tm = lambda a: a.transpose(1, 0, 2)
    out = {k: [] for k in ("swa_k_p", "swa_v_p", "swa_k_s", "swa_v_s", "mla_c_p", "mla_r_p", "mla_c_s", "mla_r_s",
                           "conv_p", "conv_s")}
    nsa_p = [[] for _ in range(6)]
    nsa_s = [[] for _ in range(6)]
    ia = ib = ic = 0
    for layer in range(depth):
        kind = layer % N_MIXERS
        if kind == 0:
            w_in = a_w_in[ia].astype(BF16)
            qn, kn, sinks = row2(a_q_norm[ia]), row2(a_k_norm[ia]), row2(a_sinks[ia])
            yp = _proj(xp, gp, norm_mix3, layer, w_in)
            op, kp, vp = _band_attn(yp, qn, kn, sinks, cw2_p, sw2_p)
            ys = _proj(xs, gs, norm_mix3, layer, w_in)
            os_, ks_, vs_ = _step_attn(ys, cache_swa_k, cache_swa_v, ia, qn, kn, sinks, cw_s, sw_s)
            kv4 = lambda a: a.reshape(a.shape[0], a.shape[1], SWA_KV_HEADS, HEAD_DIM)
            out["swa_k_p"].append(kv4(kp)); out["swa_v_p"].append(kv4(vp))
            out["swa_k_s"].append(ks_); out["swa_v_s"].append(vs_)
            w_out = a_w_out[ia].astype(BF16)
            ia += 1
        elif kind == 1:
            w_in = b_w_in[ib].astype(BF16)
            wqb = b_w_qb[ib].reshape(MLA_Q_LORA, N_HEADS, MLA_NOPE + MLA_ROPE)
            wqb = jnp.concatenate([wqb[:, :, :MLA_NOPE].reshape(MLA_Q_LORA, -1),
                                   wqb[:, :, MLA_NOPE:].reshape(MLA_Q_LORA, -1)], axis=1).astype(BF16)
            wuk = b_w_uk[ib].transpose(1, 2, 0).astype(BF16)
            wuv = b_w_uv[ib].transpose(1, 0, 2).astype(BF16)
            norms = (row2(b_qa_norm[ib]), wqb, row2(b_kv_norm[ib]), row2(b_krope_norm[ib]))
            nn = jnp.tile(row2(b_q_norm_nope[ib]), (1, N_HEADS))
            nr = jnp.tile(row2(b_q_norm_rope[ib]), (1, N_HEADS))
            wide = lambda t: jnp.tile(t, (1, LANES // MLA_ROPE))
            qp, cp, rp = _mla_proj(xp, gp, norm_mix3, layer, w_in, *norms, cm_p[None], sm_p[None])
            op = _mla_attn(qp, cp, rp, wide(cm_p), wide(sm_p), nn, nr, wuk, wuv, math.gcd(seq, 128),
                           math.gcd(seq, 256), 256)
            qs, cs_, rs_ = _mla_proj(xs, gs, norm_mix3, layer, w_in, *norms, cm_s[:, None], sm_s[:, None])
            os_ = _mla_decode(qs, cs_, rs_, cache_mla_latent, cache_mla_krope, ib, page_table,
                              wide(cm_s), wide(sm_s), nn, nr, wuk, wuv)
            out["mla_c_p"].append(cp); out["mla_r_p"].append(rp)
            out["mla_c_s"].append(tm(cs_)); out["mla_r_s"].append(tm(rs_))
            w_out = b_w_out[ib].astype(BF16)
            ib += 1
        else:
            n_in = c_w_in.shape[2]
            n_pad = -(-n_in // 128) * 128
            w_in = jnp.pad(c_w_in[ic], ((0, 0), (0, n_pad - n_in))).astype(BF16)
            qn, kn3 = row2(c_q_norm[ic]), c_k_norm[ic]
            kv4 = lambda a: a.reshape(a.shape[0], a.shape[1], NSA_KV_HEADS, HEAD_DIM)
            col = lambda a, j: a[:, :, hd + j * NSA_KVD:hd + (j + 1) * NSA_KVD]
            yp = _proj(xp, gp, norm_mix3, layer, w_in)
            ksn, kwn, kcmp, vcmp = _nsa_prep(yp, c_cmp_wk[ic], c_cmp_wv[ic], kn3, cw_p, sw_p)
            op = _nsa_attn(yp, ksn, kwn, kcmp, vcmp, qn, cw2_p, sw2_p, math.gcd(seq, 256))
            for j, a in enumerate((col(yp, 0), col(yp, 1), ksn, col(yp, 3), kwn[:, -WINDOW:],
                                   col(yp, 5)[:, -WINDOW:])):
                nsa_p[j].append(kv4(a))
            ys = _proj(xs, gs, norm_mix3, layer, w_in)
            pools = (cache_nsa_kcmp, cache_nsa_vcmp, cache_nsa_ksel, cache_nsa_vsel)
            os_, kso, kwo, vwo = _nsa_decode(ys, pools, cache_nsa_kwin, cache_nsa_vwin, ic, page_table,
                                             c_cmp_wk[ic], c_cmp_wv[ic], qn, kn3, cw_s, sw_s)
            for j, a in enumerate((tm(col(ys, 0)), tm(col(ys, 1)), kso, tm(col(ys, 3)))):
                nsa_s[j].append(kv4(a))
            nsa_s[4].append(kwo)
            nsa_s[5].append(vwo)
            w_out = c_w_out[ic].astype(BF16)
            ic += 1
        xp = _outproj(op, xp, gp, layer, w_out)
        xs = _outproj(os_, xs, gs, layer, w_out)
        wup, wd = ffn_w_up[layer].astype(BF16), ffn_w_down[layer].astype(BF16)
        tf = math.gcd(ff, 256)
        xp, bp = _ffn_prompt(xp, gp, norm_ffn3, layer, wup, ffn_conv[layer], wd, math.gcd(seq, 1024), tf)
        xs, bs = _ffn_sample(xs, gs, norm_ffn3, layer, wup, ffn_conv[layer], wd, state_conv_ffn[layer], tf)
        out["conv_p"].append(bp); out["conv_s"].append(bs)

    st = lambda xs_: jnp.stack(xs_)
    return (xp, xs.transpose(1, 0, 2),
            st(out["swa_k_p"]), st(out["swa_v_p"]), st(out["mla_c_p"]), st(out["mla_r_p"]),
            *[st(a) for a in nsa_p], st(out["conv_p"]),
            st(out["swa_k_s"]), st(out["swa_v_s"]), st(out["mla_c_s"]), st(out["mla_r_s"]),
            *[st(a) for a in nsa_s], st(out["conv_s"]))
```

```python
import functools
import math

import jax
import jax.numpy as jnp
from jax import lax
from jax.experimental import pallas as pl
from jax.experimental.pallas import tpu as pltpu

F32 = jnp.float32
BF16 = jnp.bfloat16

N_HEADS = 16
HEAD_DIM = 64
ROPE_DIM = 16
ROPE_THETA = 500000.0
EPS = 1e-6
PAGE_SIZE = 128
SWA_KV_HEADS = 4
WINDOW = 128
MLA_Q_LORA = 384
MLA_KV_LORA = 256
MLA_NOPE = 64
MLA_ROPE = 32
MLA_THETA = 10000.0
NSA_KV_HEADS = 2
NSA_CMP_BLOCK = 32
NSA_SEL_BLOCK = 64
NSA_TOPK = 16
N_MIXERS = 3
NEG = -1e30
TINY = float(jnp.finfo(jnp.float32).tiny)
VMEM_LIMIT = 56 * 1024 * 1024
PAGE_SLOTS = 8


def _cparams(sem):
    return pltpu.CompilerParams(dimension_semantics=sem, vmem_limit_bytes=VMEM_LIMIT)


def _dot(a, b):
    return jnp.dot(a, b, preferred_element_type=F32)


def _dot_nt(a, b):
    return lax.dot_general(a, b, (((1,), (1,)), ((), ())), preferred_element_type=F32)


def _rms(x, g):
    return x * lax.rsqrt(jnp.mean(x * x, axis=-1, keepdims=True) + EPS) * g


def _rope(x, c, s, rot):
    half = rot // 2
    parts = [x[:, half:rot], x[:, :half]]
    if x.shape[1] > rot:
        parts.append(x[:, rot:])
    return x * c + jnp.concatenate(parts, axis=1) * s


def _silu(x):
    return x * jax.nn.sigmoid(x)


def _norm_mod(x, g, sc, sh):
    return _rms(x, g) * (1.0 + sc) + sh


def _rope_tables(pos, rot, theta, width):
    half = rot // 2
    inv = jnp.power(jnp.float32(theta), -jnp.arange(half, dtype=F32) / half)
    ang = pos.astype(F32)[:, None] * inv[None, :]
    cos, sin = jnp.cos(ang), jnp.sin(ang)
    n = pos.shape[0]
    c = jnp.concatenate([cos, cos, jnp.ones((n, width - rot), F32)], axis=1)
    s = jnp.concatenate([-sin, sin, jnp.zeros((n, width - rot), F32)], axis=1)
    return c, s


class _Group:
    def __init__(self, per_group, mod, tr):
        self.pm = per_group
        self.mods = [mod[layer] for layer in range(mod.shape[0])]
        self.tr = tr

    def mod(self, layer):
        return self.mods[layer]

    def mod_spec(self, j, nargs=2):
        d = self.mods[0].shape[-1]
        if self.pm:
            shape, f = (1, 1, 1, d), (lambda g: (j, g, 0, 0))
        else:
            shape, f = (1, self.mods[0].shape[1], d), (lambda g: (j, 0, 0))
        if nargs == 2:
            return pl.BlockSpec(shape, lambda g, r: f(g))
        return pl.BlockSpec(shape, lambda g, r, k: f(g))


def _mod_val(ref, pm):
    return ref[0, 0] if pm else ref[0]


def _mod_kernel(c_ref, w_ref, b_ref, o_ref):
    a = _silu(c_ref[...])
    o_ref[0, 0] = _dot(a.astype(BF16), w_ref[0].astype(BF16)) + b_ref[0]


def _modulate(c_all, ada_w, ada_b):
    nl, d, d6 = ada_w.shape
    n = c_all.shape[0]
    nj = d6 // d
    return pl.pallas_call(
        _mod_kernel, grid=(nl, nj),
        in_specs=[pl.BlockSpec((n, d), lambda l, j: (0, 0)),
                  pl.BlockSpec((1, d, d), lambda l, j: (l, 0, j)),
                  pl.BlockSpec((1, 1, d), lambda l, j: (l, 0, j))],
        out_specs=pl.BlockSpec((1, 1, n, d), lambda l, j: (l, j, 0, 0)),
        out_shape=jax.ShapeDtypeStruct((nl, nj, n, d), F32),
        compiler_params=_cparams(("arbitrary", "arbitrary")),
    )(c_all, ada_w, ada_b.reshape(nl, 1, d6))


def _proj_kernel(x_ref, g_ref, sc_ref, sh_ref, w_ref, o_ref, *, pm):
    h = _norm_mod(x_ref[0], g_ref[0], _mod_val(sc_ref, pm), _mod_val(sh_ref, pm))
    o_ref[0] = _dot(h.astype(BF16), w_ref[...])


def _proj(x, grp, gains, layer, w_bf):
    ng, nr, d = x.shape
    n = w_bf.shape[1]
    tr = grp.tr
    return pl.pallas_call(
        functools.partial(_proj_kernel, pm=grp.pm), grid=(ng, nr // tr),
        in_specs=[pl.BlockSpec((1, tr, d), lambda g, r: (g, r, 0)),
                  pl.BlockSpec((1, 1, d), lambda g, r: (0, 0, 0)),
                  grp.mod_spec(1), grp.mod_spec(0),
                  pl.BlockSpec((d, n), lambda g, r: (0, 0))],
        out_specs=pl.BlockSpec((1, tr, n), lambda g, r: (g, r, 0)),
        out_shape=jax.ShapeDtypeStruct((ng, nr, n), F32),
        compiler_params=_cparams(("arbitrary", "arbitrary")),
    )(x, gains[layer:layer + 1], grp.mod(layer), grp.mod(layer), w_bf)


def _mla_proj_kernel(x_ref, g_ref, sc_ref, sh_ref, w_ref, qan_ref, wqb_ref, kvn_ref, krn_ref,
                     c_ref, s_ref, q_ref, lat_ref, kr_ref, *, pm):
    h = _norm_mod(x_ref[0], g_ref[0], _mod_val(sc_ref, pm), _mod_val(sh_ref, pm))
    y = _dot(h.astype(BF16), w_ref[...])
    a, b = MLA_Q_LORA, MLA_Q_LORA + MLA_KV_LORA
    qa = _rms(y[:, :a], qan_ref[...])
    q_ref[0] = _dot(qa.astype(BF16), wqb_ref[...])
    lat_ref[0] = _rms(y[:, a:b], kvn_ref[...])
    kr = _rms(y[:, b:], krn_ref[...])
    kr_ref[0] = _rope(kr, c_ref[0], s_ref[0], MLA_ROPE)


def _mla_proj(x, grp, gains, layer, w_bf, qa_norm, wqb_bf, kv_norm, krope_norm, cs, sn):
    ng, nr, d = x.shape
    tr = grp.tr
    nq = wqb_bf.shape[1]
    if grp.pm:
        tab = pl.BlockSpec((1, tr, MLA_ROPE), lambda g, r: (0, r, 0))
    else:
        tab = pl.BlockSpec((1, 1, MLA_ROPE), lambda g, r: (g, 0, 0))
    full = lambda a: pl.BlockSpec(a.shape, lambda g, r: (0,) * a.ndim)
    return pl.pallas_call(
        functools.partial(_mla_proj_kernel, pm=grp.pm), grid=(ng, nr // tr),
        in_specs=[pl.BlockSpec((1, tr, d), lambda g, r: (g, r, 0)),
                  pl.BlockSpec((1, 1, d), lambda g, r: (0, 0, 0)),
                  grp.mod_spec(1), grp.mod_spec(0),
                  full(w_bf), full(qa_norm), full(wqb_bf), full(kv_norm), full(krope_norm), tab, tab],
        out_specs=[pl.BlockSpec((1, tr, nq), lambda g, r: (g, r, 0)),
                   pl.BlockSpec((1, tr, MLA_KV_LORA), lambda g, r: (g, r, 0)),
                   pl.BlockSpec((1, tr, MLA_ROPE), lambda g, r: (g, r, 0))],
        out_shape=[jax.ShapeDtypeStruct((ng, nr, nq), F32),
                   jax.ShapeDtypeStruct((ng, nr, MLA_KV_LORA), F32),
                   jax.ShapeDtypeStruct((ng, nr, MLA_ROPE), F32)],
        compiler_params=_cparams(("arbitrary", "arbitrary")),
    )(x, gains[layer:layer + 1], grp.mod(layer), grp.mod(layer), w_bf, qa_norm, wqb_bf, kv_norm, krope_norm, cs, sn)


def _outproj_kernel(o_ref, x_ref, gt_ref, w_ref, y_ref, *, pm):
    y_ref[0] = x_ref[0] + _mod_val(gt_ref, pm) * _dot(o_ref[0].astype(BF16), w_ref[...])


def _outproj(o, x, grp, layer, w_bf):
    ng, nr, d = x.shape
    k = o.shape[-1]
    tr = grp.tr
    return pl.pallas_call(
        functools.partial(_outproj_kernel, pm=grp.pm), grid=(ng, nr // tr),
        in_specs=[pl.BlockSpec((1, tr, k), lambda g, r: (g, r, 0)),
                  pl.BlockSpec((1, tr, d), lambda g, r: (g, r, 0)),
                  grp.mod_spec(2),
                  pl.BlockSpec((k, d), lambda g, r: (0, 0))],
        out_specs=pl.BlockSpec((1, tr, d), lambda g, r: (g, r, 0)),
        out_shape=jax.ShapeDtypeStruct((ng, nr, d), F32),
        compiler_params=_cparams(("arbitrary", "arbitrary")),
    )(o, x, grp.mod(layer), w_bf)


HALO = 16


def _ffn_tail(mg, mv, wd_ref, x_ref, gt, y_ref):
    f = pl.program_id(2)
    d = _dot((_silu(mg) * mv).astype(BF16), wd_ref[...])

    @pl.when(f == 0)
    def _():
        y_ref[0] = d

    @pl.when(f > 0)
    def _():
        y_ref[0] += d

    @pl.when(f == pl.num_programs(2) - 1)
    def _():
        y_ref[0] = x_ref[0] + gt * y_ref[0]


def _ffn_prompt_kernel(x_ref, xh_ref, g_ref, sc_ref, sh_ref, gt_ref, wup_ref, cw_ref, wd_ref,
                       y_ref, b_ref, h_s, act_s, u_s, *, tr, tf):
    r = pl.program_id(1)
    ff = wd_ref.shape[0]
    g, sc, sh = g_ref[0], sc_ref[0, 0], sh_ref[0, 0]
    h_s[HALO:, :] = _norm_mod(x_ref[0], g, sc, sh).astype(BF16)
    hh = _norm_mod(xh_ref[0], g, sc, sh)
    h_s[:HALO, :] = jnp.where(r > 0, hh, 0.0).astype(BF16)
    nf = ff // tf
    cols = lambda half, f: pl.ds(pl.multiple_of(half * ff + f * tf, tf), tf)

    def up(f, slot):
        for half in range(2):
            u_s[slot, half] = _dot(h_s[...], wup_ref[:, cols(half, f)])

    def mixed(f, slot, half):
        u = u_s[slot, half]
        cw = cw_ref[:, cols(half, f)]
        b_ref[0, 0, :, cols(half, f)] = u[HALO + tr - 2:]
        a = pltpu.roll(u, 2, 0)[HALO:]
        b = pltpu.roll(u, 1, 0)[HALO:]
        return cw[0:1] * a + cw[1:2] * b + cw[2:3] * u[HALO:]

    def activate(f, slot):
        act_s[:, cols(0, f)] = (_silu(mixed(f, slot, 0)) * mixed(f, slot, 1)).astype(BF16)

    def body(it, carry):
        f = 2 * it
        up(f + 1, 1)
        activate(f, 0)
        up(f + 2, 0)
        activate(f + 1, 1)
        return carry

    assert nf % 2 == 1
    up(0, 0)
    lax.fori_loop(0, nf // 2, body, 0)
    activate(nf - 1, 0)
    y_ref[0] = x_ref[0] + gt_ref[0, 0] * _dot(act_s[...], wd_ref[...])


def _ffn_prompt(x, grp, gains, layer, wup_bf, conv_w, wd_bf, tr, tf):
    ng, nr, d = x.shape
    ff = wd_bf.shape[0]
    ms = lambda j: grp.mod_spec(j)
    once = lambda a: pl.BlockSpec(a.shape, lambda g, r: (0,) * a.ndim, pipeline_mode=pl.Buffered(1))
    y, buf = pl.pallas_call(
        functools.partial(_ffn_prompt_kernel, tr=tr, tf=tf), grid=(ng, nr // tr),
        in_specs=[pl.BlockSpec((1, tr, d), lambda g, r: (g, r, 0)),
                  pl.BlockSpec((1, HALO, d), lambda g, r: (g, jnp.maximum(r * (tr // HALO) - 1, 0), 0)),
                  pl.BlockSpec((1, 1, d), lambda g, r: (0, 0, 0)),
                  ms(4), ms(3), ms(5), once(wup_bf), once(conv_w), once(wd_bf)],
        out_specs=[pl.BlockSpec((1, tr, d), lambda g, r: (g, r, 0)),
                   pl.BlockSpec((1, 1, 2, 2 * ff), lambda g, r: (g, r, 0, 0))],
        out_shape=[jax.ShapeDtypeStruct((ng, nr, d), F32),
                   jax.ShapeDtypeStruct((ng, nr // tr, 2, 2 * ff), F32)],
        scratch_shapes=[pltpu.VMEM((tr + HALO, d), BF16), pltpu.VMEM((tr, ff), BF16),
                        pltpu.VMEM((2, 2, tr + HALO, tf), F32)],
        compiler_params=_cparams(("arbitrary", "arbitrary")),
    )(x, x, gains[layer:layer + 1], grp.mod(layer), grp.mod(layer), grp.mod(layer), wup_bf, conv_w, wd_bf)
    return y, buf[:, -1]


def _ffn_sample_kernel(x_ref, g_ref, sc_ref, sh_ref, gt_ref, sg_ref, sv_ref, wg_ref, wv_ref, cg_ref, cv_ref,
                       wd_ref, y_ref, bg_ref, bv_ref, h_s, *, nt, p):
    tile = lambda m: jnp.concatenate([m] * nt, axis=0)

    @pl.when(pl.program_id(2) == 0)
    def _():
        h_s[...] = _norm_mod(x_ref[0], g_ref[0], tile(sc_ref[0]), tile(sh_ref[0])).astype(BF16)

    h = h_s[...]
    n = nt * p

    def conv(st, u, cw):
        e = jnp.concatenate([st, u], axis=0)
        return cw[0:1] * e[0:n] + cw[1:2] * e[p:p + n] + cw[2:3] * e[2 * p:]

    ug = _dot(h, wg_ref[...])
    uv = _dot(h, wv_ref[...])
    bg_ref[...] = jnp.concatenate([sg_ref[...], ug], axis=0)[n:]
    bv_ref[...] = jnp.concatenate([sv_ref[...], uv], axis=0)[n:]
    _ffn_tail(conv(sg_ref[...], ug, cg_ref[...]), conv(sv_ref[...], uv, cv_ref[...]), wd_ref, x_ref,
              tile(gt_ref[0]), y_ref)


def _ffn_sample(x, grp, gains, layer, wup_bf, conv_w, wd_bf, state, tf):
    nt, p, d = x.shape
    ff = wd_bf.shape[0]
    nf = ff // tf
    n = nt * p
    st = state.transpose(1, 0, 2).reshape(2 * p, 2 * ff)
    ms = lambda j: grp.mod_spec(j, nargs=3)
    y, bg, bv = pl.pallas_call(
        functools.partial(_ffn_sample_kernel, nt=nt, p=p), grid=(1, 1, nf),
        in_specs=[pl.BlockSpec((1, n, d), lambda g, r, f: (0, 0, 0)),
                  pl.BlockSpec((1, 1, d), lambda g, r, f: (0, 0, 0)),
                  ms(4), ms(3), ms(5),
                  pl.BlockSpec((2 * p, tf), lambda g, r, f: (0, f)),
                  pl.BlockSpec((2 * p, tf), lambda g, r, f: (0, nf + f)),
                  pl.BlockSpec((d, tf), lambda g, r, f: (0, f)),
                  pl.BlockSpec((d, tf), lambda g, r, f: (0, nf + f)),
                  pl.BlockSpec((3, tf), lambda g, r, f: (0, f)),
                  pl.BlockSpec((3, tf), lambda g, r, f: (0, nf + f)),
                  pl.BlockSpec((tf, d), lambda g, r, f: (f, 0))],
        out_specs=[pl.BlockSpec((1, n, d), lambda g, r, f: (0, 0, 0)),
                   pl.BlockSpec((2 * p, tf), lambda g, r, f: (0, f)),
                   pl.BlockSpec((2 * p, tf), lambda g, r, f: (0, f))],
        out_shape=[jax.ShapeDtypeStruct((1, n, d), F32),
                   jax.ShapeDtypeStruct((2 * p, ff), F32),
                   jax.ShapeDtypeStruct((2 * p, ff), F32)],
        scratch_shapes=[pltpu.VMEM((n, d), BF16)],
        compiler_params=_cparams(("arbitrary", "arbitrary", "arbitrary")),
    )(x.reshape(1, n, d), gains[layer:layer + 1], grp.mod(layer), grp.mod(layer), grp.mod(layer), st, st, wup_bf, wup_bf, conv_w, conv_w, wd_bf)
    new_state = jnp.concatenate([bg, bv], axis=-1).reshape(2, p, 2 * ff).transpose(1, 0, 2)
    return y.reshape(nt, p, d), new_state


def _sink_col(sink_ref, heads, nq):
    return jnp.concatenate([jnp.broadcast_to(sink_ref[:, h:h + 1], (nq, 1)) for h in heads], axis=0)


def _band_mask(nq, w, prev_off):
    qi = lax.broadcasted_iota(jnp.int32, (nq, w + nq), 0)
    kj = lax.broadcasted_iota(jnp.int32, (nq, w + nq), 1)
    return ((kj < w) & (kj >= qi + prev_off)) | ((kj >= w) & ((kj - w) <= qi))


def _head(x, h):
    return x[:, h * HEAD_DIM:(h + 1) * HEAD_DIM]


def _band_attn_kernel(q_ref, kp_ref, kc_ref, vp_ref, vc_ref, qg_ref, kg_ref, cq_ref, sq_ref, cp_ref, sp_ref,
                      sink_ref, o_ref, ko_ref, vo_ref, q_s, m_s, acc_s, *, n_kv, chunk):
    i = pl.program_id(1)
    w = q_ref.shape[1]
    rows_kv = (N_HEADS // n_kv) * w
    seg = _seg_ones(2 * LANES)
    cq, sq = cq_ref[...], sq_ref[...]
    _stack_heads(_heads_rope(_heads_rms(q_ref[0], qg_ref[...], seg), cq, sq) * HEAD_DIM ** -0.5, q_s)
    kc = _heads_rope(_heads_rms(kc_ref[0], kg_ref[...], seg), cq, sq)
    kp = _heads_rope(_heads_rms(kp_ref[0], kg_ref[...], seg), cp_ref[...], sp_ref[...])
    kk = jnp.concatenate([kp, kc], axis=0).astype(BF16)
    vv = jnp.concatenate([vp_ref[0], vc_ref[0]], axis=0).astype(BF16)
    kvs = [([_head(kk, kv)], _with_ones(_head(vv, kv))) for kv in range(n_kv)]
    for h in range(N_HEADS):
        m_s[h * w:(h + 1) * w, :] = jnp.broadcast_to(sink_ref[:, h:h + 1], (w, LANES))
    acc_s[...] = jnp.concatenate([jnp.zeros((N_HEADS * w, HEAD_DIM), F32), jnp.ones((N_HEADS * w, HEAD_DIM), F32)],
                                 axis=1)
    mask = _band_mask(w, w, jnp.where(i > 0, 0, w))
    mk = jnp.concatenate([mask] * (chunk // w), axis=0)
    _flash_step([q_s], lambda c: kvs[c * chunk // rows_kv], lambda c: mk, m_s, acc_s, chunk)
    o_ref[0] = _unstack_heads(_normalized(acc_s[...]), w).astype(o_ref.dtype)
    ko_ref[0] = kc
    vo_ref[0] = vc_ref[0]


def _band_attn(y, q_norm, k_norm, sinks, cs, sn):
    b, s, _ = y.shape
    w = WINDOW
    kvd = SWA_KV_HEADS * HEAD_DIM
    hd = N_HEADS * HEAD_DIM
    kb, vb = hd // kvd, hd // kvd + 1
    prev = lambda i: jnp.maximum(i - 1, 0)
    full = lambda a: pl.BlockSpec(a.shape, lambda bb, i: (0,) * a.ndim)
    tab_c = pl.BlockSpec((w, LANES), lambda bb, i: (i, 0))
    tab_p = pl.BlockSpec((w, LANES), lambda bb, i: (prev(i), 0))
    q_norm, k_norm = jnp.tile(q_norm, (1, N_HEADS)), jnp.tile(k_norm, (1, SWA_KV_HEADS))
    return pl.pallas_call(
        functools.partial(_band_attn_kernel, n_kv=SWA_KV_HEADS, chunk=2 * w), grid=(b, s // w),
        scratch_shapes=[pltpu.VMEM((N_HEADS * w, HEAD_DIM), BF16),
                        pltpu.VMEM((N_HEADS * w, LANES), F32),
                        pltpu.VMEM((N_HEADS * w, 2 * HEAD_DIM), F32)],
        in_specs=[pl.BlockSpec((1, w, hd), lambda bb, i: (bb, i, 0)),
                  pl.BlockSpec((1, w, kvd), lambda bb, i: (bb, prev(i), kb)),
                  pl.BlockSpec((1, w, kvd), lambda bb, i: (bb, i, kb)),
                  pl.BlockSpec((1, w, kvd), lambda bb, i: (bb, prev(i), vb)),
                  pl.BlockSpec((1, w, kvd), lambda bb, i: (bb, i, vb)),
                  full(q_norm), full(k_norm), tab_c, tab_c, tab_p, tab_p, full(sinks)],
        out_specs=[pl.BlockSpec((1, w, hd), lambda bb, i: (bb, i, 0)),
                   pl.BlockSpec((1, w, kvd), lambda bb, i: (bb, 0, 0)),
                   pl.BlockSpec((1, w, kvd), lambda bb, i: (bb, 0, 0))],
        out_shape=[jax.ShapeDtypeStruct((b, s, hd), BF16),
                   jax.ShapeDtypeStruct((b, w, kvd), F32),
                   jax.ShapeDtypeStruct((b, w, kvd), F32)],
        compiler_params=_cparams(("arbitrary", "arbitrary")),
    )(y, y, y, y, y, q_norm, k_norm, cs, sn, cs, sn, sinks)


def _rows(ref, n):
    return jnp.concatenate([ref[t, 0] for t in range(n)], axis=0)


def _window_step(q, k_new, v_new, kbt, vbt, qn, kn, cs, sn, sink_ref, n_kv):
    ns, w = q.shape[0], kbt[0].shape[1]
    grp = N_HEADS // n_kv
    scale = HEAD_DIM ** -0.5
    t_q = lax.broadcasted_iota(jnp.int32, (grp * ns, w), 0) % ns
    col = lax.broadcasted_iota(jnp.int32, (grp * ns, w), 1)
    see_buf = col >= t_q
    see_new = col <= t_q
    pad = lambda a: jnp.concatenate([a, jnp.zeros((w - ns, a.shape[1]), a.dtype)], axis=0)
    outs, knew = [], []
    for kv in range(n_kv):
        kn_h = _rope(_rms(_head(k_new, kv), kn), cs, sn, ROPE_DIM)
        knew.append(kn_h)
        heads = range(kv * grp, (kv + 1) * grp)
        qg = jnp.concatenate([_rope(_rms(_head(q, h), qn), cs, sn, ROPE_DIM) * scale for h in heads],
                             axis=0).astype(BF16)
        s1 = jnp.where(see_buf, _dot(qg, kbt[kv].astype(BF16)), NEG)
        s2 = jnp.where(see_new, _dot_nt(qg, pad(kn_h).astype(BF16)), NEG)
        m = jnp.maximum(jnp.max(s1, axis=-1, keepdims=True), jnp.max(s2, axis=-1, keepdims=True))
        if sink_ref is not None:
            sink = _sink_col(sink_ref, heads, ns)
            m = jnp.maximum(m, sink)
        p1, p2 = jnp.exp(s1 - m), jnp.exp(s2 - m)
        den = jnp.sum(p1, axis=-1, keepdims=True) + jnp.sum(p2, axis=-1, keepdims=True)
        if sink_ref is not None:
            den = den + jnp.exp(sink - m)
        o = (_dot_nt(p1.astype(BF16), vbt[kv].astype(BF16))
             + _dot(p2.astype(BF16), pad(_head(v_new, kv)).astype(BF16))) / den
        outs += [o[g * ns:(g + 1) * ns] for g in range(grp)]

    def shifted(bt, new):
        new_t = jnp.concatenate([jnp.zeros((w - ns, new.shape[1]), F32), new], axis=0).T
        lane = lax.broadcasted_iota(jnp.int32, (HEAD_DIM, w), 1)
        return [jnp.where(lane >= w - ns, new_t[kv * HEAD_DIM:(kv + 1) * HEAD_DIM], pltpu.roll(bt[kv], w - ns, 1))
                for kv in range(n_kv)]

    return outs, shifted(kbt, jnp.concatenate(knew, axis=1)), shifted(vbt, v_new)


def _step_attn_kernel(q_ref, k_ref, v_ref, kb_ref, vb_ref, qn_ref, kn_ref, cs_ref, sn_ref, sink_ref,
                      o_ref, ko_ref, vo_ref, *, n_kv):
    ns = q_ref.shape[0]
    outs, k_out, v_out = _window_step(_rows(q_ref, ns), _rows(k_ref, ns), _rows(v_ref, ns),
                                      [kb_ref[0, 0, kv] for kv in range(n_kv)],
                                      [vb_ref[0, 0, kv] for kv in range(n_kv)], qn_ref[...], kn_ref[...],
                                      cs_ref[...], sn_ref[...], sink_ref, n_kv)
    o = jnp.concatenate(outs, axis=1)
    for t in range(ns):
        o_ref[t, 0] = o[t:t + 1]
    for kv in range(n_kv):
        ko_ref[0, kv] = k_out[kv]
        vo_ref[0, kv] = v_out[kv]


def _row_minor(cache):
    return cache.transpose(0, 1, 3, 4, 2)


def _step_attn(y, k_cache, v_cache, li, q_norm, k_norm, sinks, cs, sn):
    ns, db, n = y.shape
    w, n_kv = k_cache.shape[2], k_cache.shape[3]
    kvd = n_kv * HEAD_DIM
    hd = N_HEADS * HEAD_DIM
    kb, vb = hd // kvd, hd // kvd + 1
    y4 = y.reshape(ns, db, 1, n)
    full = lambda a: pl.BlockSpec(a.shape, lambda b: (0,) * a.ndim)
    buf_in = pl.BlockSpec((1, 1, n_kv, HEAD_DIM, w), lambda b: (li, b, 0, 0, 0))
    buf_out = pl.BlockSpec((1, n_kv, HEAD_DIM, w), lambda b: (b, 0, 0, 0))
    o, ko, vo = pl.pallas_call(
        functools.partial(_step_attn_kernel, n_kv=n_kv), grid=(db,),
        in_specs=[pl.BlockSpec((ns, 1, 1, hd), lambda b: (0, b, 0, 0)),
                  pl.BlockSpec((ns, 1, 1, kvd), lambda b: (0, b, 0, kb)),
                  pl.BlockSpec((ns, 1, 1, kvd), lambda b: (0, b, 0, vb)),
                  buf_in, buf_in,
                  full(q_norm), full(k_norm), full(cs), full(sn), full(sinks)],
        out_specs=[pl.BlockSpec((ns, 1, 1, hd), lambda b: (0, b, 0, 0)), buf_out, buf_out],
        out_shape=[jax.ShapeDtypeStruct((ns, db, 1, hd), F32),
                   jax.ShapeDtypeStruct((db, n_kv, HEAD_DIM, w), F32),
                   jax.ShapeDtypeStruct((db, n_kv, HEAD_DIM, w), F32)],
        compiler_params=_cparams(("arbitrary",)),
    )(y4, y4, y4, _row_minor(k_cache), _row_minor(v_cache), q_norm, k_norm, cs, sn, sinks)
    back = lambda a: a.transpose(0, 3, 1, 2)
    return o.reshape(ns, db, hd), back(ko), back(vo)


LANES = 128


def _lanes(x, n):
    if n <= LANES:
        return x[:, :n]
    return jnp.concatenate([x] * (n // LANES), axis=1)


def _flash_step(q_refs, kv_fn, mask_fn, m_ref, acc_ref, chunk):
    n = m_ref.shape[0] // chunk

    def scores(c):
        rows = pl.ds(c * chunk, chunk)
        k_parts = kv_fn(c)[0]
        s = _dot_nt(q_refs[0][rows, :], k_parts[0])
        for qr, kp in zip(q_refs[1:], k_parts[1:]):
            s = s + _dot_nt(qr[rows, :], kp)
        mk = None if mask_fn is None else mask_fn(c)
        return s if mk is None else jnp.where(mk, s, NEG)

    s = scores(0)
    for c in range(n):
        s_next = scores(c + 1) if c + 1 < n else None
        rows = pl.ds(c * chunk, chunk)
        vals = kv_fn(c)[1]
        m_old = m_ref[rows, :]
        m_new = jnp.maximum(m_old, jnp.max(s, axis=-1, keepdims=True))
        alpha = jnp.exp(m_old - m_new)
        p = jnp.exp(s - _lanes(m_new, s.shape[1])).astype(BF16)
        acc_ref[rows, :] = _lanes(alpha, vals.shape[1]) * acc_ref[rows, :] + _dot(p, vals)
        m_ref[rows, :] = m_new
        s = s_next


def _seg_ones(n, head=HEAD_DIM):
    r = lax.broadcasted_iota(jnp.int32, (n, n), 0) // head
    c = lax.broadcasted_iota(jnp.int32, (n, n), 1) // head
    return jnp.where(r == c, 1.0, 0.0).astype(BF16)


def _heads_rms(x, g, seg, head=HEAD_DIM):
    sq = x * x
    hi = sq.astype(BF16)
    lo = (sq - hi.astype(F32)).astype(BF16)
    n = seg.shape[0]
    parts = []
    for j in range(0, x.shape[1], n):
        k = min(n, x.shape[1] - j)
        b = seg[:k, :k]
        parts.append(_dot(hi[:, j:j + k], b) + _dot(lo[:, j:j + k], b))
    ss = parts[0] if len(parts) == 1 else jnp.concatenate(parts, axis=1)
    return x * lax.rsqrt(ss * (1.0 / head) + EPS) * g


def _heads_rope(x, c2, s2, head=HEAD_DIM, rot=ROPE_DIM):
    w = x.shape[1]
    half = rot // 2
    lane = lax.broadcasted_iota(jnp.int32, x.shape, 1) % head
    sw = jnp.where(lane < half, pltpu.roll(x, w - half, 1), pltpu.roll(x, half, 1))
    return x * _lanes(c2, w) + sw * _lanes(s2, w)


def _stack_heads(x, q_s):
    nq = x.shape[0]
    for h in range(x.shape[1] // HEAD_DIM):
        q_s[h * nq:(h + 1) * nq, :] = _head(x, h).astype(q_s.dtype)


def _unstack_heads(o, nq):
    return jnp.concatenate([o[h * nq:(h + 1) * nq] for h in range(o.shape[0] // nq)], axis=1)


def _normalized(acc):
    return (acc * pltpu.roll(1.0 / acc, HEAD_DIM, 1))[:, :HEAD_DIM]


def _flash_init(m_ref, acc_ref):
    m_ref[...] = jnp.full(m_ref.shape, NEG, F32)
    acc_ref[...] = jnp.zeros(acc_ref.shape, F32)


def _with_ones(v):
    pad = LANES - v.shape[1] % LANES
    return jnp.concatenate([v, jnp.ones((v.shape[0], pad), v.dtype)], axis=1)


def _mla_queries(q, nn, nr, cs, sn, wuk_ref, ql_s, qr_s):
    nq = q.shape[0]
    scale = (MLA_NOPE + MLA_ROPE) ** -0.5
    off = N_HEADS * MLA_NOPE
    qn = _heads_rms(q[:, :off], nn, _seg_ones(2 * LANES, MLA_NOPE), MLA_NOPE).astype(BF16)
    qr = _heads_rms(q[:, off:], nr, _seg_ones(2 * LANES, MLA_ROPE), MLA_ROPE)
    qr = (_heads_rope(qr, cs, sn, MLA_ROPE, MLA_ROPE) * scale).astype(BF16)
    for h in range(N_HEADS):
        rows = slice(h * nq, (h + 1) * nq)
        ql_s[rows, :] = (_dot(qn[:, h * MLA_NOPE:(h + 1) * MLA_NOPE], wuk_ref[h]) * scale).astype(BF16)
        qr_s[rows, :] = qr[:, h * MLA_ROPE:(h + 1) * MLA_ROPE]


def _mla_output(acc, den, wuv_ref, nq):
    o_lat = (acc / den).astype(BF16)
    return jnp.concatenate([_dot(o_lat[h * nq:(h + 1) * nq], wuv_ref[h]) for h in range(N_HEADS)], axis=1)


def _mla_attn_kernel(q_ref, c_ref, kr_ref, cs_ref, sn_ref, nn_ref, nr_ref, wuk_ref, wuv_ref, o_ref,
                     ql_s, qr_s, m_s, acc_s, cb_s, kb_s, *, tq, tk, chunk):
    i = pl.program_id(1)

    @pl.when(i == 0)
    def _():
        cb_s[...] = _with_ones(c_ref[0].astype(BF16))
        kb_s[...] = kr_ref[0].astype(BF16)

    _mla_queries(q_ref[0], nn_ref[...], nr_ref[...], cs_ref[...], sn_ref[...], wuk_ref, ql_s, qr_s)
    _flash_init(m_s, acc_s)
    t_pos = i * tq + lax.broadcasted_iota(jnp.int32, (tq, tk), 0)
    k_off = lax.broadcasted_iota(jnp.int32, (tq, tk), 1)

    def step(j, masked):
        start = pl.multiple_of(j * tk, tk)
        cbx = cb_s[pl.ds(start, tk), :]
        kb = kb_s[pl.ds(start, tk), :]
        mask_fn = None
        if masked:
            mk = jnp.concatenate([(start + k_off) <= t_pos] * (chunk // tq), axis=0)
            mask_fn = lambda c: mk
        kv = ([cbx[:, :MLA_KV_LORA], kb], cbx)
        _flash_step([ql_s, qr_s], lambda c: kv, mask_fn, m_s, acc_s, chunk)

    def full_body(j, carry):
        step(j, False)
        return carry

    def diag_body(j, carry):
        step(j, True)
        return carry

    n_full = (i * tq) // tk
    lax.fori_loop(0, n_full, full_body, 0)
    lax.fori_loop(n_full, (i * tq + tq - 1) // tk + 1, diag_body, 0)
    o_ref[0] = _mla_output(acc_s[:, :MLA_KV_LORA], _lanes(acc_s[:, MLA_KV_LORA:], MLA_KV_LORA), wuv_ref,
                           tq).astype(o_ref.dtype)


def _mla_attn(q, c, kr, cs, sn, nn, nr, wuk, wuv, tq, tk, chunk):
    b, s, nqc = q.shape
    hd = N_HEADS * HEAD_DIM
    full = lambda a: pl.BlockSpec(a.shape, lambda bb, i: (0,) * a.ndim)
    tab = pl.BlockSpec((tq, LANES), lambda bb, i: (i, 0))
    return pl.pallas_call(
        functools.partial(_mla_attn_kernel, tq=tq, tk=tk, chunk=chunk), grid=(b, s // tq),
        in_specs=[pl.BlockSpec((1, tq, nqc), lambda bb, i: (bb, i, 0)),
                  pl.BlockSpec((1, s, MLA_KV_LORA), lambda bb, i: (bb, 0, 0)),
                  pl.BlockSpec((1, s, MLA_ROPE), lambda bb, i: (bb, 0, 0)),
                  tab, tab, full(nn), full(nr), full(wuk), full(wuv)],
        out_specs=pl.BlockSpec((1, tq, hd), lambda bb, i: (bb, i, 0)),
        out_shape=jax.ShapeDtypeStruct((b, s, hd), BF16),
        scratch_shapes=[pltpu.VMEM((N_HEADS * tq, MLA_KV_LORA), BF16),
                        pltpu.VMEM((N_HEADS * tq, MLA_ROPE), BF16),
                        pltpu.VMEM((N_HEADS * tq, LANES), F32),
                        pltpu.VMEM((N_HEADS * tq, MLA_KV_LORA + LANES), F32),
                        pltpu.VMEM((s, MLA_KV_LORA + LANES), BF16),
                        pltpu.VMEM((s, MLA_ROPE), BF16)],
        compiler_params=_cparams(("arbitrary", "arbitrary")),
    )(q, c, kr, cs, sn, nn, nr, wuk, wuv)


def _online_step(s, pv_fn, m_ref, acc_ref):
    m_old = m_ref[...]
    m_new = jnp.maximum(m_old, jnp.max(s, axis=-1, keepdims=True))
    alpha = jnp.exp(m_old - m_new)
    p = jnp.exp(s - _lanes(m_new, s.shape[1])).astype(BF16)
    acc_ref[...] = _lanes(alpha, acc_ref.shape[1]) * acc_ref[...] + pv_fn(p)
    m_ref[...] = m_new


class _PageStream:
    def __init__(self, pt_ref, seq, layer, pools, bufs, sems, slots):
        self.pt_ref, self.seq, self.layer, self.slots = pt_ref, seq, layer, slots
        self.pools, self.bufs, self.sems = pools, bufs, sems

    def _copies(self, group, half):
        out = []
        for k in range(self.slots):
            page = self.pt_ref[self.seq, group * self.slots + k]
            for i, (pool, buf) in enumerate(zip(self.pools, self.bufs)):
                out.append(pltpu.make_async_copy(pool.at[self.layer, page], buf.at[half, k], self.sems.at[half, i]))
        return out

    def start(self, group, half):
        for c in self._copies(group, half):
            c.start()

    def wait(self, group, half):
        for c in self._copies(group, half):
            c.wait()

    def sweep(self, n_groups, compute):
        def body(g, carry):
            half = g % 2

            @pl.when(g + 1 < n_groups)
            def _():
                self.start(g + 1, 1 - half)

            self.wait(g, half)
            compute(g, half)
            return carry

        lax.fori_loop(0, n_groups, body, 0)


def _mla_decode_kernel(pt_ref, lat_hbm, kr_hbm, q_ref, cn_ref, kn_ref, cs_ref, sn_ref, nn_ref, nr_ref, wuk_ref,
                       wuv_ref, o_ref, ql_s, qr_s, m_s, acc_s, lat_buf, kr_buf, sems, *, ns, slots, npg, layer):
    stream = _PageStream(pt_ref, pl.program_id(0), layer, [lat_hbm, kr_hbm], [lat_buf, kr_buf], sems, slots)
    stream.start(0, 0)
    _mla_queries(_rows(q_ref, ns), nn_ref[...], nr_ref[...], cs_ref[...], sn_ref[...], wuk_ref, ql_s, qr_s)
    _flash_init(m_s, acc_s)

    def cached(g, half):
        cbx = _with_ones(lat_buf[half].reshape(slots * PAGE_SIZE, MLA_KV_LORA).astype(BF16))
        krt = jnp.concatenate([kr_buf[half, k] for k in range(slots)], axis=1).astype(BF16)
        _online_step(_dot_nt(ql_s[...], cbx[:, :MLA_KV_LORA]) + _dot(qr_s[...], krt), lambda p: _dot(p, cbx),
                     m_s, acc_s)

    stream.sweep(npg, cached)

    pad = PAGE_SIZE - ns
    cb = jnp.concatenate([_rows(cn_ref, ns), jnp.zeros((pad, MLA_KV_LORA), F32)], axis=0).astype(BF16)
    kb = jnp.concatenate([_rows(kn_ref, ns), jnp.zeros((pad, MLA_ROPE), F32)], axis=0).astype(BF16)
    s = _dot_nt(ql_s[...], cb) + _dot_nt(qr_s[...], kb)
    row = lax.broadcasted_iota(jnp.int32, s.shape, 0) % ns
    col = lax.broadcasted_iota(jnp.int32, s.shape, 1)
    cbx_new = _with_ones(cb)
    _online_step(jnp.where(col <= row, s, NEG), lambda p: _dot(p, cbx_new), m_s, acc_s)
    o = _mla_output(acc_s[:, :MLA_KV_LORA], _lanes(acc_s[:, MLA_KV_LORA:], MLA_KV_LORA), wuv_ref, ns)
    for t in range(ns):
        o_ref[t, 0] = o[t:t + 1]


def _mla_decode(q, c, kr, lat_pool, kr_pool, li, page_table, cs, sn, nn, nr, wuk, wuv):
    ns, db, nqc = q.shape
    hd = N_HEADS * HEAD_DIM
    slots = PAGE_SLOTS
    npg = page_table.shape[1] // slots
    full = lambda a: pl.BlockSpec(a.shape, lambda b, pt: (0,) * a.ndim)
    hbm = pl.BlockSpec(memory_space=pl.ANY)
    step = lambda width: pl.BlockSpec((ns, 1, 1, width), lambda b, pt: (0, b, 0, 0))
    grid_spec = pltpu.PrefetchScalarGridSpec(
        num_scalar_prefetch=1, grid=(db,),
        in_specs=[hbm, hbm, step(nqc), step(MLA_KV_LORA), step(MLA_ROPE),
                  full(cs), full(sn), full(nn), full(nr), full(wuk), full(wuv)],
        out_specs=step(hd),
        scratch_shapes=[pltpu.VMEM((N_HEADS * ns, MLA_KV_LORA), BF16),
                        pltpu.VMEM((N_HEADS * ns, MLA_ROPE), BF16),
                        pltpu.VMEM((N_HEADS * ns, LANES), F32),
                        pltpu.VMEM((N_HEADS * ns, MLA_KV_LORA + LANES), F32),
                        pltpu.VMEM((2, slots, PAGE_SIZE, MLA_KV_LORA), F32),
                        pltpu.VMEM((2, slots, MLA_ROPE, PAGE_SIZE), F32),
                        pltpu.SemaphoreType.DMA((2, 2))])
    kr_pool_t = kr_pool.transpose(0, 1, 3, 2)
    o = pl.pallas_call(
        functools.partial(_mla_decode_kernel, ns=ns, slots=slots, npg=npg, layer=li), grid_spec=grid_spec,
        out_shape=jax.ShapeDtypeStruct((ns, db, 1, hd), F32),
        compiler_params=_cparams(("arbitrary",)),
    )(page_table, lat_pool, kr_pool_t,
      q.reshape(ns, db, 1, nqc), c.reshape(ns, db, 1, MLA_KV_LORA), kr.reshape(ns, db, 1, MLA_ROPE),
      cs, sn, nn, nr, wuk, wuv)
    return o.reshape(ns, db, hd)


NSA_KVD = NSA_KV_HEADS * HEAD_DIM
NSA_GRP = N_HEADS // NSA_KV_HEADS
NSA_PAIR = NSA_SEL_BLOCK // NSA_CMP_BLOCK


def _nsa_prep_kernel(kc_ref, vc_ref, ks_ref, kw_ref, wk_ref, wv_ref, kcn_ref, ksn_ref, kwn_ref, cs_ref, sn_ref,
                     kso_ref, kwo_ref, kcmp_ref, vcmp_ref):
    cs, sn = cs_ref[...], sn_ref[...]
    kc, vc = kc_ref[0], vc_ref[0]
    seg = _seg_ones(NSA_KVD)
    kso_ref[0] = _heads_rope(_heads_rms(ks_ref[0], ksn_ref[...], seg), cs, sn)
    kwo_ref[0] = _heads_rope(_heads_rms(kw_ref[0], kwn_ref[...], seg), cs, sn)
    kcmp, vcmp = [], []
    for kv in range(NSA_KV_HEADS):
        kcmp.append(_rms(_dot(wk_ref[kv], _head(kc, kv).astype(BF16)), kcn_ref[...]))
        vcmp.append(_dot(wv_ref[kv], _head(vc, kv).astype(BF16)))
    kcmp_ref[0] = jnp.concatenate(kcmp, axis=1)
    vcmp_ref[0] = jnp.concatenate(vcmp, axis=1)


def _cmp_matrix(w, nc):
    eye = jnp.eye(nc, dtype=F32)
    return (eye[None, :, :, None] * w.T[:, None, None, :]).reshape(w.shape[1], nc, nc * NSA_CMP_BLOCK).astype(BF16)


def _nsa_prep(y, cmp_wk, cmp_wv, k_norm, cs, sn):
    b, s, _ = y.shape
    nc = s // NSA_CMP_BLOCK
    base = N_HEADS * HEAD_DIM // NSA_KVD
    col = lambda j: pl.BlockSpec((1, s, NSA_KVD), lambda bb: (bb, 0, base + j))
    full = lambda a: pl.BlockSpec(a.shape, lambda bb: (0,) * a.ndim)
    wk, wv = _cmp_matrix(cmp_wk, nc), _cmp_matrix(cmp_wv, nc)
    kn = [k_norm[0:1]] + [jnp.tile(k_norm[j:j + 1], (1, NSA_KV_HEADS)) for j in (1, 2)]
    seq = pl.BlockSpec((1, s, NSA_KVD), lambda bb: (bb, 0, 0))
    blk = pl.BlockSpec((1, nc, NSA_KVD), lambda bb: (bb, 0, 0))
    return pl.pallas_call(
        _nsa_prep_kernel, grid=(b,),
        in_specs=[col(0), col(1), col(2), col(4), full(wk), full(wv), full(kn[0]), full(kn[1]), full(kn[2]),
                  full(cs), full(sn)],
        out_specs=[seq, seq, blk, blk],
        out_shape=[jax.ShapeDtypeStruct((b, s, NSA_KVD), F32), jax.ShapeDtypeStruct((b, s, NSA_KVD), F32),
                   jax.ShapeDtypeStruct((b, nc, NSA_KVD), F32), jax.ShapeDtypeStruct((b, nc, NSA_KVD), F32)],
        compiler_params=_cparams(("arbitrary",)),
    )(y, y, y, y, wk, wv, kn[0], kn[1], kn[2], cs, sn)


def _cmp_attend(q, kcmp, vcmp, mask, ng, row_minor=False):
    nq = q.shape[0] // ng
    mk = jnp.concatenate([mask] * ng, axis=0)
    qk, pv = (_dot, _dot_nt) if row_minor else (_dot_nt, _dot)
    s = jnp.where(mk, qk(q, kcmp.astype(BF16)), NEG)
    m = jnp.max(s, axis=-1, keepdims=True)
    p = jnp.where(mk, jnp.exp(s - m), 0.0)
    p = p / jnp.maximum(jnp.sum(p, axis=-1, keepdims=True), TINY)
    o = pv(p.astype(BF16), vcmp.astype(BF16))
    imp = p[0:nq]
    for g in range(1, ng):
        imp = imp + p[g * nq:(g + 1) * nq]
    return o, imp


def _pair_sum(imp):
    n = imp.shape[1]
    lane = lax.broadcasted_iota(jnp.int32, imp.shape, 1)
    return imp + jnp.where(lane % 2 == 0, pltpu.roll(imp, n - 1, 1), pltpu.roll(imp, 1, 1))


def _select(impx, nblk, seg, n_sel):
    lane = lax.broadcasted_iota(jnp.int32, impx.shape, 1)
    blk = (lane % seg) // NSA_PAIR
    nseg = impx.shape[1] // seg
    cnt = jnp.zeros(impx.shape, jnp.int32)
    for j in range(nblk):
        col = impx[:, j * NSA_PAIR:j * NSA_PAIR + 1]
        for sg in range(1, nseg):
            c = sg * seg + j * NSA_PAIR
            col = jnp.where(lane < sg * seg, col, impx[:, c:c + 1])
        beats = (col > impx) | ((col == impx) & (j < blk))
        cnt = cnt + beats.astype(jnp.int32)
    return (cnt < n_sel).astype(F32)


def _select_rows(impx, nblk, seg, n_sel):
    row = lax.broadcasted_iota(jnp.int32, impx.shape, 0)
    blk = (row % seg) // NSA_PAIR
    nseg, nq = impx.shape[0] // seg, impx.shape[1]
    cnt = jnp.zeros(impx.shape, jnp.int32)
    for j in range(nblk):
        parts = [jnp.broadcast_to(impx[sg * seg + j * NSA_PAIR:sg * seg + j * NSA_PAIR + 1, :], (seg, nq))
                 for sg in range(nseg)]
        col = parts[0] if nseg == 1 else jnp.concatenate(parts, axis=0)
        beats = (col > impx) | ((col == impx) & (j < blk))
        cnt = cnt + beats.astype(jnp.int32)
    return (cnt < n_sel).astype(F32)


def _expand_mask(sel, start, tk):
    nc = sel.shape[1]
    n = lax.broadcasted_iota(jnp.int32, (nc, tk), 0)
    k = lax.broadcasted_iota(jnp.int32, (nc, tk), 1)
    e = jnp.where((start + k) // NSA_CMP_BLOCK == n, 1.0, 0.0).astype(BF16)
    return _dot(sel.astype(BF16), e) > 0.5


def _nsa_attn_kernel(q_ref, gl_ref, ks_ref, vs_ref, kwp_ref, kwc_ref, vwp_ref, vwc_ref, kcmp_ref, vcmp_ref,
                     qg_ref, c2_ref, s2_ref, o_ref, qn_s, qr_s, ks_s, vs_s, m_s, acc_s, mw_s, accw_s,
                     *, tk, n_sel, chunk):
    i = pl.program_id(1)
    tq = q_ref.shape[1]
    nc = kcmp_ref.shape[1]
    rows_kv = NSA_GRP * tq
    rep = chunk // tq
    kv_of = lambda c: c * chunk // rows_kv
    kv_range = range(NSA_KV_HEADS)

    @pl.when(i == 0)
    def _():
        ks_s[...] = ks_ref[0].astype(BF16)
        vs = vs_ref[0].astype(BF16)
        vs_s[...] = jnp.concatenate([_with_ones(_head(vs, kv)) for kv in kv_range], axis=1)

    qn = _heads_rms(q_ref[0], qg_ref[...], _seg_ones(2 * LANES)) * HEAD_DIM ** -0.5
    _stack_heads(qn, qn_s)
    _stack_heads(_heads_rope(qn, c2_ref[...], s2_ref[...]), qr_s)

    t_c = i * tq + lax.broadcasted_iota(jnp.int32, (tq, nc), 0)
    n_c = lax.broadcasted_iota(jnp.int32, (tq, nc), 1)
    cmask = (n_c + 1) * NSA_CMP_BLOCK - 1 <= t_c
    o_cmp, imps = [], []
    for kv in kv_range:
        oc, imp = _cmp_attend(qn_s[kv * rows_kv:(kv + 1) * rows_kv, :], _head(kcmp_ref[0], kv),
                              _head(vcmp_ref[0], kv), cmask, NSA_GRP)
        o_cmp.append(oc)
        imps.append(imp)
    imp = jnp.concatenate(imps, axis=1).T
    row = lax.broadcasted_iota(jnp.int32, imp.shape, 0)
    imp = imp + jnp.where(row % 2 == 0, pltpu.roll(imp, imp.shape[0] - 1, 0), pltpu.roll(imp, 1, 0))
    t_i = i * tq + lax.broadcasted_iota(jnp.int32, imp.shape, 1)
    blk = (row % nc) // NSA_PAIR
    impx = jnp.where(blk == t_i // NSA_SEL_BLOCK, jnp.inf, jnp.where(blk * NSA_SEL_BLOCK <= t_i, imp, -jnp.inf))
    sel = _select_rows(impx, nc // NSA_PAIR, nc, n_sel).T

    _flash_init(m_s, acc_s)
    selk = [sel[:, kv * nc:(kv + 1) * nc] for kv in kv_range]
    t_k = i * tq + lax.broadcasted_iota(jnp.int32, (tq, tk), 0)
    k_off = lax.broadcasted_iota(jnp.int32, (tq, tk), 1)

    def body(j, carry):
        start = pl.multiple_of(j * tk, tk)
        kb = ks_s[pl.ds(start, tk), :]
        vbx = vs_s[pl.ds(start, tk), :]
        causal = (start + k_off) <= t_k
        mks = [jnp.concatenate([_expand_mask(selk[kv], start, tk) & causal] * rep, axis=0) for kv in kv_range]
        kvs = [([_head(kb, kv)], vbx[:, kv * LANES:(kv + 1) * LANES]) for kv in kv_range]
        _flash_step([qr_s], lambda c: kvs[kv_of(c)], lambda c: mks[kv_of(c)], m_s, acc_s, chunk)
        return carry

    lax.fori_loop(0, (i * tq + tq - 1) // tk + 1, body, 0)

    _flash_init(mw_s, accw_s)
    kk = jnp.concatenate([kwp_ref[0], kwc_ref[0]], axis=0).astype(BF16)
    vv = jnp.concatenate([vwp_ref[0], vwc_ref[0]], axis=0).astype(BF16)
    kvw = [([_head(kk, kv)], _with_ones(_head(vv, kv))) for kv in kv_range]
    wmk = jnp.concatenate([_band_mask(tq, tq, jnp.where(i > 0, 0, tq))] * rep, axis=0)
    _flash_step([qr_s], lambda c: kvw[kv_of(c)], lambda c: wmk, mw_s, accw_s, chunk)

    o_sel = _normalized(acc_s[...])
    o_win = _normalized(accw_s[...])
    gates = jax.nn.sigmoid(gl_ref[0])
    outs = []
    for h in range(N_HEADS):
        kv, g = divmod(h, NSA_GRP)
        outs.append(gates[:, 3 * h:3 * h + 1] * o_cmp[kv][g * tq:(g + 1) * tq]
                    + gates[:, 3 * h + 1:3 * h + 2] * o_sel[h * tq:(h + 1) * tq]
                    + gates[:, 3 * h + 2:3 * h + 3] * o_win[h * tq:(h + 1) * tq])
    o_ref[0] = jnp.concatenate(outs, axis=1).astype(o_ref.dtype)


def _nsa_attn(y, ksn, kwn, kcmp, vcmp, q_norm, cs, sn, tk):
    b, s, _ = y.shape
    tq = WINDOW
    hd = N_HEADS * HEAD_DIM
    nc = kcmp.shape[1]
    base = hd // NSA_KVD
    n_sel = min(NSA_TOPK, s // NSA_SEL_BLOCK)
    prev = lambda i: jnp.maximum(i - 1, 0)
    full = lambda a: pl.BlockSpec(a.shape, lambda bb, i: (0,) * a.ndim)
    tab = pl.BlockSpec((tq, LANES), lambda bb, i: (i, 0))
    q_norm = jnp.tile(q_norm, (1, N_HEADS))
    return pl.pallas_call(
        functools.partial(_nsa_attn_kernel, tk=tk, n_sel=n_sel, chunk=2 * tq), grid=(b, s // tq),
        in_specs=[pl.BlockSpec((1, tq, hd), lambda bb, i: (bb, i, 0)),
                  pl.BlockSpec((1, tq, NSA_KVD), lambda bb, i: (bb, i, base + 6)),
                  pl.BlockSpec((1, s, NSA_KVD), lambda bb, i: (bb, 0, 0)),
                  pl.BlockSpec((1, s, NSA_KVD), lambda bb, i: (bb, 0, base + 3)),
                  pl.BlockSpec((1, tq, NSA_KVD), lambda bb, i: (bb, prev(i), 0)),
                  pl.BlockSpec((1, tq, NSA_KVD), lambda bb, i: (bb, i, 0)),
                  pl.BlockSpec((1, tq, NSA_KVD), lambda bb, i: (bb, prev(i), base + 5)),
                  pl.BlockSpec((1, tq, NSA_KVD), lambda bb, i: (bb, i, base + 5)),
                  pl.BlockSpec((1, nc, NSA_KVD), lambda bb, i: (bb, 0, 0)),
                  pl.BlockSpec((1, nc, NSA_KVD), lambda bb, i: (bb, 0, 0)),
                  full(q_norm), tab, tab],
        out_specs=pl.BlockSpec((1, tq, hd), lambda bb, i: (bb, i, 0)),
        out_shape=jax.ShapeDtypeStruct((b, s, hd), BF16),
        scratch_shapes=[pltpu.VMEM((N_HEADS * tq, HEAD_DIM), BF16),
                        pltpu.VMEM((N_HEADS * tq, HEAD_DIM), BF16),
                        pltpu.VMEM((s, NSA_KVD), BF16),
                        pltpu.VMEM((s, NSA_KV_HEADS * LANES), BF16),
                        pltpu.VMEM((N_HEADS * tq, LANES), F32),
                        pltpu.VMEM((N_HEADS * tq, 2 * HEAD_DIM), F32),
                        pltpu.VMEM((N_HEADS * tq, LANES), F32),
                        pltpu.VMEM((N_HEADS * tq, 2 * HEAD_DIM), F32)],
        compiler_params=_cparams(("arbitrary", "arbitrary")),
    )(y, y, ksn, y, kwn, kwn, y, y, kcmp, vcmp, q_norm, cs, sn)


def _nsa_decode_kernel(pt_ref, kc_hbm, vc_hbm, ks_hbm, vs_hbm,
                       q_ref, gl_ref, ks_ref, vs_ref, kw_ref, vw_ref, kwb_ref, vwb_ref, wrow_ref, eloc_ref,
                       qn_ref, kcn_ref, ksn_ref, kwn_ref, cs_ref, sn_ref,
                       o_ref, kso_ref, kwo_ref, vwo_ref,
                       cmp_s, sel_s, ocmp_s, q_s, m_s, acc_s, k_buf, v_buf, sems,
                       *, ns, slots, npg, past, n_sel, layer):
    seq = pl.program_id(0)
    scale = HEAD_DIM ** -0.5
    nc = cmp_s.shape[1]
    rows_per_step = slots * PAGE_SIZE
    cper = rows_per_step // NSA_CMP_BLOCK
    kv_range = range(NSA_KV_HEADS)
    grp = lambda xs, kv: xs[kv * NSA_GRP:(kv + 1) * NSA_GRP]
    cmp_stream = _PageStream(pt_ref, seq, layer, [kc_hbm, vc_hbm], [k_buf, v_buf], sems, slots)
    sel_stream = _PageStream(pt_ref, seq, layer, [ks_hbm, vs_hbm], [k_buf, v_buf], sems, slots)
    cmp_stream.start(0, 0)
    pages = lambda buf, half, kv: jnp.concatenate([buf[half, k, kv] for k in range(slots)], axis=1)

    def block_sums(g, half):
        lhs = jnp.concatenate([pages(buf, half, kv) * wrow_ref[2 * j + kv:2 * j + kv + 1, :]
                               for j, buf in enumerate((k_buf, v_buf)) for kv in kv_range], axis=0)
        res = _dot(lhs.astype(BF16), eloc_ref[...])
        for k in range(npg):
            @pl.when(g == k)
            def _():
                cmp_s[:, k * cper:(k + 1) * cper] = res

    cmp_stream.sweep(npg, block_sums)
    sel_stream.start(0, 0)

    q = _rows(q_ref, ns)
    qn = [_rms(_head(q, h), qn_ref[...]) * scale for h in range(N_HEADS)]
    qr = [_rope(x, cs_ref[...], sn_ref[...], ROPE_DIM) for x in qn]
    t_c = past + lax.broadcasted_iota(jnp.int32, (ns, nc), 0)
    n_c = lax.broadcasted_iota(jnp.int32, (ns, nc), 1)
    cmask = (n_c + 1) * NSA_CMP_BLOCK - 1 <= t_c
    imps = []
    for kv in kv_range:
        kct = cmp_s[kv * HEAD_DIM:(kv + 1) * HEAD_DIM, :]
        kct = kct * lax.rsqrt(jnp.mean(kct * kct, axis=0, keepdims=True) + EPS) * kcn_ref[...]
        vct = cmp_s[(NSA_KV_HEADS + kv) * HEAD_DIM:(NSA_KV_HEADS + kv + 1) * HEAD_DIM, :]
        oc, imp = _cmp_attend(jnp.concatenate(grp(qn, kv), axis=0).astype(BF16), kct, vct, cmask, NSA_GRP,
                              row_minor=True)
        ocmp_s[kv] = oc
        imps.append(_pair_sum(imp))
        q_s[kv] = jnp.concatenate(grp(qr, kv), axis=0).astype(BF16)
    for kv in kv_range:
        sel = _select(imps[kv], nc // NSA_PAIR, nc, n_sel - 1)
        for k in range(npg):
            sel_s[kv, k] = sel[:, k * cper:(k + 1) * cper]
    _flash_init(m_s, acc_s)

    def selected(g, half):
        for kv in kv_range:
            kst = pages(k_buf, half, kv).astype(BF16)
            vst = pages(v_buf, half, kv)
            vtx = jnp.concatenate([vst, jnp.ones(vst.shape, F32)], axis=0).astype(BF16)
            mk = jnp.concatenate([_dot_nt(sel_s[kv, g].astype(BF16), eloc_ref[...]) > 0.5] * NSA_GRP, axis=0)
            _online_step(jnp.where(mk, _dot(q_s[kv], kst), NEG), lambda p, v=vtx: _dot_nt(p, v),
                         m_s.at[kv], acc_s.at[kv])

    sel_stream.sweep(npg, selected)

    pad = lambda a: jnp.concatenate([a, jnp.zeros((PAGE_SIZE - ns, a.shape[1]), a.dtype)], axis=0)
    vs_new = _rows(vs_ref, ns)
    ks_raw = _rows(ks_ref, ns)
    ksn = jnp.concatenate([_rope(_rms(_head(ks_raw, kv), ksn_ref[...]), cs_ref[...], sn_ref[...], ROPE_DIM)
                           for kv in kv_range], axis=1)
    kso_ref[0] = ksn
    kb = pad(ksn).astype(BF16)
    row = lax.broadcasted_iota(jnp.int32, (NSA_GRP * ns, PAGE_SIZE), 0) % ns
    col = lax.broadcasted_iota(jnp.int32, (NSA_GRP * ns, PAGE_SIZE), 1)
    for kv in kv_range:
        vbx = _with_ones(pad(_head(vs_new, kv)).astype(BF16))
        _online_step(jnp.where(col <= row, _dot_nt(q_s[kv], _head(kb, kv)), NEG), lambda p, v=vbx: _dot(p, v),
                     m_s.at[kv], acc_s.at[kv])
    outs_w, kw_out, vw_out = _window_step(
        q, _rows(kw_ref, ns), _rows(vw_ref, ns), [kwb_ref[0, 0, kv] for kv in kv_range],
        [vwb_ref[0, 0, kv] for kv in kv_range], qn_ref[...], kwn_ref[...], cs_ref[...], sn_ref[...], None,
        NSA_KV_HEADS)
    for kv in kv_range:
        kwo_ref[0, kv] = kw_out[kv]
        vwo_ref[0, kv] = vw_out[kv]
    gates = jax.nn.sigmoid(_rows(gl_ref, ns))
    outs = []
    for kv in kv_range:
        o_sel = _normalized(acc_s[kv])
        o_cmp = ocmp_s[kv]
        for g in range(NSA_GRP):
            h = kv * NSA_GRP + g
            outs.append(gates[:, 3 * h:3 * h + 1] * o_cmp[g * ns:(g + 1) * ns]
                        + gates[:, 3 * h + 1:3 * h + 2] * o_sel[g * ns:(g + 1) * ns]
                        + gates[:, 3 * h + 2:3 * h + 3] * outs_w[h])
    o = jnp.concatenate(outs, axis=1)
    for t in range(ns):
        o_ref[t, 0] = o[t:t + 1]


def _nsa_decode(y, pools, kw_cache, vw_cache, li, page_table, cmp_wk, cmp_wv, q_norm, k_norm, cs, sn):
    ns, db, n = y.shape
    hd = N_HEADS * HEAD_DIM
    slots = PAGE_SLOTS
    n_pages = page_table.shape[1]
    npg = n_pages // slots
    past = n_pages * PAGE_SIZE
    nc = past // NSA_CMP_BLOCK
    base = hd // NSA_KVD
    w = kw_cache.shape[2]
    n_blocks = -(-(past + ns) // NSA_SEL_BLOCK)
    n_sel = min(NSA_TOPK, n_blocks)
    y4 = y.reshape(ns, db, 1, n)
    full = lambda a: pl.BlockSpec(a.shape, lambda b, pt: (0,) * a.ndim)
    hbm = pl.BlockSpec(memory_space=pl.ANY)
    step = lambda width, cb: pl.BlockSpec((ns, 1, 1, width), lambda b, pt: (0, b, 0, cb))
    rows = slots * PAGE_SIZE
    cper = rows // NSA_CMP_BLOCK
    wrow = jnp.tile(jnp.concatenate([cmp_wk.T, cmp_wv.T], axis=0), (1, cper))
    eloc = (jnp.arange(rows)[:, None] // NSA_CMP_BLOCK == jnp.arange(cper)[None, :]).astype(BF16)
    kn = [k_norm[j:j + 1] for j in range(3)]
    kcn_col = k_norm[0].reshape(HEAD_DIM, 1)
    seq_out = pl.BlockSpec((1, ns, NSA_KVD), lambda b, pt: (b, 0, 0))
    buf_out = pl.BlockSpec((1, NSA_KV_HEADS, HEAD_DIM, w), lambda b, pt: (b, 0, 0, 0))
    buf_in = pl.BlockSpec((1, 1, NSA_KV_HEADS, HEAD_DIM, w), lambda b, pt: (li, b, 0, 0, 0))
    page_buf = pltpu.VMEM((2, slots, NSA_KV_HEADS, HEAD_DIM, PAGE_SIZE), F32)
    grid_spec = pltpu.PrefetchScalarGridSpec(
        num_scalar_prefetch=1, grid=(db,),
        in_specs=[hbm, hbm, hbm, hbm,
                  step(hd, 0), step(NSA_KVD, base + 6), step(NSA_KVD, base + 2), step(NSA_KVD, base + 3),
                  step(NSA_KVD, base + 4), step(NSA_KVD, base + 5), buf_in, buf_in,
                  full(wrow), full(eloc),
                  full(q_norm), full(kcn_col), full(kn[1]), full(kn[2]), full(cs), full(sn)],
        out_specs=[step(hd, 0), seq_out, buf_out, buf_out],
        scratch_shapes=[pltpu.VMEM((2 * NSA_KV_HEADS * HEAD_DIM, nc), F32),
                        pltpu.VMEM((NSA_KV_HEADS, npg, ns, cper), F32),
                        pltpu.VMEM((NSA_KV_HEADS, NSA_GRP * ns, HEAD_DIM), F32),
                        pltpu.VMEM((NSA_KV_HEADS, NSA_GRP * ns, HEAD_DIM), BF16),
                        pltpu.VMEM((NSA_KV_HEADS, NSA_GRP * ns, LANES), F32),
                        pltpu.VMEM((NSA_KV_HEADS, NSA_GRP * ns, 2 * HEAD_DIM), F32),
                        page_buf, page_buf, pltpu.SemaphoreType.DMA((2, 2))])
    kc_pool, vc_pool, ks_pool, vs_pool = [_row_minor(p) for p in pools]
    o, kso, kwo, vwo = pl.pallas_call(
        functools.partial(_nsa_decode_kernel, ns=ns, slots=slots, npg=npg, past=past, n_sel=n_sel, layer=li),
        grid_spec=grid_spec,
        out_shape=[jax.ShapeDtypeStruct((ns, db, 1, hd), F32),
                   jax.ShapeDtypeStruct((db, ns, NSA_KVD), F32),
                   jax.ShapeDtypeStruct((db, NSA_KV_HEADS, HEAD_DIM, w), F32),
                   jax.ShapeDtypeStruct((db, NSA_KV_HEADS, HEAD_DIM, w), F32)],
        compiler_params=_cparams(("arbitrary",)),
    )(page_table, kc_pool, vc_pool, ks_pool, vs_pool,
      y4, y4, y4, y4, y4, y4, _row_minor(kw_cache), _row_minor(vw_cache), wrow, eloc,
      q_norm, kcn_col, kn[1], kn[2], cs, sn)
    back = lambda a: a.transpose(0, 3, 1, 2)
    return o.reshape(ns, db, hd), kso, back(kwo), back(vwo)


def kernel(x_prompt, x_sample, cache_swa_k, cache_swa_v, cache_mla_latent, cache_mla_krope, cache_nsa_kcmp, cache_nsa_vcmp, cache_nsa_ksel, cache_nsa_vsel, cache_nsa_kwin, cache_nsa_vwin, state_conv_ffn, page_table, c_prompt, c_sample, ada_w, ada_b, norm_mix, norm_ffn, ffn_w_up, ffn_conv, ffn_w_down, a_w_in, a_q_norm, a_k_norm, a_sinks, a_w_out, b_w_in, b_qa_norm, b_w_qb, b_q_norm_nope, b_q_norm_rope, b_kv_norm, b_krope_norm, b_w_uk, b_w_uv, b_w_out, c_w_in, c_q_norm, c_k_norm, c_cmp_wk, c_cmp_wv, c_w_out):
    nb, seq, d = x_prompt.shape
    db, ds, _ = x_sample.shape
    depth = ada_w.shape[0]
    ff = ffn_w_down.shape[1]
    past = page_table.shape[1] * PAGE_SIZE
    hd = N_HEADS * HEAD_DIM

    mod = _modulate(jnp.concatenate([c_prompt, c_sample], axis=0), ada_w, ada_b)
    tr_p = math.gcd(seq, 512)
    gp = _Group(True, mod[:, :, :nb].reshape(depth, 6, nb, 1, d), tr_p)
    gs = _Group(False, mod[:, :, nb:], db)
    norm_mix3 = norm_mix.reshape(depth, 1, d)
    norm_ffn3 = norm_ffn.reshape(depth, 1, d)

    pos_p = jnp.arange(seq, dtype=jnp.int32)
    pos_s = past + jnp.arange(ds, dtype=jnp.int32)
    cw_p, sw_p = _rope_tables(pos_p, ROPE_DIM, ROPE_THETA, HEAD_DIM)
    cw2_p, sw2_p = jnp.tile(cw_p, (1, LANES // HEAD_DIM)), jnp.tile(sw_p, (1, LANES // HEAD_DIM))
    cw_s, sw_s = _rope_tables(pos_s, ROPE_DIM, ROPE_THETA, HEAD_DIM)
    cm_p, sm_p = _rope_tables(pos_p, MLA_ROPE, MLA_THETA, MLA_ROPE)
    cm_s, sm_s = _rope_tables(pos_s, MLA_ROPE, MLA_THETA, MLA_ROPE)

    xp = x_prompt
    xs = x_sample.transpose(1, 0, 2)
    row2 = lambda v: v.reshape(1, -1)
    tm = lambda a: a.transpose(1, 0, 2)
    out = {k: [] for k in ("swa_k_p", "swa_v_p", "swa_k_s", "swa_v_s", "mla_c_p", "mla_r_p", "mla_c_s", "mla_r_s",
                           "conv_p", "conv_s")}
    nsa_p = [[] for _ in range(6)]
    nsa_s = [[] for _ in range(6)]
    ia = ib = ic = 0
    for layer in range(depth):
        kind = layer % N_MIXERS
        if kind == 0:
            w_in = a_w_in[ia].astype(BF16)
            qn, kn, sinks = row2(a_q_norm[ia]), row2(a_k_norm[ia]), row2(a_sinks[ia])
            yp = _proj(xp, gp, norm_mix3, layer, w_in)
            op, kp, vp = _band_attn(yp, qn, kn, sinks, cw2_p, sw2_p)
            ys = _proj(xs, gs, norm_mix3, layer, w_in)
            os_, ks_, vs_ = _step_attn(ys, cache_swa_k, cache_swa_v, ia, qn, kn, sinks, cw_s, sw_s)
            kv4 = lambda a: a.reshape(a.shape[0], a.shape[1], SWA_KV_HEADS, HEAD_DIM)
            out["swa_k_p"].append(kv4(kp)); out["swa_v_p"].append(kv4(vp))
            out["swa_k_s"].append(ks_); out["swa_v_s"].append(vs_)
            w_out = a_w_out[ia].astype(BF16)
            ia += 1
        elif kind == 1:
            w_in = b_w_in[ib].astype(BF16)
            wqb = b_w_qb[ib].reshape(MLA_Q_LORA, N_HEADS, MLA_NOPE + MLA_ROPE)
            wqb = jnp.concatenate([wqb[:, :, :MLA_NOPE].reshape(MLA_Q_LORA, -1),
                                   wqb[:, :, MLA_NOPE:].reshape(MLA_Q_LORA, -1)], axis=1).astype(BF16)
            wuk = b_w_uk[ib].transpose(1, 2, 0).astype(BF16)
            wuv = b_w_uv[ib].transpose(1, 0, 2).astype(BF16)
            norms = (row2(b_qa_norm[ib]), wqb, row2(b_kv_norm[ib]), row2(b_krope_norm[ib]))
            nn = jnp.tile(row2(b_q_norm_nope[ib]), (1, N_HEADS))
            nr = jnp.tile(row2(b_q_norm_rope[ib]), (1, N_HEADS))
            wide = lambda t: jnp.tile(t, (1, LANES // MLA_ROPE))
            qp, cp, rp = _mla_proj(xp, gp, norm_mix3, layer, w_in, *norms, cm_p[None], sm_p[None])
            op = _mla_attn(qp, cp, rp, wide(cm_p), wide(sm_p), nn, nr, wuk, wuv, math.gcd(seq, 128),
                           math.gcd(seq, 256), 256)
            qs, cs_, rs_ = _mla_proj(xs, gs, norm_mix3, layer, w_in, *norms, cm_s[:, None], sm_s[:, None])
            os_ = _mla_decode(qs, cs_, rs_, cache_mla_latent, cache_mla_krope, ib, page_table,
                              wide(cm_s), wide(sm_s), nn, nr, wuk, wuv)
            out["mla_c_p"].append(cp); out["mla_r_p"].append(rp)
            out["mla_c_s"].append(tm(cs_)); out["mla_r_s"].append(tm(rs_))
            w_out = b_w_out[ib].astype(BF16)
            ib += 1
        else:
            n_in = c_w_in.shape[2]
            n_pad = -(-n_in // 128) * 128
            w_in = jnp.pad(c_w_in[ic], ((0, 0), (0, n_pad - n_in))).astype(BF16)
            qn, kn3 = row2(c_q_norm[ic]), c_k_norm[ic]
            kv4 = lambda a: a.reshape(a.shape[0], a.shape[1], NSA_KV_HEADS, HEAD_DIM)
            col = lambda a, j: a[:, :, hd + j * NSA_KVD:hd + (j + 1) * NSA_KVD]
            yp = _proj(xp, gp, norm_mix3, layer, w_in)
            ksn, kwn, kcmp, vcmp = _nsa_prep(yp, c_cmp_wk[ic], c_cmp_wv[ic], kn3, cw2_p, sw2_p)
            op = _nsa_attn(yp, ksn, kwn, kcmp, vcmp, qn, cw2_p, sw2_p, math.gcd(seq, 256))
            for j, a in enumerate((col(yp, 0), col(yp, 1), ksn, col(yp, 3), kwn[:, -WINDOW:],
                                   col(yp, 5)[:, -WINDOW:])):
                nsa_p[j].append(kv4(a))
            ys = _proj(xs, gs, norm_mix3, layer, w_in)
            pools = (cache_nsa_kcmp, cache_nsa_vcmp, cache_nsa_ksel, cache_nsa_vsel)
            os_, kso, kwo, vwo = _nsa_decode(ys, pools, cache_nsa_kwin, cache_nsa_vwin, ic, page_table,
                                             c_cmp_wk[ic], c_cmp_wv[ic], qn, kn3, cw_s, sw_s)
            for j, a in enumerate((tm(col(ys, 0)), tm(col(ys, 1)), kso, tm(col(ys, 3)))):
                nsa_s[j].append(kv4(a))
            nsa_s[4].append(kwo)
            nsa_s[5].append(vwo)
            w_out = c_w_out[ic].astype(BF16)
            ic += 1
        xp = _outproj(op, xp, gp, layer, w_out)
        xs = _outproj(os_, xs, gs, layer, w_out)
        wup, wd = ffn_w_up[layer].astype(BF16), ffn_w_down[layer].astype(BF16)
        tf = math.gcd(ff, 256)
        xp, bp = _ffn_prompt(xp, gp, norm_ffn3, layer, wup, ffn_conv[layer], wd, math.gcd(seq, 1024), tf)
        xs, bs = _ffn_sample(xs, gs, norm_ffn3, layer, wup, ffn_conv[layer], wd, state_conv_ffn[layer], tf)
        out["conv_p"].append(bp); out["conv_s"].append(bs)

    st = lambda xs_: jnp.stack(xs_)
    return (xp, xs.transpose(1, 0, 2),
            st(out["swa_k_p"]), st(out["swa_v_p"]), st(out["mla_c_p"]), st(out["mla_r_p"]),
            *[st(a) for a in nsa_p], st(out["conv_p"]),
            st(out["swa_k_s"]), st(out["swa_v_s"]), st(out["mla_c_s"]), st(out["mla_r_s"]),
            *[st(a) for a in nsa_s], st(out["conv_s"]))
```

```python
import functools
import math

import jax
import jax.numpy as jnp
from jax import lax
from jax.experimental import pallas as pl
from jax.experimental.pallas import tpu as pltpu

F32 = jnp.float32
BF16 = jnp.bfloat16

N_HEADS = 16
HEAD_DIM = 64
ROPE_DIM = 16
ROPE_THETA = 500000.0
EPS = 1e-6
PAGE_SIZE = 128
SWA_KV_HEADS = 4
WINDOW = 128
MLA_Q_LORA = 384
MLA_KV_LORA = 256
MLA_NOPE = 64
MLA_ROPE = 32
MLA_THETA = 10000.0
NSA_KV_HEADS = 2
NSA_CMP_BLOCK = 32
NSA_SEL_BLOCK = 64
NSA_TOPK = 16
N_MIXERS = 3
NEG = -1e30
TINY = float(jnp.finfo(jnp.float32).tiny)
VMEM_LIMIT = 56 * 1024 * 1024
PAGE_SLOTS = 8


def _cparams(sem):
    return pltpu.CompilerParams(dimension_semantics=sem, vmem_limit_bytes=VMEM_LIMIT)


def _dot(a, b):
    return jnp.dot(a, b, preferred_element_type=F32)


def _dot_nt(a, b):
    return lax.dot_general(a, b, (((1,), (1,)), ((), ())), preferred_element_type=F32)


def _rms(x, g):
    return x * lax.rsqrt(jnp.mean(x * x, axis=-1, keepdims=True) + EPS) * g


def _rope(x, c, s, rot):
    half = rot // 2
    parts = [x[:, half:rot], x[:, :half]]
    if x.shape[1] > rot:
        parts.append(x[:, rot:])
    return x * c + jnp.concatenate(parts, axis=1) * s


def _silu(x):
    return x * jax.nn.sigmoid(x)


def _norm_mod(x, g, sc, sh):
    return _rms(x, g) * (1.0 + sc) + sh


def _rope_tables(pos, rot, theta, width):
    half = rot // 2
    inv = jnp.power(jnp.float32(theta), -jnp.arange(half, dtype=F32) / half)
    ang = pos.astype(F32)[:, None] * inv[None, :]
    cos, sin = jnp.cos(ang), jnp.sin(ang)
    n = pos.shape[0]
    c = jnp.concatenate([cos, cos, jnp.ones((n, width - rot), F32)], axis=1)
    s = jnp.concatenate([-sin, sin, jnp.zeros((n, width - rot), F32)], axis=1)
    return c, s


class _Group:
    def __init__(self, per_group, mod, tr):
        self.pm = per_group
        self.mods = [mod[layer] for layer in range(mod.shape[0])]
        self.tr = tr

    def mod(self, layer):
        return self.mods[layer]

    def mod_spec(self, j, nargs=2):
        d = self.mods[0].shape[-1]
        if self.pm:
            shape, f = (1, 1, 1, d), (lambda g: (j, g, 0, 0))
        else:
            shape, f = (1, self.mods[0].shape[1], d), (lambda g: (j, 0, 0))
        if nargs == 2:
            return pl.BlockSpec(shape, lambda g, r: f(g))
        return pl.BlockSpec(shape, lambda g, r, k: f(g))


def _mod_val(ref, pm):
    return ref[0, 0] if pm else ref[0]


def _mod_kernel(c_ref, w_ref, b_ref, o_ref):
    a = _silu(c_ref[...])
    o_ref[0, 0] = _dot(a.astype(BF16), w_ref[0].astype(BF16)) + b_ref[0]


def _modulate(c_all, ada_w, ada_b):
    nl, d, d6 = ada_w.shape
    n = c_all.shape[0]
    nj = d6 // d
    return pl.pallas_call(
        _mod_kernel, grid=(nl, nj),
        in_specs=[pl.BlockSpec((n, d), lambda l, j: (0, 0)),
                  pl.BlockSpec((1, d, d), lambda l, j: (l, 0, j)),
                  pl.BlockSpec((1, 1, d), lambda l, j: (l, 0, j))],
        out_specs=pl.BlockSpec((1, 1, n, d), lambda l, j: (l, j, 0, 0)),
        out_shape=jax.ShapeDtypeStruct((nl, nj, n, d), F32),
        compiler_params=_cparams(("arbitrary", "arbitrary")),
    )(c_all, ada_w, ada_b.reshape(nl, 1, d6))


def _proj_kernel(x_ref, g_ref, sc_ref, sh_ref, w_ref, o_ref, *, pm):
    h = _norm_mod(x_ref[0], g_ref[0], _mod_val(sc_ref, pm), _mod_val(sh_ref, pm))
    o_ref[0] = _dot(h.astype(BF16), w_ref[...])


def _proj(x, grp, gains, layer, w_bf):
    ng, nr, d = x.shape
    n = w_bf.shape[1]
    tr = grp.tr
    return pl.pallas_call(
        functools.partial(_proj_kernel, pm=grp.pm), grid=(ng, nr // tr),
        in_specs=[pl.BlockSpec((1, tr, d), lambda g, r: (g, r, 0)),
                  pl.BlockSpec((1, 1, d), lambda g, r: (0, 0, 0)),
                  grp.mod_spec(1), grp.mod_spec(0),
                  pl.BlockSpec((d, n), lambda g, r: (0, 0))],
        out_specs=pl.BlockSpec((1, tr, n), lambda g, r: (g, r, 0)),
        out_shape=jax.ShapeDtypeStruct((ng, nr, n), F32),
        compiler_params=_cparams(("arbitrary", "arbitrary")),
    )(x, gains[layer:layer + 1], grp.mod(layer), grp.mod(layer), w_bf)


def _mla_proj_kernel(x_ref, g_ref, sc_ref, sh_ref, w_ref, qan_ref, wqb_ref, kvn_ref, krn_ref,
                     c_ref, s_ref, q_ref, lat_ref, kr_ref, *, pm):
    h = _norm_mod(x_ref[0], g_ref[0], _mod_val(sc_ref, pm), _mod_val(sh_ref, pm))
    y = _dot(h.astype(BF16), w_ref[...])
    a, b = MLA_Q_LORA, MLA_Q_LORA + MLA_KV_LORA
    qa = _rms(y[:, :a], qan_ref[...])
    q_ref[0] = _dot(qa.astype(BF16), wqb_ref[...])
    lat_ref[0] = _rms(y[:, a:b], kvn_ref[...])
    kr = _rms(y[:, b:], krn_ref[...])
    kr_ref[0] = _rope(kr, c_ref[0], s_ref[0], MLA_ROPE)


def _mla_proj(x, grp, gains, layer, w_bf, qa_norm, wqb_bf, kv_norm, krope_norm, cs, sn):
    ng, nr, d = x.shape
    tr = grp.tr
    nq = wqb_bf.shape[1]
    if grp.pm:
        tab = pl.BlockSpec((1, tr, MLA_ROPE), lambda g, r: (0, r, 0))
    else:
        tab = pl.BlockSpec((1, 1, MLA_ROPE), lambda g, r: (g, 0, 0))
    full = lambda a: pl.BlockSpec(a.shape, lambda g, r: (0,) * a.ndim)
    return pl.pallas_call(
        functools.partial(_mla_proj_kernel, pm=grp.pm), grid=(ng, nr // tr),
        in_specs=[pl.BlockSpec((1, tr, d), lambda g, r: (g, r, 0)),
                  pl.BlockSpec((1, 1, d), lambda g, r: (0, 0, 0)),
                  grp.mod_spec(1), grp.mod_spec(0),
                  full(w_bf), full(qa_norm), full(wqb_bf), full(kv_norm), full(krope_norm), tab, tab],
        out_specs=[pl.BlockSpec((1, tr, nq), lambda g, r: (g, r, 0)),
                   pl.BlockSpec((1, tr, MLA_KV_LORA), lambda g, r: (g, r, 0)),
                   pl.BlockSpec((1, tr, MLA_ROPE), lambda g, r: (g, r, 0))],
        out_shape=[jax.ShapeDtypeStruct((ng, nr, nq), F32),
                   jax.ShapeDtypeStruct((ng, nr, MLA_KV_LORA), F32),
                   jax.ShapeDtypeStruct((ng, nr, MLA_ROPE), F32)],
        compiler_params=_cparams(("arbitrary", "arbitrary")),
    )(x, gains[layer:layer + 1], grp.mod(layer), grp.mod(layer), w_bf, qa_norm, wqb_bf, kv_norm, krope_norm, cs, sn)


def _outproj_kernel(o_ref, x_ref, gt_ref, w_ref, y_ref, *, pm):
    y_ref[0] = x_ref[0] + _mod_val(gt_ref, pm) * _dot(o_ref[0].astype(BF16), w_ref[...])


def _outproj(o, x, grp, layer, w_bf):
    ng, nr, d = x.shape
    k = o.shape[-1]
    tr = grp.tr
    return pl.pallas_call(
        functools.partial(_outproj_kernel, pm=grp.pm), grid=(ng, nr // tr),
        in_specs=[pl.BlockSpec((1, tr, k), lambda g, r: (g, r, 0)),
                  pl.BlockSpec((1, tr, d), lambda g, r: (g, r, 0)),
                  grp.mod_spec(2),
                  pl.BlockSpec((k, d), lambda g, r: (0, 0))],
        out_specs=pl.BlockSpec((1, tr, d), lambda g, r: (g, r, 0)),
        out_shape=jax.ShapeDtypeStruct((ng, nr, d), F32),
        compiler_params=_cparams(("arbitrary", "arbitrary")),
    )(o, x, grp.mod(layer), w_bf)


HALO = 16


def _ffn_tail(mg, mv, wd_ref, x_ref, gt, y_ref):
    f = pl.program_id(2)
    d = _dot((_silu(mg) * mv).astype(BF16), wd_ref[...])

    @pl.when(f == 0)
    def _():
        y_ref[0] = d

    @pl.when(f > 0)
    def _():
        y_ref[0] += d

    @pl.when(f == pl.num_programs(2) - 1)
    def _():
        y_ref[0] = x_ref[0] + gt * y_ref[0]


def _ffn_prompt_kernel(x_ref, xh_ref, g_ref, sc_ref, sh_ref, gt_ref, wup_ref, cw_ref, wd_ref,
                       y_ref, b_ref, h_s, act_s, u_s, *, tr, tf):
    r = pl.program_id(1)
    ff = wd_ref.shape[0]
    g, sc, sh = g_ref[0], sc_ref[0, 0], sh_ref[0, 0]
    h_s[HALO:, :] = _norm_mod(x_ref[0], g, sc, sh).astype(BF16)
    hh = _norm_mod(xh_ref[0], g, sc, sh)
    h_s[:HALO, :] = jnp.where(r > 0, hh, 0.0).astype(BF16)
    nf = ff // tf
    cols = lambda half, f: pl.ds(pl.multiple_of(half * ff + f * tf, tf), tf)

    def up(f, slot):
        for half in range(2):
            u_s[slot, half] = _dot(h_s[...], wup_ref[:, cols(half, f)])

    def mixed(f, slot, half):
        u = u_s[slot, half]
        cw = cw_ref[:, cols(half, f)]
        b_ref[0, 0, :, cols(half, f)] = u[HALO + tr - 2:]
        a = pltpu.roll(u, 2, 0)[HALO:]
        b = pltpu.roll(u, 1, 0)[HALO:]
        return cw[0:1] * a + cw[1:2] * b + cw[2:3] * u[HALO:]

    def activate(f, slot):
        act_s[:, cols(0, f)] = (_silu(mixed(f, slot, 0)) * mixed(f, slot, 1)).astype(BF16)

    def body(it, carry):
        f = 2 * it
        up(f + 1, 1)
        activate(f, 0)
        up(f + 2, 0)
        activate(f + 1, 1)
        return carry

    assert nf % 2 == 1
    up(0, 0)
    lax.fori_loop(0, nf // 2, body, 0)
    activate(nf - 1, 0)
    y_ref[0] = x_ref[0] + gt_ref[0, 0] * _dot(act_s[...], wd_ref[...])


def _ffn_prompt(x, grp, gains, layer, wup_bf, conv_w, wd_bf, tr, tf):
    ng, nr, d = x.shape
    ff = wd_bf.shape[0]
    ms = lambda j: grp.mod_spec(j)
    once = lambda a: pl.BlockSpec(a.shape, lambda g, r: (0,) * a.ndim, pipeline_mode=pl.Buffered(1))
    y, buf = pl.pallas_call(
        functools.partial(_ffn_prompt_kernel, tr=tr, tf=tf), grid=(ng, nr // tr),
        in_specs=[pl.BlockSpec((1, tr, d), lambda g, r: (g, r, 0)),
                  pl.BlockSpec((1, HALO, d), lambda g, r: (g, jnp.maximum(r * (tr // HALO) - 1, 0), 0)),
                  pl.BlockSpec((1, 1, d), lambda g, r: (0, 0, 0)),
                  ms(4), ms(3), ms(5), once(wup_bf), once(conv_w), once(wd_bf)],
        out_specs=[pl.BlockSpec((1, tr, d), lambda g, r: (g, r, 0)),
                   pl.BlockSpec((1, 1, 2, 2 * ff), lambda g, r: (g, r, 0, 0))],
        out_shape=[jax.ShapeDtypeStruct((ng, nr, d), F32),
                   jax.ShapeDtypeStruct((ng, nr // tr, 2, 2 * ff), F32)],
        scratch_shapes=[pltpu.VMEM((tr + HALO, d), BF16), pltpu.VMEM((tr, ff), BF16),
                        pltpu.VMEM((2, 2, tr + HALO, tf), F32)],
        compiler_params=_cparams(("arbitrary", "arbitrary")),
    )(x, x, gains[layer:layer + 1], grp.mod(layer), grp.mod(layer), grp.mod(layer), wup_bf, conv_w, wd_bf)
    return y, buf[:, -1]


def _ffn_sample_kernel(x_ref, g_ref, sc_ref, sh_ref, gt_ref, sg_ref, sv_ref, wg_ref, wv_ref, cg_ref, cv_ref,
                       wd_ref, y_ref, bg_ref, bv_ref, h_s, *, nt, p):
    tile = lambda m: jnp.concatenate([m] * nt, axis=0)

    @pl.when(pl.program_id(2) == 0)
    def _():
        h_s[...] = _norm_mod(x_ref[0], g_ref[0], tile(sc_ref[0]), tile(sh_ref[0])).astype(BF16)

    h = h_s[...]
    n = nt * p

    def conv(st, u, cw):
        e = jnp.concatenate([st, u], axis=0)
        return cw[0:1] * e[0:n] + cw[1:2] * e[p:p + n] + cw[2:3] * e[2 * p:]

    ug = _dot(h, wg_ref[...])
    uv = _dot(h, wv_ref[...])
    bg_ref[...] = jnp.concatenate([sg_ref[...], ug], axis=0)[n:]
    bv_ref[...] = jnp.concatenate([sv_ref[...], uv], axis=0)[n:]
    _ffn_tail(conv(sg_ref[...], ug, cg_ref[...]), conv(sv_ref[...], uv, cv_ref[...]), wd_ref, x_ref,
              tile(gt_ref[0]), y_ref)


def _ffn_sample(x, grp, gains, layer, wup_bf, conv_w, wd_bf, state, tf):
    nt, p, d = x.shape
    ff = wd_bf.shape[0]
    nf = ff // tf
    n = nt * p
    st = state.transpose(1, 0, 2).reshape(2 * p, 2 * ff)
    ms = lambda j: grp.mod_spec(j, nargs=3)
    y, bg, bv = pl.pallas_call(
        functools.partial(_ffn_sample_kernel, nt=nt, p=p), grid=(1, 1, nf),
        in_specs=[pl.BlockSpec((1, n, d), lambda g, r, f: (0, 0, 0)),
                  pl.BlockSpec((1, 1, d), lambda g, r, f: (0, 0, 0)),
                  ms(4), ms(3), ms(5),
                  pl.BlockSpec((2 * p, tf), lambda g, r, f: (0, f)),
                  pl.BlockSpec((2 * p, tf), lambda g, r, f: (0, nf + f)),
                  pl.BlockSpec((d, tf), lambda g, r, f: (0, f)),
                  pl.BlockSpec((d, tf), lambda g, r, f: (0, nf + f)),
                  pl.BlockSpec((3, tf), lambda g, r, f: (0, f)),
                  pl.BlockSpec((3, tf), lambda g, r, f: (0, nf + f)),
                  pl.BlockSpec((tf, d), lambda g, r, f: (f, 0))],
        out_specs=[pl.BlockSpec((1, n, d), lambda g, r, f: (0, 0, 0)),
                   pl.BlockSpec((2 * p, tf), lambda g, r, f: (0, f)),
                   pl.BlockSpec((2 * p, tf), lambda g, r, f: (0, f))],
        out_shape=[jax.ShapeDtypeStruct((1, n, d), F32),
                   jax.ShapeDtypeStruct((2 * p, ff), F32),
                   jax.ShapeDtypeStruct((2 * p, ff), F32)],
        scratch_shapes=[pltpu.VMEM((n, d), BF16)],
        compiler_params=_cparams(("arbitrary", "arbitrary", "arbitrary")),
    )(x.reshape(1, n, d), gains[layer:layer + 1], grp.mod(layer), grp.mod(layer), grp.mod(layer), st, st, wup_bf, wup_bf, conv_w, conv_w, wd_bf)
    new_state = jnp.concatenate([bg, bv], axis=-1).reshape(2, p, 2 * ff).transpose(1, 0, 2)
    return y.reshape(nt, p, d), new_state


def _sink_col(sink_ref, heads, nq):
    return jnp.concatenate([jnp.broadcast_to(sink_ref[:, h:h + 1], (nq, 1)) for h in heads], axis=0)


def _band_mask(nq, w, prev_off):
    qi = lax.broadcasted_iota(jnp.int32, (nq, w + nq), 0)
    kj = lax.broadcasted_iota(jnp.int32, (nq, w + nq), 1)
    return ((kj < w) & (kj >= qi + prev_off)) | ((kj >= w) & ((kj - w) <= qi))


def _head(x, h):
    return x[:, h * HEAD_DIM:(h + 1) * HEAD_DIM]


def _band_attn_kernel(q_ref, kp_ref, kc_ref, vp_ref, vc_ref, qg_ref, kg_ref, cq_ref, sq_ref, cp_ref, sp_ref,
                      sink_ref, o_ref, ko_ref, vo_ref, q_s, m_s, acc_s, *, n_kv, chunk):
    i = pl.program_id(1)
    w = q_ref.shape[1]
    rows_kv = (N_HEADS // n_kv) * w
    seg = _seg_ones(2 * LANES)
    cq, sq = cq_ref[...], sq_ref[...]
    _stack_heads(_heads_rope(_heads_rms(q_ref[0], qg_ref[...], seg), cq, sq) * HEAD_DIM ** -0.5, q_s)
    kc = _heads_rope(_heads_rms(kc_ref[0], kg_ref[...], seg), cq, sq)
    kp = _heads_rope(_heads_rms(kp_ref[0], kg_ref[...], seg), cp_ref[...], sp_ref[...])
    kk = jnp.concatenate([kp, kc], axis=0).astype(BF16)
    vv = jnp.concatenate([vp_ref[0], vc_ref[0]], axis=0).astype(BF16)
    kvs = [([_head(kk, kv)], _with_ones(_head(vv, kv))) for kv in range(n_kv)]
    for h in range(N_HEADS):
        m_s[h * w:(h + 1) * w, :] = jnp.broadcast_to(sink_ref[:, h:h + 1], (w, LANES))
    acc_s[...] = jnp.concatenate([jnp.zeros((N_HEADS * w, HEAD_DIM), F32), jnp.ones((N_HEADS * w, HEAD_DIM), F32)],
                                 axis=1)
    mask = _band_mask(w, w, jnp.where(i > 0, 0, w))
    mk = jnp.concatenate([mask] * (chunk // w), axis=0)
    _flash_step([q_s], lambda c: kvs[c * chunk // rows_kv], lambda c: mk, m_s, acc_s, chunk)
    o_ref[0] = _unstack_heads(_normalized(acc_s[...]), w).astype(o_ref.dtype)
    ko_ref[0] = kc
    vo_ref[0] = vc_ref[0]


def _band_attn(y, q_norm, k_norm, sinks, cs, sn):
    b, s, _ = y.shape
    w = WINDOW
    kvd = SWA_KV_HEADS * HEAD_DIM
    hd = N_HEADS * HEAD_DIM
    kb, vb = hd // kvd, hd // kvd + 1
    prev = lambda i: jnp.maximum(i - 1, 0)
    full = lambda a: pl.BlockSpec(a.shape, lambda bb, i: (0,) * a.ndim)
    tab_c = pl.BlockSpec((w, LANES), lambda bb, i: (i, 0))
    tab_p = pl.BlockSpec((w, LANES), lambda bb, i: (prev(i), 0))
    q_norm, k_norm = jnp.tile(q_norm, (1, N_HEADS)), jnp.tile(k_norm, (1, SWA_KV_HEADS))
    return pl.pallas_call(
        functools.partial(_band_attn_kernel, n_kv=SWA_KV_HEADS, chunk=2 * w), grid=(b, s // w),
        scratch_shapes=[pltpu.VMEM((N_HEADS * w, HEAD_DIM), BF16),
                        pltpu.VMEM((N_HEADS * w, LANES), F32),
                        pltpu.VMEM((N_HEADS * w, 2 * HEAD_DIM), F32)],
        in_specs=[pl.BlockSpec((1, w, hd), lambda bb, i: (bb, i, 0)),
                  pl.BlockSpec((1, w, kvd), lambda bb, i: (bb, prev(i), kb)),
                  pl.BlockSpec((1, w, kvd), lambda bb, i: (bb, i, kb)),
                  pl.BlockSpec((1, w, kvd), lambda bb, i: (bb, prev(i), vb)),
                  pl.BlockSpec((1, w, kvd), lambda bb, i: (bb, i, vb)),
                  full(q_norm), full(k_norm), tab_c, tab_c, tab_p, tab_p, full(sinks)],
        out_specs=[pl.BlockSpec((1, w, hd), lambda bb, i: (bb, i, 0)),
                   pl.BlockSpec((1, w, kvd), lambda bb, i: (bb, 0, 0)),
                   pl.BlockSpec((1, w, kvd), lambda bb, i: (bb, 0, 0))],
        out_shape=[jax.ShapeDtypeStruct((b, s, hd), BF16),
                   jax.ShapeDtypeStruct((b, w, kvd), F32),
                   jax.ShapeDtypeStruct((b, w, kvd), F32)],
        compiler_params=_cparams(("arbitrary", "arbitrary")),
    )(y, y, y, y, y, q_norm, k_norm, cs, sn, cs, sn, sinks)


def _rows(ref, n):
    return jnp.concatenate([ref[t, 0] for t in range(n)], axis=0)


def _window_step(q, k_new, v_new, kbt, vbt, qn, kn, cs, sn, sink_ref, n_kv):
    ns, w = q.shape[0], kbt[0].shape[1]
    grp = N_HEADS // n_kv
    scale = HEAD_DIM ** -0.5
    t_q = lax.broadcasted_iota(jnp.int32, (grp * ns, w), 0) % ns
    col = lax.broadcasted_iota(jnp.int32, (grp * ns, w), 1)
    see_buf = col >= t_q
    see_new = col <= t_q
    pad = lambda a: jnp.concatenate([a, jnp.zeros((w - ns, a.shape[1]), a.dtype)], axis=0)
    outs, knew = [], []
    for kv in range(n_kv):
        kn_h = _rope(_rms(_head(k_new, kv), kn), cs, sn, ROPE_DIM)
        knew.append(kn_h)
        heads = range(kv * grp, (kv + 1) * grp)
        qg = jnp.concatenate([_rope(_rms(_head(q, h), qn), cs, sn, ROPE_DIM) * scale for h in heads],
                             axis=0).astype(BF16)
        s1 = jnp.where(see_buf, _dot(qg, kbt[kv].astype(BF16)), NEG)
        s2 = jnp.where(see_new, _dot_nt(qg, pad(kn_h).astype(BF16)), NEG)
        m = jnp.maximum(jnp.max(s1, axis=-1, keepdims=True), jnp.max(s2, axis=-1, keepdims=True))
        if sink_ref is not None:
            sink = _sink_col(sink_ref, heads, ns)
            m = jnp.maximum(m, sink)
        p1, p2 = jnp.exp(s1 - m), jnp.exp(s2 - m)
        den = jnp.sum(p1, axis=-1, keepdims=True) + jnp.sum(p2, axis=-1, keepdims=True)
        if sink_ref is not None:
            den = den + jnp.exp(sink - m)
        o = (_dot_nt(p1.astype(BF16), vbt[kv].astype(BF16))
             + _dot(p2.astype(BF16), pad(_head(v_new, kv)).astype(BF16))) / den
        outs += [o[g * ns:(g + 1) * ns] for g in range(grp)]

    def shifted(bt, new):
        new_t = jnp.concatenate([jnp.zeros((w - ns, new.shape[1]), F32), new], axis=0).T
        lane = lax.broadcasted_iota(jnp.int32, (HEAD_DIM, w), 1)
        return [jnp.where(lane >= w - ns, new_t[kv * HEAD_DIM:(kv + 1) * HEAD_DIM], pltpu.roll(bt[kv], w - ns, 1))
                for kv in range(n_kv)]

    return outs, shifted(kbt, jnp.concatenate(knew, axis=1)), shifted(vbt, v_new)


def _step_attn_kernel(q_ref, k_ref, v_ref, kb_ref, vb_ref, qn_ref, kn_ref, cs_ref, sn_ref, sink_ref,
                      o_ref, ko_ref, vo_ref, *, n_kv):
    ns = q_ref.shape[0]
    outs, k_out, v_out = _window_step(_rows(q_ref, ns), _rows(k_ref, ns), _rows(v_ref, ns),
                                      [kb_ref[0, 0, kv] for kv in range(n_kv)],
                                      [vb_ref[0, 0, kv] for kv in range(n_kv)], qn_ref[...], kn_ref[...],
                                      cs_ref[...], sn_ref[...], sink_ref, n_kv)
    o = jnp.concatenate(outs, axis=1)
    for t in range(ns):
        o_ref[t, 0] = o[t:t + 1]
    for kv in range(n_kv):
        ko_ref[0, kv] = k_out[kv]
        vo_ref[0, kv] = v_out[kv]


def _row_minor(cache):
    return cache.transpose(0, 1, 3, 4, 2)


def _step_attn(y, k_cache, v_cache, li, q_norm, k_norm, sinks, cs, sn):
    ns, db, n = y.shape
    w, n_kv = k_cache.shape[2], k_cache.shape[3]
    kvd = n_kv * HEAD_DIM
    hd = N_HEADS * HEAD_DIM
    kb, vb = hd // kvd, hd // kvd + 1
    y4 = y.reshape(ns, db, 1, n)
    full = lambda a: pl.BlockSpec(a.shape, lambda b: (0,) * a.ndim)
    buf_in = pl.BlockSpec((1, 1, n_kv, HEAD_DIM, w), lambda b: (li, b, 0, 0, 0))
    buf_out = pl.BlockSpec((1, n_kv, HEAD_DIM, w), lambda b: (b, 0, 0, 0))
    o, ko, vo = pl.pallas_call(
        functools.partial(_step_attn_kernel, n_kv=n_kv), grid=(db,),
        in_specs=[pl.BlockSpec((ns, 1, 1, hd), lambda b: (0, b, 0, 0)),
                  pl.BlockSpec((ns, 1, 1, kvd), lambda b: (0, b, 0, kb)),
                  pl.BlockSpec((ns, 1, 1, kvd), lambda b: (0, b, 0, vb)),
                  buf_in, buf_in,
                  full(q_norm), full(k_norm), full(cs), full(sn), full(sinks)],
        out_specs=[pl.BlockSpec((ns, 1, 1, hd), lambda b: (0, b, 0, 0)), buf_out, buf_out],
        out_shape=[jax.ShapeDtypeStruct((ns, db, 1, hd), F32),
                   jax.ShapeDtypeStruct((db, n_kv, HEAD_DIM, w), F32),
                   jax.ShapeDtypeStruct((db, n_kv, HEAD_DIM, w), F32)],
        compiler_params=_cparams(("arbitrary",)),
    )(y4, y4, y4, _row_minor(k_cache), _row_minor(v_cache), q_norm, k_norm, cs, sn, sinks)
    back = lambda a: a.transpose(0, 3, 1, 2)
    return o.reshape(ns, db, hd), back(ko), back(vo)


LANES = 128


def _lanes(x, n):
    if n <= LANES:
        return x[:, :n]
    return jnp.concatenate([x] * (n // LANES), axis=1)


def _flash_step(q_refs, kv_fn, mask_fn, m_ref, acc_ref, chunk):
    n = m_ref.shape[0] // chunk

    def scores(c):
        rows = pl.ds(c * chunk, chunk)
        k_parts = kv_fn(c)[0]
        s = _dot_nt(q_refs[0][rows, :], k_parts[0])
        for qr, kp in zip(q_refs[1:], k_parts[1:]):
            s = s + _dot_nt(qr[rows, :], kp)
        mk = None if mask_fn is None else mask_fn(c)
        return s if mk is None else jnp.where(mk, s, NEG)

    s = scores(0)
    for c in range(n):
        s_next = scores(c + 1) if c + 1 < n else None
        rows = pl.ds(c * chunk, chunk)
        vals = kv_fn(c)[1]
        m_old = m_ref[rows, :]
        m_new = jnp.maximum(m_old, jnp.max(s, axis=-1, keepdims=True))
        alpha = jnp.exp(m_old - m_new)
        p = jnp.exp(s - _lanes(m_new, s.shape[1])).astype(BF16)
        acc_ref[rows, :] = _lanes(alpha, vals.shape[1]) * acc_ref[rows, :] + _dot(p, vals)
        m_ref[rows, :] = m_new
        s = s_next


def _seg_ones(n, head=HEAD_DIM):
    r = lax.broadcasted_iota(jnp.int32, (n, n), 0) // head
    c = lax.broadcasted_iota(jnp.int32, (n, n), 1) // head
    return jnp.where(r == c, 1.0, 0.0).astype(BF16)


def _heads_rms(x, g, seg, head=HEAD_DIM):
    sq = x * x
    hi = sq.astype(BF16)
    lo = (sq - hi.astype(F32)).astype(BF16)
    n = seg.shape[0]
    parts = []
    for j in range(0, x.shape[1], n):
        k = min(n, x.shape[1] - j)
        b = seg[:k, :k]
        parts.append(_dot(hi[:, j:j + k], b) + _dot(lo[:, j:j + k], b))
    ss = parts[0] if len(parts) == 1 else jnp.concatenate(parts, axis=1)
    return x * lax.rsqrt(ss * (1.0 / head) + EPS) * g


def _heads_rope(x, c2, s2, head=HEAD_DIM, rot=ROPE_DIM):
    w = x.shape[1]
    half = rot // 2
    lane = lax.broadcasted_iota(jnp.int32, x.shape, 1) % head
    sw = jnp.where(lane < half, pltpu.roll(x, w - half, 1), pltpu.roll(x, half, 1))
    return x * _lanes(c2, w) + sw * _lanes(s2, w)


def _stack_heads(x, q_s):
    nq = x.shape[0]
    for h in range(x.shape[1] // HEAD_DIM):
        q_s[h * nq:(h + 1) * nq, :] = _head(x, h).astype(q_s.dtype)


def _unstack_heads(o, nq):
    return jnp.concatenate([o[h * nq:(h + 1) * nq] for h in range(o.shape[0] // nq)], axis=1)


def _normalized(acc):
    return (acc * pltpu.roll(1.0 / acc, HEAD_DIM, 1))[:, :HEAD_DIM]


def _flash_init(m_ref, acc_ref):
    m_ref[...] = jnp.full(m_ref.shape, NEG, F32)
    acc_ref[...] = jnp.zeros(acc_ref.shape, F32)


def _with_ones(v):
    pad = LANES - v.shape[1] % LANES
    return jnp.concatenate([v, jnp.ones((v.shape[0], pad), v.dtype)], axis=1)


def _mla_queries(q, nn, nr, cs, sn, wuk_ref, ql_s, qr_s):
    nq = q.shape[0]
    scale = (MLA_NOPE + MLA_ROPE) ** -0.5
    off = N_HEADS * MLA_NOPE
    qn = _heads_rms(q[:, :off], nn, _seg_ones(2 * LANES, MLA_NOPE), MLA_NOPE).astype(BF16)
    qr = _heads_rms(q[:, off:], nr, _seg_ones(2 * LANES, MLA_ROPE), MLA_ROPE)
    qr = (_heads_rope(qr, cs, sn, MLA_ROPE, MLA_ROPE) * scale).astype(BF16)
    for h in range(N_HEADS):
        rows = slice(h * nq, (h + 1) * nq)
        ql_s[rows, :] = (_dot(qn[:, h * MLA_NOPE:(h + 1) * MLA_NOPE], wuk_ref[h]) * scale).astype(BF16)
        qr_s[rows, :] = qr[:, h * MLA_ROPE:(h + 1) * MLA_ROPE]


def _mla_output(acc, den, wuv_ref, nq):
    o_lat = (acc / den).astype(BF16)
    return jnp.concatenate([_dot(o_lat[h * nq:(h + 1) * nq], wuv_ref[h]) for h in range(N_HEADS)], axis=1)


def _mla_attn_kernel(q_ref, c_ref, kr_ref, cs_ref, sn_ref, nn_ref, nr_ref, wuk_ref, wuv_ref, o_ref,
                     ql_s, qr_s, m_s, acc_s, cb_s, kb_s, *, tq, tk, chunk):
    i = pl.program_id(1)

    @pl.when(i == 0)
    def _():
        cb_s[...] = _with_ones(c_ref[0].astype(BF16))
        kb_s[...] = kr_ref[0].astype(BF16)

    _mla_queries(q_ref[0], nn_ref[...], nr_ref[...], cs_ref[...], sn_ref[...], wuk_ref, ql_s, qr_s)
    _flash_init(m_s, acc_s)
    t_pos = i * tq + lax.broadcasted_iota(jnp.int32, (tq, tk), 0)
    k_off = lax.broadcasted_iota(jnp.int32, (tq, tk), 1)

    def step(j, masked):
        start = pl.multiple_of(j * tk, tk)
        cbx = cb_s[pl.ds(start, tk), :]
        kb = kb_s[pl.ds(start, tk), :]
        mask_fn = None
        if masked:
            mk = jnp.concatenate([(start + k_off) <= t_pos] * (chunk // tq), axis=0)
            mask_fn = lambda c: mk
        kv = ([cbx[:, :MLA_KV_LORA], kb], cbx)
        _flash_step([ql_s, qr_s], lambda c: kv, mask_fn, m_s, acc_s, chunk)

    def full_body(j, carry):
        step(j, False)
        return carry

    def diag_body(j, carry):
        step(j, True)
        return carry

    n_full = (i * tq) // tk
    lax.fori_loop(0, n_full, full_body, 0)
    lax.fori_loop(n_full, (i * tq + tq - 1) // tk + 1, diag_body, 0)
    o_ref[0] = _mla_output(acc_s[:, :MLA_KV_LORA], _lanes(acc_s[:, MLA_KV_LORA:], MLA_KV_LORA), wuv_ref,
                           tq).astype(o_ref.dtype)


def _mla_attn(q, c, kr, cs, sn, nn, nr, wuk, wuv, tq, tk, chunk):
    b, s, nqc = q.shape
    hd = N_HEADS * HEAD_DIM
    full = lambda a: pl.BlockSpec(a.shape, lambda bb, i: (0,) * a.ndim)
    tab = pl.BlockSpec((tq, LANES), lambda bb, i: (i, 0))
    return pl.pallas_call(
        functools.partial(_mla_attn_kernel, tq=tq, tk=tk, chunk=chunk), grid=(b, s // tq),
        in_specs=[pl.BlockSpec((1, tq, nqc), lambda bb, i: (bb, i, 0)),
                  pl.BlockSpec((1, s, MLA_KV_LORA), lambda bb, i: (bb, 0, 0)),
                  pl.BlockSpec((1, s, MLA_ROPE), lambda bb, i: (bb, 0, 0)),
                  tab, tab, full(nn), full(nr), full(wuk), full(wuv)],
        out_specs=pl.BlockSpec((1, tq, hd), lambda bb, i: (bb, i, 0)),
        out_shape=jax.ShapeDtypeStruct((b, s, hd), BF16),
        scratch_shapes=[pltpu.VMEM((N_HEADS * tq, MLA_KV_LORA), BF16),
                        pltpu.VMEM((N_HEADS * tq, MLA_ROPE), BF16),
                        pltpu.VMEM((N_HEADS * tq, LANES), F32),
                        pltpu.VMEM((N_HEADS * tq, MLA_KV_LORA + LANES), F32),
                        pltpu.VMEM((s, MLA_KV_LORA + LANES), BF16),
                        pltpu.VMEM((s, MLA_ROPE), BF16)],
        compiler_params=_cparams(("arbitrary", "arbitrary")),
    )(q, c, kr, cs, sn, nn, nr, wuk, wuv)


def _online_step(s, pv_fn, m_ref, acc_ref):
    m_old = m_ref[...]
    m_new = jnp.maximum(m_old, jnp.max(s, axis=-1, keepdims=True))
    alpha = jnp.exp(m_old - m_new)
    p = jnp.exp(s - _lanes(m_new, s.shape[1])).astype(BF16)
    acc_ref[...] = _lanes(alpha, acc_ref.shape[1]) * acc_ref[...] + pv_fn(p)
    m_ref[...] = m_new


class _PageStream:
    def __init__(self, pt_ref, seq, layer, pools, bufs, sems, slots):
        self.pt_ref, self.seq, self.layer, self.slots = pt_ref, seq, layer, slots
        self.pools, self.bufs, self.sems = pools, bufs, sems

    def _copies(self, group, half):
        out = []
        for k in range(self.slots):
            page = self.pt_ref[self.seq, group * self.slots + k]
            for i, (pool, buf) in enumerate(zip(self.pools, self.bufs)):
                out.append(pltpu.make_async_copy(pool.at[self.layer, page], buf.at[half, k], self.sems.at[half, i]))
        return out

    def start(self, group, half):
        for c in self._copies(group, half):
            c.start()

    def wait(self, group, half):
        for c in self._copies(group, half):
            c.wait()

    def start_first(self):
        @pl.when(self.seq == 0)
        def _():
            self.start(0, 0)

    def prefetch_next_sequence(self):
        @pl.when(self.seq + 1 < pl.num_programs(0))
        def _():
            _PageStream(self.pt_ref, self.seq + 1, self.layer, self.pools, self.bufs, self.sems,
                        self.slots).start(0, 0)

    def sweep(self, n_groups, compute):
        def body(g, carry):
            half = g % 2

            @pl.when(g + 1 < n_groups)
            def _():
                self.start(g + 1, 1 - half)

            self.wait(g, half)
            compute(g, half)
            return carry

        lax.fori_loop(0, n_groups, body, 0)


def _mla_decode_kernel(pt_ref, lat_hbm, kr_hbm, q_ref, cn_ref, kn_ref, cs_ref, sn_ref, nn_ref, nr_ref, wuk_ref,
                       wuv_ref, o_ref, ql_s, qr_s, m_s, acc_s, lat_buf, kr_buf, sems, *, ns, slots, npg, layer):
    stream = _PageStream(pt_ref, pl.program_id(0), layer, [lat_hbm, kr_hbm], [lat_buf, kr_buf], sems, slots)
    stream.start_first()
    _mla_queries(_rows(q_ref, ns), nn_ref[...], nr_ref[...], cs_ref[...], sn_ref[...], wuk_ref, ql_s, qr_s)
    _flash_init(m_s, acc_s)

    def cached(g, half):
        cbx = _with_ones(lat_buf[half].reshape(slots * PAGE_SIZE, MLA_KV_LORA).astype(BF16))
        krt = jnp.concatenate([kr_buf[half, k] for k in range(slots)], axis=1).astype(BF16)
        _online_step(_dot_nt(ql_s[...], cbx[:, :MLA_KV_LORA]) + _dot(qr_s[...], krt), lambda p: _dot(p, cbx),
                     m_s, acc_s)

    stream.sweep(npg, cached)
    stream.prefetch_next_sequence()

    pad = PAGE_SIZE - ns
    cb = jnp.concatenate([_rows(cn_ref, ns), jnp.zeros((pad, MLA_KV_LORA), F32)], axis=0).astype(BF16)
    kb = jnp.concatenate([_rows(kn_ref, ns), jnp.zeros((pad, MLA_ROPE), F32)], axis=0).astype(BF16)
    s = _dot_nt(ql_s[...], cb) + _dot_nt(qr_s[...], kb)
    row = lax.broadcasted_iota(jnp.int32, s.shape, 0) % ns
    col = lax.broadcasted_iota(jnp.int32, s.shape, 1)
    cbx_new = _with_ones(cb)
    _online_step(jnp.where(col <= row, s, NEG), lambda p: _dot(p, cbx_new), m_s, acc_s)
    o = _mla_output(acc_s[:, :MLA_KV_LORA], _lanes(acc_s[:, MLA_KV_LORA:], MLA_KV_LORA), wuv_ref, ns)
    for t in range(ns):
        o_ref[t, 0] = o[t:t + 1]


def _mla_decode(q, c, kr, lat_pool, kr_pool, li, page_table, cs, sn, nn, nr, wuk, wuv):
    ns, db, nqc = q.shape
    hd = N_HEADS * HEAD_DIM
    slots = PAGE_SLOTS
    npg = page_table.shape[1] // slots
    full = lambda a: pl.BlockSpec(a.shape, lambda b, pt: (0,) * a.ndim)
    hbm = pl.BlockSpec(memory_space=pl.ANY)
    step = lambda width: pl.BlockSpec((ns, 1, 1, width), lambda b, pt: (0, b, 0, 0))
    grid_spec = pltpu.PrefetchScalarGridSpec(
        num_scalar_prefetch=1, grid=(db,),
        in_specs=[hbm, hbm, step(nqc), step(MLA_KV_LORA), step(MLA_ROPE),
                  full(cs), full(sn), full(nn), full(nr), full(wuk), full(wuv)],
        out_specs=step(hd),
        scratch_shapes=[pltpu.VMEM((N_HEADS * ns, MLA_KV_LORA), BF16),
                        pltpu.VMEM((N_HEADS * ns, MLA_ROPE), BF16),
                        pltpu.VMEM((N_HEADS * ns, LANES), F32),
                        pltpu.VMEM((N_HEADS * ns, MLA_KV_LORA + LANES), F32),
                        pltpu.VMEM((2, slots, PAGE_SIZE, MLA_KV_LORA), F32),
                        pltpu.VMEM((2, slots, MLA_ROPE, PAGE_SIZE), F32),
                        pltpu.SemaphoreType.DMA((2, 2))])
    kr_pool_t = kr_pool.transpose(0, 1, 3, 2)
    o = pl.pallas_call(
        functools.partial(_mla_decode_kernel, ns=ns, slots=slots, npg=npg, layer=li), grid_spec=grid_spec,
        out_shape=jax.ShapeDtypeStruct((ns, db, 1, hd), F32),
        compiler_params=_cparams(("arbitrary",)),
    )(page_table, lat_pool, kr_pool_t,
      q.reshape(ns, db, 1, nqc), c.reshape(ns, db, 1, MLA_KV_LORA), kr.reshape(ns, db, 1, MLA_ROPE),
      cs, sn, nn, nr, wuk, wuv)
    return o.reshape(ns, db, hd)


NSA_KVD = NSA_KV_HEADS * HEAD_DIM
NSA_GRP = N_HEADS // NSA_KV_HEADS
NSA_PAIR = NSA_SEL_BLOCK // NSA_CMP_BLOCK


def _nsa_prep_kernel(kc_ref, vc_ref, ks_ref, kw_ref, wk_ref, wv_ref, kcn_ref, ksn_ref, kwn_ref, cs_ref, sn_ref,
                     kso_ref, kwo_ref, kcmp_ref, vcmp_ref):
    cs, sn = cs_ref[...], sn_ref[...]
    kc, vc = kc_ref[0], vc_ref[0]
    seg = _seg_ones(NSA_KVD)
    kso_ref[0] = _heads_rope(_heads_rms(ks_ref[0], ksn_ref[...], seg), cs, sn)
    kwo_ref[0] = _heads_rope(_heads_rms(kw_ref[0], kwn_ref[...], seg), cs, sn)
    kcmp, vcmp = [], []
    for kv in range(NSA_KV_HEADS):
        kcmp.append(_rms(_dot(wk_ref[kv], _head(kc, kv).astype(BF16)), kcn_ref[...]))
        vcmp.append(_dot(wv_ref[kv], _head(vc, kv).astype(BF16)))
    kcmp_ref[0] = jnp.concatenate(kcmp, axis=1)
    vcmp_ref[0] = jnp.concatenate(vcmp, axis=1)


def _cmp_matrix(w, nc):
    eye = jnp.eye(nc, dtype=F32)
    return (eye[None, :, :, None] * w.T[:, None, None, :]).reshape(w.shape[1], nc, nc * NSA_CMP_BLOCK).astype(BF16)


def _nsa_prep(y, cmp_wk, cmp_wv, k_norm, cs, sn):
    b, s, _ = y.shape
    nc = s // NSA_CMP_BLOCK
    base = N_HEADS * HEAD_DIM // NSA_KVD
    col = lambda j: pl.BlockSpec((1, s, NSA_KVD), lambda bb: (bb, 0, base + j))
    full = lambda a: pl.BlockSpec(a.shape, lambda bb: (0,) * a.ndim)
    wk, wv = _cmp_matrix(cmp_wk, nc), _cmp_matrix(cmp_wv, nc)
    kn = [k_norm[0:1]] + [jnp.tile(k_norm[j:j + 1], (1, NSA_KV_HEADS)) for j in (1, 2)]
    seq = pl.BlockSpec((1, s, NSA_KVD), lambda bb: (bb, 0, 0))
    blk = pl.BlockSpec((1, nc, NSA_KVD), lambda bb: (bb, 0, 0))
    return pl.pallas_call(
        _nsa_prep_kernel, grid=(b,),
        in_specs=[col(0), col(1), col(2), col(4), full(wk), full(wv), full(kn[0]), full(kn[1]), full(kn[2]),
                  full(cs), full(sn)],
        out_specs=[seq, seq, blk, blk],
        out_shape=[jax.ShapeDtypeStruct((b, s, NSA_KVD), F32), jax.ShapeDtypeStruct((b, s, NSA_KVD), F32),
                   jax.ShapeDtypeStruct((b, nc, NSA_KVD), F32), jax.ShapeDtypeStruct((b, nc, NSA_KVD), F32)],
        compiler_params=_cparams(("arbitrary",)),
    )(y, y, y, y, wk, wv, kn[0], kn[1], kn[2], cs, sn)


def _cmp_attend(q, kcmp, vcmp, mask, ng, row_minor=False):
    nq = q.shape[0] // ng
    mk = jnp.concatenate([mask] * ng, axis=0)
    qk, pv = (_dot, _dot_nt) if row_minor else (_dot_nt, _dot)
    s = jnp.where(mk, qk(q, kcmp.astype(BF16)), NEG)
    m = jnp.max(s, axis=-1, keepdims=True)
    p = jnp.where(mk, jnp.exp(s - m), 0.0)
    p = p / jnp.maximum(jnp.sum(p, axis=-1, keepdims=True), TINY)
    o = pv(p.astype(BF16), vcmp.astype(BF16))
    imp = p[0:nq]
    for g in range(1, ng):
        imp = imp + p[g * nq:(g + 1) * nq]
    return o, imp


def _pair_sum(imp):
    n = imp.shape[1]
    lane = lax.broadcasted_iota(jnp.int32, imp.shape, 1)
    return imp + jnp.where(lane % 2 == 0, pltpu.roll(imp, n - 1, 1), pltpu.roll(imp, 1, 1))


def _select(impx, nblk, seg, n_sel):
    lane = lax.broadcasted_iota(jnp.int32, impx.shape, 1)
    blk = (lane % seg) // NSA_PAIR
    nseg = impx.shape[1] // seg
    cnt = jnp.zeros(impx.shape, jnp.int32)
    for j in range(nblk):
        col = impx[:, j * NSA_PAIR:j * NSA_PAIR + 1]
        for sg in range(1, nseg):
            c = sg * seg + j * NSA_PAIR
            col = jnp.where(lane < sg * seg, col, impx[:, c:c + 1])
        beats = (col > impx) | ((col == impx) & (j < blk))
        cnt = cnt + beats.astype(jnp.int32)
    return (cnt < n_sel).astype(F32)


def _select_rows(impx, nblk, seg, n_sel):
    row = lax.broadcasted_iota(jnp.int32, impx.shape, 0)
    blk = (row % seg) // NSA_PAIR
    nseg, nq = impx.shape[0] // seg, impx.shape[1]
    cnt = jnp.zeros(impx.shape, jnp.int32)
    for j in range(nblk):
        parts = [jnp.broadcast_to(impx[sg * seg + j * NSA_PAIR:sg * seg + j * NSA_PAIR + 1, :], (seg, nq))
                 for sg in range(nseg)]
        col = parts[0] if nseg == 1 else jnp.concatenate(parts, axis=0)
        beats = (col > impx) | ((col == impx) & (j < blk))
        cnt = cnt + beats.astype(jnp.int32)
    return (cnt < n_sel).astype(F32)


def _expand_mask(sel, start, tk):
    nc = sel.shape[1]
    n = lax.broadcasted_iota(jnp.int32, (nc, tk), 0)
    k = lax.broadcasted_iota(jnp.int32, (nc, tk), 1)
    e = jnp.where((start + k) // NSA_CMP_BLOCK == n, 1.0, 0.0).astype(BF16)
    return _dot(sel.astype(BF16), e) > 0.5


def _nsa_attn_kernel(q_ref, gl_ref, ks_ref, vs_ref, kwp_ref, kwc_ref, vwp_ref, vwc_ref, kcmp_ref, vcmp_ref,
                     qg_ref, c2_ref, s2_ref, o_ref, qn_s, qr_s, ks_s, vs_s, m_s, acc_s, mw_s, accw_s,
                     *, tk, n_sel, chunk):
    i = pl.program_id(1)
    tq = q_ref.shape[1]
    nc = kcmp_ref.shape[1]
    rows_kv = NSA_GRP * tq
    rep = chunk // tq
    kv_of = lambda c: c * chunk // rows_kv
    kv_range = range(NSA_KV_HEADS)

    @pl.when(i == 0)
    def _():
        ks_s[...] = ks_ref[0].astype(BF16)
        vs = vs_ref[0].astype(BF16)
        vs_s[...] = jnp.concatenate([_with_ones(_head(vs, kv)) for kv in kv_range], axis=1)

    qn = _heads_rms(q_ref[0], qg_ref[...], _seg_ones(2 * LANES)) * HEAD_DIM ** -0.5
    _stack_heads(qn, qn_s)
    _stack_heads(_heads_rope(qn, c2_ref[...], s2_ref[...]), qr_s)

    t_c = i * tq + lax.broadcasted_iota(jnp.int32, (tq, nc), 0)
    n_c = lax.broadcasted_iota(jnp.int32, (tq, nc), 1)
    cmask = (n_c + 1) * NSA_CMP_BLOCK - 1 <= t_c
    o_cmp, imps = [], []
    for kv in kv_range:
        oc, imp = _cmp_attend(qn_s[kv * rows_kv:(kv + 1) * rows_kv, :], _head(kcmp_ref[0], kv),
                              _head(vcmp_ref[0], kv), cmask, NSA_GRP)
        o_cmp.append(oc)
        imps.append(imp)
    imp = jnp.concatenate(imps, axis=1).T
    row = lax.broadcasted_iota(jnp.int32, imp.shape, 0)
    imp = imp + jnp.where(row % 2 == 0, pltpu.roll(imp, imp.shape[0] - 1, 0), pltpu.roll(imp, 1, 0))
    t_i = i * tq + lax.broadcasted_iota(jnp.int32, imp.shape, 1)
    blk = (row % nc) // NSA_PAIR
    impx = jnp.where(blk == t_i // NSA_SEL_BLOCK, jnp.inf, jnp.where(blk * NSA_SEL_BLOCK <= t_i, imp, -jnp.inf))
    sel = _select_rows(impx, nc // NSA_PAIR, nc, n_sel).T

    _flash_init(m_s, acc_s)
    selk = [sel[:, kv * nc:(kv + 1) * nc] for kv in kv_range]
    t_k = i * tq + lax.broadcasted_iota(jnp.int32, (tq, tk), 0)
    k_off = lax.broadcasted_iota(jnp.int32, (tq, tk), 1)

    def body(j, carry):
        start = pl.multiple_of(j * tk, tk)
        kb = ks_s[pl.ds(start, tk), :]
        vbx = vs_s[pl.ds(start, tk), :]
        causal = (start + k_off) <= t_k
        mks = [jnp.concatenate([_expand_mask(selk[kv], start, tk) & causal] * rep, axis=0) for kv in kv_range]
        kvs = [([_head(kb, kv)], vbx[:, kv * LANES:(kv + 1) * LANES]) for kv in kv_range]
        _flash_step([qr_s], lambda c: kvs[kv_of(c)], lambda c: mks[kv_of(c)], m_s, acc_s, chunk)
        return carry

    lax.fori_loop(0, (i * tq + tq - 1) // tk + 1, body, 0)

    _flash_init(mw_s, accw_s)
    kk = jnp.concatenate([kwp_ref[0], kwc_ref[0]], axis=0).astype(BF16)
    vv = jnp.concatenate([vwp_ref[0], vwc_ref[0]], axis=0).astype(BF16)
    kvw = [([_head(kk, kv)], _with_ones(_head(vv, kv))) for kv in kv_range]
    wmk = jnp.concatenate([_band_mask(tq, tq, jnp.where(i > 0, 0, tq))] * rep, axis=0)
    _flash_step([qr_s], lambda c: kvw[kv_of(c)], lambda c: wmk, mw_s, accw_s, chunk)

    o_sel = _normalized(acc_s[...])
    o_win = _normalized(accw_s[...])
    gates = jax.nn.sigmoid(gl_ref[0])
    outs = []
    for h in range(N_HEADS):
        kv, g = divmod(h, NSA_GRP)
        outs.append(gates[:, 3 * h:3 * h + 1] * o_cmp[kv][g * tq:(g + 1) * tq]
                    + gates[:, 3 * h + 1:3 * h + 2] * o_sel[h * tq:(h + 1) * tq]
                    + gates[:, 3 * h + 2:3 * h + 3] * o_win[h * tq:(h + 1) * tq])
    o_ref[0] = jnp.concatenate(outs, axis=1).astype(o_ref.dtype)


def _nsa_attn(y, ksn, kwn, kcmp, vcmp, q_norm, cs, sn, tk):
    b, s, _ = y.shape
    tq = WINDOW
    hd = N_HEADS * HEAD_DIM
    nc = kcmp.shape[1]
    base = hd // NSA_KVD
    n_sel = min(NSA_TOPK, s // NSA_SEL_BLOCK)
    prev = lambda i: jnp.maximum(i - 1, 0)
    full = lambda a: pl.BlockSpec(a.shape, lambda bb, i: (0,) * a.ndim)
    tab = pl.BlockSpec((tq, LANES), lambda bb, i: (i, 0))
    q_norm = jnp.tile(q_norm, (1, N_HEADS))
    return pl.pallas_call(
        functools.partial(_nsa_attn_kernel, tk=tk, n_sel=n_sel, chunk=2 * tq), grid=(b, s // tq),
        in_specs=[pl.BlockSpec((1, tq, hd), lambda bb, i: (bb, i, 0)),
                  pl.BlockSpec((1, tq, NSA_KVD), lambda bb, i: (bb, i, base + 6)),
                  pl.BlockSpec((1, s, NSA_KVD), lambda bb, i: (bb, 0, 0)),
                  pl.BlockSpec((1, s, NSA_KVD), lambda bb, i: (bb, 0, base + 3)),
                  pl.BlockSpec((1, tq, NSA_KVD), lambda bb, i: (bb, prev(i), 0)),
                  pl.BlockSpec((1, tq, NSA_KVD), lambda bb, i: (bb, i, 0)),
                  pl.BlockSpec((1, tq, NSA_KVD), lambda bb, i: (bb, prev(i), base + 5)),
                  pl.BlockSpec((1, tq, NSA_KVD), lambda bb, i: (bb, i, base + 5)),
                  pl.BlockSpec((1, nc, NSA_KVD), lambda bb, i: (bb, 0, 0)),
                  pl.BlockSpec((1, nc, NSA_KVD), lambda bb, i: (bb, 0, 0)),
                  full(q_norm), tab, tab],
        out_specs=pl.BlockSpec((1, tq, hd), lambda bb, i: (bb, i, 0)),
        out_shape=jax.ShapeDtypeStruct((b, s, hd), BF16),
        scratch_shapes=[pltpu.VMEM((N_HEADS * tq, HEAD_DIM), BF16),
                        pltpu.VMEM((N_HEADS * tq, HEAD_DIM), BF16),
                        pltpu.VMEM((s, NSA_KVD), BF16),
                        pltpu.VMEM((s, NSA_KV_HEADS * LANES), BF16),
                        pltpu.VMEM((N_HEADS * tq, LANES), F32),
                        pltpu.VMEM((N_HEADS * tq, 2 * HEAD_DIM), F32),
                        pltpu.VMEM((N_HEADS * tq, LANES), F32),
                        pltpu.VMEM((N_HEADS * tq, 2 * HEAD_DIM), F32)],
        compiler_params=_cparams(("arbitrary", "arbitrary")),
    )(y, y, ksn, y, kwn, kwn, y, y, kcmp, vcmp, q_norm, cs, sn)


def _nsa_decode_kernel(pt_ref, kc_hbm, vc_hbm, ks_hbm, vs_hbm,
                       q_ref, gl_ref, ks_ref, vs_ref, kw_ref, vw_ref, kwb_ref, vwb_ref, wrow_ref, eloc_ref,
                       qn_ref, kcn_ref, ksn_ref, kwn_ref, cs_ref, sn_ref,
                       o_ref, kso_ref, kwo_ref, vwo_ref,
                       cmp_s, sel_s, ocmp_s, q_s, m_s, acc_s, k_buf, v_buf, sems,
                       *, ns, slots, npg, past, n_sel, layer):
    seq = pl.program_id(0)
    scale = HEAD_DIM ** -0.5
    nc = cmp_s.shape[1]
    rows_per_step = slots * PAGE_SIZE
    cper = rows_per_step // NSA_CMP_BLOCK
    kv_range = range(NSA_KV_HEADS)
    grp = lambda xs, kv: xs[kv * NSA_GRP:(kv + 1) * NSA_GRP]
    cmp_stream = _PageStream(pt_ref, seq, layer, [kc_hbm, vc_hbm], [k_buf, v_buf], sems, slots)
    sel_stream = _PageStream(pt_ref, seq, layer, [ks_hbm, vs_hbm], [k_buf, v_buf], sems, slots)
    cmp_stream.start_first()
    pages = lambda buf, half, kv: jnp.concatenate([buf[half, k, kv] for k in range(slots)], axis=1)

    def block_sums(g, half):
        lhs = jnp.concatenate([pages(buf, half, kv) * wrow_ref[2 * j + kv:2 * j + kv + 1, :]
                               for j, buf in enumerate((k_buf, v_buf)) for kv in kv_range], axis=0)
        res = _dot(lhs.astype(BF16), eloc_ref[...])
        for k in range(npg):
            @pl.when(g == k)
            def _():
                cmp_s[:, k * cper:(k + 1) * cper] = res

    cmp_stream.sweep(npg, block_sums)
    sel_stream.start(0, 0)

    q = _rows(q_ref, ns)
    qn = [_rms(_head(q, h), qn_ref[...]) * scale for h in range(N_HEADS)]
    qr = [_rope(x, cs_ref[...], sn_ref[...], ROPE_DIM) for x in qn]
    t_c = past + lax.broadcasted_iota(jnp.int32, (ns, nc), 0)
    n_c = lax.broadcasted_iota(jnp.int32, (ns, nc), 1)
    cmask = (n_c + 1) * NSA_CMP_BLOCK - 1 <= t_c
    imps = []
    for kv in kv_range:
        kct = cmp_s[kv * HEAD_DIM:(kv + 1) * HEAD_DIM, :]
        kct = kct * lax.rsqrt(jnp.mean(kct * kct, axis=0, keepdims=True) + EPS) * kcn_ref[...]
        vct = cmp_s[(NSA_KV_HEADS + kv) * HEAD_DIM:(NSA_KV_HEADS + kv + 1) * HEAD_DIM, :]
        oc, imp = _cmp_attend(jnp.concatenate(grp(qn, kv), axis=0).astype(BF16), kct, vct, cmask, NSA_GRP,
                              row_minor=True)
        ocmp_s[kv] = oc
        imps.append(_pair_sum(imp))
        q_s[kv] = jnp.concatenate(grp(qr, kv), axis=0).astype(BF16)
    for kv in kv_range:
        sel = _select(imps[kv], nc // NSA_PAIR, nc, n_sel - 1)
        for k in range(npg):
            sel_s[kv, k] = sel[:, k * cper:(k + 1) * cper]
    _flash_init(m_s, acc_s)

    def selected(g, half):
        for kv in kv_range:
            kst = pages(k_buf, half, kv).astype(BF16)
            vst = pages(v_buf, half, kv)
            vtx = jnp.concatenate([vst, jnp.ones(vst.shape, F32)], axis=0).astype(BF16)
            mk = jnp.concatenate([_dot_nt(sel_s[kv, g].astype(BF16), eloc_ref[...]) > 0.5] * NSA_GRP, axis=0)
            _online_step(jnp.where(mk, _dot(q_s[kv], kst), NEG), lambda p, v=vtx: _dot_nt(p, v),
                         m_s.at[kv], acc_s.at[kv])

    sel_stream.sweep(npg, selected)
    cmp_stream.prefetch_next_sequence()

    pad = lambda a: jnp.concatenate([a, jnp.zeros((PAGE_SIZE - ns, a.shape[1]), a.dtype)], axis=0)
    vs_new = _rows(vs_ref, ns)
    ks_raw = _rows(ks_ref, ns)
    ksn = jnp.concatenate([_rope(_rms(_head(ks_raw, kv), ksn_ref[...]), cs_ref[...], sn_ref[...], ROPE_DIM)
                           for kv in kv_range], axis=1)
    kso_ref[0] = ksn
    kb = pad(ksn).astype(BF16)
    row = lax.broadcasted_iota(jnp.int32, (NSA_GRP * ns, PAGE_SIZE), 0) % ns
    col = lax.broadcasted_iota(jnp.int32, (NSA_GRP * ns, PAGE_SIZE), 1)
    for kv in kv_range:
        vbx = _with_ones(pad(_head(vs_new, kv)).astype(BF16))
        _online_step(jnp.where(col <= row, _dot_nt(q_s[kv], _head(kb, kv)), NEG), lambda p, v=vbx: _dot(p, v),
                     m_s.at[kv], acc_s.at[kv])
    outs_w, kw_out, vw_out = _window_step(
        q, _rows(kw_ref, ns), _rows(vw_ref, ns), [kwb_ref[0, 0, kv] for kv in kv_range],
        [vwb_ref[0, 0, kv] for kv in kv_range], qn_ref[...], kwn_ref[...], cs_ref[...], sn_ref[...], None,
        NSA_KV_HEADS)
    for kv in kv_range:
        kwo_ref[0, kv] = kw_out[kv]
        vwo_ref[0, kv] = vw_out[kv]
    gates = jax.nn.sigmoid(_rows(gl_ref, ns))
    outs = []
    for kv in kv_range:
        o_sel = _normalized(acc_s[kv])
        o_cmp = ocmp_s[kv]
        for g in range(NSA_GRP):
            h = kv * NSA_GRP + g
            outs.append(gates[:, 3 * h:3 * h + 1] * o_cmp[g * ns:(g + 1) * ns]
                        + gates[:, 3 * h + 1:3 * h + 2] * o_sel[g * ns:(g + 1) * ns]
                        + gates[:, 3 * h + 2:3 * h + 3] * outs_w[h])
    o = jnp.concatenate(outs, axis=1)
    for t in range(ns):
        o_ref[t, 0] = o[t:t + 1]


def _nsa_decode(y, pools, kw_cache, vw_cache, li, page_table, cmp_wk, cmp_wv, q_norm, k_norm, cs, sn):
    ns, db, n = y.shape
    hd = N_HEADS * HEAD_DIM
    slots = PAGE_SLOTS
    n_pages = page_table.shape[1]
    npg = n_pages // slots
    past = n_pages * PAGE_SIZE
    nc = past // NSA_CMP_BLOCK
    base = hd // NSA_KVD
    w = kw_cache.shape[2]
    n_blocks = -(-(past + ns) // NSA_SEL_BLOCK)
    n_sel = min(NSA_TOPK, n_blocks)
    y4 = y.reshape(ns, db, 1, n)
    full = lambda a: pl.BlockSpec(a.shape, lambda b, pt: (0,) * a.ndim)
    hbm = pl.BlockSpec(memory_space=pl.ANY)
    step = lambda width, cb: pl.BlockSpec((ns, 1, 1, width), lambda b, pt: (0, b, 0, cb))
    rows = slots * PAGE_SIZE
    cper = rows // NSA_CMP_BLOCK
    wrow = jnp.tile(jnp.concatenate([cmp_wk.T, cmp_wv.T], axis=0), (1, cper))
    eloc = (jnp.arange(rows)[:, None] // NSA_CMP_BLOCK == jnp.arange(cper)[None, :]).astype(BF16)
    kn = [k_norm[j:j + 1] for j in range(3)]
    kcn_col = k_norm[0].reshape(HEAD_DIM, 1)
    seq_out = pl.BlockSpec((1, ns, NSA_KVD), lambda b, pt: (b, 0, 0))
    buf_out = pl.BlockSpec((1, NSA_KV_HEADS, HEAD_DIM, w), lambda b, pt: (b, 0, 0, 0))
    buf_in = pl.BlockSpec((1, 1, NSA_KV_HEADS, HEAD_DIM, w), lambda b, pt: (li, b, 0, 0, 0))
    page_buf = pltpu.VMEM((2, slots, NSA_KV_HEADS, HEAD_DIM, PAGE_SIZE), F32)
    grid_spec = pltpu.PrefetchScalarGridSpec(
        num_scalar_prefetch=1, grid=(db,),
        in_specs=[hbm, hbm, hbm, hbm,
                  step(hd, 0), step(NSA_KVD, base + 6), step(NSA_KVD, base + 2), step(NSA_KVD, base + 3),
                  step(NSA_KVD, base + 4), step(NSA_KVD, base + 5), buf_in, buf_in,
                  full(wrow), full(eloc),
                  full(q_norm), full(kcn_col), full(kn[1]), full(kn[2]), full(cs), full(sn)],
        out_specs=[step(hd, 0), seq_out, buf_out, buf_out],
        scratch_shapes=[pltpu.VMEM((2 * NSA_KV_HEADS * HEAD_DIM, nc), F32),
                        pltpu.VMEM((NSA_KV_HEADS, npg, ns, cper), F32),
                        pltpu.VMEM((NSA_KV_HEADS, NSA_GRP * ns, HEAD_DIM), F32),
                        pltpu.VMEM((NSA_KV_HEADS, NSA_GRP * ns, HEAD_DIM), BF16),
                        pltpu.VMEM((NSA_KV_HEADS, NSA_GRP * ns, LANES), F32),
                        pltpu.VMEM((NSA_KV_HEADS, NSA_GRP * ns, 2 * HEAD_DIM), F32),
                        page_buf, page_buf, pltpu.SemaphoreType.DMA((2, 2))])
    kc_pool, vc_pool, ks_pool, vs_pool = [_row_minor(p) for p in pools]
    o, kso, kwo, vwo = pl.pallas_call(
        functools.partial(_nsa_decode_kernel, ns=ns, slots=slots, npg=npg, past=past, n_sel=n_sel, layer=li),
        grid_spec=grid_spec,
        out_shape=[jax.ShapeDtypeStruct((ns, db, 1, hd), F32),
                   jax.ShapeDtypeStruct((db, ns, NSA_KVD), F32),
                   jax.ShapeDtypeStruct((db, NSA_KV_HEADS, HEAD_DIM, w), F32),
                   jax.ShapeDtypeStruct((db, NSA_KV_HEADS, HEAD_DIM, w), F32)],
        compiler_params=_cparams(("arbitrary",)),
    )(page_table, kc_pool, vc_pool, ks_pool, vs_pool,
      y4, y4, y4, y4, y4, y4, _row_minor(kw_cache), _row_minor(vw_cache), wrow, eloc,
      q_norm, kcn_col, kn[1], kn[2], cs, sn)
    back = lambda a: a.transpose(0, 3, 1, 2)
    return o.reshape(ns, db, hd), kso, back(kwo), back(vwo)


def kernel(x_prompt, x_sample, cache_swa_k, cache_swa_v, cache_mla_latent, cache_mla_krope, cache_nsa_kcmp, cache_nsa_vcmp, cache_nsa_ksel, cache_nsa_vsel, cache_nsa_kwin, cache_nsa_vwin, state_conv_ffn, page_table, c_prompt, c_sample, ada_w, ada_b, norm_mix, norm_ffn, ffn_w_up, ffn_conv, ffn_w_down, a_w_in, a_q_norm, a_k_norm, a_sinks, a_w_out, b_w_in, b_qa_norm, b_w_qb, b_q_norm_nope, b_q_norm_rope, b_kv_norm, b_krope_norm, b_w_uk, b_w_uv, b_w_out, c_w_in, c_q_norm, c_k_norm, c_cmp_wk, c_cmp_wv, c_w_out):
    nb, seq, d = x_prompt.shape
    db, ds, _ = x_sample.shape
    depth = ada_w.shape[0]
    ff = ffn_w_down.shape[1]
    past = page_table.shape[1] * PAGE_SIZE
    hd = N_HEADS * HEAD_DIM

    mod = _modulate(jnp.concatenate([c_prompt, c_sample], axis=0), ada_w, ada_b)
    tr_p = math.gcd(seq, 512)
    gp = _Group(True, mod[:, :, :nb].reshape(depth, 6, nb, 1, d), tr_p)
    gs = _Group(False, mod[:, :, nb:], db)
    norm_mix3 = norm_mix.reshape(depth, 1, d)
    norm_ffn3 = norm_ffn.reshape(depth, 1, d)

    pos_p = jnp.arange(seq, dtype=jnp.int32)
    pos_s = past + jnp.arange(ds, dtype=jnp.int32)
    cw_p, sw_p = _rope_tables(pos_p, ROPE_DIM, ROPE_THETA, HEAD_DIM)
    cw2_p, sw2_p = jnp.tile(cw_p, (1, LANES // HEAD_DIM)), jnp.tile(sw_p, (1, LANES // HEAD_DIM))
    cw_s, sw_s = _rope_tables(pos_s, ROPE_DIM, ROPE_THETA, HEAD_DIM)
    cm_p, sm_p = _rope_tables(pos_p, MLA_ROPE, MLA_THETA, MLA_ROPE)
    cm_s, sm_s = _rope_tables(pos_s, MLA_ROPE, MLA_THETA, MLA_ROPE)

    xp = x_prompt
    xs = x_sample.transpose(1, 0, 2)
    row2 = lambda v: v.reshape(1, -1)
    tm = lambda a: a.transpose(1, 0, 2)
    out = {k: [] for k in ("swa_k_p", "swa_v_p", "swa_k_s", "swa_v_s", "mla_c_p", "mla_r_p", "mla_c_s", "mla_r_s",
                           "conv_p", "conv_s")}
    nsa_p = [[] for _ in range(6)]
    nsa_s = [[] for _ in range(6)]
    ia = ib = ic = 0
    for layer in range(depth):
        kind = layer % N_MIXERS
        if kind == 0:
            w_in = a_w_in[ia].astype(BF16)
            qn, kn, sinks = row2(a_q_norm[ia]), row2(a_k_norm[ia]), row2(a_sinks[ia])
            yp = _proj(xp, gp, norm_mix3, layer, w_in)
            op, kp, vp = _band_attn(yp, qn, kn, sinks, cw2_p, sw2_p)
            ys = _proj(xs, gs, norm_mix3, layer, w_in)
            os_, ks_, vs_ = _step_attn(ys, cache_swa_k, cache_swa_v, ia, qn, kn, sinks, cw_s, sw_s)
            kv4 = lambda a: a.reshape(a.shape[0], a.shape[1], SWA_KV_HEADS, HEAD_DIM)
            out["swa_k_p"].append(kv4(kp)); out["swa_v_p"].append(kv4(vp))
            out["swa_k_s"].append(ks_); out["swa_v_s"].append(vs_)
            w_out = a_w_out[ia].astype(BF16)
            ia += 1
        elif kind == 1:
            w_in = b_w_in[ib].astype(BF16)
            wqb = b_w_qb[ib].reshape(MLA_Q_LORA, N_HEADS, MLA_NOPE + MLA_ROPE)
            wqb = jnp.concatenate([wqb[:, :, :MLA_NOPE].reshape(MLA_Q_LORA, -1),
                                   wqb[:, :, MLA_NOPE:].reshape(MLA_Q_LORA, -1)], axis=1).astype(BF16)
            wuk = b_w_uk[ib].transpose(1, 2, 0).astype(BF16)
            wuv = b_w_uv[ib].transpose(1, 0, 2).astype(BF16)
            norms = (row2(b_qa_norm[ib]), wqb, row2(b_kv_norm[ib]), row2(b_krope_norm[ib]))
            nn = jnp.tile(row2(b_q_norm_nope[ib]), (1, N_HEADS))
            nr = jnp.tile(row2(b_q_norm_rope[ib]), (1, N_HEADS))
            wide = lambda t: jnp.tile(t, (1, LANES // MLA_ROPE))
            qp, cp, rp = _mla_proj(xp, gp, norm_mix3, layer, w_in, *norms, cm_p[None], sm_p[None])
            op = _mla_attn(qp, cp, rp, wide(cm_p), wide(sm_p), nn, nr, wuk, wuv, math.gcd(seq, 128),
                           math.gcd(seq, 256), 256)
            qs, cs_, rs_ = _mla_proj(xs, gs, norm_mix3, layer, w_in, *norms, cm_s[:, None], sm_s[:, None])
            os_ = _mla_decode(qs, cs_, rs_, cache_mla_latent, cache_mla_krope, ib, page_table,
                              wide(cm_s), wide(sm_s), nn, nr, wuk, wuv)
            out["mla_c_p"].append(cp); out["mla_r_p"].append(rp)
            out["mla_c_s"].append(tm(cs_)); out["mla_r_s"].append(tm(rs_))
            w_out = b_w_out[ib].astype(BF16)
            ib += 1
        else:
            n_in = c_w_in.shape[2]
            n_pad = -(-n_in // 128) * 128
            w_in = jnp.pad(c_w_in[ic], ((0, 0), (0, n_pad - n_in))).astype(BF16)
            qn, kn3 = row2(c_q_norm[ic]), c_k_norm[ic]
            kv4 = lambda a: a.reshape(a.shape[0], a.shape[1], NSA_KV_HEADS, HEAD_DIM)
            col = lambda a, j: a[:, :, hd + j * NSA_KVD:hd + (j + 1) * NSA_KVD]
            yp = _proj(xp, gp, norm_mix3, layer, w_in)
            ksn, kwn, kcmp, vcmp = _nsa_prep(yp, c_cmp_wk[ic], c_cmp_wv[ic], kn3, cw2_p, sw2_p)
            op = _nsa_attn(yp, ksn, kwn, kcmp, vcmp, qn, cw2_p, sw2_p, math.gcd(seq, 256))
            for j, a in enumerate((col(yp, 0), col(yp, 1), ksn, col(yp, 3), kwn[:, -WINDOW:],
                                   col(yp, 5)[:, -WINDOW:])):
                nsa_p[j].append(kv4(a))
            ys = _proj(xs, gs, norm_mix3, layer, w_in)
            pools = (cache_nsa_kcmp, cache_nsa_vcmp, cache_nsa_ksel, cache_nsa_vsel)
            os_, kso, kwo, vwo = _nsa_decode(ys, pools, cache_nsa_kwin, cache_nsa_vwin, ic, page_table,
                                             c_cmp_wk[ic], c_cmp_wv[ic], qn, kn3, cw_s, sw_s)
            for j, a in enumerate((tm(col(ys, 0)), tm(col(ys, 1)), kso, tm(col(ys, 3)))):
                nsa_s[j].append(kv4(a))
            nsa_s[4].append(kwo)
            nsa_s[5].append(vwo)
            w_out = c_w_out[ic].astype(BF16)
            ic += 1
        xp = _outproj(op, xp, gp, layer, w_out)
        xs = _outproj(os_, xs, gs, layer, w_out)
        wup, wd = ffn_w_up[layer].astype(BF16), ffn_w_down[layer].astype(BF16)
        tf = math.gcd(ff, 256)
        xp, bp = _ffn_prompt(xp, gp, norm_ffn3, layer, wup, ffn_conv[layer], wd, math.gcd(seq, 1024), tf)
        xs, bs = _ffn_sample(xs, gs, norm_ffn3, layer, wup, ffn_conv[layer], wd, state_conv_ffn[layer], tf)
        out["conv_p"].append(bp); out["conv_s"].append(bs)

    st = lambda xs_: jnp.stack(xs_)
    return (xp, xs.transpose(1, 0, 2),
            st(out["swa_k_p"]), st(out["swa_v_p"]), st(out["mla_c_p"]), st(out["mla_r_p"]),
            *[st(a) for a in nsa_p], st(out["conv_p"]),
            st(out["swa_k_s"]), st(out["swa_v_s"]), st(out["mla_c_s"]), st(out["mla_r_s"]),
            *[st(a) for a in nsa_s], st(out["conv_s"]))
```

```python
import functools
import math

import jax
import jax.numpy as jnp
from jax import lax
from jax.experimental import pallas as pl
from jax.experimental.pallas import tpu as pltpu

F32 = jnp.float32
BF16 = jnp.bfloat16

N_HEADS = 16
HEAD_DIM = 64
ROPE_DIM = 16
ROPE_THETA = 500000.0
EPS = 1e-6
PAGE_SIZE = 128
SWA_KV_HEADS = 4
WINDOW = 128
MLA_Q_LORA = 384
MLA_KV_LORA = 256
MLA_NOPE = 64
MLA_ROPE = 32
MLA_THETA = 10000.0
NSA_KV_HEADS = 2
NSA_CMP_BLOCK = 32
NSA_SEL_BLOCK = 64
NSA_TOPK = 16
N_MIXERS = 3
NEG = -1e30
TINY = float(jnp.finfo(jnp.float32).tiny)
VMEM_LIMIT = 56 * 1024 * 1024
PAGE_SLOTS = 8


def _cparams(sem):
    return pltpu.CompilerParams(dimension_semantics=sem, vmem_limit_bytes=VMEM_LIMIT)


def _dot(a, b):
    return jnp.dot(a, b, preferred_element_type=F32)


def _dot_nt(a, b):
    return lax.dot_general(a, b, (((1,), (1,)), ((), ())), preferred_element_type=F32)


def _rms(x, g):
    return x * lax.rsqrt(jnp.mean(x * x, axis=-1, keepdims=True) + EPS) * g


def _rope(x, c, s, rot):
    half = rot // 2
    parts = [x[:, half:rot], x[:, :half]]
    if x.shape[1] > rot:
        parts.append(x[:, rot:])
    return x * c + jnp.concatenate(parts, axis=1) * s


def _silu(x):
    return x * jax.nn.sigmoid(x)


def _norm_mod(x, g, sc, sh):
    return _rms(x, g) * (1.0 + sc) + sh


def _rope_tables(pos, rot, theta, width):
    half = rot // 2
    inv = jnp.power(jnp.float32(theta), -jnp.arange(half, dtype=F32) / half)
    ang = pos.astype(F32)[:, None] * inv[None, :]
    cos, sin = jnp.cos(ang), jnp.sin(ang)
    n = pos.shape[0]
    c = jnp.concatenate([cos, cos, jnp.ones((n, width - rot), F32)], axis=1)
    s = jnp.concatenate([-sin, sin, jnp.zeros((n, width - rot), F32)], axis=1)
    return c, s


class _Group:
    def __init__(self, per_group, mod, tr):
        self.pm = per_group
        self.mods = [mod[layer] for layer in range(mod.shape[0])]
        self.tr = tr

    def mod(self, layer):
        return self.mods[layer]

    def mod_spec(self, j, nargs=2):
        d = self.mods[0].shape[-1]
        if self.pm:
            shape, f = (1, 1, 1, d), (lambda g: (j, g, 0, 0))
        else:
            shape, f = (1, self.mods[0].shape[1], d), (lambda g: (j, 0, 0))
        if nargs == 2:
            return pl.BlockSpec(shape, lambda g, r: f(g))
        return pl.BlockSpec(shape, lambda g, r, k: f(g))


def _mod_val(ref, pm):
    return ref[0, 0] if pm else ref[0]


def _mod_kernel(c_ref, w_ref, b_ref, o_ref):
    a = _silu(c_ref[...])
    o_ref[0, 0] = _dot(a.astype(BF16), w_ref[0].astype(BF16)) + b_ref[0]


def _modulate(c_all, ada_w, ada_b):
    nl, d, d6 = ada_w.shape
    n = c_all.shape[0]
    nj = d6 // d
    return pl.pallas_call(
        _mod_kernel, grid=(nl, nj),
        in_specs=[pl.BlockSpec((n, d), lambda l, j: (0, 0)),
                  pl.BlockSpec((1, d, d), lambda l, j: (l, 0, j)),
                  pl.BlockSpec((1, 1, d), lambda l, j: (l, 0, j))],
        out_specs=pl.BlockSpec((1, 1, n, d), lambda l, j: (l, j, 0, 0)),
        out_shape=jax.ShapeDtypeStruct((nl, nj, n, d), F32),
        compiler_params=_cparams(("arbitrary", "arbitrary")),
    )(c_all, ada_w, ada_b.reshape(nl, 1, d6))


def _proj_kernel(x_ref, g_ref, sc_ref, sh_ref, w_ref, o_ref, *, pm):
    h = _norm_mod(x_ref[0], g_ref[0], _mod_val(sc_ref, pm), _mod_val(sh_ref, pm))
    o_ref[0] = _dot(h.astype(BF16), w_ref[...])


def _proj(x, grp, gains, layer, w_bf):
    ng, nr, d = x.shape
    n = w_bf.shape[1]
    tr = grp.tr
    return pl.pallas_call(
        functools.partial(_proj_kernel, pm=grp.pm), grid=(ng, nr // tr),
        in_specs=[pl.BlockSpec((1, tr, d), lambda g, r: (g, r, 0)),
                  pl.BlockSpec((1, 1, d), lambda g, r: (0, 0, 0)),
                  grp.mod_spec(1), grp.mod_spec(0),
                  pl.BlockSpec((d, n), lambda g, r: (0, 0))],
        out_specs=pl.BlockSpec((1, tr, n), lambda g, r: (g, r, 0)),
        out_shape=jax.ShapeDtypeStruct((ng, nr, n), F32),
        compiler_params=_cparams(("arbitrary", "arbitrary")),
    )(x, gains[layer:layer + 1], grp.mod(layer), grp.mod(layer), w_bf)


def _mla_proj_kernel(x_ref, g_ref, sc_ref, sh_ref, w_ref, qan_ref, wqb_ref, kvn_ref, krn_ref,
                     c_ref, s_ref, q_ref, lat_ref, kr_ref, *, pm):
    h = _norm_mod(x_ref[0], g_ref[0], _mod_val(sc_ref, pm), _mod_val(sh_ref, pm))
    y = _dot(h.astype(BF16), w_ref[...])
    a, b = MLA_Q_LORA, MLA_Q_LORA + MLA_KV_LORA
    qa = _rms(y[:, :a], qan_ref[...])
    q_ref[0] = _dot(qa.astype(BF16), wqb_ref[...])
    lat_ref[0] = _rms(y[:, a:b], kvn_ref[...])
    kr = _rms(y[:, b:], krn_ref[...])
    kr_ref[0] = _rope(kr, c_ref[0], s_ref[0], MLA_ROPE)


def _mla_proj(x, grp, gains, layer, w_bf, qa_norm, wqb_bf, kv_norm, krope_norm, cs, sn):
    ng, nr, d = x.shape
    tr = grp.tr
    nq = wqb_bf.shape[1]
    if grp.pm:
        tab = pl.BlockSpec((1, tr, MLA_ROPE), lambda g, r: (0, r, 0))
    else:
        tab = pl.BlockSpec((1, 1, MLA_ROPE), lambda g, r: (g, 0, 0))
    full = lambda a: pl.BlockSpec(a.shape, lambda g, r: (0,) * a.ndim)
    return pl.pallas_call(
        functools.partial(_mla_proj_kernel, pm=grp.pm), grid=(ng, nr // tr),
        in_specs=[pl.BlockSpec((1, tr, d), lambda g, r: (g, r, 0)),
                  pl.BlockSpec((1, 1, d), lambda g, r: (0, 0, 0)),
                  grp.mod_spec(1), grp.mod_spec(0),
                  full(w_bf), full(qa_norm), full(wqb_bf), full(kv_norm), full(krope_norm), tab, tab],
        out_specs=[pl.BlockSpec((1, tr, nq), lambda g, r: (g, r, 0)),
                   pl.BlockSpec((1, tr, MLA_KV_LORA), lambda g, r: (g, r, 0)),
                   pl.BlockSpec((1, tr, MLA_ROPE), lambda g, r: (g, r, 0))],
        out_shape=[jax.ShapeDtypeStruct((ng, nr, nq), F32),
                   jax.ShapeDtypeStruct((ng, nr, MLA_KV_LORA), F32),
                   jax.ShapeDtypeStruct((ng, nr, MLA_ROPE), F32)],
        compiler_params=_cparams(("arbitrary", "arbitrary")),
    )(x, gains[layer:layer + 1], grp.mod(layer), grp.mod(layer), w_bf, qa_norm, wqb_bf, kv_norm, krope_norm, cs, sn)


def _outproj_kernel(o_ref, x_ref, gt_ref, w_ref, y_ref, *, pm):
    y_ref[0] = x_ref[0] + _mod_val(gt_ref, pm) * _dot(o_ref[0].astype(BF16), w_ref[...])


def _outproj(o, x, grp, layer, w_bf):
    ng, nr, d = x.shape
    k = o.shape[-1]
    tr = grp.tr
    return pl.pallas_call(
        functools.partial(_outproj_kernel, pm=grp.pm), grid=(ng, nr // tr),
        in_specs=[pl.BlockSpec((1, tr, k), lambda g, r: (g, r, 0)),
                  pl.BlockSpec((1, tr, d), lambda g, r: (g, r, 0)),
                  grp.mod_spec(2),
                  pl.BlockSpec((k, d), lambda g, r: (0, 0))],
        out_specs=pl.BlockSpec((1, tr, d), lambda g, r: (g, r, 0)),
        out_shape=jax.ShapeDtypeStruct((ng, nr, d), F32),
        compiler_params=_cparams(("arbitrary", "arbitrary")),
    )(o, x, grp.mod(layer), w_bf)


HALO = 16


def _ffn_tail(mg, mv, wd_ref, x_ref, gt, y_ref):
    f = pl.program_id(2)
    d = _dot((_silu(mg) * mv).astype(BF16), wd_ref[...])

    @pl.when(f == 0)
    def _():
        y_ref[0] = d

    @pl.when(f > 0)
    def _():
        y_ref[0] += d

    @pl.when(f == pl.num_programs(2) - 1)
    def _():
        y_ref[0] = x_ref[0] + gt * y_ref[0]


def _ffn_prompt_kernel(x_ref, xh_ref, g_ref, sc_ref, sh_ref, gt_ref, wup_ref, cw_ref, wd_ref,
                       y_ref, b_ref, h_s, act_s, u_s, *, tr, tf):
    r = pl.program_id(1)
    ff = wd_ref.shape[0]
    g, sc, sh = g_ref[0], sc_ref[0, 0], sh_ref[0, 0]
    h_s[HALO:, :] = _norm_mod(x_ref[0], g, sc, sh).astype(BF16)
    hh = _norm_mod(xh_ref[0], g, sc, sh)
    h_s[:HALO, :] = jnp.where(r > 0, hh, 0.0).astype(BF16)
    nf = ff // tf
    cols = lambda half, f: pl.ds(pl.multiple_of(half * ff + f * tf, tf), tf)

    def up(f, slot):
        for half in range(2):
            u_s[slot, half] = _dot(h_s[...], wup_ref[:, cols(half, f)])

    def mixed(f, slot, half):
        u = u_s[slot, half]
        cw = cw_ref[:, cols(half, f)]
        b_ref[0, 0, :, cols(half, f)] = u[HALO + tr - 2:]
        a = pltpu.roll(u, 2, 0)[HALO:]
        b = pltpu.roll(u, 1, 0)[HALO:]
        return cw[0:1] * a + cw[1:2] * b + cw[2:3] * u[HALO:]

    def activate(f, slot):
        act_s[:, cols(0, f)] = (_silu(mixed(f, slot, 0)) * mixed(f, slot, 1)).astype(BF16)

    def body(it, carry):
        f = 2 * it
        up(f + 1, 1)
        activate(f, 0)
        up(f + 2, 0)
        activate(f + 1, 1)
        return carry

    assert nf % 2 == 1
    up(0, 0)
    lax.fori_loop(0, nf // 2, body, 0)
    activate(nf - 1, 0)
    y_ref[0] = x_ref[0] + gt_ref[0, 0] * _dot(act_s[...], wd_ref[...])


def _ffn_prompt(x, grp, gains, layer, wup_bf, conv_w, wd_bf, tr, tf):
    ng, nr, d = x.shape
    ff = wd_bf.shape[0]
    ms = lambda j: grp.mod_spec(j)
    once = lambda a: pl.BlockSpec(a.shape, lambda g, r: (0,) * a.ndim, pipeline_mode=pl.Buffered(1))
    y, buf = pl.pallas_call(
        functools.partial(_ffn_prompt_kernel, tr=tr, tf=tf), grid=(ng, nr // tr),
        in_specs=[pl.BlockSpec((1, tr, d), lambda g, r: (g, r, 0)),
                  pl.BlockSpec((1, HALO, d), lambda g, r: (g, jnp.maximum(r * (tr // HALO) - 1, 0), 0)),
                  pl.BlockSpec((1, 1, d), lambda g, r: (0, 0, 0)),
                  ms(4), ms(3), ms(5), once(wup_bf), once(conv_w), once(wd_bf)],
        out_specs=[pl.BlockSpec((1, tr, d), lambda g, r: (g, r, 0)),
                   pl.BlockSpec((1, 1, 2, 2 * ff), lambda g, r: (g, r, 0, 0))],
        out_shape=[jax.ShapeDtypeStruct((ng, nr, d), F32),
                   jax.ShapeDtypeStruct((ng, nr // tr, 2, 2 * ff), F32)],
        scratch_shapes=[pltpu.VMEM((tr + HALO, d), BF16), pltpu.VMEM((tr, ff), BF16),
                        pltpu.VMEM((2, 2, tr + HALO, tf), F32)],
        compiler_params=_cparams(("arbitrary", "arbitrary")),
    )(x, x, gains[layer:layer + 1], grp.mod(layer), grp.mod(layer), grp.mod(layer), wup_bf, conv_w, wd_bf)
    return y, buf[:, -1]


def _ffn_sample_kernel(x_ref, g_ref, sc_ref, sh_ref, gt_ref, sg_ref, sv_ref, wg_ref, wv_ref, cg_ref, cv_ref,
                       wd_ref, y_ref, bg_ref, bv_ref, h_s, *, nt, p):
    tile = lambda m: jnp.concatenate([m] * nt, axis=0)

    @pl.when(pl.program_id(2) == 0)
    def _():
        h_s[...] = _norm_mod(x_ref[0], g_ref[0], tile(sc_ref[0]), tile(sh_ref[0])).astype(BF16)

    h = h_s[...]
    n = nt * p

    def conv(st, u, cw):
        e = jnp.concatenate([st, u], axis=0)
        return cw[0:1] * e[0:n] + cw[1:2] * e[p:p + n] + cw[2:3] * e[2 * p:]

    ug = _dot(h, wg_ref[...])
    uv = _dot(h, wv_ref[...])
    bg_ref[...] = jnp.concatenate([sg_ref[...], ug], axis=0)[n:]
    bv_ref[...] = jnp.concatenate([sv_ref[...], uv], axis=0)[n:]
    _ffn_tail(conv(sg_ref[...], ug, cg_ref[...]), conv(sv_ref[...], uv, cv_ref[...]), wd_ref, x_ref,
              tile(gt_ref[0]), y_ref)


def _ffn_sample(x, grp, gains, layer, wup_bf, conv_w, wd_bf, state, tf):
    nt, p, d = x.shape
    ff = wd_bf.shape[0]
    nf = ff // tf
    n = nt * p
    st = state.transpose(1, 0, 2).reshape(2 * p, 2 * ff)
    ms = lambda j: grp.mod_spec(j, nargs=3)
    y, bg, bv = pl.pallas_call(
        functools.partial(_ffn_sample_kernel, nt=nt, p=p), grid=(1, 1, nf),
        in_specs=[pl.BlockSpec((1, n, d), lambda g, r, f: (0, 0, 0)),
                  pl.BlockSpec((1, 1, d), lambda g, r, f: (0, 0, 0)),
                  ms(4), ms(3), ms(5),
                  pl.BlockSpec((2 * p, tf), lambda g, r, f: (0, f)),
                  pl.BlockSpec((2 * p, tf), lambda g, r, f: (0, nf + f)),
                  pl.BlockSpec((d, tf), lambda g, r, f: (0, f)),
                  pl.BlockSpec((d, tf), lambda g, r, f: (0, nf + f)),
                  pl.BlockSpec((3, tf), lambda g, r, f: (0, f)),
                  pl.BlockSpec((3, tf), lambda g, r, f: (0, nf + f)),
                  pl.BlockSpec((tf, d), lambda g, r, f: (f, 0))],
        out_specs=[pl.BlockSpec((1, n, d), lambda g, r, f: (0, 0, 0)),
                   pl.BlockSpec((2 * p, tf), lambda g, r, f: (0, f)),
                   pl.BlockSpec((2 * p, tf), lambda g, r, f: (0, f))],
        out_shape=[jax.ShapeDtypeStruct((1, n, d), F32),
                   jax.ShapeDtypeStruct((2 * p, ff), F32),
                   jax.ShapeDtypeStruct((2 * p, ff), F32)],
        scratch_shapes=[pltpu.VMEM((n, d), BF16)],
        compiler_params=_cparams(("arbitrary", "arbitrary", "arbitrary")),
    )(x.reshape(1, n, d), gains[layer:layer + 1], grp.mod(layer), grp.mod(layer), grp.mod(layer), st, st, wup_bf, wup_bf, conv_w, conv_w, wd_bf)
    new_state = jnp.concatenate([bg, bv], axis=-1).reshape(2, p, 2 * ff).transpose(1, 0, 2)
    return y.reshape(nt, p, d), new_state


def _sink_col(sink_ref, heads, nq):
    return jnp.concatenate([jnp.broadcast_to(sink_ref[:, h:h + 1], (nq, 1)) for h in heads], axis=0)


def _band_mask(nq, w, prev_off):
    qi = lax.broadcasted_iota(jnp.int32, (nq, w + nq), 0)
    kj = lax.broadcasted_iota(jnp.int32, (nq, w + nq), 1)
    return ((kj < w) & (kj >= qi + prev_off)) | ((kj >= w) & ((kj - w) <= qi))


def _head(x, h):
    return x[:, h * HEAD_DIM:(h + 1) * HEAD_DIM]


def _band_attn_kernel(q_ref, kp_ref, kc_ref, vp_ref, vc_ref, qg_ref, kg_ref, cq_ref, sq_ref, cp_ref, sp_ref,
                      sink_ref, o_ref, ko_ref, vo_ref, q_s, m_s, acc_s, *, n_kv, chunk):
    i = pl.program_id(1)
    w = q_ref.shape[1]
    rows_kv = (N_HEADS // n_kv) * w
    seg = _seg_ones(2 * LANES)
    cq, sq = cq_ref[...], sq_ref[...]
    _stack_heads(_heads_rope(_heads_rms(q_ref[0], qg_ref[...], seg), cq, sq) * HEAD_DIM ** -0.5, q_s)
    kc = _heads_rope(_heads_rms(kc_ref[0], kg_ref[...], seg), cq, sq)
    kp = _heads_rope(_heads_rms(kp_ref[0], kg_ref[...], seg), cp_ref[...], sp_ref[...])
    kk = jnp.concatenate([kp, kc], axis=0).astype(BF16)
    vv = jnp.concatenate([vp_ref[0], vc_ref[0]], axis=0).astype(BF16)
    kvs = [([_head(kk, kv)], _with_ones(_head(vv, kv))) for kv in range(n_kv)]
    for h in range(N_HEADS):
        m_s[h * w:(h + 1) * w, :] = jnp.broadcast_to(sink_ref[:, h:h + 1], (w, LANES))
    acc_s[...] = jnp.concatenate([jnp.zeros((N_HEADS * w, HEAD_DIM), F32), jnp.ones((N_HEADS * w, HEAD_DIM), F32)],
                                 axis=1)
    mask = _band_mask(w, w, jnp.where(i > 0, 0, w))
    mk = jnp.concatenate([mask] * (chunk // w), axis=0)
    _flash_step([q_s], lambda c: kvs[c * chunk // rows_kv], lambda c: mk, m_s, acc_s, chunk)
    o_ref[0] = _unstack_heads(_normalized(acc_s[...]), w).astype(o_ref.dtype)
    ko_ref[0] = kc
    vo_ref[0] = vc_ref[0]


def _band_attn(y, q_norm, k_norm, sinks, cs, sn):
    b, s, _ = y.shape
    w = WINDOW
    kvd = SWA_KV_HEADS * HEAD_DIM
    hd = N_HEADS * HEAD_DIM
    kb, vb = hd // kvd, hd // kvd + 1
    prev = lambda i: jnp.maximum(i - 1, 0)
    full = lambda a: pl.BlockSpec(a.shape, lambda bb, i: (0,) * a.ndim)
    tab_c = pl.BlockSpec((w, LANES), lambda bb, i: (i, 0))
    tab_p = pl.BlockSpec((w, LANES), lambda bb, i: (prev(i), 0))
    q_norm, k_norm = jnp.tile(q_norm, (1, N_HEADS)), jnp.tile(k_norm, (1, SWA_KV_HEADS))
    return pl.pallas_call(
        functools.partial(_band_attn_kernel, n_kv=SWA_KV_HEADS, chunk=4 * w), grid=(b, s // w),
        scratch_shapes=[pltpu.VMEM((N_HEADS * w, HEAD_DIM), BF16),
                        pltpu.VMEM((N_HEADS * w, LANES), F32),
                        pltpu.VMEM((N_HEADS * w, 2 * HEAD_DIM), F32)],
        in_specs=[pl.BlockSpec((1, w, hd), lambda bb, i: (bb, i, 0)),
                  pl.BlockSpec((1, w, kvd), lambda bb, i: (bb, prev(i), kb)),
                  pl.BlockSpec((1, w, kvd), lambda bb, i: (bb, i, kb)),
                  pl.BlockSpec((1, w, kvd), lambda bb, i: (bb, prev(i), vb)),
                  pl.BlockSpec((1, w, kvd), lambda bb, i: (bb, i, vb)),
                  full(q_norm), full(k_norm), tab_c, tab_c, tab_p, tab_p, full(sinks)],
        out_specs=[pl.BlockSpec((1, w, hd), lambda bb, i: (bb, i, 0)),
                   pl.BlockSpec((1, w, kvd), lambda bb, i: (bb, 0, 0)),
                   pl.BlockSpec((1, w, kvd), lambda bb, i: (bb, 0, 0))],
        out_shape=[jax.ShapeDtypeStruct((b, s, hd), BF16),
                   jax.ShapeDtypeStruct((b, w, kvd), F32),
                   jax.ShapeDtypeStruct((b, w, kvd), F32)],
        compiler_params=_cparams(("arbitrary", "arbitrary")),
    )(y, y, y, y, y, q_norm, k_norm, cs, sn, cs, sn, sinks)


def _rows(ref, n):
    return jnp.concatenate([ref[t, 0] for t in range(n)], axis=0)


def _window_step(q, k_new, v_new, kbt, vbt, qn, kn, cs, sn, sink_ref, n_kv):
    ns, w = q.shape[0], kbt[0].shape[1]
    grp = N_HEADS // n_kv
    scale = HEAD_DIM ** -0.5
    t_q = lax.broadcasted_iota(jnp.int32, (grp * ns, w), 0) % ns
    col = lax.broadcasted_iota(jnp.int32, (grp * ns, w), 1)
    see_buf = col >= t_q
    see_new = col <= t_q
    pad = lambda a: jnp.concatenate([a, jnp.zeros((w - ns, a.shape[1]), a.dtype)], axis=0)
    outs, knew = [], []
    for kv in range(n_kv):
        kn_h = _rope(_rms(_head(k_new, kv), kn), cs, sn, ROPE_DIM)
        knew.append(kn_h)
        heads = range(kv * grp, (kv + 1) * grp)
        qg = jnp.concatenate([_rope(_rms(_head(q, h), qn), cs, sn, ROPE_DIM) * scale for h in heads],
                             axis=0).astype(BF16)
        s1 = jnp.where(see_buf, _dot(qg, kbt[kv].astype(BF16)), NEG)
        s2 = jnp.where(see_new, _dot_nt(qg, pad(kn_h).astype(BF16)), NEG)
        m = jnp.maximum(jnp.max(s1, axis=-1, keepdims=True), jnp.max(s2, axis=-1, keepdims=True))
        if sink_ref is not None:
            sink = _sink_col(sink_ref, heads, ns)
            m = jnp.maximum(m, sink)
        p1, p2 = jnp.exp(s1 - m), jnp.exp(s2 - m)
        den = jnp.sum(p1, axis=-1, keepdims=True) + jnp.sum(p2, axis=-1, keepdims=True)
        if sink_ref is not None:
            den = den + jnp.exp(sink - m)
        o = (_dot_nt(p1.astype(BF16), vbt[kv].astype(BF16))
             + _dot(p2.astype(BF16), pad(_head(v_new, kv)).astype(BF16))) / den
        outs += [o[g * ns:(g + 1) * ns] for g in range(grp)]

    def shifted(bt, new):
        new_t = jnp.concatenate([jnp.zeros((w - ns, new.shape[1]), F32), new], axis=0).T
        lane = lax.broadcasted_iota(jnp.int32, (HEAD_DIM, w), 1)
        return [jnp.where(lane >= w - ns, new_t[kv * HEAD_DIM:(kv + 1) * HEAD_DIM], pltpu.roll(bt[kv], w - ns, 1))
                for kv in range(n_kv)]

    return outs, shifted(kbt, jnp.concatenate(knew, axis=1)), shifted(vbt, v_new)


def _step_attn_kernel(q_ref, k_ref, v_ref, kb_ref, vb_ref, qn_ref, kn_ref, cs_ref, sn_ref, sink_ref,
                      o_ref, ko_ref, vo_ref, *, n_kv):
    ns = q_ref.shape[0]
    outs, k_out, v_out = _window_step(_rows(q_ref, ns), _rows(k_ref, ns), _rows(v_ref, ns),
                                      [kb_ref[0, 0, kv] for kv in range(n_kv)],
                                      [vb_ref[0, 0, kv] for kv in range(n_kv)], qn_ref[...], kn_ref[...],
                                      cs_ref[...], sn_ref[...], sink_ref, n_kv)
    o = jnp.concatenate(outs, axis=1)
    for t in range(ns):
        o_ref[t, 0] = o[t:t + 1]
    for kv in range(n_kv):
        ko_ref[0, kv] = k_out[kv]
        vo_ref[0, kv] = v_out[kv]


def _row_minor(cache):
    return cache.transpose(0, 1, 3, 4, 2)


def _step_attn(y, k_cache, v_cache, li, q_norm, k_norm, sinks, cs, sn):
    ns, db, n = y.shape
    w, n_kv = k_cache.shape[2], k_cache.shape[3]
    kvd = n_kv * HEAD_DIM
    hd = N_HEADS * HEAD_DIM
    kb, vb = hd // kvd, hd // kvd + 1
    y4 = y.reshape(ns, db, 1, n)
    full = lambda a: pl.BlockSpec(a.shape, lambda b: (0,) * a.ndim)
    buf_in = pl.BlockSpec((1, 1, n_kv, HEAD_DIM, w), lambda b: (li, b, 0, 0, 0))
    buf_out = pl.BlockSpec((1, n_kv, HEAD_DIM, w), lambda b: (b, 0, 0, 0))
    o, ko, vo = pl.pallas_call(
        functools.partial(_step_attn_kernel, n_kv=n_kv), grid=(db,),
        in_specs=[pl.BlockSpec((ns, 1, 1, hd), lambda b: (0, b, 0, 0)),
                  pl.BlockSpec((ns, 1, 1, kvd), lambda b: (0, b, 0, kb)),
                  pl.BlockSpec((ns, 1, 1, kvd), lambda b: (0, b, 0, vb)),
                  buf_in, buf_in,
                  full(q_norm), full(k_norm), full(cs), full(sn), full(sinks)],
        out_specs=[pl.BlockSpec((ns, 1, 1, hd), lambda b: (0, b, 0, 0)), buf_out, buf_out],
        out_shape=[jax.ShapeDtypeStruct((ns, db, 1, hd), F32),
                   jax.ShapeDtypeStruct((db, n_kv, HEAD_DIM, w), F32),
                   jax.ShapeDtypeStruct((db, n_kv, HEAD_DIM, w), F32)],
        compiler_params=_cparams(("arbitrary",)),
    )(y4, y4, y4, _row_minor(k_cache), _row_minor(v_cache), q_norm, k_norm, cs, sn, sinks)
    back = lambda a: a.transpose(0, 3, 1, 2)
    return o.reshape(ns, db, hd), back(ko), back(vo)


LANES = 128


def _lanes(x, n):
    if n <= LANES:
        return x[:, :n]
    return jnp.concatenate([x] * (n // LANES), axis=1)


def _flash_step(q_refs, kv_fn, mask_fn, m_ref, acc_ref, chunk):
    n = m_ref.shape[0] // chunk

    def scores(c):
        rows = pl.ds(c * chunk, chunk)
        k_parts = kv_fn(c)[0]
        s = _dot_nt(q_refs[0][rows, :], k_parts[0])
        for qr, kp in zip(q_refs[1:], k_parts[1:]):
            s = s + _dot_nt(qr[rows, :], kp)
        mk = None if mask_fn is None else mask_fn(c)
        return s if mk is None else jnp.where(mk, s, NEG)

    s = scores(0)
    for c in range(n):
        s_next = scores(c + 1) if c + 1 < n else None
        rows = pl.ds(c * chunk, chunk)
        vals = kv_fn(c)[1]
        m_old = m_ref[rows, :]
        m_new = jnp.maximum(m_old, jnp.max(s, axis=-1, keepdims=True))
        alpha = jnp.exp(m_old - m_new)
        p = jnp.exp(s - _lanes(m_new, s.shape[1])).astype(BF16)
        acc_ref[rows, :] = _lanes(alpha, vals.shape[1]) * acc_ref[rows, :] + _dot(p, vals)
        m_ref[rows, :] = m_new
        s = s_next


def _seg_ones(n, head=HEAD_DIM):
    r = lax.broadcasted_iota(jnp.int32, (n, n), 0) // head
    c = lax.broadcasted_iota(jnp.int32, (n, n), 1) // head
    return jnp.where(r == c, 1.0, 0.0).astype(BF16)


def _heads_rms(x, g, seg, head=HEAD_DIM):
    sq = x * x
    hi = sq.astype(BF16)
    lo = (sq - hi.astype(F32)).astype(BF16)
    n = seg.shape[0]
    parts = []
    for j in range(0, x.shape[1], n):
        k = min(n, x.shape[1] - j)
        b = seg[:k, :k]
        parts.append(_dot(hi[:, j:j + k], b) + _dot(lo[:, j:j + k], b))
    ss = parts[0] if len(parts) == 1 else jnp.concatenate(parts, axis=1)
    return x * lax.rsqrt(ss * (1.0 / head) + EPS) * g


def _heads_rope(x, c2, s2, head=HEAD_DIM, rot=ROPE_DIM):
    w = x.shape[1]
    half = rot // 2
    lane = lax.broadcasted_iota(jnp.int32, x.shape, 1) % head
    sw = jnp.where(lane < half, pltpu.roll(x, w - half, 1), pltpu.roll(x, half, 1))
    return x * _lanes(c2, w) + sw * _lanes(s2, w)


def _stack_heads(x, q_s):
    nq = x.shape[0]
    for h in range(x.shape[1] // HEAD_DIM):
        q_s[h * nq:(h + 1) * nq, :] = _head(x, h).astype(q_s.dtype)


def _unstack_heads(o, nq):
    return jnp.concatenate([o[h * nq:(h + 1) * nq] for h in range(o.shape[0] // nq)], axis=1)


def _normalized(acc):
    return (acc * pltpu.roll(1.0 / acc, HEAD_DIM, 1))[:, :HEAD_DIM]


def _flash_init(m_ref, acc_ref):
    m_ref[...] = jnp.full(m_ref.shape, NEG, F32)
    acc_ref[...] = jnp.zeros(acc_ref.shape, F32)


def _with_ones(v):
    pad = LANES - v.shape[1] % LANES
    return jnp.concatenate([v, jnp.ones((v.shape[0], pad), v.dtype)], axis=1)


def _mla_queries(q, nn, nr, cs, sn, wuk_ref, ql_s, qr_s):
    nq = q.shape[0]
    scale = (MLA_NOPE + MLA_ROPE) ** -0.5
    off = N_HEADS * MLA_NOPE
    qn = _heads_rms(q[:, :off], nn, _seg_ones(2 * LANES, MLA_NOPE), MLA_NOPE).astype(BF16)
    qr = _heads_rms(q[:, off:], nr, _seg_ones(2 * LANES, MLA_ROPE), MLA_ROPE)
    qr = (_heads_rope(qr, cs, sn, MLA_ROPE, MLA_ROPE) * scale).astype(BF16)
    for h in range(N_HEADS):
        rows = slice(h * nq, (h + 1) * nq)
        ql_s[rows, :] = (_dot(qn[:, h * MLA_NOPE:(h + 1) * MLA_NOPE], wuk_ref[h]) * scale).astype(BF16)
        qr_s[rows, :] = qr[:, h * MLA_ROPE:(h + 1) * MLA_ROPE]


def _mla_output(acc, den, wuv_ref, nq):
    o_lat = (acc / den).astype(BF16)
    return jnp.concatenate([_dot(o_lat[h * nq:(h + 1) * nq], wuv_ref[h]) for h in range(N_HEADS)], axis=1)


def _mla_attn_kernel(q_ref, c_ref, kr_ref, cs_ref, sn_ref, nn_ref, nr_ref, wuk_ref, wuv_ref, o_ref,
                     ql_s, qr_s, m_s, acc_s, cb_s, kb_s, *, tq, tk, chunk):
    i = pl.program_id(1)

    @pl.when(i == 0)
    def _():
        cb_s[...] = _with_ones(c_ref[0].astype(BF16))
        kb_s[...] = kr_ref[0].astype(BF16)

    _mla_queries(q_ref[0], nn_ref[...], nr_ref[...], cs_ref[...], sn_ref[...], wuk_ref, ql_s, qr_s)
    _flash_init(m_s, acc_s)
    t_pos = i * tq + lax.broadcasted_iota(jnp.int32, (tq, tk), 0)
    k_off = lax.broadcasted_iota(jnp.int32, (tq, tk), 1)

    def step(j, masked):
        start = pl.multiple_of(j * tk, tk)
        cbx = cb_s[pl.ds(start, tk), :]
        kb = kb_s[pl.ds(start, tk), :]
        mask_fn = None
        if masked:
            mk = jnp.concatenate([(start + k_off) <= t_pos] * (chunk // tq), axis=0)
            mask_fn = lambda c: mk
        kv = ([cbx[:, :MLA_KV_LORA], kb], cbx)
        _flash_step([ql_s, qr_s], lambda c: kv, mask_fn, m_s, acc_s, chunk)

    def full_body(j, carry):
        step(j, False)
        return carry

    def diag_body(j, carry):
        step(j, True)
        return carry

    n_full = (i * tq) // tk
    lax.fori_loop(0, n_full, full_body, 0)
    lax.fori_loop(n_full, (i * tq + tq - 1) // tk + 1, diag_body, 0)
    o_ref[0] = _mla_output(acc_s[:, :MLA_KV_LORA], _lanes(acc_s[:, MLA_KV_LORA:], MLA_KV_LORA), wuv_ref,
                           tq).astype(o_ref.dtype)


def _mla_attn(q, c, kr, cs, sn, nn, nr, wuk, wuv, tq, tk, chunk):
    b, s, nqc = q.shape
    hd = N_HEADS * HEAD_DIM
    full = lambda a: pl.BlockSpec(a.shape, lambda bb, i: (0,) * a.ndim)
    tab = pl.BlockSpec((tq, LANES), lambda bb, i: (i, 0))
    return pl.pallas_call(
        functools.partial(_mla_attn_kernel, tq=tq, tk=tk, chunk=chunk), grid=(b, s // tq),
        in_specs=[pl.BlockSpec((1, tq, nqc), lambda bb, i: (bb, i, 0)),
                  pl.BlockSpec((1, s, MLA_KV_LORA), lambda bb, i: (bb, 0, 0)),
                  pl.BlockSpec((1, s, MLA_ROPE), lambda bb, i: (bb, 0, 0)),
                  tab, tab, full(nn), full(nr), full(wuk), full(wuv)],
        out_specs=pl.BlockSpec((1, tq, hd), lambda bb, i: (bb, i, 0)),
        out_shape=jax.ShapeDtypeStruct((b, s, hd), BF16),
        scratch_shapes=[pltpu.VMEM((N_HEADS * tq, MLA_KV_LORA), BF16),
                        pltpu.VMEM((N_HEADS * tq, MLA_ROPE), BF16),
                        pltpu.VMEM((N_HEADS * tq, LANES), F32),
                        pltpu.VMEM((N_HEADS * tq, MLA_KV_LORA + LANES), F32),
                        pltpu.VMEM((s, MLA_KV_LORA + LANES), BF16),
                        pltpu.VMEM((s, MLA_ROPE), BF16)],
        compiler_params=_cparams(("arbitrary", "arbitrary")),
    )(q, c, kr, cs, sn, nn, nr, wuk, wuv)


def _online_step(s, pv_fn, m_ref, acc_ref):
    m_old = m_ref[...]
    m_new = jnp.maximum(m_old, jnp.max(s, axis=-1, keepdims=True))
    alpha = jnp.exp(m_old - m_new)
    p = jnp.exp(s - _lanes(m_new, s.shape[1])).astype(BF16)
    acc_ref[...] = _lanes(alpha, acc_ref.shape[1]) * acc_ref[...] + pv_fn(p)
    m_ref[...] = m_new


class _PageStream:
    def __init__(self, pt_ref, seq, layer, pools, bufs, sems, slots):
        self.pt_ref, self.seq, self.layer, self.slots = pt_ref, seq, layer, slots
        self.pools, self.bufs, self.sems = pools, bufs, sems

    def _copies(self, group, half):
        out = []
        for k in range(self.slots):
            page = self.pt_ref[self.seq, group * self.slots + k]
            for i, (pool, buf) in enumerate(zip(self.pools, self.bufs)):
                out.append(pltpu.make_async_copy(pool.at[self.layer, page], buf.at[half, k], self.sems.at[half, i]))
        return out

    def start(self, group, half):
        for c in self._copies(group, half):
            c.start()

    def wait(self, group, half):
        for c in self._copies(group, half):
            c.wait()

    def start_first(self):
        @pl.when(self.seq == 0)
        def _():
            self.start(0, 0)

    def prefetch_next_sequence(self):
        @pl.when(self.seq + 1 < pl.num_programs(0))
        def _():
            _PageStream(self.pt_ref, self.seq + 1, self.layer, self.pools, self.bufs, self.sems,
                        self.slots).start(0, 0)

    def sweep(self, n_groups, compute):
        def body(g, carry):
            half = g % 2

            @pl.when(g + 1 < n_groups)
            def _():
                self.start(g + 1, 1 - half)

            self.wait(g, half)
            compute(g, half)
            return carry

        lax.fori_loop(0, n_groups, body, 0)


def _mla_decode_kernel(pt_ref, lat_hbm, kr_hbm, q_ref, cn_ref, kn_ref, cs_ref, sn_ref, nn_ref, nr_ref, wuk_ref,
                       wuv_ref, o_ref, ql_s, qr_s, m_s, acc_s, lat_buf, kr_buf, sems, *, ns, slots, npg, layer):
    stream = _PageStream(pt_ref, pl.program_id(0), layer, [lat_hbm, kr_hbm], [lat_buf, kr_buf], sems, slots)
    stream.start_first()
    _mla_queries(_rows(q_ref, ns), nn_ref[...], nr_ref[...], cs_ref[...], sn_ref[...], wuk_ref, ql_s, qr_s)
    _flash_init(m_s, acc_s)

    def cached(g, half):
        cbx = _with_ones(lat_buf[half].reshape(slots * PAGE_SIZE, MLA_KV_LORA).astype(BF16))
        krt = jnp.concatenate([kr_buf[half, k] for k in range(slots)], axis=1).astype(BF16)
        _online_step(_dot_nt(ql_s[...], cbx[:, :MLA_KV_LORA]) + _dot(qr_s[...], krt), lambda p: _dot(p, cbx),
                     m_s, acc_s)

    stream.sweep(npg, cached)
    stream.prefetch_next_sequence()

    pad = PAGE_SIZE - ns
    cb = jnp.concatenate([_rows(cn_ref, ns), jnp.zeros((pad, MLA_KV_LORA), F32)], axis=0).astype(BF16)
    kb = jnp.concatenate([_rows(kn_ref, ns), jnp.zeros((pad, MLA_ROPE), F32)], axis=0).astype(BF16)
    s = _dot_nt(ql_s[...], cb) + _dot_nt(qr_s[...], kb)
    row = lax.broadcasted_iota(jnp.int32, s.shape, 0) % ns
    col = lax.broadcasted_iota(jnp.int32, s.shape, 1)
    cbx_new = _with_ones(cb)
    _online_step(jnp.where(col <= row, s, NEG), lambda p: _dot(p, cbx_new), m_s, acc_s)
    o = _mla_output(acc_s[:, :MLA_KV_LORA], _lanes(acc_s[:, MLA_KV_LORA:], MLA_KV_LORA), wuv_ref, ns)
    for t in range(ns):
        o_ref[t, 0] = o[t:t + 1]


def _mla_decode(q, c, kr, lat_pool, kr_pool, li, page_table, cs, sn, nn, nr, wuk, wuv):
    ns, db, nqc = q.shape
    hd = N_HEADS * HEAD_DIM
    slots = PAGE_SLOTS
    npg = page_table.shape[1] // slots
    full = lambda a: pl.BlockSpec(a.shape, lambda b, pt: (0,) * a.ndim)
    hbm = pl.BlockSpec(memory_space=pl.ANY)
    step = lambda width: pl.BlockSpec((ns, 1, 1, width), lambda b, pt: (0, b, 0, 0))
    grid_spec = pltpu.PrefetchScalarGridSpec(
        num_scalar_prefetch=1, grid=(db,),
        in_specs=[hbm, hbm, step(nqc), step(MLA_KV_LORA), step(MLA_ROPE),
                  full(cs), full(sn), full(nn), full(nr), full(wuk), full(wuv)],
        out_specs=step(hd),
        scratch_shapes=[pltpu.VMEM((N_HEADS * ns, MLA_KV_LORA), BF16),
                        pltpu.VMEM((N_HEADS * ns, MLA_ROPE), BF16),
                        pltpu.VMEM((N_HEADS * ns, LANES), F32),
                        pltpu.VMEM((N_HEADS * ns, MLA_KV_LORA + LANES), F32),
                        pltpu.VMEM((2, slots, PAGE_SIZE, MLA_KV_LORA), F32),
                        pltpu.VMEM((2, slots, MLA_ROPE, PAGE_SIZE), F32),
                        pltpu.SemaphoreType.DMA((2, 2))])
    kr_pool_t = kr_pool.transpose(0, 1, 3, 2)
    o = pl.pallas_call(
        functools.partial(_mla_decode_kernel, ns=ns, slots=slots, npg=npg, layer=li), grid_spec=grid_spec,
        out_shape=jax.ShapeDtypeStruct((ns, db, 1, hd), F32),
        compiler_params=_cparams(("arbitrary",)),
    )(page_table, lat_pool, kr_pool_t,
      q.reshape(ns, db, 1, nqc), c.reshape(ns, db, 1, MLA_KV_LORA), kr.reshape(ns, db, 1, MLA_ROPE),
      cs, sn, nn, nr, wuk, wuv)
    return o.reshape(ns, db, hd)


NSA_KVD = NSA_KV_HEADS * HEAD_DIM
NSA_GRP = N_HEADS // NSA_KV_HEADS
NSA_PAIR = NSA_SEL_BLOCK // NSA_CMP_BLOCK


def _nsa_prep_kernel(kc_ref, vc_ref, ks_ref, kw_ref, wk_ref, wv_ref, kcn_ref, ksn_ref, kwn_ref, cs_ref, sn_ref,
                     kso_ref, kwo_ref, kcmp_ref, vcmp_ref):
    cs, sn = cs_ref[...], sn_ref[...]
    kc, vc = kc_ref[0], vc_ref[0]
    seg = _seg_ones(NSA_KVD)
    kso_ref[0] = _heads_rope(_heads_rms(ks_ref[0], ksn_ref[...], seg), cs, sn)
    kwo_ref[0] = _heads_rope(_heads_rms(kw_ref[0], kwn_ref[...], seg), cs, sn)
    kcmp, vcmp = [], []
    for kv in range(NSA_KV_HEADS):
        kcmp.append(_rms(_dot(wk_ref[kv], _head(kc, kv).astype(BF16)), kcn_ref[...]))
        vcmp.append(_dot(wv_ref[kv], _head(vc, kv).astype(BF16)))
    kcmp_ref[0] = jnp.concatenate(kcmp, axis=1)
    vcmp_ref[0] = jnp.concatenate(vcmp, axis=1)


def _cmp_matrix(w, nc):
    eye = jnp.eye(nc, dtype=F32)
    return (eye[None, :, :, None] * w.T[:, None, None, :]).reshape(w.shape[1], nc, nc * NSA_CMP_BLOCK).astype(BF16)


def _nsa_prep(y, cmp_wk, cmp_wv, k_norm, cs, sn):
    b, s, _ = y.shape
    nc = s // NSA_CMP_BLOCK
    base = N_HEADS * HEAD_DIM // NSA_KVD
    col = lambda j: pl.BlockSpec((1, s, NSA_KVD), lambda bb: (bb, 0, base + j))
    full = lambda a: pl.BlockSpec(a.shape, lambda bb: (0,) * a.ndim)
    wk, wv = _cmp_matrix(cmp_wk, nc), _cmp_matrix(cmp_wv, nc)
    kn = [k_norm[0:1]] + [jnp.tile(k_norm[j:j + 1], (1, NSA_KV_HEADS)) for j in (1, 2)]
    seq = pl.BlockSpec((1, s, NSA_KVD), lambda bb: (bb, 0, 0))
    blk = pl.BlockSpec((1, nc, NSA_KVD), lambda bb: (bb, 0, 0))
    return pl.pallas_call(
        _nsa_prep_kernel, grid=(b,),
        in_specs=[col(0), col(1), col(2), col(4), full(wk), full(wv), full(kn[0]), full(kn[1]), full(kn[2]),
                  full(cs), full(sn)],
        out_specs=[seq, seq, blk, blk],
        out_shape=[jax.ShapeDtypeStruct((b, s, NSA_KVD), F32), jax.ShapeDtypeStruct((b, s, NSA_KVD), F32),
                   jax.ShapeDtypeStruct((b, nc, NSA_KVD), F32), jax.ShapeDtypeStruct((b, nc, NSA_KVD), F32)],
        compiler_params=_cparams(("arbitrary",)),
    )(y, y, y, y, wk, wv, kn[0], kn[1], kn[2], cs, sn)


def _cmp_attend(q, kcmp, vcmp, mask, ng, row_minor=False):
    nq = q.shape[0] // ng
    mk = jnp.concatenate([mask] * ng, axis=0)
    qk, pv = (_dot, _dot_nt) if row_minor else (_dot_nt, _dot)
    s = jnp.where(mk, qk(q, kcmp.astype(BF16)), NEG)
    m = jnp.max(s, axis=-1, keepdims=True)
    p = jnp.where(mk, jnp.exp(s - m), 0.0)
    p = p / jnp.maximum(jnp.sum(p, axis=-1, keepdims=True), TINY)
    o = pv(p.astype(BF16), vcmp.astype(BF16))
    imp = p[0:nq]
    for g in range(1, ng):
        imp = imp + p[g * nq:(g + 1) * nq]
    return o, imp


def _pair_sum(imp):
    n = imp.shape[1]
    lane = lax.broadcasted_iota(jnp.int32, imp.shape, 1)
    return imp + jnp.where(lane % 2 == 0, pltpu.roll(imp, n - 1, 1), pltpu.roll(imp, 1, 1))


def _select(impx, nblk, seg, n_sel):
    lane = lax.broadcasted_iota(jnp.int32, impx.shape, 1)
    blk = (lane % seg) // NSA_PAIR
    nseg = impx.shape[1] // seg
    cnt = jnp.zeros(impx.shape, jnp.int32)
    for j in range(nblk):
        col = impx[:, j * NSA_PAIR:j * NSA_PAIR + 1]
        for sg in range(1, nseg):
            c = sg * seg + j * NSA_PAIR
            col = jnp.where(lane < sg * seg, col, impx[:, c:c + 1])
        beats = (col > impx) | ((col == impx) & (j < blk))
        cnt = cnt + beats.astype(jnp.int32)
    return (cnt < n_sel).astype(F32)


def _select_rows(impx, nblk, seg, n_sel):
    row = lax.broadcasted_iota(jnp.int32, impx.shape, 0)
    blk = (row % seg) // NSA_PAIR
    nseg, nq = impx.shape[0] // seg, impx.shape[1]
    cnt = jnp.zeros(impx.shape, jnp.int32)
    for j in range(nblk):
        parts = [jnp.broadcast_to(impx[sg * seg + j * NSA_PAIR:sg * seg + j * NSA_PAIR + 1, :], (seg, nq))
                 for sg in range(nseg)]
        col = parts[0] if nseg == 1 else jnp.concatenate(parts, axis=0)
        beats = (col > impx) | ((col == impx) & (j < blk))
        cnt = cnt + beats.astype(jnp.int32)
    return (cnt < n_sel).astype(F32)


def _expand_mask(sel, start, tk):
    nc = sel.shape[1]
    n = lax.broadcasted_iota(jnp.int32, (nc, tk), 0)
    k = lax.broadcasted_iota(jnp.int32, (nc, tk), 1)
    e = jnp.where((start + k) // NSA_CMP_BLOCK == n, 1.0, 0.0).astype(BF16)
    return _dot(sel.astype(BF16), e) > 0.5


def _nsa_attn_kernel(q_ref, gl_ref, ks_ref, vs_ref, kwp_ref, kwc_ref, vwp_ref, vwc_ref, kcmp_ref, vcmp_ref,
                     qg_ref, c2_ref, s2_ref, o_ref, qn_s, qr_s, ks_s, vs_s, m_s, acc_s, mw_s, accw_s,
                     *, tk, n_sel, chunk):
    i = pl.program_id(1)
    tq = q_ref.shape[1]
    nc = kcmp_ref.shape[1]
    rows_kv = NSA_GRP * tq
    rep = chunk // tq
    kv_of = lambda c: c * chunk // rows_kv
    kv_range = range(NSA_KV_HEADS)

    @pl.when(i == 0)
    def _():
        ks_s[...] = ks_ref[0].astype(BF16)
        vs = vs_ref[0].astype(BF16)
        vs_s[...] = jnp.concatenate([_with_ones(_head(vs, kv)) for kv in kv_range], axis=1)

    qn = _heads_rms(q_ref[0], qg_ref[...], _seg_ones(2 * LANES)) * HEAD_DIM ** -0.5
    _stack_heads(qn, qn_s)
    _stack_heads(_heads_rope(qn, c2_ref[...], s2_ref[...]), qr_s)

    t_c = i * tq + lax.broadcasted_iota(jnp.int32, (tq, nc), 0)
    n_c = lax.broadcasted_iota(jnp.int32, (tq, nc), 1)
    cmask = (n_c + 1) * NSA_CMP_BLOCK - 1 <= t_c
    o_cmp, imps = [], []
    for kv in kv_range:
        oc, imp = _cmp_attend(qn_s[kv * rows_kv:(kv + 1) * rows_kv, :], _head(kcmp_ref[0], kv),
                              _head(vcmp_ref[0], kv), cmask, NSA_GRP)
        o_cmp.append(oc)
        imps.append(imp)
    imp = jnp.concatenate(imps, axis=1).T
    row = lax.broadcasted_iota(jnp.int32, imp.shape, 0)
    imp = imp + jnp.where(row % 2 == 0, pltpu.roll(imp, imp.shape[0] - 1, 0), pltpu.roll(imp, 1, 0))
    t_i = i * tq + lax.broadcasted_iota(jnp.int32, imp.shape, 1)
    blk = (row % nc) // NSA_PAIR
    impx = jnp.where(blk == t_i // NSA_SEL_BLOCK, jnp.inf, jnp.where(blk * NSA_SEL_BLOCK <= t_i, imp, -jnp.inf))
    sel = _select_rows(impx, nc // NSA_PAIR, nc, n_sel).T

    _flash_init(m_s, acc_s)
    selk = [sel[:, kv * nc:(kv + 1) * nc] for kv in kv_range]
    t_k = i * tq + lax.broadcasted_iota(jnp.int32, (tq, tk), 0)
    k_off = lax.broadcasted_iota(jnp.int32, (tq, tk), 1)

    def body(j, carry):
        start = pl.multiple_of(j * tk, tk)
        kb = ks_s[pl.ds(start, tk), :]
        vbx = vs_s[pl.ds(start, tk), :]
        causal = (start + k_off) <= t_k
        mks = [jnp.concatenate([_expand_mask(selk[kv], start, tk) & causal] * rep, axis=0) for kv in kv_range]
        kvs = [([_head(kb, kv)], vbx[:, kv * LANES:(kv + 1) * LANES]) for kv in kv_range]
        _flash_step([qr_s], lambda c: kvs[kv_of(c)], lambda c: mks[kv_of(c)], m_s, acc_s, chunk)
        return carry

    lax.fori_loop(0, (i * tq + tq - 1) // tk + 1, body, 0)

    _flash_init(mw_s, accw_s)
    kk = jnp.concatenate([kwp_ref[0], kwc_ref[0]], axis=0).astype(BF16)
    vv = jnp.concatenate([vwp_ref[0], vwc_ref[0]], axis=0).astype(BF16)
    kvw = [([_head(kk, kv)], _with_ones(_head(vv, kv))) for kv in kv_range]
    wmk = jnp.concatenate([_band_mask(tq, tq, jnp.where(i > 0, 0, tq))] * rep, axis=0)
    _flash_step([qr_s], lambda c: kvw[kv_of(c)], lambda c: wmk, mw_s, accw_s, chunk)

    o_sel = _normalized(acc_s[...])
    o_win = _normalized(accw_s[...])
    gates = jax.nn.sigmoid(gl_ref[0])
    outs = []
    for h in range(N_HEADS):
        kv, g = divmod(h, NSA_GRP)
        outs.append(gates[:, 3 * h:3 * h + 1] * o_cmp[kv][g * tq:(g + 1) * tq]
                    + gates[:, 3 * h + 1:3 * h + 2] * o_sel[h * tq:(h + 1) * tq]
                    + gates[:, 3 * h + 2:3 * h + 3] * o_win[h * tq:(h + 1) * tq])
    o_ref[0] = jnp.concatenate(outs, axis=1).astype(o_ref.dtype)


def _nsa_attn(y, ksn, kwn, kcmp, vcmp, q_norm, cs, sn, tk):
    b, s, _ = y.shape
    tq = WINDOW
    hd = N_HEADS * HEAD_DIM
    nc = kcmp.shape[1]
    base = hd // NSA_KVD
    n_sel = min(NSA_TOPK, s // NSA_SEL_BLOCK)
    prev = lambda i: jnp.maximum(i - 1, 0)
    full = lambda a: pl.BlockSpec(a.shape, lambda bb, i: (0,) * a.ndim)
    tab = pl.BlockSpec((tq, LANES), lambda bb, i: (i, 0))
    q_norm = jnp.tile(q_norm, (1, N_HEADS))
    return pl.pallas_call(
        functools.partial(_nsa_attn_kernel, tk=tk, n_sel=n_sel, chunk=4 * tq), grid=(b, s // tq),
        in_specs=[pl.BlockSpec((1, tq, hd), lambda bb, i: (bb, i, 0)),
                  pl.BlockSpec((1, tq, NSA_KVD), lambda bb, i: (bb, i, base + 6)),
                  pl.BlockSpec((1, s, NSA_KVD), lambda bb, i: (bb, 0, 0)),
                  pl.BlockSpec((1, s, NSA_KVD), lambda bb, i: (bb, 0, base + 3)),
                  pl.BlockSpec((1, tq, NSA_KVD), lambda bb, i: (bb, prev(i), 0)),
                  pl.BlockSpec((1, tq, NSA_KVD), lambda bb, i: (bb, i, 0)),
                  pl.BlockSpec((1, tq, NSA_KVD), lambda bb, i: (bb, prev(i), base + 5)),
                  pl.BlockSpec((1, tq, NSA_KVD), lambda bb, i: (bb, i, base + 5)),
                  pl.BlockSpec((1, nc, NSA_KVD), lambda bb, i: (bb, 0, 0)),
                  pl.BlockSpec((1, nc, NSA_KVD), lambda bb, i: (bb, 0, 0)),
                  full(q_norm), tab, tab],
        out_specs=pl.BlockSpec((1, tq, hd), lambda bb, i: (bb, i, 0)),
        out_shape=jax.ShapeDtypeStruct((b, s, hd), BF16),
        scratch_shapes=[pltpu.VMEM((N_HEADS * tq, HEAD_DIM), BF16),
                        pltpu.VMEM((N_HEADS * tq, HEAD_DIM), BF16),
                        pltpu.VMEM((s, NSA_KVD), BF16),
                        pltpu.VMEM((s, NSA_KV_HEADS * LANES), BF16),
                        pltpu.VMEM((N_HEADS * tq, LANES), F32),
                        pltpu.VMEM((N_HEADS * tq, 2 * HEAD_DIM), F32),
                        pltpu.VMEM((N_HEADS * tq, LANES), F32),
                        pltpu.VMEM((N_HEADS * tq, 2 * HEAD_DIM), F32)],
        compiler_params=_cparams(("arbitrary", "arbitrary")),
    )(y, y, ksn, y, kwn, kwn, y, y, kcmp, vcmp, q_norm, cs, sn)


def _nsa_decode_kernel(pt_ref, kc_hbm, vc_hbm, ks_hbm, vs_hbm,
                       q_ref, gl_ref, ks_ref, vs_ref, kw_ref, vw_ref, kwb_ref, vwb_ref, wrow_ref, eloc_ref,
                       qn_ref, kcn_ref, ksn_ref, kwn_ref, cs_ref, sn_ref,
                       o_ref, kso_ref, kwo_ref, vwo_ref,
                       cmp_s, sel_s, ocmp_s, q_s, m_s, acc_s, k_buf, v_buf, sems,
                       *, ns, slots, npg, past, n_sel, layer):
    seq = pl.program_id(0)
    scale = HEAD_DIM ** -0.5
    nc = cmp_s.shape[1]
    rows_per_step = slots * PAGE_SIZE
    cper = rows_per_step // NSA_CMP_BLOCK
    kv_range = range(NSA_KV_HEADS)
    grp = lambda xs, kv: xs[kv * NSA_GRP:(kv + 1) * NSA_GRP]
    cmp_stream = _PageStream(pt_ref, seq, layer, [kc_hbm, vc_hbm], [k_buf, v_buf], sems, slots)
    sel_stream = _PageStream(pt_ref, seq, layer, [ks_hbm, vs_hbm], [k_buf, v_buf], sems, slots)
    cmp_stream.start_first()
    pages = lambda buf, half, kv: jnp.concatenate([buf[half, k, kv] for k in range(slots)], axis=1)

    def block_sums(g, half):
        lhs = jnp.concatenate([pages(buf, half, kv) * wrow_ref[2 * j + kv:2 * j + kv + 1, :]
                               for j, buf in enumerate((k_buf, v_buf)) for kv in kv_range], axis=0)
        res = _dot(lhs.astype(BF16), eloc_ref[...])
        for k in range(npg):
            @pl.when(g == k)
            def _():
                cmp_s[:, k * cper:(k + 1) * cper] = res

    cmp_stream.sweep(npg, block_sums)
    sel_stream.start(0, 0)

    q = _rows(q_ref, ns)
    qn = [_rms(_head(q, h), qn_ref[...]) * scale for h in range(N_HEADS)]
    qr = [_rope(x, cs_ref[...], sn_ref[...], ROPE_DIM) for x in qn]
    t_c = past + lax.broadcasted_iota(jnp.int32, (ns, nc), 0)
    n_c = lax.broadcasted_iota(jnp.int32, (ns, nc), 1)
    cmask = (n_c + 1) * NSA_CMP_BLOCK - 1 <= t_c
    imps = []
    for kv in kv_range:
        kct = cmp_s[kv * HEAD_DIM:(kv + 1) * HEAD_DIM, :]
        kct = kct * lax.rsqrt(jnp.mean(kct * kct, axis=0, keepdims=True) + EPS) * kcn_ref[...]
        vct = cmp_s[(NSA_KV_HEADS + kv) * HEAD_DIM:(NSA_KV_HEADS + kv + 1) * HEAD_DIM, :]
        oc, imp = _cmp_attend(jnp.concatenate(grp(qn, kv), axis=0).astype(BF16), kct, vct, cmask, NSA_GRP,
                              row_minor=True)
        ocmp_s[kv] = oc
        imps.append(_pair_sum(imp))
        q_s[kv] = jnp.concatenate(grp(qr, kv), axis=0).astype(BF16)
    for kv in kv_range:
        sel = _select(imps[kv], nc // NSA_PAIR, nc, n_sel - 1)
        for k in range(npg):
            sel_s[kv, k] = sel[:, k * cper:(k + 1) * cper]
    _flash_init(m_s, acc_s)

    def selected(g, half):
        for kv in kv_range:
            kst = pages(k_buf, half, kv).astype(BF16)
            vst = pages(v_buf, half, kv)
            vtx = jnp.concatenate([vst, jnp.ones(vst.shape, F32)], axis=0).astype(BF16)
            mk = jnp.concatenate([_dot_nt(sel_s[kv, g].astype(BF16), eloc_ref[...]) > 0.5] * NSA_GRP, axis=0)
            _online_step(jnp.where(mk, _dot(q_s[kv], kst), NEG), lambda p, v=vtx: _dot_nt(p, v),
                         m_s.at[kv], acc_s.at[kv])

    sel_stream.sweep(npg, selected)
    cmp_stream.prefetch_next_sequence()

    pad = lambda a: jnp.concatenate([a, jnp.zeros((PAGE_SIZE - ns, a.shape[1]), a.dtype)], axis=0)
    vs_new = _rows(vs_ref, ns)
    ks_raw = _rows(ks_ref, ns)
    ksn = jnp.concatenate([_rope(_rms(_head(ks_raw, kv), ksn_ref[...]), cs_ref[...], sn_ref[...], ROPE_DIM)
                           for kv in kv_range], axis=1)
    kso_ref[0] = ksn
    kb = pad(ksn).astype(BF16)
    row = lax.broadcasted_iota(jnp.int32, (NSA_GRP * ns, PAGE_SIZE), 0) % ns
    col = lax.broadcasted_iota(jnp.int32, (NSA_GRP * ns, PAGE_SIZE), 1)
    for kv in kv_range:
        vbx = _with_ones(pad(_head(vs_new, kv)).astype(BF16))
        _online_step(jnp.where(col <= row, _dot_nt(q_s[kv], _head(kb, kv)), NEG), lambda p, v=vbx: _dot(p, v),
                     m_s.at[kv], acc_s.at[kv])
    outs_w, kw_out, vw_out = _window_step(
        q, _rows(kw_ref, ns), _rows(vw_ref, ns), [kwb_ref[0, 0, kv] for kv in kv_range],
        [vwb_ref[0, 0, kv] for kv in kv_range], qn_ref[...], kwn_ref[...], cs_ref[...], sn_ref[...], None,
        NSA_KV_HEADS)
    for kv in kv_range:
        kwo_ref[0, kv] = kw_out[kv]
        vwo_ref[0, kv] = vw_out[kv]
    gates = jax.nn.sigmoid(_rows(gl_ref, ns))
    outs = []
    for kv in kv_range:
        o_sel = _normalized(acc_s[kv])
        o_cmp = ocmp_s[kv]
        for g in range(NSA_GRP):
            h = kv * NSA_GRP + g
            outs.append(gates[:, 3 * h:3 * h + 1] * o_cmp[g * ns:(g + 1) * ns]
                        + gates[:, 3 * h + 1:3 * h + 2] * o_sel[g * ns:(g + 1) * ns]
                        + gates[:, 3 * h + 2:3 * h + 3] * outs_w[h])
    o = jnp.concatenate(outs, axis=1)
    for t in range(ns):
        o_ref[t, 0] = o[t:t + 1]


def _nsa_decode(y, pools, kw_cache, vw_cache, li, page_table, cmp_wk, cmp_wv, q_norm, k_norm, cs, sn):
    ns, db, n = y.shape
    hd = N_HEADS * HEAD_DIM
    slots = PAGE_SLOTS
    n_pages = page_table.shape[1]
    npg = n_pages // slots
    past = n_pages * PAGE_SIZE
    nc = past // NSA_CMP_BLOCK
    base = hd // NSA_KVD
    w = kw_cache.shape[2]
    n_blocks = -(-(past + ns) // NSA_SEL_BLOCK)
    n_sel = min(NSA_TOPK, n_blocks)
    y4 = y.reshape(ns, db, 1, n)
    full = lambda a: pl.BlockSpec(a.shape, lambda b, pt: (0,) * a.ndim)
    hbm = pl.BlockSpec(memory_space=pl.ANY)
    step = lambda width, cb: pl.BlockSpec((ns, 1, 1, width), lambda b, pt: (0, b, 0, cb))
    rows = slots * PAGE_SIZE
    cper = rows // NSA_CMP_BLOCK
    wrow = jnp.tile(jnp.concatenate([cmp_wk.T, cmp_wv.T], axis=0), (1, cper))
    eloc = (jnp.arange(rows)[:, None] // NSA_CMP_BLOCK == jnp.arange(cper)[None, :]).astype(BF16)
    kn = [k_norm[j:j + 1] for j in range(3)]
    kcn_col = k_norm[0].reshape(HEAD_DIM, 1)
    seq_out = pl.BlockSpec((1, ns, NSA_KVD), lambda b, pt: (b, 0, 0))
    buf_out = pl.BlockSpec((1, NSA_KV_HEADS, HEAD_DIM, w), lambda b, pt: (b, 0, 0, 0))
    buf_in = pl.BlockSpec((1, 1, NSA_KV_HEADS, HEAD_DIM, w), lambda b, pt: (li, b, 0, 0, 0))
    page_buf = pltpu.VMEM((2, slots, NSA_KV_HEADS, HEAD_DIM, PAGE_SIZE), F32)
    grid_spec = pltpu.PrefetchScalarGridSpec(
        num_scalar_prefetch=1, grid=(db,),
        in_specs=[hbm, hbm, hbm, hbm,
                  step(hd, 0), step(NSA_KVD, base + 6), step(NSA_KVD, base + 2), step(NSA_KVD, base + 3),
                  step(NSA_KVD, base + 4), step(NSA_KVD, base + 5), buf_in, buf_in,
                  full(wrow), full(eloc),
                  full(q_norm), full(kcn_col), full(kn[1]), full(kn[2]), full(cs), full(sn)],
        out_specs=[step(hd, 0), seq_out, buf_out, buf_out],
        scratch_shapes=[pltpu.VMEM((2 * NSA_KV_HEADS * HEAD_DIM, nc), F32),
                        pltpu.VMEM((NSA_KV_HEADS, npg, ns, cper), F32),
                        pltpu.VMEM((NSA_KV_HEADS, NSA_GRP * ns, HEAD_DIM), F32),
                        pltpu.VMEM((NSA_KV_HEADS, NSA_GRP * ns, HEAD_DIM), BF16),
                        pltpu.VMEM((NSA_KV_HEADS, NSA_GRP * ns, LANES), F32),
                        pltpu.VMEM((NSA_KV_HEADS, NSA_GRP * ns, 2 * HEAD_DIM), F32),
                        page_buf, page_buf, pltpu.SemaphoreType.DMA((2, 2))])
    kc_pool, vc_pool, ks_pool, vs_pool = [_row_minor(p) for p in pools]
    o, kso, kwo, vwo = pl.pallas_call(
        functools.partial(_nsa_decode_kernel, ns=ns, slots=slots, npg=npg, past=past, n_sel=n_sel, layer=li),
        grid_spec=grid_spec,
        out_shape=[jax.ShapeDtypeStruct((ns, db, 1, hd), F32),
                   jax.ShapeDtypeStruct((db, ns, NSA_KVD), F32),
                   jax.ShapeDtypeStruct((db, NSA_KV_HEADS, HEAD_DIM, w), F32),
                   jax.ShapeDtypeStruct((db, NSA_KV_HEADS, HEAD_DIM, w), F32)],
        compiler_params=_cparams(("arbitrary",)),
    )(page_table, kc_pool, vc_pool, ks_pool, vs_pool,
      y4, y4, y4, y4, y4, y4, _row_minor(kw_cache), _row_minor(vw_cache), wrow, eloc,
      q_norm, kcn_col, kn[1], kn[2], cs, sn)
    back = lambda a: a.transpose(0, 3, 1, 2)
    return o.reshape(ns, db, hd), kso, back(kwo), back(vwo)


def kernel(x_prompt, x_sample, cache_swa_k, cache_swa_v, cache_mla_latent, cache_mla_krope, cache_nsa_kcmp, cache_nsa_vcmp, cache_nsa_ksel, cache_nsa_vsel, cache_nsa_kwin, cache_nsa_vwin, state_conv_ffn, page_table, c_prompt, c_sample, ada_w, ada_b, norm_mix, norm_ffn, ffn_w_up, ffn_conv, ffn_w_down, a_w_in, a_q_norm, a_k_norm, a_sinks, a_w_out, b_w_in, b_qa_norm, b_w_qb, b_q_norm_nope, b_q_norm_rope, b_kv_norm, b_krope_norm, b_w_uk, b_w_uv, b_w_out, c_w_in, c_q_norm, c_k_norm, c_cmp_wk, c_cmp_wv, c_w_out):
    nb, seq, d = x_prompt.shape
    db, ds, _ = x_sample.shape
    depth = ada_w.shape[0]
    ff = ffn_w_down.shape[1]
    past = page_table.shape[1] * PAGE_SIZE
    hd = N_HEADS * HEAD_DIM

    mod = _modulate(jnp.concatenate([c_prompt, c_sample], axis=0), ada_w, ada_b)
    tr_p = math.gcd(seq, 512)
    gp = _Group(True, mod[:, :, :nb].reshape(depth, 6, nb, 1, d), tr_p)
    gs = _Group(False, mod[:, :, nb:], db)
    norm_mix3 = norm_mix.reshape(depth, 1, d)
    norm_ffn3 = norm_ffn.reshape(depth, 1, d)

    pos_p = jnp.arange(seq, dtype=jnp.int32)
    pos_s = past + jnp.arange(ds, dtype=jnp.int32)
    cw_p, sw_p = _rope_tables(pos_p, ROPE_DIM, ROPE_THETA, HEAD_DIM)
    cw2_p, sw2_p = jnp.tile(cw_p, (1, LANES // HEAD_DIM)), jnp.tile(sw_p, (1, LANES // HEAD_DIM))
    cw_s, sw_s = _rope_tables(pos_s, ROPE_DIM, ROPE_THETA, HEAD_DIM)
    cm_p, sm_p = _rope_tables(pos_p, MLA_ROPE, MLA_THETA, MLA_ROPE)
    cm_s, sm_s = _rope_tables(pos_s, MLA_ROPE, MLA_THETA, MLA_ROPE)

    xp = x_prompt
    xs = x_sample.transpose(1, 0, 2)
    row2 = lambda v: v.reshape(1, -1)
    tm = lambda a: a.transpose(1, 0, 2)
    out = {k: [] for k in ("swa_k_p", "swa_v_p", "swa_k_s", "swa_v_s", "mla_c_p", "mla_r_p", "mla_c_s", "mla_r_s",
                           "conv_p", "conv_s")}
    nsa_p = [[] for _ in range(6)]
    nsa_s = [[] for _ in range(6)]
    ia = ib = ic = 0
    for layer in range(depth):
        kind = layer % N_MIXERS
        if kind == 0:
            w_in = a_w_in[ia].astype(BF16)
            qn, kn, sinks = row2(a_q_norm[ia]), row2(a_k_norm[ia]), row2(a_sinks[ia])
            yp = _proj(xp, gp, norm_mix3, layer, w_in)
            op, kp, vp = _band_attn(yp, qn, kn, sinks, cw2_p, sw2_p)
            ys = _proj(xs, gs, norm_mix3, layer, w_in)
            os_, ks_, vs_ = _step_attn(ys, cache_swa_k, cache_swa_v, ia, qn, kn, sinks, cw_s, sw_s)
            kv4 = lambda a: a.reshape(a.shape[0], a.shape[1], SWA_KV_HEADS, HEAD_DIM)
            out["swa_k_p"].append(kv4(kp)); out["swa_v_p"].append(kv4(vp))
            out["swa_k_s"].append(ks_); out["swa_v_s"].append(vs_)
            w_out = a_w_out[ia].astype(BF16)
            ia += 1
        elif kind == 1:
            w_in = b_w_in[ib].astype(BF16)
            wqb = b_w_qb[ib].reshape(MLA_Q_LORA, N_HEADS, MLA_NOPE + MLA_ROPE)
            wqb = jnp.concatenate([wqb[:, :, :MLA_NOPE].reshape(MLA_Q_LORA, -1),
                                   wqb[:, :, MLA_NOPE:].reshape(MLA_Q_LORA, -1)], axis=1).astype(BF16)
            wuk = b_w_uk[ib].transpose(1, 2, 0).astype(BF16)
            wuv = b_w_uv[ib].transpose(1, 0, 2).astype(BF16)
            norms = (row2(b_qa_norm[ib]), wqb, row2(b_kv_norm[ib]), row2(b_krope_norm[ib]))
            nn = jnp.tile(row2(b_q_norm_nope[ib]), (1, N_HEADS))
            nr = jnp.tile(row2(b_q_norm_rope[ib]), (1, N_HEADS))
            wide = lambda t: jnp.tile(t, (1, LANES // MLA_ROPE))
            qp, cp, rp = _mla_proj(xp, gp, norm_mix3, layer, w_in, *norms, cm_p[None], sm_p[None])
            op = _mla_attn(qp, cp, rp, wide(cm_p), wide(sm_p), nn, nr, wuk, wuv, math.gcd(seq, 128),
                           math.gcd(seq, 256), 512)
            qs, cs_, rs_ = _mla_proj(xs, gs, norm_mix3, layer, w_in, *norms, cm_s[:, None], sm_s[:, None])
            os_ = _mla_decode(qs, cs_, rs_, cache_mla_latent, cache_mla_krope, ib, page_table,
                              wide(cm_s), wide(sm_s), nn, nr, wuk, wuv)
            out["mla_c_p"].append(cp); out["mla_r_p"].append(rp)
            out["mla_c_s"].append(tm(cs_)); out["mla_r_s"].append(tm(rs_))
            w_out = b_w_out[ib].astype(BF16)
            ib += 1
        else:
            n_in = c_w_in.shape[2]
            n_pad = -(-n_in // 128) * 128
            w_in = jnp.pad(c_w_in[ic], ((0, 0), (0, n_pad - n_in))).astype(BF16)
            qn, kn3 = row2(c_q_norm[ic]), c_k_norm[ic]
            kv4 = lambda a: a.reshape(a.shape[0], a.shape[1], NSA_KV_HEADS, HEAD_DIM)
            col = lambda a, j: a[:, :, hd + j * NSA_KVD:hd + (j + 1) * NSA_KVD]
            yp = _proj(xp, gp, norm_mix3, layer, w_in)
            ksn, kwn, kcmp, vcmp = _nsa_prep(yp, c_cmp_wk[ic], c_cmp_wv[ic], kn3, cw2_p, sw2_p)
            op = _nsa_attn(yp, ksn, kwn, kcmp, vcmp, qn, cw2_p, sw2_p, math.gcd(seq, 256))
            for j, a in enumerate((col(yp, 0), col(yp, 1), ksn, col(yp, 3), kwn[:, -WINDOW:],
                                   col(yp, 5)[:, -WINDOW:])):
                nsa_p[j].append(kv4(a))
            ys = _proj(xs, gs, norm_mix3, layer, w_in)
            pools = (cache_nsa_kcmp, cache_nsa_vcmp, cache_nsa_ksel, cache_nsa_vsel)
            os_, kso, kwo, vwo = _nsa_decode(ys, pools, cache_nsa_kwin, cache_nsa_vwin, ic, page_table,
                                             c_cmp_wk[ic], c_cmp_wv[ic], qn, kn3, cw_s, sw_s)
            for j, a in enumerate((tm(col(ys, 0)), tm(col(ys, 1)), kso, tm(col(ys, 3)))):
                nsa_s[j].append(kv4(a))
            nsa_s[4].append(kwo)
            nsa_s[5].append(vwo)
            w_out = c_w_out[ic].astype(BF16)
            ic += 1
        xp = _outproj(op, xp, gp, layer, w_out)
        xs = _outproj(os_, xs, gs, layer, w_out)
        wup, wd = ffn_w_up[layer].astype(BF16), ffn_w_down[layer].astype(BF16)
        tf = math.gcd(ff, 256)
        xp, bp = _ffn_prompt(xp, gp, norm_ffn3, layer, wup, ffn_conv[layer], wd, math.gcd(seq, 1024), tf)
        xs, bs = _ffn_sample(xs, gs, norm_ffn3, layer, wup, ffn_conv[layer], wd, state_conv_ffn[layer], tf)
        out["conv_p"].append(bp); out["conv_s"].append(bs)

    st = lambda xs_: jnp.stack(xs_)
    return (xp, xs.transpose(1, 0, 2),
            st(out["swa_k_p"]), st(out["swa_v_p"]), st(out["mla_c_p"]), st(out["mla_r_p"]),
            *[st(a) for a in nsa_p], st(out["conv_p"]),
            st(out["swa_k_s"]), st(out["swa_v_s"]), st(out["mla_c_s"]), st(out["mla_r_s"]),
            *[st(a) for a in nsa_s], st(out["conv_s"]))
```

```python
import functools
import math

import jax
import jax.numpy as jnp
from jax import lax
from jax.experimental import pallas as pl
from jax.experimental.pallas import tpu as pltpu

F32 = jnp.float32
BF16 = jnp.bfloat16

N_HEADS = 16
HEAD_DIM = 64
ROPE_DIM = 16
ROPE_THETA = 500000.0
EPS = 1e-6
PAGE_SIZE = 128
SWA_KV_HEADS = 4
WINDOW = 128
MLA_Q_LORA = 384
MLA_KV_LORA = 256
MLA_NOPE = 64
MLA_ROPE = 32
MLA_THETA = 10000.0
NSA_KV_HEADS = 2
NSA_CMP_BLOCK = 32
NSA_SEL_BLOCK = 64
NSA_TOPK = 16
N_MIXERS = 3
NEG = -1e30
TINY = float(jnp.finfo(jnp.float32).tiny)
VMEM_LIMIT = 56 * 1024 * 1024
PAGE_SLOTS = 8
N_BUF = 3


def _cparams(sem):
    return pltpu.CompilerParams(dimension_semantics=sem, vmem_limit_bytes=VMEM_LIMIT)


def _dot(a, b):
    return jnp.dot(a, b, preferred_element_type=F32)


def _dot_nt(a, b):
    return lax.dot_general(a, b, (((1,), (1,)), ((), ())), preferred_element_type=F32)


def _rms(x, g):
    return x * lax.rsqrt(jnp.mean(x * x, axis=-1, keepdims=True) + EPS) * g


def _rope(x, c, s, rot):
    half = rot // 2
    parts = [x[:, half:rot], x[:, :half]]
    if x.shape[1] > rot:
        parts.append(x[:, rot:])
    return x * c + jnp.concatenate(parts, axis=1) * s


def _silu(x):
    return x * jax.nn.sigmoid(x)


def _norm_mod(x, g, sc, sh):
    return _rms(x, g) * (1.0 + sc) + sh


def _rope_tables(pos, rot, theta, width):
    half = rot // 2
    inv = jnp.power(jnp.float32(theta), -jnp.arange(half, dtype=F32) / half)
    ang = pos.astype(F32)[:, None] * inv[None, :]
    cos, sin = jnp.cos(ang), jnp.sin(ang)
    n = pos.shape[0]
    c = jnp.concatenate([cos, cos, jnp.ones((n, width - rot), F32)], axis=1)
    s = jnp.concatenate([-sin, sin, jnp.zeros((n, width - rot), F32)], axis=1)
    return c, s


class _Group:
    def __init__(self, per_group, mod, tr):
        self.pm = per_group
        self.mods = [mod[layer] for layer in range(mod.shape[0])]
        self.tr = tr

    def mod(self, layer):
        return self.mods[layer]

    def mod_spec(self, j, nargs=2):
        d = self.mods[0].shape[-1]
        if self.pm:
            shape, f = (1, 1, 1, d), (lambda g: (j, g, 0, 0))
        else:
            shape, f = (1, self.mods[0].shape[1], d), (lambda g: (j, 0, 0))
        if nargs == 2:
            return pl.BlockSpec(shape, lambda g, r: f(g))
        return pl.BlockSpec(shape, lambda g, r, k: f(g))


def _mod_val(ref, pm):
    return ref[0, 0] if pm else ref[0]


def _mod_kernel(c_ref, w_ref, b_ref, o_ref):
    a = _silu(c_ref[...])
    o_ref[0, 0] = _dot(a.astype(BF16), w_ref[0].astype(BF16)) + b_ref[0]


def _modulate(c_all, ada_w, ada_b):
    nl, d, d6 = ada_w.shape
    n = c_all.shape[0]
    nj = d6 // d
    return pl.pallas_call(
        _mod_kernel, grid=(nl, nj),
        in_specs=[pl.BlockSpec((n, d), lambda l, j: (0, 0)),
                  pl.BlockSpec((1, d, d), lambda l, j: (l, 0, j)),
                  pl.BlockSpec((1, 1, d), lambda l, j: (l, 0, j))],
        out_specs=pl.BlockSpec((1, 1, n, d), lambda l, j: (l, j, 0, 0)),
        out_shape=jax.ShapeDtypeStruct((nl, nj, n, d), F32),
        compiler_params=_cparams(("arbitrary", "arbitrary")),
    )(c_all, ada_w, ada_b.reshape(nl, 1, d6))


def _proj_kernel(x_ref, g_ref, sc_ref, sh_ref, w_ref, o_ref, *, pm):
    h = _norm_mod(x_ref[0], g_ref[0], _mod_val(sc_ref, pm), _mod_val(sh_ref, pm))
    o_ref[0] = _dot(h.astype(BF16), w_ref[...])


def _proj(x, grp, gains, layer, w_bf):
    ng, nr, d = x.shape
    n = w_bf.shape[1]
    tr = grp.tr
    return pl.pallas_call(
        functools.partial(_proj_kernel, pm=grp.pm), grid=(ng, nr // tr),
        in_specs=[pl.BlockSpec((1, tr, d), lambda g, r: (g, r, 0)),
                  pl.BlockSpec((1, 1, d), lambda g, r: (0, 0, 0)),
                  grp.mod_spec(1), grp.mod_spec(0),
                  pl.BlockSpec((d, n), lambda g, r: (0, 0))],
        out_specs=pl.BlockSpec((1, tr, n), lambda g, r: (g, r, 0)),
        out_shape=jax.ShapeDtypeStruct((ng, nr, n), F32),
        compiler_params=_cparams(("arbitrary", "arbitrary")),
    )(x, gains[layer:layer + 1], grp.mod(layer), grp.mod(layer), w_bf)


def _mla_proj_kernel(x_ref, g_ref, sc_ref, sh_ref, w_ref, qan_ref, wqb_ref, kvn_ref, krn_ref,
                     c_ref, s_ref, q_ref, lat_ref, kr_ref, *, pm):
    h = _norm_mod(x_ref[0], g_ref[0], _mod_val(sc_ref, pm), _mod_val(sh_ref, pm))
    y = _dot(h.astype(BF16), w_ref[...])
    a, b = MLA_Q_LORA, MLA_Q_LORA + MLA_KV_LORA
    qa = _rms(y[:, :a], qan_ref[...])
    q_ref[0] = _dot(qa.astype(BF16), wqb_ref[...])
    lat_ref[0] = _rms(y[:, a:b], kvn_ref[...])
    kr = _rms(y[:, b:], krn_ref[...])
    kr_ref[0] = _rope(kr, c_ref[0], s_ref[0], MLA_ROPE)


def _mla_proj(x, grp, gains, layer, w_bf, qa_norm, wqb_bf, kv_norm, krope_norm, cs, sn):
    ng, nr, d = x.shape
    tr = grp.tr
    nq = wqb_bf.shape[1]
    if grp.pm:
        tab = pl.BlockSpec((1, tr, MLA_ROPE), lambda g, r: (0, r, 0))
    else:
        tab = pl.BlockSpec((1, 1, MLA_ROPE), lambda g, r: (g, 0, 0))
    full = lambda a: pl.BlockSpec(a.shape, lambda g, r: (0,) * a.ndim)
    return pl.pallas_call(
        functools.partial(_mla_proj_kernel, pm=grp.pm), grid=(ng, nr // tr),
        in_specs=[pl.BlockSpec((1, tr, d), lambda g, r: (g, r, 0)),
                  pl.BlockSpec((1, 1, d), lambda g, r: (0, 0, 0)),
                  grp.mod_spec(1), grp.mod_spec(0),
                  full(w_bf), full(qa_norm), full(wqb_bf), full(kv_norm), full(krope_norm), tab, tab],
        out_specs=[pl.BlockSpec((1, tr, nq), lambda g, r: (g, r, 0)),
                   pl.BlockSpec((1, tr, MLA_KV_LORA), lambda g, r: (g, r, 0)),
                   pl.BlockSpec((1, tr, MLA_ROPE), lambda g, r: (g, r, 0))],
        out_shape=[jax.ShapeDtypeStruct((ng, nr, nq), F32),
                   jax.ShapeDtypeStruct((ng, nr, MLA_KV_LORA), F32),
                   jax.ShapeDtypeStruct((ng, nr, MLA_ROPE), F32)],
        compiler_params=_cparams(("arbitrary", "arbitrary")),
    )(x, gains[layer:layer + 1], grp.mod(layer), grp.mod(layer), w_bf, qa_norm, wqb_bf, kv_norm, krope_norm, cs, sn)


def _outproj_kernel(o_ref, x_ref, gt_ref, w_ref, y_ref, *, pm):
    y_ref[0] = x_ref[0] + _mod_val(gt_ref, pm) * _dot(o_ref[0].astype(BF16), w_ref[...])


def _outproj(o, x, grp, layer, w_bf):
    ng, nr, d = x.shape
    k = o.shape[-1]
    tr = grp.tr
    return pl.pallas_call(
        functools.partial(_outproj_kernel, pm=grp.pm), grid=(ng, nr // tr),
        in_specs=[pl.BlockSpec((1, tr, k), lambda g, r: (g, r, 0)),
                  pl.BlockSpec((1, tr, d), lambda g, r: (g, r, 0)),
                  grp.mod_spec(2),
                  pl.BlockSpec((k, d), lambda g, r: (0, 0))],
        out_specs=pl.BlockSpec((1, tr, d), lambda g, r: (g, r, 0)),
        out_shape=jax.ShapeDtypeStruct((ng, nr, d), F32),
        compiler_params=_cparams(("arbitrary", "arbitrary")),
    )(o, x, grp.mod(layer), w_bf)


HALO = 16


def _ffn_tail(mg, mv, wd_ref, x_ref, gt, y_ref):
    f = pl.program_id(2)
    d = _dot((_silu(mg) * mv).astype(BF16), wd_ref[...])

    @pl.when(f == 0)
    def _():
        y_ref[0] = d

    @pl.when(f > 0)
    def _():
        y_ref[0] += d

    @pl.when(f == pl.num_programs(2) - 1)
    def _():
        y_ref[0] = x_ref[0] + gt * y_ref[0]


def _ffn_prompt_kernel(x_ref, xh_ref, g_ref, sc_ref, sh_ref, gt_ref, wup_ref, cw_ref, wd_ref,
                       y_ref, b_ref, h_s, act_s, u_s, *, tr, tf):
    r = pl.program_id(1)
    ff = wd_ref.shape[0]
    g, sc, sh = g_ref[0], sc_ref[0, 0], sh_ref[0, 0]
    h_s[HALO:, :] = _norm_mod(x_ref[0], g, sc, sh).astype(BF16)
    hh = _norm_mod(xh_ref[0], g, sc, sh)
    h_s[:HALO, :] = jnp.where(r > 0, hh, 0.0).astype(BF16)
    nf = ff // tf
    cols = lambda half, f: pl.ds(pl.multiple_of(half * ff + f * tf, tf), tf)

    def up(f, slot):
        for half in range(2):
            u_s[slot, half] = _dot(h_s[...], wup_ref[:, cols(half, f)])

    def mixed(f, slot, half):
        u = u_s[slot, half]
        cw = cw_ref[:, cols(half, f)]
        b_ref[0, 0, :, cols(half, f)] = u[HALO + tr - 2:]
        a = pltpu.roll(u, 2, 0)[HALO:]
        b = pltpu.roll(u, 1, 0)[HALO:]
        return cw[0:1] * a + cw[1:2] * b + cw[2:3] * u[HALO:]

    def activate(f, slot):
        act_s[:, cols(0, f)] = (_silu(mixed(f, slot, 0)) * mixed(f, slot, 1)).astype(BF16)

    def body(it, carry):
        f = 2 * it
        up(f + 1, 1)
        activate(f, 0)
        up(f + 2, 0)
        activate(f + 1, 1)
        return carry

    assert nf % 2 == 1
    up(0, 0)
    lax.fori_loop(0, nf // 2, body, 0)
    activate(nf - 1, 0)
    y_ref[0] = x_ref[0] + gt_ref[0, 0] * _dot(act_s[...], wd_ref[...])


def _ffn_prompt(x, grp, gains, layer, wup_bf, conv_w, wd_bf, tr, tf):
    ng, nr, d = x.shape
    ff = wd_bf.shape[0]
    ms = lambda j: grp.mod_spec(j)
    once = lambda a: pl.BlockSpec(a.shape, lambda g, r: (0,) * a.ndim, pipeline_mode=pl.Buffered(1))
    y, buf = pl.pallas_call(
        functools.partial(_ffn_prompt_kernel, tr=tr, tf=tf), grid=(ng, nr // tr),
        in_specs=[pl.BlockSpec((1, tr, d), lambda g, r: (g, r, 0)),
                  pl.BlockSpec((1, HALO, d), lambda g, r: (g, jnp.maximum(r * (tr // HALO) - 1, 0), 0)),
                  pl.BlockSpec((1, 1, d), lambda g, r: (0, 0, 0)),
                  ms(4), ms(3), ms(5), once(wup_bf), once(conv_w), once(wd_bf)],
        out_specs=[pl.BlockSpec((1, tr, d), lambda g, r: (g, r, 0)),
                   pl.BlockSpec((1, 1, 2, 2 * ff), lambda g, r: (g, r, 0, 0))],
        out_shape=[jax.ShapeDtypeStruct((ng, nr, d), F32),
                   jax.ShapeDtypeStruct((ng, nr // tr, 2, 2 * ff), F32)],
        scratch_shapes=[pltpu.VMEM((tr + HALO, d), BF16), pltpu.VMEM((tr, ff), BF16),
                        pltpu.VMEM((2, 2, tr + HALO, tf), F32)],
        compiler_params=_cparams(("arbitrary", "arbitrary")),
    )(x, x, gains[layer:layer + 1], grp.mod(layer), grp.mod(layer), grp.mod(layer), wup_bf, conv_w, wd_bf)
    return y, buf[:, -1]


def _ffn_sample_kernel(x_ref, g_ref, sc_ref, sh_ref, gt_ref, sg_ref, sv_ref, wg_ref, wv_ref, cg_ref, cv_ref,
                       wd_ref, y_ref, bg_ref, bv_ref, h_s, *, nt, p):
    tile = lambda m: jnp.concatenate([m] * nt, axis=0)

    @pl.when(pl.program_id(2) == 0)
    def _():
        h_s[...] = _norm_mod(x_ref[0], g_ref[0], tile(sc_ref[0]), tile(sh_ref[0])).astype(BF16)

    h = h_s[...]
    n = nt * p

    def conv(st, u, cw):
        e = jnp.concatenate([st, u], axis=0)
        return cw[0:1] * e[0:n] + cw[1:2] * e[p:p + n] + cw[2:3] * e[2 * p:]

    ug = _dot(h, wg_ref[...])
    uv = _dot(h, wv_ref[...])
    bg_ref[...] = jnp.concatenate([sg_ref[...], ug], axis=0)[n:]
    bv_ref[...] = jnp.concatenate([sv_ref[...], uv], axis=0)[n:]
    _ffn_tail(conv(sg_ref[...], ug, cg_ref[...]), conv(sv_ref[...], uv, cv_ref[...]), wd_ref, x_ref,
              tile(gt_ref[0]), y_ref)


def _ffn_sample(x, grp, gains, layer, wup_bf, conv_w, wd_bf, state, tf):
    nt, p, d = x.shape
    ff = wd_bf.shape[0]
    nf = ff // tf
    n = nt * p
    st = state.transpose(1, 0, 2).reshape(2 * p, 2 * ff)
    ms = lambda j: grp.mod_spec(j, nargs=3)
    y, bg, bv = pl.pallas_call(
        functools.partial(_ffn_sample_kernel, nt=nt, p=p), grid=(1, 1, nf),
        in_specs=[pl.BlockSpec((1, n, d), lambda g, r, f: (0, 0, 0)),
                  pl.BlockSpec((1, 1, d), lambda g, r, f: (0, 0, 0)),
                  ms(4), ms(3), ms(5),
                  pl.BlockSpec((2 * p, tf), lambda g, r, f: (0, f)),
                  pl.BlockSpec((2 * p, tf), lambda g, r, f: (0, nf + f)),
                  pl.BlockSpec((d, tf), lambda g, r, f: (0, f)),
                  pl.BlockSpec((d, tf), lambda g, r, f: (0, nf + f)),
                  pl.BlockSpec((3, tf), lambda g, r, f: (0, f)),
                  pl.BlockSpec((3, tf), lambda g, r, f: (0, nf + f)),
                  pl.BlockSpec((tf, d), lambda g, r, f: (f, 0))],
        out_specs=[pl.BlockSpec((1, n, d), lambda g, r, f: (0, 0, 0)),
                   pl.BlockSpec((2 * p, tf), lambda g, r, f: (0, f)),
                   pl.BlockSpec((2 * p, tf), lambda g, r, f: (0, f))],
        out_shape=[jax.ShapeDtypeStruct((1, n, d), F32),
                   jax.ShapeDtypeStruct((2 * p, ff), F32),
                   jax.ShapeDtypeStruct((2 * p, ff), F32)],
        scratch_shapes=[pltpu.VMEM((n, d), BF16)],
        compiler_params=_cparams(("arbitrary", "arbitrary", "arbitrary")),
    )(x.reshape(1, n, d), gains[layer:layer + 1], grp.mod(layer), grp.mod(layer), grp.mod(layer), st, st, wup_bf, wup_bf, conv_w, conv_w, wd_bf)
    new_state = jnp.concatenate([bg, bv], axis=-1).reshape(2, p, 2 * ff).transpose(1, 0, 2)
    return y.reshape(nt, p, d), new_state


def _sink_col(sink_ref, heads, nq):
    return jnp.concatenate([jnp.broadcast_to(sink_ref[:, h:h + 1], (nq, 1)) for h in heads], axis=0)


def _band_mask(nq, w, prev_off):
    qi = lax.broadcasted_iota(jnp.int32, (nq, w + nq), 0)
    kj = lax.broadcasted_iota(jnp.int32, (nq, w + nq), 1)
    return ((kj < w) & (kj >= qi + prev_off)) | ((kj >= w) & ((kj - w) <= qi))


def _head(x, h):
    return x[:, h * HEAD_DIM:(h + 1) * HEAD_DIM]


def _band_attn_kernel(q_ref, kp_ref, kc_ref, vp_ref, vc_ref, qg_ref, kg_ref, cq_ref, sq_ref, cp_ref, sp_ref,
                      sink_ref, o_ref, ko_ref, vo_ref, q_s, m_s, acc_s, *, n_kv, chunk):
    i = pl.program_id(1)
    w = q_ref.shape[1]
    rows_kv = (N_HEADS // n_kv) * w
    seg = _seg_ones(2 * LANES)
    cq, sq = cq_ref[...], sq_ref[...]
    _stack_heads(_heads_rope(_heads_rms(q_ref[0], qg_ref[...], seg), cq, sq) * HEAD_DIM ** -0.5, q_s)
    kc = _heads_rope(_heads_rms(kc_ref[0], kg_ref[...], seg), cq, sq)
    kp = _heads_rope(_heads_rms(kp_ref[0], kg_ref[...], seg), cp_ref[...], sp_ref[...])
    kk = jnp.concatenate([kp, kc], axis=0).astype(BF16)
    vv = jnp.concatenate([vp_ref[0], vc_ref[0]], axis=0).astype(BF16)
    kvs = [([_head(kk, kv)], _with_ones(_head(vv, kv))) for kv in range(n_kv)]
    for h in range(N_HEADS):
        m_s[h * w:(h + 1) * w, :] = jnp.broadcast_to(sink_ref[:, h:h + 1], (w, LANES))
    acc_s[...] = jnp.concatenate([jnp.zeros((N_HEADS * w, HEAD_DIM), F32), jnp.ones((N_HEADS * w, HEAD_DIM), F32)],
                                 axis=1)
    mask = _band_mask(w, w, jnp.where(i > 0, 0, w))
    mk = jnp.concatenate([mask] * (chunk // w), axis=0)
    _flash_step([q_s], lambda c: kvs[c * chunk // rows_kv], lambda c: mk, m_s, acc_s, chunk)
    o_ref[0] = _unstack_heads(_normalized(acc_s[...]), w).astype(o_ref.dtype)
    ko_ref[0] = kc
    vo_ref[0] = vc_ref[0]


def _band_attn(y, q_norm, k_norm, sinks, cs, sn):
    b, s, _ = y.shape
    w = WINDOW
    kvd = SWA_KV_HEADS * HEAD_DIM
    hd = N_HEADS * HEAD_DIM
    kb, vb = hd // kvd, hd // kvd + 1
    prev = lambda i: jnp.maximum(i - 1, 0)
    full = lambda a: pl.BlockSpec(a.shape, lambda bb, i: (0,) * a.ndim)
    tab_c = pl.BlockSpec((w, LANES), lambda bb, i: (i, 0))
    tab_p = pl.BlockSpec((w, LANES), lambda bb, i: (prev(i), 0))
    q_norm, k_norm = jnp.tile(q_norm, (1, N_HEADS)), jnp.tile(k_norm, (1, SWA_KV_HEADS))
    return pl.pallas_call(
        functools.partial(_band_attn_kernel, n_kv=SWA_KV_HEADS, chunk=4 * w), grid=(b, s // w),
        scratch_shapes=[pltpu.VMEM((N_HEADS * w, HEAD_DIM), BF16),
                        pltpu.VMEM((N_HEADS * w, LANES), F32),
                        pltpu.VMEM((N_HEADS * w, 2 * HEAD_DIM), F32)],
        in_specs=[pl.BlockSpec((1, w, hd), lambda bb, i: (bb, i, 0)),
                  pl.BlockSpec((1, w, kvd), lambda bb, i: (bb, prev(i), kb)),
                  pl.BlockSpec((1, w, kvd), lambda bb, i: (bb, i, kb)),
                  pl.BlockSpec((1, w, kvd), lambda bb, i: (bb, prev(i), vb)),
                  pl.BlockSpec((1, w, kvd), lambda bb, i: (bb, i, vb)),
                  full(q_norm), full(k_norm), tab_c, tab_c, tab_p, tab_p, full(sinks)],
        out_specs=[pl.BlockSpec((1, w, hd), lambda bb, i: (bb, i, 0)),
                   pl.BlockSpec((1, w, kvd), lambda bb, i: (bb, 0, 0)),
                   pl.BlockSpec((1, w, kvd), lambda bb, i: (bb, 0, 0))],
        out_shape=[jax.ShapeDtypeStruct((b, s, hd), BF16),
                   jax.ShapeDtypeStruct((b, w, kvd), F32),
                   jax.ShapeDtypeStruct((b, w, kvd), F32)],
        compiler_params=_cparams(("arbitrary", "arbitrary")),
    )(y, y, y, y, y, q_norm, k_norm, cs, sn, cs, sn, sinks)


def _rows(ref, n):
    return jnp.concatenate([ref[t, 0] for t in range(n)], axis=0)


def _window_step(q, k_new, v_new, kbt, vbt, qn, kn, cs, sn, sink_ref, n_kv):
    ns, w = q.shape[0], kbt[0].shape[1]
    grp = N_HEADS // n_kv
    scale = HEAD_DIM ** -0.5
    t_q = lax.broadcasted_iota(jnp.int32, (grp * ns, w), 0) % ns
    col = lax.broadcasted_iota(jnp.int32, (grp * ns, w), 1)
    see_buf = col >= t_q
    see_new = col <= t_q
    pad = lambda a: jnp.concatenate([a, jnp.zeros((w - ns, a.shape[1]), a.dtype)], axis=0)
    outs, knew = [], []
    for kv in range(n_kv):
        kn_h = _rope(_rms(_head(k_new, kv), kn), cs, sn, ROPE_DIM)
        knew.append(kn_h)
        heads = range(kv * grp, (kv + 1) * grp)
        qg = jnp.concatenate([_rope(_rms(_head(q, h), qn), cs, sn, ROPE_DIM) * scale for h in heads],
                             axis=0).astype(BF16)
        s1 = jnp.where(see_buf, _dot(qg, kbt[kv].astype(BF16)), NEG)
        s2 = jnp.where(see_new, _dot_nt(qg, pad(kn_h).astype(BF16)), NEG)
        m = jnp.maximum(jnp.max(s1, axis=-1, keepdims=True), jnp.max(s2, axis=-1, keepdims=True))
        if sink_ref is not None:
            sink = _sink_col(sink_ref, heads, ns)
            m = jnp.maximum(m, sink)
        p1, p2 = jnp.exp(s1 - m), jnp.exp(s2 - m)
        den = jnp.sum(p1, axis=-1, keepdims=True) + jnp.sum(p2, axis=-1, keepdims=True)
        if sink_ref is not None:
            den = den + jnp.exp(sink - m)
        o = (_dot_nt(p1.astype(BF16), vbt[kv].astype(BF16))
             + _dot(p2.astype(BF16), pad(_head(v_new, kv)).astype(BF16))) / den
        outs += [o[g * ns:(g + 1) * ns] for g in range(grp)]

    def shifted(bt, new):
        new_t = jnp.concatenate([jnp.zeros((w - ns, new.shape[1]), F32), new], axis=0).T
        lane = lax.broadcasted_iota(jnp.int32, (HEAD_DIM, w), 1)
        return [jnp.where(lane >= w - ns, new_t[kv * HEAD_DIM:(kv + 1) * HEAD_DIM], pltpu.roll(bt[kv], w - ns, 1))
                for kv in range(n_kv)]

    return outs, shifted(kbt, jnp.concatenate(knew, axis=1)), shifted(vbt, v_new)


def _step_attn_kernel(q_ref, k_ref, v_ref, kb_ref, vb_ref, qn_ref, kn_ref, cs_ref, sn_ref, sink_ref,
                      o_ref, ko_ref, vo_ref, *, n_kv):
    ns = q_ref.shape[0]
    outs, k_out, v_out = _window_step(_rows(q_ref, ns), _rows(k_ref, ns), _rows(v_ref, ns),
                                      [kb_ref[0, 0, kv] for kv in range(n_kv)],
                                      [vb_ref[0, 0, kv] for kv in range(n_kv)], qn_ref[...], kn_ref[...],
                                      cs_ref[...], sn_ref[...], sink_ref, n_kv)
    o = jnp.concatenate(outs, axis=1)
    for t in range(ns):
        o_ref[t, 0] = o[t:t + 1]
    for kv in range(n_kv):
        ko_ref[0, kv] = k_out[kv]
        vo_ref[0, kv] = v_out[kv]


def _row_minor(cache):
    return cache.transpose(0, 1, 3, 4, 2)


def _step_attn(y, k_cache, v_cache, li, q_norm, k_norm, sinks, cs, sn):
    ns, db, n = y.shape
    w, n_kv = k_cache.shape[2], k_cache.shape[3]
    kvd = n_kv * HEAD_DIM
    hd = N_HEADS * HEAD_DIM
    kb, vb = hd // kvd, hd // kvd + 1
    y4 = y.reshape(ns, db, 1, n)
    full = lambda a: pl.BlockSpec(a.shape, lambda b: (0,) * a.ndim)
    buf_in = pl.BlockSpec((1, 1, n_kv, HEAD_DIM, w), lambda b: (li, b, 0, 0, 0))
    buf_out = pl.BlockSpec((1, n_kv, HEAD_DIM, w), lambda b: (b, 0, 0, 0))
    o, ko, vo = pl.pallas_call(
        functools.partial(_step_attn_kernel, n_kv=n_kv), grid=(db,),
        in_specs=[pl.BlockSpec((ns, 1, 1, hd), lambda b: (0, b, 0, 0)),
                  pl.BlockSpec((ns, 1, 1, kvd), lambda b: (0, b, 0, kb)),
                  pl.BlockSpec((ns, 1, 1, kvd), lambda b: (0, b, 0, vb)),
                  buf_in, buf_in,
                  full(q_norm), full(k_norm), full(cs), full(sn), full(sinks)],
        out_specs=[pl.BlockSpec((ns, 1, 1, hd), lambda b: (0, b, 0, 0)), buf_out, buf_out],
        out_shape=[jax.ShapeDtypeStruct((ns, db, 1, hd), F32),
                   jax.ShapeDtypeStruct((db, n_kv, HEAD_DIM, w), F32),
                   jax.ShapeDtypeStruct((db, n_kv, HEAD_DIM, w), F32)],
        compiler_params=_cparams(("arbitrary",)),
    )(y4, y4, y4, _row_minor(k_cache), _row_minor(v_cache), q_norm, k_norm, cs, sn, sinks)
    back = lambda a: a.transpose(0, 3, 1, 2)
    return o.reshape(ns, db, hd), back(ko), back(vo)


LANES = 128


def _lanes(x, n):
    if n <= LANES:
        return x[:, :n]
    return jnp.concatenate([x] * (n // LANES), axis=1)


def _flash_step(q_refs, kv_fn, mask_fn, m_ref, acc_ref, chunk):
    n = m_ref.shape[0] // chunk

    def scores(c):
        rows = pl.ds(c * chunk, chunk)
        k_parts = kv_fn(c)[0]
        s = _dot_nt(q_refs[0][rows, :], k_parts[0])
        for qr, kp in zip(q_refs[1:], k_parts[1:]):
            s = s + _dot_nt(qr[rows, :], kp)
        mk = None if mask_fn is None else mask_fn(c)
        return s if mk is None else jnp.where(mk, s, NEG)

    s = scores(0)
    for c in range(n):
        s_next = scores(c + 1) if c + 1 < n else None
        rows = pl.ds(c * chunk, chunk)
        vals = kv_fn(c)[1]
        m_old = m_ref[rows, :]
        m_new = jnp.maximum(m_old, jnp.max(s, axis=-1, keepdims=True))
        alpha = jnp.exp(m_old - m_new)
        p = jnp.exp(s - _lanes(m_new, s.shape[1])).astype(BF16)
        acc_ref[rows, :] = _lanes(alpha, vals.shape[1]) * acc_ref[rows, :] + _dot(p, vals)
        m_ref[rows, :] = m_new
        s = s_next


def _seg_ones(n, head=HEAD_DIM):
    r = lax.broadcasted_iota(jnp.int32, (n, n), 0) // head
    c = lax.broadcasted_iota(jnp.int32, (n, n), 1) // head
    return jnp.where(r == c, 1.0, 0.0).astype(BF16)


def _heads_rms(x, g, seg, head=HEAD_DIM):
    sq = x * x
    hi = sq.astype(BF16)
    lo = (sq - hi.astype(F32)).astype(BF16)
    n = seg.shape[0]
    parts = []
    for j in range(0, x.shape[1], n):
        k = min(n, x.shape[1] - j)
        b = seg[:k, :k]
        parts.append(_dot(hi[:, j:j + k], b) + _dot(lo[:, j:j + k], b))
    ss = parts[0] if len(parts) == 1 else jnp.concatenate(parts, axis=1)
    return x * lax.rsqrt(ss * (1.0 / head) + EPS) * g


def _heads_rope(x, c2, s2, head=HEAD_DIM, rot=ROPE_DIM):
    w = x.shape[1]
    half = rot // 2
    lane = lax.broadcasted_iota(jnp.int32, x.shape, 1) % head
    sw = jnp.where(lane < half, pltpu.roll(x, w - half, 1), pltpu.roll(x, half, 1))
    return x * _lanes(c2, w) + sw * _lanes(s2, w)


def _stack_heads(x, q_s):
    nq = x.shape[0]
    for h in range(x.shape[1] // HEAD_DIM):
        q_s[h * nq:(h + 1) * nq, :] = _head(x, h).astype(q_s.dtype)


def _unstack_heads(o, nq):
    return jnp.concatenate([o[h * nq:(h + 1) * nq] for h in range(o.shape[0] // nq)], axis=1)


def _normalized(acc):
    return (acc * pltpu.roll(1.0 / acc, HEAD_DIM, 1))[:, :HEAD_DIM]


def _flash_init(m_ref, acc_ref):
    m_ref[...] = jnp.full(m_ref.shape, NEG, F32)
    acc_ref[...] = jnp.zeros(acc_ref.shape, F32)


def _with_ones(v):
    pad = LANES - v.shape[1] % LANES
    return jnp.concatenate([v, jnp.ones((v.shape[0], pad), v.dtype)], axis=1)


def _mla_queries(q, nn, nr, cs, sn, wuk_ref, ql_s, qr_s):
    nq = q.shape[0]
    scale = (MLA_NOPE + MLA_ROPE) ** -0.5
    off = N_HEADS * MLA_NOPE
    qn = _heads_rms(q[:, :off], nn, _seg_ones(2 * LANES, MLA_NOPE), MLA_NOPE).astype(BF16)
    qr = _heads_rms(q[:, off:], nr, _seg_ones(2 * LANES, MLA_ROPE), MLA_ROPE)
    qr = (_heads_rope(qr, cs, sn, MLA_ROPE, MLA_ROPE) * scale).astype(BF16)
    for h in range(N_HEADS):
        rows = slice(h * nq, (h + 1) * nq)
        ql_s[rows, :] = (_dot(qn[:, h * MLA_NOPE:(h + 1) * MLA_NOPE], wuk_ref[h]) * scale).astype(BF16)
        qr_s[rows, :] = qr[:, h * MLA_ROPE:(h + 1) * MLA_ROPE]


def _mla_output(acc, den, wuv_ref, nq):
    o_lat = (acc / den).astype(BF16)
    return jnp.concatenate([_dot(o_lat[h * nq:(h + 1) * nq], wuv_ref[h]) for h in range(N_HEADS)], axis=1)


def _mla_attn_kernel(q_ref, c_ref, kr_ref, cs_ref, sn_ref, nn_ref, nr_ref, wuk_ref, wuv_ref, o_ref,
                     ql_s, qr_s, m_s, acc_s, cb_s, kb_s, *, tq, tk, chunk):
    i = pl.program_id(1)

    @pl.when(i == 0)
    def _():
        cb_s[...] = _with_ones(c_ref[0].astype(BF16))
        kb_s[...] = kr_ref[0].astype(BF16)

    _mla_queries(q_ref[0], nn_ref[...], nr_ref[...], cs_ref[...], sn_ref[...], wuk_ref, ql_s, qr_s)
    _flash_init(m_s, acc_s)
    t_pos = i * tq + lax.broadcasted_iota(jnp.int32, (tq, tk), 0)
    k_off = lax.broadcasted_iota(jnp.int32, (tq, tk), 1)

    def step(j, masked):
        start = pl.multiple_of(j * tk, tk)
        cbx = cb_s[pl.ds(start, tk), :]
        kb = kb_s[pl.ds(start, tk), :]
        mask_fn = None
        if masked:
            mk = jnp.concatenate([(start + k_off) <= t_pos] * (chunk // tq), axis=0)
            mask_fn = lambda c: mk
        kv = ([cbx[:, :MLA_KV_LORA], kb], cbx)
        _flash_step([ql_s, qr_s], lambda c: kv, mask_fn, m_s, acc_s, chunk)

    def full_body(j, carry):
        step(j, False)
        return carry

    def diag_body(j, carry):
        step(j, True)
        return carry

    n_full = (i * tq) // tk
    lax.fori_loop(0, n_full, full_body, 0)
    lax.fori_loop(n_full, (i * tq + tq - 1) // tk + 1, diag_body, 0)
    o_ref[0] = _mla_output(acc_s[:, :MLA_KV_LORA], _lanes(acc_s[:, MLA_KV_LORA:], MLA_KV_LORA), wuv_ref,
                           tq).astype(o_ref.dtype)


def _mla_attn(q, c, kr, cs, sn, nn, nr, wuk, wuv, tq, tk, chunk):
    b, s, nqc = q.shape
    hd = N_HEADS * HEAD_DIM
    full = lambda a: pl.BlockSpec(a.shape, lambda bb, i: (0,) * a.ndim)
    tab = pl.BlockSpec((tq, LANES), lambda bb, i: (i, 0))
    return pl.pallas_call(
        functools.partial(_mla_attn_kernel, tq=tq, tk=tk, chunk=chunk), grid=(b, s // tq),
        in_specs=[pl.BlockSpec((1, tq, nqc), lambda bb, i: (bb, i, 0)),
                  pl.BlockSpec((1, s, MLA_KV_LORA), lambda bb, i: (bb, 0, 0)),
                  pl.BlockSpec((1, s, MLA_ROPE), lambda bb, i: (bb, 0, 0)),
                  tab, tab, full(nn), full(nr), full(wuk), full(wuv)],
        out_specs=pl.BlockSpec((1, tq, hd), lambda bb, i: (bb, i, 0)),
        out_shape=jax.ShapeDtypeStruct((b, s, hd), BF16),
        scratch_shapes=[pltpu.VMEM((N_HEADS * tq, MLA_KV_LORA), BF16),
                        pltpu.VMEM((N_HEADS * tq, MLA_ROPE), BF16),
                        pltpu.VMEM((N_HEADS * tq, LANES), F32),
                        pltpu.VMEM((N_HEADS * tq, MLA_KV_LORA + LANES), F32),
                        pltpu.VMEM((s, MLA_KV_LORA + LANES), BF16),
                        pltpu.VMEM((s, MLA_ROPE), BF16)],
        compiler_params=_cparams(("arbitrary", "arbitrary")),
    )(q, c, kr, cs, sn, nn, nr, wuk, wuv)


def _online_step(s, pv_fn, m_ref, acc_ref):
    m_old = m_ref[...]
    m_new = jnp.maximum(m_old, jnp.max(s, axis=-1, keepdims=True))
    alpha = jnp.exp(m_old - m_new)
    p = jnp.exp(s - _lanes(m_new, s.shape[1])).astype(BF16)
    acc_ref[...] = _lanes(alpha, acc_ref.shape[1]) * acc_ref[...] + pv_fn(p)
    m_ref[...] = m_new


class _PageStream:
    def __init__(self, pt_ref, seq, layer, pools, bufs, sems, slots):
        self.pt_ref, self.seq, self.layer, self.slots = pt_ref, seq, layer, slots
        self.pools, self.bufs, self.sems = pools, bufs, sems

    def _copies(self, group, half):
        out = []
        for k in range(self.slots):
            page = self.pt_ref[self.seq, group * self.slots + k]
            for i, (pool, buf) in enumerate(zip(self.pools, self.bufs)):
                out.append(pltpu.make_async_copy(pool.at[self.layer, page], buf.at[half, k], self.sems.at[half, i]))
        return out

    def start(self, group, half):
        for c in self._copies(group, half):
            c.start()

    def wait(self, group, half):
        for c in self._copies(group, half):
            c.wait()

    def start_first(self):
        @pl.when(self.seq == 0)
        def _():
            for g in range(N_BUF - 1):
                self.start(g, g)

    def prefetch_next_sequence(self):
        @pl.when(self.seq + 1 < pl.num_programs(0))
        def _():
            nxt = _PageStream(self.pt_ref, self.seq + 1, self.layer, self.pools, self.bufs, self.sems, self.slots)
            for g in range(N_BUF - 1):
                nxt.start(g, g)

    def sweep(self, n_groups, compute):
        def body(g, carry):
            half = g % N_BUF
            ahead = g + N_BUF - 1

            @pl.when(ahead < n_groups)
            def _():
                self.start(ahead, ahead % N_BUF)

            self.wait(g, half)
            compute(g, half)
            return carry

        lax.fori_loop(0, n_groups, body, 0)


def _mla_decode_kernel(pt_ref, lat_hbm, kr_hbm, q_ref, cn_ref, kn_ref, cs_ref, sn_ref, nn_ref, nr_ref, wuk_ref,
                       wuv_ref, o_ref, ql_s, qr_s, m_s, acc_s, lat_buf, kr_buf, sems, *, ns, slots, npg, layer):
    stream = _PageStream(pt_ref, pl.program_id(0), layer, [lat_hbm, kr_hbm], [lat_buf, kr_buf], sems, slots)
    stream.start_first()
    _mla_queries(_rows(q_ref, ns), nn_ref[...], nr_ref[...], cs_ref[...], sn_ref[...], wuk_ref, ql_s, qr_s)
    _flash_init(m_s, acc_s)

    def cached(g, half):
        cbx = _with_ones(lat_buf[half].reshape(slots * PAGE_SIZE, MLA_KV_LORA).astype(BF16))
        krt = jnp.concatenate([kr_buf[half, k] for k in range(slots)], axis=1).astype(BF16)
        _online_step(_dot_nt(ql_s[...], cbx[:, :MLA_KV_LORA]) + _dot(qr_s[...], krt), lambda p: _dot(p, cbx),
                     m_s, acc_s)

    stream.sweep(npg, cached)
    stream.prefetch_next_sequence()

    pad = PAGE_SIZE - ns
    cb = jnp.concatenate([_rows(cn_ref, ns), jnp.zeros((pad, MLA_KV_LORA), F32)], axis=0).astype(BF16)
    kb = jnp.concatenate([_rows(kn_ref, ns), jnp.zeros((pad, MLA_ROPE), F32)], axis=0).astype(BF16)
    s = _dot_nt(ql_s[...], cb) + _dot_nt(qr_s[...], kb)
    row = lax.broadcasted_iota(jnp.int32, s.shape, 0) % ns
    col = lax.broadcasted_iota(jnp.int32, s.shape, 1)
    cbx_new = _with_ones(cb)
    _online_step(jnp.where(col <= row, s, NEG), lambda p: _dot(p, cbx_new), m_s, acc_s)
    o = _mla_output(acc_s[:, :MLA_KV_LORA], _lanes(acc_s[:, MLA_KV_LORA:], MLA_KV_LORA), wuv_ref, ns)
    for t in range(ns):
        o_ref[t, 0] = o[t:t + 1]


def _mla_decode(q, c, kr, lat_pool, kr_pool, li, page_table, cs, sn, nn, nr, wuk, wuv):
    ns, db, nqc = q.shape
    hd = N_HEADS * HEAD_DIM
    slots = PAGE_SLOTS
    npg = page_table.shape[1] // slots
    full = lambda a: pl.BlockSpec(a.shape, lambda b, pt: (0,) * a.ndim)
    hbm = pl.BlockSpec(memory_space=pl.ANY)
    step = lambda width: pl.BlockSpec((ns, 1, 1, width), lambda b, pt: (0, b, 0, 0))
    grid_spec = pltpu.PrefetchScalarGridSpec(
        num_scalar_prefetch=1, grid=(db,),
        in_specs=[hbm, hbm, step(nqc), step(MLA_KV_LORA), step(MLA_ROPE),
                  full(cs), full(sn), full(nn), full(nr), full(wuk), full(wuv)],
        out_specs=step(hd),
        scratch_shapes=[pltpu.VMEM((N_HEADS * ns, MLA_KV_LORA), BF16),
                        pltpu.VMEM((N_HEADS * ns, MLA_ROPE), BF16),
                        pltpu.VMEM((N_HEADS * ns, LANES), F32),
                        pltpu.VMEM((N_HEADS * ns, MLA_KV_LORA + LANES), F32),
                        pltpu.VMEM((N_BUF, slots, PAGE_SIZE, MLA_KV_LORA), F32),
                        pltpu.VMEM((N_BUF, slots, MLA_ROPE, PAGE_SIZE), F32),
                        pltpu.SemaphoreType.DMA((N_BUF, 2))])
    kr_pool_t = kr_pool.transpose(0, 1, 3, 2)
    o = pl.pallas_call(
        functools.partial(_mla_decode_kernel, ns=ns, slots=slots, npg=npg, layer=li), grid_spec=grid_spec,
        out_shape=jax.ShapeDtypeStruct((ns, db, 1, hd), F32),
        compiler_params=_cparams(("arbitrary",)),
    )(page_table, lat_pool, kr_pool_t,
      q.reshape(ns, db, 1, nqc), c.reshape(ns, db, 1, MLA_KV_LORA), kr.reshape(ns, db, 1, MLA_ROPE),
      cs, sn, nn, nr, wuk, wuv)
    return o.reshape(ns, db, hd)


NSA_KVD = NSA_KV_HEADS * HEAD_DIM
NSA_GRP = N_HEADS // NSA_KV_HEADS
NSA_PAIR = NSA_SEL_BLOCK // NSA_CMP_BLOCK


def _nsa_prep_kernel(kc_ref, vc_ref, ks_ref, kw_ref, wk_ref, wv_ref, kcn_ref, ksn_ref, kwn_ref, cs_ref, sn_ref,
                     kso_ref, kwo_ref, kcmp_ref, vcmp_ref):
    cs, sn = cs_ref[...], sn_ref[...]
    kc, vc = kc_ref[0], vc_ref[0]
    seg = _seg_ones(NSA_KVD)
    kso_ref[0] = _heads_rope(_heads_rms(ks_ref[0], ksn_ref[...], seg), cs, sn)
    kwo_ref[0] = _heads_rope(_heads_rms(kw_ref[0], kwn_ref[...], seg), cs, sn)
    kcmp, vcmp = [], []
    for kv in range(NSA_KV_HEADS):
        kcmp.append(_rms(_dot(wk_ref[kv], _head(kc, kv).astype(BF16)), kcn_ref[...]))
        vcmp.append(_dot(wv_ref[kv], _head(vc, kv).astype(BF16)))
    kcmp_ref[0] = jnp.concatenate(kcmp, axis=1)
    vcmp_ref[0] = jnp.concatenate(vcmp, axis=1)


def _cmp_matrix(w, nc):
    eye = jnp.eye(nc, dtype=F32)
    return (eye[None, :, :, None] * w.T[:, None, None, :]).reshape(w.shape[1], nc, nc * NSA_CMP_BLOCK).astype(BF16)


def _nsa_prep(y, cmp_wk, cmp_wv, k_norm, cs, sn):
    b, s, _ = y.shape
    nc = s // NSA_CMP_BLOCK
    base = N_HEADS * HEAD_DIM // NSA_KVD
    col = lambda j: pl.BlockSpec((1, s, NSA_KVD), lambda bb: (bb, 0, base + j))
    full = lambda a: pl.BlockSpec(a.shape, lambda bb: (0,) * a.ndim)
    wk, wv = _cmp_matrix(cmp_wk, nc), _cmp_matrix(cmp_wv, nc)
    kn = [k_norm[0:1]] + [jnp.tile(k_norm[j:j + 1], (1, NSA_KV_HEADS)) for j in (1, 2)]
    seq = pl.BlockSpec((1, s, NSA_KVD), lambda bb: (bb, 0, 0))
    blk = pl.BlockSpec((1, nc, NSA_KVD), lambda bb: (bb, 0, 0))
    return pl.pallas_call(
        _nsa_prep_kernel, grid=(b,),
        in_specs=[col(0), col(1), col(2), col(4), full(wk), full(wv), full(kn[0]), full(kn[1]), full(kn[2]),
                  full(cs), full(sn)],
        out_specs=[seq, seq, blk, blk],
        out_shape=[jax.ShapeDtypeStruct((b, s, NSA_KVD), F32), jax.ShapeDtypeStruct((b, s, NSA_KVD), F32),
                   jax.ShapeDtypeStruct((b, nc, NSA_KVD), F32), jax.ShapeDtypeStruct((b, nc, NSA_KVD), F32)],
        compiler_params=_cparams(("arbitrary",)),
    )(y, y, y, y, wk, wv, kn[0], kn[1], kn[2], cs, sn)


def _cmp_attend(q, kcmp, vcmp, mask, ng, row_minor=False):
    nq = q.shape[0] // ng
    mk = jnp.concatenate([mask] * ng, axis=0)
    qk, pv = (_dot, _dot_nt) if row_minor else (_dot_nt, _dot)
    s = jnp.where(mk, qk(q, kcmp.astype(BF16)), NEG)
    m = jnp.max(s, axis=-1, keepdims=True)
    p = jnp.where(mk, jnp.exp(s - m), 0.0)
    p = p / jnp.maximum(jnp.sum(p, axis=-1, keepdims=True), TINY)
    o = pv(p.astype(BF16), vcmp.astype(BF16))
    imp = p[0:nq]
    for g in range(1, ng):
        imp = imp + p[g * nq:(g + 1) * nq]
    return o, imp


def _pair_sum(imp):
    n = imp.shape[1]
    lane = lax.broadcasted_iota(jnp.int32, imp.shape, 1)
    return imp + jnp.where(lane % 2 == 0, pltpu.roll(imp, n - 1, 1), pltpu.roll(imp, 1, 1))


def _select(impx, nblk, seg, n_sel):
    lane = lax.broadcasted_iota(jnp.int32, impx.shape, 1)
    blk = (lane % seg) // NSA_PAIR
    nseg = impx.shape[1] // seg
    cnt = jnp.zeros(impx.shape, jnp.int32)
    for j in range(nblk):
        col = impx[:, j * NSA_PAIR:j * NSA_PAIR + 1]
        for sg in range(1, nseg):
            c = sg * seg + j * NSA_PAIR
            col = jnp.where(lane < sg * seg, col, impx[:, c:c + 1])
        beats = (col > impx) | ((col == impx) & (j < blk))
        cnt = cnt + beats.astype(jnp.int32)
    return (cnt < n_sel).astype(F32)


def _select_rows(impx, nblk, seg, n_sel):
    row = lax.broadcasted_iota(jnp.int32, impx.shape, 0)
    blk = (row % seg) // NSA_PAIR
    nseg, nq = impx.shape[0] // seg, impx.shape[1]
    cnt = jnp.zeros(impx.shape, jnp.int32)
    for j in range(nblk):
        parts = [jnp.broadcast_to(impx[sg * seg + j * NSA_PAIR:sg * seg + j * NSA_PAIR + 1, :], (seg, nq))
                 for sg in range(nseg)]
        col = parts[0] if nseg == 1 else jnp.concatenate(parts, axis=0)
        beats = (col > impx) | ((col == impx) & (j < blk))
        cnt = cnt + beats.astype(jnp.int32)
    return (cnt < n_sel).astype(F32)


def _expand_mask(sel, start, tk):
    nc = sel.shape[1]
    n = lax.broadcasted_iota(jnp.int32, (nc, tk), 0)
    k = lax.broadcasted_iota(jnp.int32, (nc, tk), 1)
    e = jnp.where((start + k) // NSA_CMP_BLOCK == n, 1.0, 0.0).astype(BF16)
    return _dot(sel.astype(BF16), e) > 0.5


def _nsa_attn_kernel(q_ref, gl_ref, ks_ref, vs_ref, kwp_ref, kwc_ref, vwp_ref, vwc_ref, kcmp_ref, vcmp_ref,
                     qg_ref, c2_ref, s2_ref, o_ref, qn_s, qr_s, ks_s, vs_s, m_s, acc_s, mw_s, accw_s,
                     *, tk, n_sel, chunk):
    i = pl.program_id(1)
    tq = q_ref.shape[1]
    nc = kcmp_ref.shape[1]
    rows_kv = NSA_GRP * tq
    rep = chunk // tq
    kv_of = lambda c: c * chunk // rows_kv
    kv_range = range(NSA_KV_HEADS)

    @pl.when(i == 0)
    def _():
        ks_s[...] = ks_ref[0].astype(BF16)
        vs = vs_ref[0].astype(BF16)
        vs_s[...] = jnp.concatenate([_with_ones(_head(vs, kv)) for kv in kv_range], axis=1)

    qn = _heads_rms(q_ref[0], qg_ref[...], _seg_ones(2 * LANES)) * HEAD_DIM ** -0.5
    _stack_heads(qn, qn_s)
    _stack_heads(_heads_rope(qn, c2_ref[...], s2_ref[...]), qr_s)

    t_c = i * tq + lax.broadcasted_iota(jnp.int32, (tq, nc), 0)
    n_c = lax.broadcasted_iota(jnp.int32, (tq, nc), 1)
    cmask = (n_c + 1) * NSA_CMP_BLOCK - 1 <= t_c
    o_cmp, imps = [], []
    for kv in kv_range:
        oc, imp = _cmp_attend(qn_s[kv * rows_kv:(kv + 1) * rows_kv, :], _head(kcmp_ref[0], kv),
                              _head(vcmp_ref[0], kv), cmask, NSA_GRP)
        o_cmp.append(oc)
        imps.append(imp)
    imp = jnp.concatenate(imps, axis=1).T
    row = lax.broadcasted_iota(jnp.int32, imp.shape, 0)
    imp = imp + jnp.where(row % 2 == 0, pltpu.roll(imp, imp.shape[0] - 1, 0), pltpu.roll(imp, 1, 0))
    t_i = i * tq + lax.broadcasted_iota(jnp.int32, imp.shape, 1)
    blk = (row % nc) // NSA_PAIR
    impx = jnp.where(blk == t_i // NSA_SEL_BLOCK, jnp.inf, jnp.where(blk * NSA_SEL_BLOCK <= t_i, imp, -jnp.inf))
    sel = _select_rows(impx, nc // NSA_PAIR, nc, n_sel).T

    _flash_init(m_s, acc_s)
    selk = [sel[:, kv * nc:(kv + 1) * nc] for kv in kv_range]
    t_k = i * tq + lax.broadcasted_iota(jnp.int32, (tq, tk), 0)
    k_off = lax.broadcasted_iota(jnp.int32, (tq, tk), 1)

    def body(j, carry):
        start = pl.multiple_of(j * tk, tk)
        kb = ks_s[pl.ds(start, tk), :]
        vbx = vs_s[pl.ds(start, tk), :]
        causal = (start + k_off) <= t_k
        mks = [jnp.concatenate([_expand_mask(selk[kv], start, tk) & causal] * rep, axis=0) for kv in kv_range]
        kvs = [([_head(kb, kv)], vbx[:, kv * LANES:(kv + 1) * LANES]) for kv in kv_range]
        _flash_step([qr_s], lambda c: kvs[kv_of(c)], lambda c: mks[kv_of(c)], m_s, acc_s, chunk)
        return carry

    lax.fori_loop(0, (i * tq + tq - 1) // tk + 1, body, 0)

    _flash_init(mw_s, accw_s)
    kk = jnp.concatenate([kwp_ref[0], kwc_ref[0]], axis=0).astype(BF16)
    vv = jnp.concatenate([vwp_ref[0], vwc_ref[0]], axis=0).astype(BF16)
    kvw = [([_head(kk, kv)], _with_ones(_head(vv, kv))) for kv in kv_range]
    wmk = jnp.concatenate([_band_mask(tq, tq, jnp.where(i > 0, 0, tq))] * rep, axis=0)
    _flash_step([qr_s], lambda c: kvw[kv_of(c)], lambda c: wmk, mw_s, accw_s, chunk)

    o_sel = _normalized(acc_s[...])
    o_win = _normalized(accw_s[...])
    gates = jax.nn.sigmoid(gl_ref[0])
    outs = []
    for h in range(N_HEADS):
        kv, g = divmod(h, NSA_GRP)
        outs.append(gates[:, 3 * h:3 * h + 1] * o_cmp[kv][g * tq:(g + 1) * tq]
                    + gates[:, 3 * h + 1:3 * h + 2] * o_sel[h * tq:(h + 1) * tq]
                    + gates[:, 3 * h + 2:3 * h + 3] * o_win[h * tq:(h + 1) * tq])
    o_ref[0] = jnp.concatenate(outs, axis=1).astype(o_ref.dtype)


def _nsa_attn(y, ksn, kwn, kcmp, vcmp, q_norm, cs, sn, tk):
    b, s, _ = y.shape
    tq = WINDOW
    hd = N_HEADS * HEAD_DIM
    nc = kcmp.shape[1]
    base = hd // NSA_KVD
    n_sel = min(NSA_TOPK, s // NSA_SEL_BLOCK)
    prev = lambda i: jnp.maximum(i - 1, 0)
    full = lambda a: pl.BlockSpec(a.shape, lambda bb, i: (0,) * a.ndim)
    tab = pl.BlockSpec((tq, LANES), lambda bb, i: (i, 0))
    q_norm = jnp.tile(q_norm, (1, N_HEADS))
    return pl.pallas_call(
        functools.partial(_nsa_attn_kernel, tk=tk, n_sel=n_sel, chunk=4 * tq), grid=(b, s // tq),
        in_specs=[pl.BlockSpec((1, tq, hd), lambda bb, i: (bb, i, 0)),
                  pl.BlockSpec((1, tq, NSA_KVD), lambda bb, i: (bb, i, base + 6)),
                  pl.BlockSpec((1, s, NSA_KVD), lambda bb, i: (bb, 0, 0)),
                  pl.BlockSpec((1, s, NSA_KVD), lambda bb, i: (bb, 0, base + 3)),
                  pl.BlockSpec((1, tq, NSA_KVD), lambda bb, i: (bb, prev(i), 0)),
                  pl.BlockSpec((1, tq, NSA_KVD), lambda bb, i: (bb, i, 0)),
                  pl.BlockSpec((1, tq, NSA_KVD), lambda bb, i: (bb, prev(i), base + 5)),
                  pl.BlockSpec((1, tq, NSA_KVD), lambda bb, i: (bb, i, base + 5)),
                  pl.BlockSpec((1, nc, NSA_KVD), lambda bb, i: (bb, 0, 0)),
                  pl.BlockSpec((1, nc, NSA_KVD), lambda bb, i: (bb, 0, 0)),
                  full(q_norm), tab, tab],
        out_specs=pl.BlockSpec((1, tq, hd), lambda bb, i: (bb, i, 0)),
        out_shape=jax.ShapeDtypeStruct((b, s, hd), BF16),
        scratch_shapes=[pltpu.VMEM((N_HEADS * tq, HEAD_DIM), BF16),
                        pltpu.VMEM((N_HEADS * tq, HEAD_DIM), BF16),
                        pltpu.VMEM((s, NSA_KVD), BF16),
                        pltpu.VMEM((s, NSA_KV_HEADS * LANES), BF16),
                        pltpu.VMEM((N_HEADS * tq, LANES), F32),
                        pltpu.VMEM((N_HEADS * tq, 2 * HEAD_DIM), F32),
                        pltpu.VMEM((N_HEADS * tq, LANES), F32),
                        pltpu.VMEM((N_HEADS * tq, 2 * HEAD_DIM), F32)],
        compiler_params=_cparams(("arbitrary", "arbitrary")),
    )(y, y, ksn, y, kwn, kwn, y, y, kcmp, vcmp, q_norm, cs, sn)


def _nsa_decode_kernel(pt_ref, kc_hbm, vc_hbm, ks_hbm, vs_hbm,
                       q_ref, gl_ref, ks_ref, vs_ref, kw_ref, vw_ref, kwb_ref, vwb_ref, wrow_ref, eloc_ref,
                       qn_ref, kcn_ref, ksn_ref, kwn_ref, cs_ref, sn_ref,
                       o_ref, kso_ref, kwo_ref, vwo_ref,
                       cmp_s, sel_s, ocmp_s, q_s, m_s, acc_s, k_buf, v_buf, sems,
                       *, ns, slots, npg, past, n_sel, layer):
    seq = pl.program_id(0)
    scale = HEAD_DIM ** -0.5
    nc = cmp_s.shape[1]
    rows_per_step = slots * PAGE_SIZE
    cper = rows_per_step // NSA_CMP_BLOCK
    kv_range = range(NSA_KV_HEADS)
    grp = lambda xs, kv: xs[kv * NSA_GRP:(kv + 1) * NSA_GRP]
    cmp_stream = _PageStream(pt_ref, seq, layer, [kc_hbm, vc_hbm], [k_buf, v_buf], sems, slots)
    sel_stream = _PageStream(pt_ref, seq, layer, [ks_hbm, vs_hbm], [k_buf, v_buf], sems, slots)
    cmp_stream.start_first()
    pages = lambda buf, half, kv: jnp.concatenate([buf[half, k, kv] for k in range(slots)], axis=1)

    def block_sums(g, half):
        lhs = jnp.concatenate([pages(buf, half, kv) * wrow_ref[2 * j + kv:2 * j + kv + 1, :]
                               for j, buf in enumerate((k_buf, v_buf)) for kv in kv_range], axis=0)
        res = _dot(lhs.astype(BF16), eloc_ref[...])
        for k in range(npg):
            @pl.when(g == k)
            def _():
                cmp_s[:, k * cper:(k + 1) * cper] = res

    cmp_stream.sweep(npg, block_sums)
    for g0 in range(N_BUF - 1):
        sel_stream.start(g0, g0)

    q = _rows(q_ref, ns)
    qn = [_rms(_head(q, h), qn_ref[...]) * scale for h in range(N_HEADS)]
    qr = [_rope(x, cs_ref[...], sn_ref[...], ROPE_DIM) for x in qn]
    t_c = past + lax.broadcasted_iota(jnp.int32, (ns, nc), 0)
    n_c = lax.broadcasted_iota(jnp.int32, (ns, nc), 1)
    cmask = (n_c + 1) * NSA_CMP_BLOCK - 1 <= t_c
    imps = []
    for kv in kv_range:
        kct = cmp_s[kv * HEAD_DIM:(kv + 1) * HEAD_DIM, :]
        kct = kct * lax.rsqrt(jnp.mean(kct * kct, axis=0, keepdims=True) + EPS) * kcn_ref[...]
        vct = cmp_s[(NSA_KV_HEADS + kv) * HEAD_DIM:(NSA_KV_HEADS + kv + 1) * HEAD_DIM, :]
        oc, imp = _cmp_attend(jnp.concatenate(grp(qn, kv), axis=0).astype(BF16), kct, vct, cmask, NSA_GRP,
                              row_minor=True)
        ocmp_s[kv] = oc
        imps.append(_pair_sum(imp))
        q_s[kv] = jnp.concatenate(grp(qr, kv), axis=0).astype(BF16)
    for kv in kv_range:
        sel = _select(imps[kv], nc // NSA_PAIR, nc, n_sel - 1)
        for k in range(npg):
            sel_s[kv, k] = sel[:, k * cper:(k + 1) * cper]
    _flash_init(m_s, acc_s)

    def selected(g, half):
        for kv in kv_range:
            kst = pages(k_buf, half, kv).astype(BF16)
            vst = pages(v_buf, half, kv)
            vtx = jnp.concatenate([vst, jnp.ones(vst.shape, F32)], axis=0).astype(BF16)
            mk = jnp.concatenate([_dot_nt(sel_s[kv, g].astype(BF16), eloc_ref[...]) > 0.5] * NSA_GRP, axis=0)
            _online_step(jnp.where(mk, _dot(q_s[kv], kst), NEG), lambda p, v=vtx: _dot_nt(p, v),
                         m_s.at[kv], acc_s.at[kv])

    sel_stream.sweep(npg, selected)
    cmp_stream.prefetch_next_sequence()

    pad = lambda a: jnp.concatenate([a, jnp.zeros((PAGE_SIZE - ns, a.shape[1]), a.dtype)], axis=0)
    vs_new = _rows(vs_ref, ns)
    ks_raw = _rows(ks_ref, ns)
    ksn = jnp.concatenate([_rope(_rms(_head(ks_raw, kv), ksn_ref[...]), cs_ref[...], sn_ref[...], ROPE_DIM)
                           for kv in kv_range], axis=1)
    kso_ref[0] = ksn
    kb = pad(ksn).astype(BF16)
    row = lax.broadcasted_iota(jnp.int32, (NSA_GRP * ns, PAGE_SIZE), 0) % ns
    col = lax.broadcasted_iota(jnp.int32, (NSA_GRP * ns, PAGE_SIZE), 1)
    for kv in kv_range:
        vbx = _with_ones(pad(_head(vs_new, kv)).astype(BF16))
        _online_step(jnp.where(col <= row, _dot_nt(q_s[kv], _head(kb, kv)), NEG), lambda p, v=vbx: _dot(p, v),
                     m_s.at[kv], acc_s.at[kv])
    outs_w, kw_out, vw_out = _window_step(
        q, _rows(kw_ref, ns), _rows(vw_ref, ns), [kwb_ref[0, 0, kv] for kv in kv_range],
        [vwb_ref[0, 0, kv] for kv in kv_range], qn_ref[...], kwn_ref[...], cs_ref[...], sn_ref[...], None,
        NSA_KV_HEADS)
    for kv in kv_range:
        kwo_ref[0, kv] = kw_out[kv]
        vwo_ref[0, kv] = vw_out[kv]
    gates = jax.nn.sigmoid(_rows(gl_ref, ns))
    outs = []
    for kv in kv_range:
        o_sel = _normalized(acc_s[kv])
        o_cmp = ocmp_s[kv]
        for g in range(NSA_GRP):
            h = kv * NSA_GRP + g
            outs.append(gates[:, 3 * h:3 * h + 1] * o_cmp[g * ns:(g + 1) * ns]
                        + gates[:, 3 * h + 1:3 * h + 2] * o_sel[g * ns:(g + 1) * ns]
                        + gates[:, 3 * h + 2:3 * h + 3] * outs_w[h])
    o = jnp.concatenate(outs, axis=1)
    for t in range(ns):
        o_ref[t, 0] = o[t:t + 1]


def _nsa_decode(y, pools, kw_cache, vw_cache, li, page_table, cmp_wk, cmp_wv, q_norm, k_norm, cs, sn):
    ns, db, n = y.shape
    hd = N_HEADS * HEAD_DIM
    slots = PAGE_SLOTS
    n_pages = page_table.shape[1]
    npg = n_pages // slots
    past = n_pages * PAGE_SIZE
    nc = past // NSA_CMP_BLOCK
    base = hd // NSA_KVD
    w = kw_cache.shape[2]
    n_blocks = -(-(past + ns) // NSA_SEL_BLOCK)
    n_sel = min(NSA_TOPK, n_blocks)
    y4 = y.reshape(ns, db, 1, n)
    full = lambda a: pl.BlockSpec(a.shape, lambda b, pt: (0,) * a.ndim)
    hbm = pl.BlockSpec(memory_space=pl.ANY)
    step = lambda width, cb: pl.BlockSpec((ns, 1, 1, width), lambda b, pt: (0, b, 0, cb))
    rows = slots * PAGE_SIZE
    cper = rows // NSA_CMP_BLOCK
    wrow = jnp.tile(jnp.concatenate([cmp_wk.T, cmp_wv.T], axis=0), (1, cper))
    eloc = (jnp.arange(rows)[:, None] // NSA_CMP_BLOCK == jnp.arange(cper)[None, :]).astype(BF16)
    kn = [k_norm[j:j + 1] for j in range(3)]
    kcn_col = k_norm[0].reshape(HEAD_DIM, 1)
    seq_out = pl.BlockSpec((1, ns, NSA_KVD), lambda b, pt: (b, 0, 0))
    buf_out = pl.BlockSpec((1, NSA_KV_HEADS, HEAD_DIM, w), lambda b, pt: (b, 0, 0, 0))
    buf_in = pl.BlockSpec((1, 1, NSA_KV_HEADS, HEAD_DIM, w), lambda b, pt: (li, b, 0, 0, 0))
    page_buf = pltpu.VMEM((N_BUF, slots, NSA_KV_HEADS, HEAD_DIM, PAGE_SIZE), F32)
    grid_spec = pltpu.PrefetchScalarGridSpec(
        num_scalar_prefetch=1, grid=(db,),
        in_specs=[hbm, hbm, hbm, hbm,
                  step(hd, 0), step(NSA_KVD, base + 6), step(NSA_KVD, base + 2), step(NSA_KVD, base + 3),
                  step(NSA_KVD, base + 4), step(NSA_KVD, base + 5), buf_in, buf_in,
                  full(wrow), full(eloc),
                  full(q_norm), full(kcn_col), full(kn[1]), full(kn[2]), full(cs), full(sn)],
        out_specs=[step(hd, 0), seq_out, buf_out, buf_out],
        scratch_shapes=[pltpu.VMEM((2 * NSA_KV_HEADS * HEAD_DIM, nc), F32),
                        pltpu.VMEM((NSA_KV_HEADS, npg, ns, cper), F32),
                        pltpu.VMEM((NSA_KV_HEADS, NSA_GRP * ns, HEAD_DIM), F32),
                        pltpu.VMEM((NSA_KV_HEADS, NSA_GRP * ns, HEAD_DIM), BF16),
                        pltpu.VMEM((NSA_KV_HEADS, NSA_GRP * ns, LANES), F32),
                        pltpu.VMEM((NSA_KV_HEADS, NSA_GRP * ns, 2 * HEAD_DIM), F32),
                        page_buf, page_buf, pltpu.SemaphoreType.DMA((N_BUF, 2))])
    kc_pool, vc_pool, ks_pool, vs_pool = [_row_minor(p) for p in pools]
    o, kso, kwo, vwo = pl.pallas_call(
        functools.partial(_nsa_decode_kernel, ns=ns, slots=slots, npg=npg, past=past, n_sel=n_sel, layer=li),
        grid_spec=grid_spec,
        out_shape=[jax.ShapeDtypeStruct((ns, db, 1, hd), F32),
                   jax.ShapeDtypeStruct((db, ns, NSA_KVD), F32),
                   jax.ShapeDtypeStruct((db, NSA_KV_HEADS, HEAD_DIM, w), F32),
                   jax.ShapeDtypeStruct((db, NSA_KV_HEADS, HEAD_DIM, w), F32)],
        compiler_params=_cparams(("arbitrary",)),
    )(page_table, kc_pool, vc_pool, ks_pool, vs_pool,
      y4, y4, y4, y4, y4, y4, _row_minor(kw_cache), _row_minor(vw_cache), wrow, eloc,
      q_norm, kcn_col, kn[1], kn[2], cs, sn)
    back = lambda a: a.transpose(0, 3, 1, 2)
    return o.reshape(ns, db, hd), kso, back(kwo), back(vwo)


def kernel(x_prompt, x_sample, cache_swa_k, cache_swa_v, cache_mla_latent, cache_mla_krope, cache_nsa_kcmp, cache_nsa_vcmp, cache_nsa_ksel, cache_nsa_vsel, cache_nsa_kwin, cache_nsa_vwin, state_conv_ffn, page_table, c_prompt, c_sample, ada_w, ada_b, norm_mix, norm_ffn, ffn_w_up, ffn_conv, ffn_w_down, a_w_in, a_q_norm, a_k_norm, a_sinks, a_w_out, b_w_in, b_qa_norm, b_w_qb, b_q_norm_nope, b_q_norm_rope, b_kv_norm, b_krope_norm, b_w_uk, b_w_uv, b_w_out, c_w_in, c_q_norm, c_k_norm, c_cmp_wk, c_cmp_wv, c_w_out):
    nb, seq, d = x_prompt.shape
    db, ds, _ = x_sample.shape
    depth = ada_w.shape[0]
    ff = ffn_w_down.shape[1]
    past = page_table.shape[1] * PAGE_SIZE
    hd = N_HEADS * HEAD_DIM

    mod = _modulate(jnp.concatenate([c_prompt, c_sample], axis=0), ada_w, ada_b)
    tr_p = math.gcd(seq, 512)
    gp = _Group(True, mod[:, :, :nb].reshape(depth, 6, nb, 1, d), tr_p)
    gs = _Group(False, mod[:, :, nb:], db)
    norm_mix3 = norm_mix.reshape(depth, 1, d)
    norm_ffn3 = norm_ffn.reshape(depth, 1, d)

    pos_p = jnp.arange(seq, dtype=jnp.int32)
    pos_s = past + jnp.arange(ds, dtype=jnp.int32)
    cw_p, sw_p = _rope_tables(pos_p, ROPE_DIM, ROPE_THETA, HEAD_DIM)
    cw2_p, sw2_p = jnp.tile(cw_p, (1, LANES // HEAD_DIM)), jnp.tile(sw_p, (1, LANES // HEAD_DIM))
    cw_s, sw_s = _rope_tables(pos_s, ROPE_DIM, ROPE_THETA, HEAD_DIM)
    cm_p, sm_p = _rope_tables(pos_p, MLA_ROPE, MLA_THETA, MLA_ROPE)
    cm_s, sm_s = _rope_tables(pos_s, MLA_ROPE, MLA_THETA, MLA_ROPE)

    xp = x_prompt
    xs = x_sample.transpose(1, 0, 2)
    row2 = lambda v: v.reshape(1, -1)
    tm = lambda a: a.transpose(1, 0, 2)
    out = {k: [] for k in ("swa_k_p", "swa_v_p", "swa_k_s", "swa_v_s", "mla_c_p", "mla_r_p", "mla_c_s", "mla_r_s",
                           "conv_p", "conv_s")}
    nsa_p = [[] for _ in range(6)]
    nsa_s = [[] for _ in range(6)]
    ia = ib = ic = 0
    for layer in range(depth):
        kind = layer % N_MIXERS
        if kind == 0:
            w_in = a_w_in[ia].astype(BF16)
            qn, kn, sinks = row2(a_q_norm[ia]), row2(a_k_norm[ia]), row2(a_sinks[ia])
            yp = _proj(xp, gp, norm_mix3, layer, w_in)
            op, kp, vp = _band_attn(yp, qn, kn, sinks, cw2_p, sw2_p)
            ys = _proj(xs, gs, norm_mix3, layer, w_in)
            os_, ks_, vs_ = _step_attn(ys, cache_swa_k, cache_swa_v, ia, qn, kn, sinks, cw_s, sw_s)
            kv4 = lambda a: a.reshape(a.shape[0], a.shape[1], SWA_KV_HEADS, HEAD_DIM)
            out["swa_k_p"].append(kv4(kp)); out["swa_v_p"].append(kv4(vp))
            out["swa_k_s"].append(ks_); out["swa_v_s"].append(vs_)
            w_out = a_w_out[ia].astype(BF16)
            ia += 1
        elif kind == 1:
            w_in = b_w_in[ib].astype(BF16)
            wqb = b_w_qb[ib].reshape(MLA_Q_LORA, N_HEADS, MLA_NOPE + MLA_ROPE)
            wqb = jnp.concatenate([wqb[:, :, :MLA_NOPE].reshape(MLA_Q_LORA, -1),
                                   wqb[:, :, MLA_NOPE:].reshape(MLA_Q_LORA, -1)], axis=1).astype(BF16)
            wuk = b_w_uk[ib].transpose(1, 2, 0).astype(BF16)
            wuv = b_w_uv[ib].transpose(1, 0, 2).astype(BF16)
            norms = (row2(b_qa_norm[ib]), wqb, row2(b_kv_norm[ib]), row2(b_krope_norm[ib]))
            nn = jnp.tile(row2(b_q_norm_nope[ib]), (1, N_HEADS))
            nr = jnp.tile(row2(b_q_norm_rope[ib]), (1, N_HEADS))
            wide = lambda t: jnp.tile(t, (1, LANES // MLA_ROPE))
            qp, cp, rp = _mla_proj(xp, gp, norm_mix3, layer, w_in, *norms, cm_p[None], sm_p[None])
            op = _mla_attn(qp, cp, rp, wide(cm_p), wide(sm_p), nn, nr, wuk, wuv, math.gcd(seq, 128),
                           math.gcd(seq, 256), 512)
            qs, cs_, rs_ = _mla_proj(xs, gs, norm_mix3, layer, w_in, *norms, cm_s[:, None], sm_s[:, None])
            os_ = _mla_decode(qs, cs_, rs_, cache_mla_latent, cache_mla_krope, ib, page_table,
                              wide(cm_s), wide(sm_s), nn, nr, wuk, wuv)
            out["mla_c_p"].append(cp); out["mla_r_p"].append(rp)
            out["mla_c_s"].append(tm(cs_)); out["mla_r_s"].append(tm(rs_))
            w_out = b_w_out[ib].astype(BF16)
            ib += 1
        else:
            n_in = c_w_in.shape[2]
            n_pad = -(-n_in // 128) * 128
            w_in = jnp.pad(c_w_in[ic], ((0, 0), (0, n_pad - n_in))).astype(BF16)
            qn, kn3 = row2(c_q_norm[ic]), c_k_norm[ic]
            kv4 = lambda a: a.reshape(a.shape[0], a.shape[1], NSA_KV_HEADS, HEAD_DIM)
            col = lambda a, j: a[:, :, hd + j * NSA_KVD:hd + (j + 1) * NSA_KVD]
            yp = _proj(xp, gp, norm_mix3, layer, w_in)
            ksn, kwn, kcmp, vcmp = _nsa_prep(yp, c_cmp_wk[ic], c_cmp_wv[ic], kn3, cw2_p, sw2_p)
            op = _nsa_attn(yp, ksn, kwn, kcmp, vcmp, qn, cw2_p, sw2_p, math.gcd(seq, 256))
            for j, a in enumerate((col(yp, 0), col(yp, 1), ksn, col(yp, 3), kwn[:, -WINDOW:],
                                   col(yp, 5)[:, -WINDOW:])):
                nsa_p[j].append(kv4(a))
            ys = _proj(xs, gs, norm_mix3, layer, w_in)
            pools = (cache_nsa_kcmp, cache_nsa_vcmp, cache_nsa_ksel, cache_nsa_vsel)
            os_, kso, kwo, vwo = _nsa_decode(ys, pools, cache_nsa_kwin, cache_nsa_vwin, ic, page_table,
                                             c_cmp_wk[ic], c_cmp_wv[ic], qn, kn3, cw_s, sw_s)
            for j, a in enumerate((tm(col(ys, 0)), tm(col(ys, 1)), kso, tm(col(ys, 3)))):
                nsa_s[j].append(kv4(a))
            nsa_s[4].append(kwo)
            nsa_s[5].append(vwo)
            w_out = c_w_out[ic].astype(BF16)
            ic += 1
        xp = _outproj(op, xp, gp, layer, w_out)
        xs = _outproj(os_, xs, gs, layer, w_out)
        wup, wd = ffn_w_up[layer].astype(BF16), ffn_w_down[layer].astype(BF16)
        tf = math.gcd(ff, 256)
        xp, bp = _ffn_prompt(xp, gp, norm_ffn3, layer, wup, ffn_conv[layer], wd, math.gcd(seq, 1024), tf)
        xs, bs = _ffn_sample(xs, gs, norm_ffn3, layer, wup, ffn_conv[layer], wd, state_conv_ffn[layer], tf)
        out["conv_p"].append(bp); out["conv_s"].append(bs)

    st = lambda xs_: jnp.stack(xs_)
    return (xp, xs.transpose(1, 0, 2),
            st(out["swa_k_p"]), st(out["swa_v_p"]), st(out["mla_c_p"]), st(out["mla_r_p"]),
            *[st(a) for a in nsa_p], st(out["conv_p"]),
            st(out["swa_k_s"]), st(out["swa_v_s"]), st(out["mla_c_s"]), st(out["mla_r_s"]),
            *[st(a) for a in nsa_s], st(out["conv_s"]))
```
